```python
import jax, jax.numpy as jnp
from jax import lax
import numpy as np

D_MODEL = 1024
BATCH = 8
SEQ = 2048
DEPTH = 1

ATTN_HEADS = 16
HEAD_DIM = 64
ATTN_WIDTH = ATTN_HEADS * HEAD_DIM
KV_HEADS = 4
Q_PER_KV = ATTN_HEADS // KV_HEADS
KV_WIDTH = KV_HEADS * HEAD_DIM
ROPE_DIM = HEAD_DIM // 4
ROPE_THETA = 500000.0
CMP_BLOCK = 32
CMP_STRIDE = 16
CMP_HIDDEN = 256
SLC_BLOCK = 64
SLC_TOPN = 16
WINDOW = 512
N_BRANCH = 3
Q_BLOCK = 128
SLC_Q_BLOCK = 32

SSM_HEADS = 16
SSM_HEAD_DIM = 64
SSM_WIDTH = SSM_HEADS * SSM_HEAD_DIM
SSM_GROUPS = 4
SSM_STATE = 128
CONV_WIDTH = 4
CONV_CH = SSM_WIDTH + 2 * SSM_GROUPS * SSM_STATE
SSM_CHUNK = 256

MIX_WIDTH = ATTN_WIDTH + SSM_WIDTH
IN_WIDTH = (2 * ATTN_WIDTH + 6 * KV_WIDTH + ATTN_HEADS * N_BRANCH
            + 2 * SSM_WIDTH + 2 * SSM_GROUPS * SSM_STATE + SSM_HEADS)
EPS = 1e-6
NEG = -1e30
BIG = 1e30

kernel_name = "nsa_ssd_parallel_hybrid"


def _in_split_points():
    gb = SSM_GROUPS * SSM_STATE
    sizes = [ATTN_WIDTH, KV_WIDTH, KV_WIDTH, KV_WIDTH, KV_WIDTH, KV_WIDTH, KV_WIDTH,
             ATTN_HEADS * N_BRANCH, ATTN_WIDTH, SSM_WIDTH, SSM_WIDTH, gb, gb, SSM_HEADS]
    return np.cumsum(sizes)[:-1].tolist()


def rms_norm(x, w):
    xf = x.astype(jnp.float32)
    y = xf * lax.rsqrt(jnp.mean(xf * xf, axis=-1, keepdims=True) + EPS)
    return (y * w.astype(jnp.float32)).astype(x.dtype)


def partial_rope(t, pos):
    half = ROPE_DIM // 2
    inv_freq = ROPE_THETA ** (-jnp.arange(half, dtype=jnp.float32) * 2.0 / ROPE_DIM)
    ang = pos.astype(jnp.float32)[:, None] * inv_freq[None, :]
    cos = jnp.cos(ang)[None, :, None, :].astype(t.dtype)
    sin = jnp.sin(ang)[None, :, None, :].astype(t.dtype)
    t1, t2, rest = t[..., :half], t[..., half:ROPE_DIM], t[..., ROPE_DIM:]
    return jnp.concatenate([t1 * cos - t2 * sin, t2 * cos + t1 * sin, rest], axis=-1)


def masked_softmax(s, mask):
    s = jnp.where(mask, s.astype(jnp.float32), NEG)
    p = jax.nn.softmax(s, axis=-1)
    return jnp.where(mask, p, 0.0)


def compress_tokens(t, pos_emb, w1, b1, w2, b2):
    bsz, g, s, dh = t.shape
    nc = (s - CMP_BLOCK) // CMP_STRIDE + 1
    idx = np.arange(nc)[:, None] * CMP_STRIDE + np.arange(CMP_BLOCK)[None, :]
    blk = t[:, :, idx] + pos_emb
    h = jax.nn.silu(blk.reshape(bsz, g, nc, CMP_BLOCK * dh) @ w1 + b1)
    return h @ w2 + b2


def nsa_compressed(q, kc, vc):
    s_len = q.shape[3]
    nc = kc.shape[2]
    pos = jnp.arange(s_len)
    blk_end = jnp.arange(nc) * CMP_STRIDE + CMP_BLOCK - 1
    mask = blk_end[None, :] <= pos[:, None]
    s = jnp.einsum('bgrsd,bgcd->bgrsc', q, kc) * (HEAD_DIM ** -0.5)
    p = masked_softmax(s, mask)
    o = jnp.einsum('bgrsc,bgcd->bgrsd', p.astype(vc.dtype), vc)
    return o, p


def select_blocks(p_cmp):
    s_len, nc = p_cmp.shape[3], p_cmp.shape[4]
    nb = s_len // SLC_BLOCK
    n_sel = min(SLC_TOPN, nb)
    ci = np.arange(nc)[:, None] * CMP_STRIDE
    bj = np.arange(nb)[None, :] * SLC_BLOCK
    overlap = jnp.asarray((ci <= bj + SLC_BLOCK - 1) & (ci + CMP_BLOCK - 1 >= bj), jnp.float32)
    imp = jnp.einsum('bgrsc,cj->bgsj', p_cmp, overlap)
    j = jnp.arange(nb)[None, :]
    cur = (jnp.arange(s_len) // SLC_BLOCK)[:, None]
    forced = (j == 0) | (j == cur) | (j == cur - 1)
    imp = jnp.where(forced, BIG, imp)
    imp = jnp.where(j > cur, -BIG, imp)
    _, idx = lax.top_k(imp, n_sel)
    return idx


def nsa_selected(q, k, v, blk_idx):
    bsz, g, r, s_len, dh = q.shape
    nb = s_len // SLC_BLOCK
    n_sel = blk_idx.shape[-1]
    nq = s_len // SLC_Q_BLOCK
    kb = k.reshape(bsz, g, nb, SLC_BLOCK, dh)
    vb = v.reshape(bsz, g, nb, SLC_BLOCK, dh)
    gather = jax.vmap(jax.vmap(lambda tb, ib: tb[ib]))
    qs = jnp.moveaxis(q.reshape(bsz, g, r, nq, SLC_Q_BLOCK, dh), 3, 0)
    ids = jnp.moveaxis(blk_idx.reshape(bsz, g, nq, SLC_Q_BLOCK, n_sel), 2, 0)
    ps = jnp.arange(s_len).reshape(nq, SLC_Q_BLOCK)
    offs = jnp.arange(SLC_BLOCK)
    n_keys = n_sel * SLC_BLOCK

    def block(args):
        qc, ic, pc = args
        kg = gather(kb, ic).reshape(bsz, g, SLC_Q_BLOCK, n_keys, dh)
        vg = gather(vb, ic).reshape(bsz, g, SLC_Q_BLOCK, n_keys, dh)
        kpos = (ic[..., None] * SLC_BLOCK + offs).reshape(bsz, g, SLC_Q_BLOCK, n_keys)
        mask = (kpos <= pc[None, None, :, None])[:, :, None]
        s = jnp.einsum('bgrqd,bgqkd->bgrqk', qc, kg) * (HEAD_DIM ** -0.5)
        p = masked_softmax(s, mask)
        return jnp.einsum('bgrqk,bgqkd->bgrqd', p.astype(vg.dtype), vg)

    o = lax.map(block, (qs, ids, ps))
    return jnp.moveaxis(o, 0, 3).reshape(bsz, g, r, s_len, dh)


def nsa_window(q, k, v):
    bsz, g, r, s_len, dh = q.shape
    nq = s_len // Q_BLOCK
    span = WINDOW + Q_BLOCK
    kp = jnp.pad(k, ((0, 0), (0, 0), (WINDOW, 0), (0, 0)))
    vp = jnp.pad(v, ((0, 0), (0, 0), (WINDOW, 0), (0, 0)))
    qs = jnp.moveaxis(q.reshape(bsz, g, r, nq, Q_BLOCK, dh), 3, 0)

    def block(args):
        qc, qb = args
        start = qb * Q_BLOCK
        kc = lax.dynamic_slice_in_dim(kp, start, span, axis=2)
        vc = lax.dynamic_slice_in_dim(vp, start, span, axis=2)
        qpos = start + jnp.arange(Q_BLOCK)
        kpos = start - WINDOW + jnp.arange(span)
        diff = qpos[:, None] - kpos[None, :]
        mask = (diff >= 0) & (diff < WINDOW) & (kpos[None, :] >= 0)
        s = jnp.einsum('bgrqd,bgkd->bgrqk', qc, kc) * (HEAD_DIM ** -0.5)
        p = masked_softmax(s, mask)
        return jnp.einsum('bgrqk,bgkd->bgrqd', p.astype(vc.dtype), vc)

    o = lax.map(block, (qs, jnp.arange(nq)))
    return jnp.moveaxis(o, 0, 3).reshape(bsz, g, r, s_len, dh)


def causal_depthwise_conv(u, w, b):
    out = lax.conv_general_dilated(
        u, w[:, None, :].astype(u.dtype), window_strides=(1,),
        padding=[(CONV_WIDTH - 1, 0)], dimension_numbers=('NWC', 'WIO', 'NWC'),
        feature_group_count=u.shape[-1])
    return out + b


def ssd_scan(xh, dt, a_neg, bm, cm):
    bsz, s_len, nh, hp = xh.shape
    f32 = jnp.float32
    pad = (-s_len) % SSM_CHUNK
    sp = s_len + pad
    nc = sp // SSM_CHUNK
    r = nh // SSM_GROUPS
    xdt = xh.astype(f32) * dt[..., None]
    a = dt * a_neg

    def padf(t):
        return jnp.pad(t, ((0, 0), (0, pad)) + ((0, 0),) * (t.ndim - 2))

    xc = padf(xdt).reshape(bsz, nc, SSM_CHUNK, SSM_GROUPS, r, hp)
    bc = padf(bm.astype(f32)).reshape(bsz, nc, SSM_CHUNK, SSM_GROUPS, SSM_STATE)
    cc = padf(cm.astype(f32)).reshape(bsz, nc, SSM_CHUNK, SSM_GROUPS, SSM_STATE)
    ac = padf(a).reshape(bsz, nc, SSM_CHUNK, SSM_GROUPS, r).transpose(0, 3, 4, 1, 2)
    a_cs = jnp.cumsum(ac, axis=-1)
    tril = np.tril(np.ones((SSM_CHUNK, SSM_CHUNK), dtype=bool))
    decay_in = jnp.exp(jnp.where(tril, a_cs[..., :, None] - a_cs[..., None, :], -jnp.inf))
    cb = jnp.einsum('bclgn,bcsgn->bgcls', cc, bc)
    y_diag = jnp.einsum('bgrcls,bcsgrp->bclgrp', cb[:, :, None] * decay_in, xc)
    decay_to_end = jnp.exp(a_cs[..., -1:] - a_cs)
    states = jnp.einsum('bclgn,bgrcl,bclgrp->cbgrpn', bc, decay_to_end, xc)
    chunk_decay = jnp.moveaxis(jnp.exp(a_cs[..., -1]), -1, 0)

    def step(h, inp):
        st, dec = inp
        return h * dec[..., None, None] + st, h

    h0 = jnp.zeros((bsz, SSM_GROUPS, r, hp, SSM_STATE), f32)
    _, h_in = lax.scan(step, h0, (states, chunk_decay))
    y_off = jnp.einsum('bclgn,cbgrpn,bgrcl->bclgrp', cc, h_in, jnp.exp(a_cs))
    return (y_diag + y_off).reshape(bsz, sp, nh, hp)[:, :s_len]


def hybrid_layer(x, w_in, w_out, pre_w, post_w, cmp_pos, cmp_w1, cmp_b1, cmp_w2, cmp_b2,
                 gate_b, conv_w, conv_b, dt_bias, a_log, d_skip, ssm_norm_w):
    bsz, s_len, _ = x.shape
    pos = jnp.arange(s_len)
    h = rms_norm(x, pre_w)
    proj = h @ w_in
    (q, k_cm, v_cm, k_sl, v_sl, k_wn, v_wn, g_log, z_att,
     z_ssm, x_ssm, b_ssm, c_ssm, dt_raw) = jnp.split(proj, _in_split_points(), axis=-1)

    def heads(t, n):
        return t.reshape(bsz, s_len, n, HEAD_DIM)

    def kv_layout(t):
        return t.transpose(0, 2, 1, 3)

    def q_layout(t):
        return t.reshape(bsz, s_len, KV_HEADS, Q_PER_KV, HEAD_DIM).transpose(0, 2, 3, 1, 4)

    qh = heads(q, ATTN_HEADS)
    q_raw = q_layout(qh)
    q_rot = q_layout(partial_rope(qh, pos))
    kc = compress_tokens(kv_layout(heads(k_cm, KV_HEADS)), cmp_pos[0], cmp_w1[0], cmp_b1[0], cmp_w2[0], cmp_b2[0])
    vc = compress_tokens(kv_layout(heads(v_cm, KV_HEADS)), cmp_pos[1], cmp_w1[1], cmp_b1[1], cmp_w2[1], cmp_b2[1])
    o_cmp, p_cmp = nsa_compressed(q_raw, kc, vc)
    blk_idx = select_blocks(p_cmp)
    o_slc = nsa_selected(q_rot, kv_layout(partial_rope(heads(k_sl, KV_HEADS), pos)),
                         kv_layout(heads(v_sl, KV_HEADS)), blk_idx)
    o_win = nsa_window(q_rot, kv_layout(partial_rope(heads(k_wn, KV_HEADS), pos)),
                       kv_layout(heads(v_wn, KV_HEADS)))
    gates = jax.nn.sigmoid(g_log + gate_b).reshape(bsz, s_len, KV_HEADS, Q_PER_KV, N_BRANCH)
    gates = gates.transpose(0, 2, 3, 1, 4)
    o_att = gates[..., 0:1] * o_cmp + gates[..., 1:2] * o_slc + gates[..., 2:3] * o_win
    attn_out = o_att.transpose(0, 3, 1, 2, 4).reshape(bsz, s_len, ATTN_WIDTH) * jax.nn.silu(z_att)

    xbc = jnp.concatenate([x_ssm, b_ssm, c_ssm], axis=-1)
    xbc = jax.nn.silu(causal_depthwise_conv(xbc, conv_w, conv_b))
    x_c, b_c, c_c = jnp.split(xbc, [SSM_WIDTH, SSM_WIDTH + SSM_GROUPS * SSM_STATE], axis=-1)
    dt = jax.nn.softplus(dt_raw.astype(jnp.float32) + dt_bias.astype(jnp.float32))
    a_neg = -jnp.exp(a_log.astype(jnp.float32))
    xh = x_c.reshape(bsz, s_len, SSM_HEADS, SSM_HEAD_DIM)
    y = ssd_scan(xh, dt, a_neg,
                 b_c.reshape(bsz, s_len, SSM_GROUPS, SSM_STATE),
                 c_c.reshape(bsz, s_len, SSM_GROUPS, SSM_STATE))
    y = y + d_skip.astype(jnp.float32)[:, None] * xh.astype(jnp.float32)
    y = y.reshape(bsz, s_len, SSM_WIDTH) * jax.nn.silu(z_ssm.astype(jnp.float32))
    ssm_out = rms_norm(y, ssm_norm_w).astype(x.dtype)

    mix = jnp.concatenate([attn_out.astype(x.dtype), ssm_out], axis=-1)
    return x + rms_norm(mix @ w_out, post_w)


def setup_inputs(seed: int = 0) -> dict:
    key = jax.random.key(seed)
    ks = jax.random.split(key, 20)
    f32 = jnp.float32
    nrm = lambda k, shape, scale: jax.random.normal(k, shape, f32) * scale
    dt0 = jnp.exp(jax.random.uniform(ks[14], (DEPTH, SSM_HEADS), f32)
                  * (jnp.log(0.1) - jnp.log(0.001)) + jnp.log(0.001))
    return {
        "x": nrm(ks[0], (BATCH, SEQ, D_MODEL), 1.0),
        "w_in": nrm(ks[1], (DEPTH, D_MODEL, IN_WIDTH), D_MODEL ** -0.5),
        "w_out": nrm(ks[2], (DEPTH, MIX_WIDTH, D_MODEL), MIX_WIDTH ** -0.5),
        "pre_norm_w": 1.0 + nrm(ks[3], (DEPTH, D_MODEL), 0.02),
        "post_norm_w": 1.0 + nrm(ks[4], (DEPTH, D_MODEL), 0.02),
        "cmp_pos": nrm(ks[5], (DEPTH, 2, CMP_BLOCK, HEAD_DIM), 0.02),
        "cmp_w1": nrm(ks[6], (DEPTH, 2, CMP_BLOCK * HEAD_DIM, CMP_HIDDEN), (CMP_BLOCK * HEAD_DIM) ** -0.5),
        "cmp_b1": nrm(ks[7], (DEPTH, 2, CMP_HIDDEN), 0.01),
        "cmp_w2": nrm(ks[8], (DEPTH, 2, CMP_HIDDEN, HEAD_DIM), CMP_HIDDEN ** -0.5),
        "cmp_b2": nrm(ks[9], (DEPTH, 2, HEAD_DIM), 0.01),
        "gate_b": nrm(ks[10], (DEPTH, ATTN_HEADS * N_BRANCH), 0.01),
        "conv_w": nrm(ks[11], (DEPTH, CONV_WIDTH, CONV_CH), CONV_WIDTH ** -0.5),
        "conv_b": nrm(ks[12], (DEPTH, CONV_CH), 0.01),
        "dt_bias": dt0 + jnp.log(-jnp.expm1(-dt0)),
        "a_log": jnp.log(jax.random.uniform(ks[15], (DEPTH, SSM_HEADS), f32, 1.0, 16.0)),
        "d_skip": 1.0 + nrm(ks[16], (DEPTH, SSM_HEADS), 0.02),
        "ssm_norm_w": 1.0 + nrm(ks[17], (DEPTH, SSM_WIDTH), 0.02),
    }


def reference(x, w_in, w_out, pre_norm_w, post_norm_w, cmp_pos, cmp_w1, cmp_b1, cmp_w2, cmp_b2,
              gate_b, conv_w, conv_b, dt_bias, a_log, d_skip, ssm_norm_w):
    for l in range(DEPTH):
        x = hybrid_layer(x, w_in[l], w_out[l], pre_norm_w[l], post_norm_w[l],
                         cmp_pos[l], cmp_w1[l], cmp_b1[l], cmp_w2[l], cmp_b2[l],
                         gate_b[l], conv_w[l], conv_b[l], dt_bias[l], a_log[l],
                         d_skip[l], ssm_norm_w[l])
    return x
```

```python
import functools

import numpy as np
import jax
import jax.numpy as jnp
from jax import lax
from jax.experimental import pallas as pl
from jax.experimental.pallas import tpu as pltpu

F32 = jnp.float32
BF16 = jnp.bfloat16

D_MODEL = 1024
ATTN_HEADS = 16
HEAD_DIM = 64
ATTN_WIDTH = ATTN_HEADS * HEAD_DIM
KV_HEADS = 4
Q_PER_KV = ATTN_HEADS // KV_HEADS
KV_WIDTH = KV_HEADS * HEAD_DIM
ROPE_DIM = HEAD_DIM // 4
ROPE_THETA = 500000.0
CMP_BLOCK = 32
CMP_STRIDE = 16
CMP_HIDDEN = 256
SLC_BLOCK = 64
SLC_TOPN = 16
WINDOW = 512
N_BRANCH = 3
SSM_HEADS = 16
SSM_HEAD_DIM = 64
SSM_WIDTH = SSM_HEADS * SSM_HEAD_DIM
SSM_GROUPS = 4
SSM_STATE = 128
CONV_WIDTH = 4
CONV_CH = SSM_WIDTH + 2 * SSM_GROUPS * SSM_STATE
MIX_WIDTH = ATTN_WIDTH + SSM_WIDTH
EPS = 1e-6
NEG = -1e30
BIG = 1e30

LANES = 128
PAIR = 2 * HEAD_DIM
N_PAIR_COLS = KV_HEADS // 2
SMALL_W = LANES
DT_LANE0 = ATTN_HEADS * N_BRANCH
VMEM_LIMIT = 56 * 1024 * 1024

PROJ_ROWS = 512
ATTN_Q = 256
ATTN_K = 256
SSD_CHUNK = 256
OUT_ROWS = 512


def _dot(a, b):
    return jnp.dot(a, b, preferred_element_type=F32)


def _dot_nt(a, b):
    return lax.dot_general(a, b, (((1,), (1,)), ((), ())), preferred_element_type=F32)


def _split3(x):
    x1 = x.astype(BF16)
    r1 = x - x1.astype(F32)
    x2 = r1.astype(BF16)
    x3 = (r1 - x2.astype(F32)).astype(BF16)
    return x1, x2, x3


def _dot_select(x, sel):
    x1, x2, x3 = _split3(x)
    return _dot(x1, sel) + _dot(x2, sel) + _dot(x3, sel)


def _sigmoid(x):
    return 1.0 / (1.0 + jnp.exp(-x))


def _silu(x):
    return x * _sigmoid(x)


def _rope_chunk(c, cos_t, sin_a, sin_b):
    return c * cos_t + pltpu.roll(c, LANES - ROPE_DIM // 2, 1) * sin_a + pltpu.roll(c, ROPE_DIM // 2, 1) * sin_b


def _proj_kernel(x_ref, pre_w_ref, wq_ref, wcm_ref, wkv_ref, wza_ref, wzs_ref, wxbc_ref, wsm_ref,
                 cos_ref, sina_ref, sinb_ref,
                 qn_ref, qr_ref, cm_ref, kv_ref, za_ref, zs_ref, xbc_ref, sm_ref):
    x = x_ref[...]
    h = x * lax.rsqrt(jnp.mean(x * x, axis=-1, keepdims=True) + EPS) * pre_w_ref[...]
    h = h.astype(BF16)
    cos_t, sin_a, sin_b = cos_ref[...], sina_ref[...], sinb_ref[...]

    q = _dot(h, wq_ref[...])
    qn_ref[...] = q.astype(BF16)
    for k in range(ATTN_WIDTH // LANES):
        sl = slice(k * LANES, (k + 1) * LANES)
        qr_ref[:, sl] = _rope_chunk(q[:, sl], cos_t, sin_a, sin_b).astype(BF16)

    cm_ref[...] = _dot(h, wcm_ref[...]).astype(BF16)

    kv = _dot(h, wkv_ref[...])
    for t in range(4):
        for k in range(N_PAIR_COLS):
            c = kv[:, t * KV_WIDTH + k * LANES: t * KV_WIDTH + (k + 1) * LANES]
            if t % 2 == 0:
                c = _rope_chunk(c, cos_t, sin_a, sin_b)
            kv_ref[t, :, k * LANES:(k + 1) * LANES] = c.astype(BF16)

    za_ref[...] = _dot(h, wza_ref[...]).astype(BF16)
    zs_ref[...] = _dot(h, wzs_ref[...]).astype(BF16)
    xbc_ref[...] = _dot(h, wxbc_ref[...]).astype(BF16)
    sm_ref[...] = _dot(h, wsm_ref[...])


def _resident(shape):
    nd = len(shape)
    return pl.BlockSpec(shape, lambda *_: (0,) * nd, pipeline_mode=pl.Buffered(1))


def _proj_call(x2, pre_w, wq, wcm, wkv, wza, wzs, wxbc, wsm, cos_t, sin_a, sin_b, seq):
    rows = x2.shape[0]
    tm = PROJ_ROWS
    n_seq_tiles = seq // tm
    row_spec = lambda w: pl.BlockSpec((tm, w), lambda i: (i, 0))
    tab_spec = pl.BlockSpec((tm, LANES), lambda i: (i % n_seq_tiles, 0))
    return pl.pallas_call(
        _proj_kernel,
        grid=(rows // tm,),
        in_specs=[row_spec(D_MODEL), _resident(pre_w.shape), _resident(wq.shape), _resident(wcm.shape),
                  _resident(wkv.shape), _resident(wza.shape), _resident(wzs.shape), _resident(wxbc.shape),
                  _resident(wsm.shape), tab_spec, tab_spec, tab_spec],
        out_specs=[row_spec(ATTN_WIDTH), row_spec(ATTN_WIDTH), row_spec(2 * KV_WIDTH),
                   pl.BlockSpec((4, tm, KV_WIDTH), lambda i: (0, i, 0)),
                   row_spec(ATTN_WIDTH), row_spec(SSM_WIDTH), row_spec(CONV_CH), row_spec(SMALL_W)],
        out_shape=[jax.ShapeDtypeStruct((rows, ATTN_WIDTH), BF16),
                   jax.ShapeDtypeStruct((rows, ATTN_WIDTH), BF16),
                   jax.ShapeDtypeStruct((rows, 2 * KV_WIDTH), BF16),
                   jax.ShapeDtypeStruct((4, rows, KV_WIDTH), BF16),
                   jax.ShapeDtypeStruct((rows, ATTN_WIDTH), BF16),
                   jax.ShapeDtypeStruct((rows, SSM_WIDTH), BF16),
                   jax.ShapeDtypeStruct((rows, CONV_CH), BF16),
                   jax.ShapeDtypeStruct((rows, SMALL_W), F32)],
        compiler_params=pltpu.CompilerParams(dimension_semantics=("arbitrary",),
                                             vmem_limit_bytes=VMEM_LIMIT),
        name="proj",
    )(x2, pre_w, wq, wcm, wkv, wza, wzs, wxbc, wsm, cos_t, sin_a, sin_b)


def _compress_kernel(x_ref, bd1_ref, w1_ref, pos_ref, b1_ref, bd2_ref, b2_ref, out_ref, acc_ref):
    t = pl.program_id(2)
    n_rows = x_ref.shape[0]
    hid2 = 2 * CMP_HIDDEN

    @pl.when(t == 0)
    def _():
        acc_ref[...] = jnp.zeros_like(acc_ref)

    acc_ref[0:n_rows, :] += _dot(x_ref[...], bd1_ref[...])

    @pl.when(t == CMP_STRIDE - 1)
    def _():
        first = acc_ref[0:n_rows, 0:hid2]
        second = acc_ref[pl.ds(1, n_rows), hid2:2 * hid2]
        posterm = _dot(pos_ref[...].astype(BF16), w1_ref[...].astype(BF16))[0:1, :] + b1_ref[...]
        hcat = first + second + jnp.concatenate([posterm, posterm], axis=1)
        out_ref[...] = _dot(_silu(hcat).astype(BF16), bd2_ref[...]) + b2_ref[...]


def _compress_call(cm2, bd1, w1, pos8, b1, bd2, b2):
    n_rows = cm2.shape[0]
    col_blocks_per_t = 2 * N_PAIR_COLS
    return pl.pallas_call(
        _compress_kernel,
        grid=(2, N_PAIR_COLS, CMP_STRIDE),
        in_specs=[pl.BlockSpec((n_rows, LANES), lambda s, c, t: (0, t * col_blocks_per_t + s * N_PAIR_COLS + c)),
                  pl.BlockSpec((None, None, LANES, 4 * CMP_HIDDEN), lambda s, c, t: (s, t, 0, 0)),
                  pl.BlockSpec((None, CMP_BLOCK * HEAD_DIM, CMP_HIDDEN), lambda s, c, t: (s, 0, 0)),
                  pl.BlockSpec((None, 8, CMP_BLOCK * HEAD_DIM), lambda s, c, t: (s, 0, 0)),
                  pl.BlockSpec((None, 1, CMP_HIDDEN), lambda s, c, t: (s, 0, 0)),
                  pl.BlockSpec((None, 2 * CMP_HIDDEN, LANES), lambda s, c, t: (s, 0, 0)),
                  pl.BlockSpec((None, 1, LANES), lambda s, c, t: (s, 0, 0))],
        out_specs=pl.BlockSpec((None, None, n_rows, LANES), lambda s, c, t: (s, c, 0, 0)),
        out_shape=jax.ShapeDtypeStruct((2, N_PAIR_COLS, n_rows, LANES), F32),
        scratch_shapes=[pltpu.VMEM((n_rows + 8, 4 * CMP_HIDDEN), F32)],
        compiler_params=pltpu.CompilerParams(dimension_semantics=("arbitrary", "arbitrary", "arbitrary"),
                                             vmem_limit_bytes=VMEM_LIMIT),
        name="compress",
    )(cm2, bd1, w1, pos8, b1, bd2, b2)


def _attn_kernel(qn_ref, qr_ref, kc_ref, vc_ref, ksl_ref, vsl_ref, kwn_ref, vwn_ref, sm_ref, gb_ref,
                 eg_ref, ovt_ref, za_ref, out_ref,
                 qm_ref, mem_ref, m_ref, l_ref, acc_ref, osum_ref):
    tq, tk = ATTN_Q, ATTN_K
    qt = pl.program_id(2)
    q0 = qt * tq
    lane = lax.broadcasted_iota(jnp.int32, (1, LANES), 1)
    lo = lane < HEAD_DIM
    halves = (lo, jnp.logical_not(lo))
    qpos = q0 + lax.broadcasted_iota(jnp.int32, (tq, 1), 0)

    gates = _sigmoid(sm_ref[...] + gb_ref[...])
    gexp = _dot_select(gates, eg_ref[...])
    n_r = Q_PER_KV
    gate = lambda br, r: gexp[:, (br * n_r + r) * LANES:(br * n_r + r + 1) * LANES]

    def lanes_of(lo_col, hi_col):
        return jnp.where(lo, lo_col, hi_col)

    kc = kc_ref[...].astype(BF16)
    vc = vc_ref[...]
    vc_h = [jnp.where(hm, vc, 0.0).astype(BF16) for hm in halves]
    cpos = lax.broadcasted_iota(jnp.int32, (1, LANES), 1) * CMP_STRIDE + (CMP_BLOCK - 1)
    cmask = cpos <= qpos
    psum = [jnp.zeros((tq, LANES), F32), jnp.zeros((tq, LANES), F32)]
    for r in range(n_r):
        q = qn_ref[:, r * LANES:(r + 1) * LANES]
        o = jnp.zeros((tq, LANES), F32)
        for hf in range(2):
            qm = jnp.where(halves[hf], q, jnp.zeros_like(q))
            s = jnp.where(cmask, _dot_nt(qm, kc), NEG)
            m = jnp.max(s, axis=1, keepdims=True)
            p = jnp.where(cmask, jnp.exp(s - m), 0.0)
            l = jnp.sum(p, axis=1, keepdims=True)
            p = p * jnp.where(l > 0.0, 1.0 / l, 0.0)
            psum[hf] = psum[hf] + p
            o = o + _dot(p.astype(BF16), vc_h[hf])
        osum_ref[r] = gate(0, r) * o

    nb = ksl_ref.shape[0] // SLC_BLOCK
    jio = lax.broadcasted_iota(jnp.int32, (nb, tq), 0)
    cur = (q0 + lax.broadcasted_iota(jnp.int32, (nb, tq), 1)) // SLC_BLOCK
    forced = (jio == 0) | (jio == cur) | (jio == cur - 1)
    for hf in range(2):
        p1, p2, p3 = _split3(psum[hf])
        ovt = ovt_ref[...]
        imp_t = _dot_nt(ovt, p1) + _dot_nt(ovt, p2) + _dot_nt(ovt, p3)
        imp = imp_t[0:nb, :]
        imp = jnp.where(forced, BIG, imp)
        imp = jnp.where(jio > cur, -BIG, imp)
        cnt = jnp.zeros((nb, tq), F32)
        for i in range(nb):
            row = imp[i:i + 1, :]
            beats = (row > imp) | ((row == imp) & (jio > i))
            cnt = cnt + jnp.where(beats, 1.0, 0.0)
        mem_t = jnp.where(cnt < float(SLC_TOPN), 1.0, 0.0)
        mem_t = jnp.concatenate([mem_t, jnp.zeros((LANES - nb, tq), F32)], axis=0)
        mem_ref[hf] = mem_t.T.astype(BF16)

    for r in range(n_r):
        q = qr_ref[:, r * LANES:(r + 1) * LANES]
        for hf in range(2):
            qm_ref[r * 2 + hf] = jnp.where(halves[hf], q, jnp.zeros_like(q))

    def reset():
        m_ref[...] = jnp.full(m_ref.shape, NEG, F32)
        l_ref[...] = jnp.zeros(l_ref.shape, F32)
        acc_ref[...] = jnp.zeros(acc_ref.shape, F32)

    def flash_tile(k_tile, v_tile, masks):
        v_h = [jnp.where(hm, v_tile, jnp.zeros_like(v_tile)) for hm in halves]
        for r in range(n_r):
            alphas = []
            pv = jnp.zeros((tq, LANES), F32)
            for hf in range(2):
                idx = r * 2 + hf
                s = jnp.where(masks[hf], _dot_nt(qm_ref[idx], k_tile), NEG)
                m_old = m_ref[idx]
                m_new = jnp.maximum(m_old, jnp.max(s, axis=1, keepdims=True))
                p = jnp.where(masks[hf], jnp.exp(s - m_new), 0.0)
                alpha = jnp.exp(m_old - m_new)
                l_ref[idx] = alpha * l_ref[idx] + jnp.sum(p, axis=1, keepdims=True)
                m_ref[idx] = m_new
                pv = pv + _dot(p.astype(BF16), v_h[hf])
                alphas.append(alpha)
            acc_ref[r] = acc_ref[r] * lanes_of(alphas[0], alphas[1]) + pv

    def finish(branch):
        for r in range(n_r):
            l0, l1 = l_ref[r * 2], l_ref[r * 2 + 1]
            inv = lanes_of(jnp.where(l0 > 0.0, 1.0 / l0, 0.0), jnp.where(l1 > 0.0, 1.0 / l1, 0.0))
            osum_ref[r] += gate(branch, r) * (acc_ref[r] * inv)

    kio = lax.broadcasted_iota(jnp.int32, (1, tk), 1)
    sel_io = lax.broadcasted_iota(jnp.int32, (LANES, tk), 0)

    reset()

    def slc_step(kt, carry):
        k0 = pl.multiple_of(kt * tk, tk)
        kpos = k0 + kio
        causal = kpos <= qpos
        expand = jnp.where((k0 + lax.broadcasted_iota(jnp.int32, (LANES, tk), 1)) // SLC_BLOCK == sel_io,
                           1.0, 0.0).astype(BF16)
        masks = [causal & (_dot(mem_ref[hf], expand) > 0.5) for hf in range(2)]
        flash_tile(ksl_ref[pl.ds(k0, tk), :], vsl_ref[pl.ds(k0, tk), :], masks)
        return carry

    lax.fori_loop(0, qt + 1, slc_step, 0)
    finish(1)

    reset()

    def win_step(kt, carry):
        k0 = pl.multiple_of(kt * tk, tk)
        diff = qpos - (k0 + kio)
        mask = (diff >= 0) & (diff < WINDOW)
        flash_tile(kwn_ref[pl.ds(k0, tk), :], vwn_ref[pl.ds(k0, tk), :], [mask, mask])
        return carry

    lax.fori_loop(jnp.maximum(qt - WINDOW // tk, 0), qt + 1, win_step, 0)
    finish(2)

    for r in range(n_r):
        z = za_ref[:, r * LANES:(r + 1) * LANES].astype(F32)
        out_ref[:, r * LANES:(r + 1) * LANES] = (osum_ref[r] * _silu(z)).astype(BF16)


def _attn_call(qn, qr, kcvc, kv, sm, gb, eg, ovt, za, bsz, seq):
    tq = ATTN_Q
    n_qt = seq // tq
    half_w = ATTN_WIDTH // N_PAIR_COLS
    q_spec = pl.BlockSpec((tq, half_w), lambda b, c, t: (b * n_qt + t, c))
    cmp_spec = lambda s: pl.BlockSpec((None, None, LANES, LANES), lambda b, c, t: (s, c, b, 0))
    kv_spec = lambda s: pl.BlockSpec((None, seq, LANES), lambda b, c, t: (s, b, c))
    return pl.pallas_call(
        _attn_kernel,
        grid=(bsz, N_PAIR_COLS, n_qt),
        in_specs=[q_spec, q_spec, cmp_spec(0), cmp_spec(1),
                  kv_spec(0), kv_spec(1), kv_spec(2), kv_spec(3),
                  pl.BlockSpec((tq, SMALL_W), lambda b, c, t: (b * n_qt + t, 0)),
                  pl.BlockSpec((1, SMALL_W), lambda b, c, t: (0, 0)),
                  pl.BlockSpec((None, SMALL_W, N_BRANCH * half_w), lambda b, c, t: (c, 0, 0)),
                  pl.BlockSpec((LANES, LANES), lambda b, c, t: (0, 0)),
                  q_spec],
        out_specs=q_spec,
        out_shape=jax.ShapeDtypeStruct((bsz * seq, ATTN_WIDTH), BF16),
        scratch_shapes=[pltpu.VMEM((2 * Q_PER_KV, tq, LANES), BF16),
                        pltpu.VMEM((2, tq, LANES), BF16),
                        pltpu.VMEM((2 * Q_PER_KV, tq, 1), F32),
                        pltpu.VMEM((2 * Q_PER_KV, tq, 1), F32),
                        pltpu.VMEM((Q_PER_KV, tq, LANES), F32),
                        pltpu.VMEM((Q_PER_KV, tq, LANES), F32)],
        compiler_params=pltpu.CompilerParams(dimension_semantics=("arbitrary", "arbitrary", "arbitrary"),
                                             vmem_limit_bytes=VMEM_LIMIT),
        name="attn",
    )(qn, qr, kcvc, kcvc, kv, kv, kv, kv, sm, gb, eg, ovt, za)


def _ssd_kernel(xbc_ref, sm_ref, cw_ref, cb_ref, dtb_ref, alog_ref, e16_ref, dskip_ref, zs_ref, nw_ref,
                out_ref, ext_ref, state_ref, y_ref):
    L = SSD_CHUNK
    j = pl.program_id(1)
    n_pairs = SSM_HEADS // 2

    @pl.when(j == 0)
    def _():
        ext_ref[0:8, :] = jnp.zeros((8, CONV_CH), F32)
        state_ref[...] = jnp.zeros(state_ref.shape, F32)

    ext_ref[8:8 + L, :] = xbc_ref[...].astype(F32)
    conv = cb_ref[...] + sum(ext_ref[pl.ds(8 - (CONV_WIDTH - 1) + w, L), :] * cw_ref[w:w + 1, :]
                             for w in range(CONV_WIDTH))
    ext_ref[0:8, :] = ext_ref[L:L + 8, :]
    act = _silu(conv)

    sm = sm_ref[...] + dtb_ref[...]
    dt = jnp.maximum(sm, 0.0) + jnp.log1p(jnp.exp(-jnp.abs(sm)))
    a = dt * (-jnp.exp(alog_ref[...]))
    tril = lax.broadcasted_iota(jnp.int32, (L, L), 0) >= lax.broadcasted_iota(jnp.int32, (L, L), 1)
    tri = jnp.where(tril, 1.0, 0.0).astype(BF16)
    a1, a2, a3 = _split3(a)
    a_cs = _dot(tri, a1) + _dot(tri, a2) + _dot(tri, a3)
    a_cs_t = a_cs.T
    e16 = e16_ref[...]
    a_exp = _dot_select(a_cs, e16)
    dt_exp = _dot_select(dt, e16)
    a_last = a_exp[L - 1:L, :]
    x_c = act[:, 0:SSM_WIDTH]
    xdt = x_c * dt_exp
    xw = (xdt * jnp.exp(a_last - a_exp)).astype(BF16)
    ea = jnp.exp(a_exp)
    chunk_decay = jnp.exp(a_last)
    lane = lax.broadcasted_iota(jnp.int32, (1, LANES), 1)
    lo = lane < SSM_HEAD_DIM
    halves = (lo, jnp.logical_not(lo))

    for g in range(SSM_GROUPS):
        bg = act[:, SSM_WIDTH + g * SSM_STATE: SSM_WIDTH + (g + 1) * SSM_STATE]
        cg = act[:, SSM_WIDTH + SSM_GROUPS * SSM_STATE + g * SSM_STATE:
                 SSM_WIDTH + SSM_GROUPS * SSM_STATE + (g + 1) * SSM_STATE].astype(BF16)
        cb = _dot_nt(cg, bg.astype(BF16))
        bg_t = bg.T.astype(BF16)
        for i in range(2 * g, 2 * g + 2):
            sl = slice(i * LANES, (i + 1) * LANES)
            y = jnp.zeros((L, LANES), F32)
            for hh in range(2):
                h = 2 * i + hh
                col = a_cs[:, DT_LANE0 + h:DT_LANE0 + h + 1]
                row = a_cs_t[DT_LANE0 + h:DT_LANE0 + h + 1, :]
                decay = jnp.exp(jnp.where(tril, col - row, NEG))
                xh = jnp.where(halves[hh], xdt[:, sl], 0.0).astype(BF16)
                y = y + _dot((cb * decay).astype(BF16), xh)
            st = state_ref[i]
            y = y + _dot(cg, st.astype(BF16)) * ea[:, sl]
            state_ref[i] = st * chunk_decay[:, sl] + _dot(bg_t, xw[:, sl])
            y_ref[:, sl] = y + dskip_ref[:, sl] * x_c[:, sl]

    y = y_ref[...] * _silu(zs_ref[...].astype(F32))
    y = y * lax.rsqrt(jnp.mean(y * y, axis=-1, keepdims=True) + EPS) * nw_ref[...]
    out_ref[...] = y.astype(BF16)


def _ssd_call(xbc, sm, cw, cb, dtb, alog, e16, dskip, zs, nw, bsz, seq):
    L = SSD_CHUNK
    n_ch = seq // L
    row = lambda w: pl.BlockSpec((L, w), lambda b, j: (b * n_ch + j, 0))
    const = lambda a: pl.BlockSpec(a.shape, lambda b, j: (0,) * a.ndim)
    return pl.pallas_call(
        _ssd_kernel,
        grid=(bsz, n_ch),
        in_specs=[row(CONV_CH), row(SMALL_W), const(cw), const(cb), const(dtb), const(alog), const(e16),
                  const(dskip), row(SSM_WIDTH), const(nw)],
        out_specs=row(SSM_WIDTH),
        out_shape=jax.ShapeDtypeStruct((bsz * seq, SSM_WIDTH), BF16),
        scratch_shapes=[pltpu.VMEM((L + 8, CONV_CH), F32),
                        pltpu.VMEM((SSM_HEADS // 2, SSM_STATE, LANES), F32),
                        pltpu.VMEM((L, SSM_WIDTH), F32)],
        compiler_params=pltpu.CompilerParams(dimension_semantics=("arbitrary", "arbitrary"),
                                             vmem_limit_bytes=VMEM_LIMIT),
        name="ssd",
    )(xbc, sm, cw, cb, dtb, alog, e16, dskip, zs, nw)


def _out_kernel(att_ref, ssm_ref, x_ref, wa_ref, ws_ref, pw_ref, out_ref):
    o = _dot(att_ref[...], wa_ref[...]) + _dot(ssm_ref[...], ws_ref[...])
    o = o * lax.rsqrt(jnp.mean(o * o, axis=-1, keepdims=True) + EPS) * pw_ref[...]
    out_ref[...] = x_ref[...] + o


def _out_call(att, ssm, x2, wa, ws, pw):
    rows = x2.shape[0]
    tm = OUT_ROWS
    row = lambda w: pl.BlockSpec((tm, w), lambda i: (i, 0))
    return pl.pallas_call(
        _out_kernel,
        grid=(rows // tm,),
        in_specs=[row(ATTN_WIDTH), row(SSM_WIDTH), row(D_MODEL), _resident(wa.shape), _resident(ws.shape),
                  _resident(pw.shape)],
        out_specs=row(D_MODEL),
        out_shape=jax.ShapeDtypeStruct((rows, D_MODEL), F32),
        compiler_params=pltpu.CompilerParams(dimension_semantics=("arbitrary",),
                                             vmem_limit_bytes=VMEM_LIMIT),
        name="outproj",
    )(att, ssm, x2, wa, ws, pw)


def _head_slots():
    slots = []
    for c in range(N_PAIR_COLS):
        for r in range(Q_PER_KV):
            slots += [(2 * c) * Q_PER_KV + r, (2 * c + 1) * Q_PER_KV + r]
    return np.asarray(slots)


def _constants(seq):
    slots = _head_slots()
    colperm = (slots[:, None] * HEAD_DIM + np.arange(HEAD_DIM)[None, :]).reshape(-1)
    gate_cols = np.concatenate([slots * N_BRANCH + br for br in range(N_BRANCH)])

    half_w = ATTN_WIDTH // N_PAIR_COLS
    eg = np.zeros((N_PAIR_COLS, SMALL_W, N_BRANCH * half_w), np.float32)
    for c in range(N_PAIR_COLS):
        for br in range(N_BRANCH):
            for r in range(Q_PER_KV):
                for hh in range(2):
                    slot = 2 * (Q_PER_KV * c + r) + hh
                    col0 = br * half_w + r * LANES + hh * HEAD_DIM
                    eg[c, br * ATTN_HEADS + slot, col0:col0 + HEAD_DIM] = 1.0

    nc = (seq - CMP_BLOCK) // CMP_STRIDE + 1
    nb = seq // SLC_BLOCK
    ci = np.arange(nc)[:, None] * CMP_STRIDE
    bj = np.arange(nb)[None, :] * SLC_BLOCK
    overlap = ((ci <= bj + SLC_BLOCK - 1) & (ci + CMP_BLOCK - 1 >= bj)).astype(np.float32)
    ovt = np.zeros((LANES, LANES), np.float32)
    ovt[:nb, :nc] = overlap.T

    e16 = np.zeros((SMALL_W, SSM_WIDTH), np.float32)
    for h in range(SSM_HEADS):
        e16[DT_LANE0 + h, h * SSM_HEAD_DIM:(h + 1) * SSM_HEAD_DIM] = 1.0
    return colperm, gate_cols, eg, ovt, e16


def _rope_tables(seq):
    half = ROPE_DIM // 2
    inv_freq = ROPE_THETA ** (-jnp.arange(half, dtype=F32) * 2.0 / ROPE_DIM)
    ang = jnp.arange(seq).astype(F32)[:, None] * inv_freq[None, :]
    cos, sin = jnp.cos(ang), jnp.sin(ang)
    ones = jnp.ones((seq, HEAD_DIM - ROPE_DIM), F32)
    zeros_h = jnp.zeros((seq, half), F32)
    zeros_r = jnp.zeros((seq, HEAD_DIM - ROPE_DIM), F32)
    cos_h = jnp.concatenate([cos, cos, ones], axis=1)
    sina_h = jnp.concatenate([-sin, zeros_h, zeros_r], axis=1)
    sinb_h = jnp.concatenate([zeros_h, sin, zeros_r], axis=1)
    tile2 = lambda t: jnp.concatenate([t, t], axis=1)
    return tile2(cos_h), tile2(sina_h), tile2(sinb_h)


def _layer(x, w_in, w_out, pre_w, post_w, cmp_pos, cmp_w1, cmp_b1, cmp_w2, cmp_b2,
           gate_b, conv_w, conv_b, dt_bias, a_log, d_skip, ssm_norm_w):
    bsz, seq, _ = x.shape
    colperm, gate_cols, eg, ovt, e16 = _constants(seq)
    cos_t, sin_a, sin_b = _rope_tables(seq)

    o_q = 0
    o_kcm = ATTN_WIDTH
    o_ksl = o_kcm + 2 * KV_WIDTH
    o_g = o_kcm + 6 * KV_WIDTH
    o_za = o_g + ATTN_HEADS * N_BRANCH
    o_zs = o_za + ATTN_WIDTH
    o_xbc = o_zs + SSM_WIDTH
    o_dt = o_xbc + CONV_CH
    scale = HEAD_DIM ** -0.5
    wq = (w_in[:, o_q:o_q + ATTN_WIDTH][:, colperm] * scale).astype(BF16)
    wcm = w_in[:, o_kcm:o_kcm + 2 * KV_WIDTH].astype(BF16)
    wkv = w_in[:, o_ksl:o_ksl + 4 * KV_WIDTH].astype(BF16)
    wza = w_in[:, o_za:o_za + ATTN_WIDTH][:, colperm].astype(BF16)
    wzs = w_in[:, o_zs:o_zs + SSM_WIDTH].astype(BF16)
    wxbc = w_in[:, o_xbc:o_xbc + CONV_CH].astype(BF16)
    n_gate = ATTN_HEADS * N_BRANCH
    wsm = jnp.concatenate([w_in[:, o_g:o_g + n_gate][:, gate_cols], w_in[:, o_dt:o_dt + SSM_HEADS],
                           jnp.zeros((D_MODEL, SMALL_W - n_gate - SSM_HEADS), F32)], axis=1).astype(BF16)
    pad_small = lambda v, at: jnp.zeros((1, SMALL_W), F32).at[0, at:at + v.shape[0]].set(v)
    gb = pad_small(gate_b[gate_cols], 0)
    dtb = pad_small(dt_bias, DT_LANE0)
    alog = pad_small(a_log, DT_LANE0)

    x2 = x.reshape(bsz * seq, D_MODEL)
    qn, qr, cm, kv, za, zs, xbc, sm = _proj_call(x2, pre_w[None, :], wq, wcm, wkv, wza, wzs, wxbc, wsm,
                                                 cos_t, sin_a, sin_b, seq)

    w1r = cmp_w1.reshape(2, 2, CMP_STRIDE, HEAD_DIM, CMP_HIDDEN)
    wa, wb = w1r[:, 0], w1r[:, 1]
    z1 = jnp.zeros_like(wa)
    bd1 = jnp.concatenate([jnp.concatenate([wa, z1, wb, z1], axis=-1),
                           jnp.concatenate([z1, wa, z1, wb], axis=-1)], axis=-2).astype(BF16)
    z2 = jnp.zeros_like(cmp_w2)
    bd2 = jnp.concatenate([jnp.concatenate([cmp_w2, z2], axis=-1),
                           jnp.concatenate([z2, cmp_w2], axis=-1)], axis=-2).astype(BF16)
    pos8 = jnp.broadcast_to(cmp_pos.reshape(2, 1, CMP_BLOCK * HEAD_DIM), (2, 8, CMP_BLOCK * HEAD_DIM))
    b2 = jnp.concatenate([cmp_b2, cmp_b2], axis=-1)[:, None, :]
    cm2 = cm.reshape(bsz * seq // CMP_STRIDE, CMP_STRIDE * 2 * KV_WIDTH)
    kcvc = _compress_call(cm2, bd1, cmp_w1, pos8, cmp_b1[:, None, :], bd2, b2)

    att = _attn_call(qn, qr, kcvc, kv, sm, gb, jnp.asarray(eg, BF16), jnp.asarray(ovt, BF16), za, bsz, seq)

    dskip = jnp.repeat(d_skip, SSM_HEAD_DIM)[None, :]
    ssm = _ssd_call(xbc, sm, conv_w, conv_b[None, :], dtb, alog, jnp.asarray(e16, BF16), dskip, zs,
                    ssm_norm_w[None, :], bsz, seq)

    wo_a = w_out[0:ATTN_WIDTH][colperm, :].astype(BF16)
    wo_s = w_out[ATTN_WIDTH:].astype(BF16)
    out = _out_call(att, ssm, x2, wo_a, wo_s, post_w[None, :])
    return out.reshape(bsz, seq, D_MODEL)


def kernel(x, w_in, w_out, pre_norm_w, post_norm_w, cmp_pos, cmp_w1, cmp_b1, cmp_w2, cmp_b2, gate_b, conv_w,
           conv_b, dt_bias, a_log, d_skip, ssm_norm_w):
    for l in range(w_in.shape[0]):
        x = _layer(x, w_in[l], w_out[l], pre_norm_w[l], post_norm_w[l], cmp_pos[l], cmp_w1[l], cmp_b1[l],
                   cmp_w2[l], cmp_b2[l], gate_b[l], conv_w[l], conv_b[l], dt_bias[l], a_log[l], d_skip[l],
                   ssm_norm_w[l])
    return x
```

```python
import numpy as np
import jax
import jax.numpy as jnp
from jax import lax
from jax.experimental import pallas as pl
from jax.experimental.pallas import tpu as pltpu

F32 = jnp.float32
BF16 = jnp.bfloat16

D_MODEL = 1024
ATTN_HEADS = 16
HEAD_DIM = 64
ATTN_WIDTH = ATTN_HEADS * HEAD_DIM
KV_HEADS = 4
Q_PER_KV = ATTN_HEADS // KV_HEADS
KV_WIDTH = KV_HEADS * HEAD_DIM
ROPE_DIM = HEAD_DIM // 4
ROPE_HALF = ROPE_DIM // 2
ROPE_THETA = 500000.0
CMP_BLOCK = 32
CMP_STRIDE = 16
CMP_HIDDEN = 256
SLC_BLOCK = 64
SLC_TOPN = 16
WINDOW = 512
N_BRANCH = 3
SSM_HEADS = 16
SSM_HEAD_DIM = 64
SSM_WIDTH = SSM_HEADS * SSM_HEAD_DIM
SSM_GROUPS = 4
SSM_STATE = 128
CONV_WIDTH = 4
CONV_CH = SSM_WIDTH + 2 * SSM_GROUPS * SSM_STATE
MIX_WIDTH = ATTN_WIDTH + SSM_WIDTH
EPS = 1e-6
NEG = -1e30
BIG = 1e30
M_FLOOR = -1e29

LANES = 128
SUBLANES = 8
N_PAIR_COLS = KV_HEADS // 2
HALF_W = ATTN_WIDTH // N_PAIR_COLS
SMALL_W = LANES
DT_LANE0 = ATTN_HEADS * N_BRANCH
VMEM_LIMIT = 56 * 1024 * 1024

PROJ_ROWS = 512
ATTN_Q = 256
ATTN_K = 256
SSD_CHUNK = 256
OUT_ROWS = 512


def _dot(a, b):
    return jnp.dot(a, b, preferred_element_type=F32)


def _dot_nt(a, b):
    return lax.dot_general(a, b, (((1,), (1,)), ((), ())), preferred_element_type=F32)


def _split3(x):
    x1 = x.astype(BF16)
    r1 = x - x1.astype(F32)
    x2 = r1.astype(BF16)
    x3 = (r1 - x2.astype(F32)).astype(BF16)
    return x1, x2, x3


def _dot_select(x, sel):
    x1, x2, x3 = _split3(x)
    return _dot(x1, sel) + _dot(x2, sel) + _dot(x3, sel)


def _select_dot(sel, x):
    x1, x2, x3 = _split3(x)
    return _dot(sel, x1) + _dot(sel, x2) + _dot(sel, x3)


def _sigmoid(x):
    return 1.0 / (1.0 + jnp.exp(-x))


def _silu(x):
    return x * _sigmoid(x)


def _rope_chunk(c, cos_t, sin_a, sin_b):
    return c * cos_t + pltpu.roll(c, LANES - ROPE_HALF, 1) * sin_a + pltpu.roll(c, ROPE_HALF, 1) * sin_b


def _proj_kernel(x_ref, pre_w_ref, wqt_ref, wcm_ref, wk_ref, wvt_ref, wza_ref, wzs_ref, wxbc_ref, wsm_ref,
                 wsmt_ref, cos_ref, sina_ref, sinb_ref, cos8_ref, sin8_ref,
                 qnt_ref, qrt_ref, cm_ref, kk_ref, vvt_ref, za_ref, zs_ref, xbc_ref, sm_ref, smt_ref):
    x = x_ref[...]
    h = x * lax.rsqrt(jnp.mean(x * x, axis=-1, keepdims=True) + EPS) * pre_w_ref[...]
    h = h.astype(BF16)

    qt = _dot_nt(wqt_ref[...], h)
    qt_bf = qt.astype(BF16)
    qnt_ref[...] = qt_bf
    qrt_ref[...] = qt_bf
    cos8, sin8 = cos8_ref[...], sin8_ref[...]
    for hs in range(ATTN_HEADS):
        base = hs * HEAD_DIM
        t1, t2 = qt[base:base + ROPE_HALF], qt[base + ROPE_HALF:base + ROPE_DIM]
        rot = jnp.concatenate([t1 * cos8 - t2 * sin8, t2 * cos8 + t1 * sin8], axis=0)
        qrt_ref[base:base + ROPE_DIM, :] = rot.astype(BF16)

    cm_ref[...] = _dot(h, wcm_ref[...]).astype(BF16)

    cos_t, sin_a, sin_b = cos_ref[...], sina_ref[...], sinb_ref[...]
    kk = _dot(h, wk_ref[...])
    for t in range(2):
        for k in range(N_PAIR_COLS):
            c = kk[:, t * KV_WIDTH + k * LANES: t * KV_WIDTH + (k + 1) * LANES]
            kk_ref[t, :, k * LANES:(k + 1) * LANES] = _rope_chunk(c, cos_t, sin_a, sin_b).astype(BF16)

    vvt = _dot_nt(wvt_ref[...], h).astype(BF16)
    for t in range(2):
        for j in range(PROJ_ROWS // ATTN_K):
            vvt_ref[t, j] = vvt[t * KV_WIDTH:(t + 1) * KV_WIDTH, j * ATTN_K:(j + 1) * ATTN_K]

    za_ref[...] = _dot(h, wza_ref[...]).astype(BF16)
    zs_ref[...] = _dot(h, wzs_ref[...]).astype(BF16)
    xbc_ref[...] = _dot(h, wxbc_ref[...]).astype(BF16)
    sm_ref[...] = _dot(h, wsm_ref[...])
    smt_ref[...] = _dot_nt(wsmt_ref[...], h)


def _resident(shape):
    nd = len(shape)
    return pl.BlockSpec(shape, lambda *_: (0,) * nd, pipeline_mode=pl.Buffered(1))


def _proj_call(x2, pre_w, wqt, wcm, wk, wvt, wza, wzs, wxbc, wsm, wsmt, tabs, seq):
    rows = x2.shape[0]
    tm = PROJ_ROWS
    n_seq_tiles = seq // tm
    k_per_tile = tm // ATTN_K
    row_spec = lambda w: pl.BlockSpec((tm, w), lambda i: (i, 0))
    col_spec = lambda h: pl.BlockSpec((h, tm), lambda i: (0, i))
    tab_spec = pl.BlockSpec((tm, LANES), lambda i: (i % n_seq_tiles, 0))
    tab8_spec = pl.BlockSpec((ROPE_HALF, tm), lambda i: (0, i % n_seq_tiles))
    weights = (pre_w, wqt, wcm, wk, wvt, wza, wzs, wxbc, wsm, wsmt)
    return pl.pallas_call(
        _proj_kernel,
        grid=(rows // tm,),
        in_specs=[row_spec(D_MODEL)] + [_resident(w.shape) for w in weights]
                 + [tab_spec, tab_spec, tab_spec, tab8_spec, tab8_spec],
        out_specs=[col_spec(ATTN_WIDTH), col_spec(ATTN_WIDTH), row_spec(2 * KV_WIDTH),
                   pl.BlockSpec((2, tm, KV_WIDTH), lambda i: (0, i, 0)),
                   pl.BlockSpec((2, k_per_tile, KV_WIDTH, ATTN_K), lambda i: (0, i, 0, 0)),
                   row_spec(ATTN_WIDTH), row_spec(SSM_WIDTH), row_spec(CONV_CH), row_spec(SMALL_W),
                   col_spec(SMALL_W)],
        out_shape=[jax.ShapeDtypeStruct((ATTN_WIDTH, rows), BF16),
                   jax.ShapeDtypeStruct((ATTN_WIDTH, rows), BF16),
                   jax.ShapeDtypeStruct((rows, 2 * KV_WIDTH), BF16),
                   jax.ShapeDtypeStruct((2, rows, KV_WIDTH), BF16),
                   jax.ShapeDtypeStruct((2, rows // ATTN_K, KV_WIDTH, ATTN_K), BF16),
                   jax.ShapeDtypeStruct((rows, ATTN_WIDTH), BF16),
                   jax.ShapeDtypeStruct((rows, SSM_WIDTH), BF16),
                   jax.ShapeDtypeStruct((rows, CONV_CH), BF16),
                   jax.ShapeDtypeStruct((rows, SMALL_W), F32),
                   jax.ShapeDtypeStruct((SMALL_W, rows), F32)],
        compiler_params=pltpu.CompilerParams(dimension_semantics=("arbitrary",),
                                             vmem_limit_bytes=VMEM_LIMIT),
        name="proj",
    )(x2, *weights, *tabs)


def _compress_kernel(x_ref, bd1_ref, w1_ref, pos_ref, b1_ref, bd2_ref, b2_ref, bd2t_ref, b2t_ref,
                     out_ref, outt_ref, acc_ref):
    t = pl.program_id(2)
    n_rows = x_ref.shape[0]
    hid2 = 2 * CMP_HIDDEN

    @pl.when(t == 0)
    def _():
        acc_ref[...] = jnp.zeros_like(acc_ref)

    acc_ref[0:n_rows, :] += _dot(x_ref[...], bd1_ref[...])

    @pl.when(t == CMP_STRIDE - 1)
    def _():
        first = acc_ref[0:n_rows, 0:hid2]
        second = acc_ref[pl.ds(1, n_rows), hid2:2 * hid2]
        posterm = _dot(pos_ref[...].astype(BF16), w1_ref[...].astype(BF16))[0:1, :] + b1_ref[...]
        hcat = first + second + jnp.concatenate([posterm, posterm], axis=1)
        act = _silu(hcat).astype(BF16)
        out_ref[...] = _dot(act, bd2_ref[...]) + b2_ref[...]
        outt_ref[...] = _dot_nt(bd2t_ref[...], act) + b2t_ref[...]


def _compress_call(cm2, bd1, w1, pos8, b1, bd2, b2, bd2t, b2t):
    n_rows = cm2.shape[0]
    col_blocks_per_t = 2 * N_PAIR_COLS
    per_s = lambda *tail: pl.BlockSpec((None,) + tail, lambda s, c, t: (s,) + (0,) * len(tail))
    return pl.pallas_call(
        _compress_kernel,
        grid=(2, N_PAIR_COLS, CMP_STRIDE),
        in_specs=[pl.BlockSpec((n_rows, LANES), lambda s, c, t: (0, t * col_blocks_per_t + s * N_PAIR_COLS + c)),
                  pl.BlockSpec((None, None, LANES, 4 * CMP_HIDDEN), lambda s, c, t: (s, t, 0, 0)),
                  per_s(CMP_BLOCK * HEAD_DIM, CMP_HIDDEN),
                  per_s(SUBLANES, CMP_BLOCK * HEAD_DIM),
                  per_s(1, CMP_HIDDEN),
                  per_s(2 * CMP_HIDDEN, LANES),
                  per_s(1, LANES),
                  per_s(LANES, 2 * CMP_HIDDEN),
                  per_s(LANES, 1)],
        out_specs=[pl.BlockSpec((None, None, n_rows, LANES), lambda s, c, t: (s, c, 0, 0)),
                   pl.BlockSpec((None, None, LANES, n_rows), lambda s, c, t: (s, c, 0, 0))],
        out_shape=[jax.ShapeDtypeStruct((2, N_PAIR_COLS, n_rows, LANES), F32),
                   jax.ShapeDtypeStruct((2, N_PAIR_COLS, LANES, n_rows), F32)],
        scratch_shapes=[pltpu.VMEM((n_rows + SUBLANES, 4 * CMP_HIDDEN), F32)],
        compiler_params=pltpu.CompilerParams(dimension_semantics=("arbitrary", "arbitrary", "arbitrary"),
                                             vmem_limit_bytes=VMEM_LIMIT),
        name="compress",
    )(cm2, bd1, w1, pos8, b1, bd2, b2, bd2t, b2t)


def _attn_kernel(qnt_ref, qrt_ref, kc_ref, vct_ref, ksl_ref, kwn_ref, vslt_ref, vwnt_ref, smt_ref, gb_ref,
                 egt_ref, ovt_ref, tri_ref, za_ref, out_ref,
                 qm_ref, bias_ref, m_ref, l_ref, acc_ref, osum_ref, s_ref):
    tq, tk = ATTN_Q, ATTN_K
    n_r = Q_PER_KV
    qt = pl.program_id(2)
    q0 = qt * tq
    row = lax.broadcasted_iota(jnp.int32, (LANES, 1), 0)
    halves = (row < HEAD_DIM, row >= HEAD_DIM)
    qpos = q0 + lax.broadcasted_iota(jnp.int32, (1, tq), 1)

    gates = _sigmoid(smt_ref[...] + gb_ref[...])
    g1, g2, g3 = _split3(gates)

    def gate(br, r):
        sel = egt_ref[(br * n_r + r) * LANES:(br * n_r + r + 1) * LANES, :]
        return _dot(sel, g1) + _dot(sel, g2) + _dot(sel, g3)

    kc = kc_ref[...].astype(BF16)
    vct = vct_ref[...].astype(BF16)
    cpos = row * CMP_STRIDE + (CMP_BLOCK - 1)
    cbias = jnp.where(cpos <= qpos, 0.0, NEG)
    psum = [jnp.zeros((LANES, tq), F32), jnp.zeros((LANES, tq), F32)]
    for r in range(n_r):
        q = qnt_ref[r * LANES:(r + 1) * LANES, :]
        for hf in range(2):
            qm = jnp.where(halves[hf], q, jnp.zeros_like(q))
            s = _dot(kc, qm) + cbias
            m = jnp.maximum(jnp.max(s, axis=0, keepdims=True), M_FLOOR)
            p = jnp.exp(s - m)
            l = jnp.sum(p, axis=0, keepdims=True)
            p = p * jnp.where(l > 0.0, 1.0 / l, 0.0)
            psum[hf] = psum[hf] + p
            hs = slice(hf * HEAD_DIM, (hf + 1) * HEAD_DIM)
            acc_ref[r, hs, :] = _dot(vct[hs, :], p.astype(BF16))
        osum_ref[r] = gate(0, r) * acc_ref[r]

    nb = ksl_ref.shape[0] // SLC_BLOCK
    jio = lax.broadcasted_iota(jnp.int32, (nb, tq), 0)
    cur = (q0 + lax.broadcasted_iota(jnp.int32, (nb, tq), 1)) // SLC_BLOCK
    forced = (jio == 0) | (jio == cur) | (jio == cur - 1)
    for hf in range(2):
        imp = _select_dot(ovt_ref[...], psum[hf])[0:nb, :]
        imp = jnp.where(forced, BIG, imp)
        imp = jnp.where(jio > cur, -BIG, imp)
        cnt = jnp.zeros((nb, tq), F32)
        for i in range(nb):
            other = imp[i:i + 1, :]
            beats = (other > imp) | ((other == imp) & (jio > i))
            cnt = cnt + jnp.where(beats, 1.0, 0.0)
        bias_ref[hf] = jnp.where(cnt < float(SLC_TOPN), 0.0, NEG)

    for r in range(n_r):
        q = qrt_ref[r * LANES:(r + 1) * LANES, :]
        for hf in range(2):
            qm_ref[r * 2 + hf] = jnp.where(halves[hf], q, jnp.zeros_like(q))

    def reset():
        m_ref[...] = jnp.full(m_ref.shape, M_FLOOR, F32)
        l_ref[...] = jnp.zeros(l_ref.shape, F32)
        acc_ref[...] = jnp.zeros(acc_ref.shape, F32)

    def flash_tile(k_tile, vt_tile, bias_fn):
        def scores(idx):
            s = bias_fn(_dot(k_tile, qm_ref[idx]), idx % 2)
            s_ref[idx] = s
            return jnp.max(s, axis=0, keepdims=True)

        col_max = scores(0)
        for idx in range(2 * n_r):
            next_max = scores(idx + 1) if idx + 1 < 2 * n_r else None
            r, hf = divmod(idx, 2)
            m_old = m_ref[idx:idx + 1, :]
            m_new = jnp.maximum(m_old, col_max)
            p = jnp.exp(s_ref[idx] - m_new)
            alpha = jnp.exp(m_old - m_new)
            l_ref[idx:idx + 1, :] = alpha * l_ref[idx:idx + 1, :] + jnp.sum(p, axis=0, keepdims=True)
            m_ref[idx:idx + 1, :] = m_new
            hs = slice(hf * HEAD_DIM, (hf + 1) * HEAD_DIM)
            acc_ref[r, hs, :] = acc_ref[r, hs, :] * alpha + _dot(vt_tile[hs, :], p.astype(BF16))
            col_max = next_max

    def finish(branch):
        for r in range(n_r):
            g = gate(branch, r)
            for hf in range(2):
                idx = r * 2 + hf
                l = l_ref[idx:idx + 1, :]
                inv = jnp.where(l > 0.0, 1.0 / l, 0.0)
                hs = slice(hf * HEAD_DIM, (hf + 1) * HEAD_DIM)
                osum_ref[r, hs, :] += g[hs, :] * (acc_ref[r, hs, :] * inv)

    tri_causal = lambda s: s + tri_ref[0]
    tri_tail = lambda s: s + tri_ref[1]

    reset()
    blocks_per_tile = tk // SLC_BLOCK

    def slc_bias(kt):
        def fn(s, hf):
            parts = []
            for jj in range(blocks_per_tile):
                brow = bias_ref[hf, pl.ds(kt * blocks_per_tile + jj, 1), :]
                parts.append(s[jj * SLC_BLOCK:(jj + 1) * SLC_BLOCK, :] + brow)
            return jnp.concatenate(parts, axis=0)
        return fn

    def slc_step(kt, carry):
        fn = slc_bias(kt)
        flash_tile(ksl_ref[pl.ds(pl.multiple_of(kt * tk, tk), tk), :], vslt_ref[kt], fn)
        return carry

    lax.fori_loop(0, qt, slc_step, 0)
    diag_fn = slc_bias(qt)
    flash_tile(ksl_ref[pl.ds(pl.multiple_of(q0, tk), tk), :], vslt_ref[qt],
               lambda s, hf: tri_causal(diag_fn(s, hf)))
    finish(1)

    reset()
    tiles_back = WINDOW // tk

    @pl.when(qt >= tiles_back)
    def _():
        kt = qt - tiles_back
        flash_tile(kwn_ref[pl.ds(pl.multiple_of(kt * tk, tk), tk), :], vwnt_ref[kt],
                   lambda s, hf: tri_tail(s))

    def win_step(kt, carry):
        flash_tile(kwn_ref[pl.ds(pl.multiple_of(kt * tk, tk), tk), :], vwnt_ref[kt], lambda s, hf: s)
        return carry

    lax.fori_loop(jnp.maximum(qt - tiles_back + 1, 0), qt, win_step, 0)
    flash_tile(kwn_ref[pl.ds(pl.multiple_of(q0, tk), tk), :], vwnt_ref[qt], lambda s, hf: tri_causal(s))
    finish(2)

    for r in range(n_r):
        z = za_ref[:, r * LANES:(r + 1) * LANES].astype(F32)
        out_ref[:, r * LANES:(r + 1) * LANES] = (osum_ref[r].T * _silu(z)).astype(BF16)


def _attn_call(qnt, qrt, kc_std, kc_t, kk, vvt, smt, gb, egt, ovt, tri, za, bsz, seq):
    tq, tk = ATTN_Q, ATTN_K
    n_qt = seq // tq
    n_kt = seq // tk
    qt_spec = pl.BlockSpec((HALF_W, tq), lambda b, c, t: (c, b * n_qt + t))
    out_spec = pl.BlockSpec((tq, HALF_W), lambda b, c, t: (b * n_qt + t, c))
    k_spec = lambda s: pl.BlockSpec((None, seq, LANES), lambda b, c, t: (s, b, c))
    vt_spec = lambda s: pl.BlockSpec((None, n_kt, LANES, tk), lambda b, c, t: (s, b, c, 0))
    const = lambda a: pl.BlockSpec(a.shape, lambda b, c, t: (0,) * a.ndim)
    return pl.pallas_call(
        _attn_kernel,
        grid=(bsz, N_PAIR_COLS, n_qt),
        in_specs=[qt_spec, qt_spec,
                  pl.BlockSpec((None, None, LANES, LANES), lambda b, c, t: (0, c, b, 0)),
                  pl.BlockSpec((None, None, LANES, LANES), lambda b, c, t: (1, c, 0, b)),
                  k_spec(0), k_spec(1), vt_spec(0), vt_spec(1),
                  pl.BlockSpec((SMALL_W, tq), lambda b, c, t: (0, b * n_qt + t)),
                  const(gb),
                  pl.BlockSpec((None, N_BRANCH * HALF_W, SMALL_W), lambda b, c, t: (c, 0, 0)),
                  const(ovt), const(tri), out_spec],
        out_specs=out_spec,
        out_shape=jax.ShapeDtypeStruct((bsz * seq, ATTN_WIDTH), BF16),
        scratch_shapes=[pltpu.VMEM((2 * Q_PER_KV, LANES, tq), BF16),
                        pltpu.VMEM((2, seq // SLC_BLOCK, tq), F32),
                        pltpu.VMEM((2 * Q_PER_KV, tq), F32),
                        pltpu.VMEM((2 * Q_PER_KV, tq), F32),
                        pltpu.VMEM((Q_PER_KV, LANES, tq), F32),
                        pltpu.VMEM((Q_PER_KV, LANES, tq), F32),
                        pltpu.VMEM((2 * Q_PER_KV, tk, tq), F32)],
        compiler_params=pltpu.CompilerParams(dimension_semantics=("arbitrary", "arbitrary", "arbitrary"),
                                             vmem_limit_bytes=VMEM_LIMIT),
        name="attn",
    )(qnt, qrt, kc_std, kc_t, kk, kk, vvt, vvt, smt, gb, egt, ovt, tri, za)


def _ssd_kernel(xbc_ref, sm_ref, cw_ref, cb_ref, dtb_ref, alog_ref, e16_ref, dskip_ref, zs_ref, nw_ref,
                out_ref, ext_ref, state_ref, y_ref):
    L = SSD_CHUNK
    j = pl.program_id(1)

    @pl.when(j == 0)
    def _():
        ext_ref[0:8, :] = jnp.zeros((8, CONV_CH), F32)
        state_ref[...] = jnp.zeros(state_ref.shape, F32)

    ext_ref[8:8 + L, :] = xbc_ref[...].astype(F32)
    conv = cb_ref[...] + sum(ext_ref[pl.ds(8 - (CONV_WIDTH - 1) + w, L), :] * cw_ref[w:w + 1, :]
                             for w in range(CONV_WIDTH))
    ext_ref[0:8, :] = ext_ref[L:L + 8, :]
    act = _silu(conv)

    sm = sm_ref[...] + dtb_ref[...]
    dt = jnp.maximum(sm, 0.0) + jnp.log1p(jnp.exp(-jnp.abs(sm)))
    a = dt * (-jnp.exp(alog_ref[...]))
    tril = lax.broadcasted_iota(jnp.int32, (L, L), 0) >= lax.broadcasted_iota(jnp.int32, (L, L), 1)
    a_cs = _select_dot(jnp.where(tril, 1.0, 0.0).astype(BF16), a)
    a_cs_t = a_cs.T
    e16 = e16_ref[...]
    a_exp = _dot_select(a_cs, e16)
    dt_exp = _dot_select(dt, e16)
    a_last = a_exp[L - 1:L, :]
    x_c = act[:, 0:SSM_WIDTH]
    xdt = x_c * dt_exp
    xw = (xdt * jnp.exp(a_last - a_exp)).astype(BF16)
    ea = jnp.exp(a_exp)
    chunk_decay = jnp.exp(a_last)
    lane = lax.broadcasted_iota(jnp.int32, (1, LANES), 1)
    lo = lane < SSM_HEAD_DIM
    halves = (lo, jnp.logical_not(lo))

    for g in range(SSM_GROUPS):
        bg = act[:, SSM_WIDTH + g * SSM_STATE: SSM_WIDTH + (g + 1) * SSM_STATE]
        cg = act[:, SSM_WIDTH + SSM_GROUPS * SSM_STATE + g * SSM_STATE:
                 SSM_WIDTH + SSM_GROUPS * SSM_STATE + (g + 1) * SSM_STATE].astype(BF16)
        cb = _dot_nt(cg, bg.astype(BF16))
        bg_t = bg.T.astype(BF16)
        for i in range(2 * g, 2 * g + 2):
            sl = slice(i * LANES, (i + 1) * LANES)
            y = jnp.zeros((L, LANES), F32)
            for hh in range(2):
                h = 2 * i + hh
                col = a_cs[:, DT_LANE0 + h:DT_LANE0 + h + 1]
                row = a_cs_t[DT_LANE0 + h:DT_LANE0 + h + 1, :]
                decay = jnp.exp(jnp.where(tril, col - row, NEG))
                xh = jnp.where(halves[hh], xdt[:, sl], 0.0).astype(BF16)
                y = y + _dot((cb * decay).astype(BF16), xh)
            st = state_ref[i]
            y = y + _dot(cg, st.astype(BF16)) * ea[:, sl]
            state_ref[i] = st * chunk_decay[:, sl] + _dot(bg_t, xw[:, sl])
            y_ref[:, sl] = y + dskip_ref[:, sl] * x_c[:, sl]

    y = y_ref[...] * _silu(zs_ref[...].astype(F32))
    y = y * lax.rsqrt(jnp.mean(y * y, axis=-1, keepdims=True) + EPS) * nw_ref[...]
    out_ref[...] = y.astype(BF16)


def _ssd_call(xbc, sm, cw, cb, dtb, alog, e16, dskip, zs, nw, bsz, seq):
    L = SSD_CHUNK
    n_ch = seq // L
    row = lambda w: pl.BlockSpec((L, w), lambda b, j: (b * n_ch + j, 0))
    const = lambda a: pl.BlockSpec(a.shape, lambda b, j: (0,) * a.ndim)
    return pl.pallas_call(
        _ssd_kernel,
        grid=(bsz, n_ch),
        in_specs=[row(CONV_CH), row(SMALL_W), const(cw), const(cb), const(dtb), const(alog), const(e16),
                  const(dskip), row(SSM_WIDTH), const(nw)],
        out_specs=row(SSM_WIDTH),
        out_shape=jax.ShapeDtypeStruct((bsz * seq, SSM_WIDTH), BF16),
        scratch_shapes=[pltpu.VMEM((L + 8, CONV_CH), F32),
                        pltpu.VMEM((SSM_HEADS // 2, SSM_STATE, LANES), F32),
                        pltpu.VMEM((L, SSM_WIDTH), F32)],
        compiler_params=pltpu.CompilerParams(dimension_semantics=("arbitrary", "arbitrary"),
                                             vmem_limit_bytes=VMEM_LIMIT),
        name="ssd",
    )(xbc, sm, cw, cb, dtb, alog, e16, dskip, zs, nw)


def _out_kernel(att_ref, ssm_ref, x_ref, wa_ref, ws_ref, pw_ref, out_ref):
    o = _dot(att_ref[...], wa_ref[...]) + _dot(ssm_ref[...], ws_ref[...])
    o = o * lax.rsqrt(jnp.mean(o * o, axis=-1, keepdims=True) + EPS) * pw_ref[...]
    out_ref[...] = x_ref[...] + o


def _out_call(att, ssm, x2, wa, ws, pw):
    rows = x2.shape[0]
    tm = OUT_ROWS
    row = lambda w: pl.BlockSpec((tm, w), lambda i: (i, 0))
    return pl.pallas_call(
        _out_kernel,
        grid=(rows // tm,),
        in_specs=[row(ATTN_WIDTH), row(SSM_WIDTH), row(D_MODEL), _resident(wa.shape), _resident(ws.shape),
                  _resident(pw.shape)],
        out_specs=row(D_MODEL),
        out_shape=jax.ShapeDtypeStruct((rows, D_MODEL), F32),
        compiler_params=pltpu.CompilerParams(dimension_semantics=("arbitrary",),
                                             vmem_limit_bytes=VMEM_LIMIT),
        name="outproj",
    )(att, ssm, x2, wa, ws, pw)


def _head_slots():
    slots = []
    for c in range(N_PAIR_COLS):
        for r in range(Q_PER_KV):
            slots += [(2 * c) * Q_PER_KV + r, (2 * c + 1) * Q_PER_KV + r]
    return np.asarray(slots)


def _constants(seq):
    slots = _head_slots()
    colperm = (slots[:, None] * HEAD_DIM + np.arange(HEAD_DIM)[None, :]).reshape(-1)
    gate_cols = np.concatenate([slots * N_BRANCH + br for br in range(N_BRANCH)])

    egt = np.zeros((N_PAIR_COLS, N_BRANCH * HALF_W, SMALL_W), np.float32)
    for c in range(N_PAIR_COLS):
        for br in range(N_BRANCH):
            for r in range(Q_PER_KV):
                for hh in range(2):
                    slot = 2 * (Q_PER_KV * c + r) + hh
                    row0 = br * HALF_W + r * LANES + hh * HEAD_DIM
                    egt[c, row0:row0 + HEAD_DIM, br * ATTN_HEADS + slot] = 1.0

    nc = (seq - CMP_BLOCK) // CMP_STRIDE + 1
    nb = seq // SLC_BLOCK
    ci = np.arange(nc)[:, None] * CMP_STRIDE
    bj = np.arange(nb)[None, :] * SLC_BLOCK
    overlap = ((ci <= bj + SLC_BLOCK - 1) & (ci + CMP_BLOCK - 1 >= bj)).astype(np.float32)
    ovt = np.zeros((LANES, LANES), np.float32)
    ovt[:nb, :nc] = overlap.T

    e16 = np.zeros((SMALL_W, SSM_WIDTH), np.float32)
    for h in range(SSM_HEADS):
        e16[DT_LANE0 + h, h * SSM_HEAD_DIM:(h + 1) * SSM_HEAD_DIM] = 1.0

    key = np.arange(ATTN_K)[:, None]
    qry = np.arange(ATTN_Q)[None, :]
    tri = np.stack([np.where(key <= qry, 0.0, NEG), np.where(key > qry, 0.0, NEG)]).astype(np.float32)
    return colperm, gate_cols, egt, ovt, e16, tri


def _rope_tables(seq):
    inv_freq = ROPE_THETA ** (-jnp.arange(ROPE_HALF, dtype=F32) * 2.0 / ROPE_DIM)
    ang = jnp.arange(seq).astype(F32)[:, None] * inv_freq[None, :]
    cos, sin = jnp.cos(ang), jnp.sin(ang)
    ones = jnp.ones((seq, HEAD_DIM - ROPE_DIM), F32)
    zeros_h = jnp.zeros((seq, ROPE_HALF), F32)
    zeros_r = jnp.zeros((seq, HEAD_DIM - ROPE_DIM), F32)
    cos_h = jnp.concatenate([cos, cos, ones], axis=1)
    sina_h = jnp.concatenate([-sin, zeros_h, zeros_r], axis=1)
    sinb_h = jnp.concatenate([zeros_h, sin, zeros_r], axis=1)
    tile2 = lambda t: jnp.concatenate([t, t], axis=1)
    return tile2(cos_h), tile2(sina_h), tile2(sinb_h), cos.T, sin.T


def _layer(x, w_in, w_out, pre_w, post_w, cmp_pos, cmp_w1, cmp_b1, cmp_w2, cmp_b2,
           gate_b, conv_w, conv_b, dt_bias, a_log, d_skip, ssm_norm_w):
    bsz, seq, _ = x.shape
    colperm, gate_cols, egt, ovt, e16, tri = _constants(seq)
    tabs = _rope_tables(seq)

    o_q = 0
    o_kcm = ATTN_WIDTH
    o_ksl = o_kcm + 2 * KV_WIDTH
    o_vsl = o_ksl + KV_WIDTH
    o_kwn = o_vsl + KV_WIDTH
    o_vwn = o_kwn + KV_WIDTH
    o_g = o_vwn + KV_WIDTH
    o_za = o_g + ATTN_HEADS * N_BRANCH
    o_zs = o_za + ATTN_WIDTH
    o_xbc = o_zs + SSM_WIDTH
    o_dt = o_xbc + CONV_CH
    cols = lambda o, n: w_in[:, o:o + n]
    scale = HEAD_DIM ** -0.5
    wqt = (cols(o_q, ATTN_WIDTH)[:, colperm] * scale).T.astype(BF16)
    wcm = cols(o_kcm, 2 * KV_WIDTH).astype(BF16)
    wk = jnp.concatenate([cols(o_ksl, KV_WIDTH), cols(o_kwn, KV_WIDTH)], axis=1).astype(BF16)
    wvt = jnp.concatenate([cols(o_vsl, KV_WIDTH), cols(o_vwn, KV_WIDTH)], axis=1).T.astype(BF16)
    wza = cols(o_za, ATTN_WIDTH)[:, colperm].astype(BF16)
    wzs = cols(o_zs, SSM_WIDTH).astype(BF16)
    wxbc = cols(o_xbc, CONV_CH).astype(BF16)
    n_gate = ATTN_HEADS * N_BRANCH
    wsm = jnp.concatenate([cols(o_g, n_gate)[:, gate_cols], cols(o_dt, SSM_HEADS),
                           jnp.zeros((D_MODEL, SMALL_W - n_gate - SSM_HEADS), F32)], axis=1).astype(BF16)
    pad_small = lambda v, at: jnp.zeros((1, SMALL_W), F32).at[0, at:at + v.shape[0]].set(v)
    gb = pad_small(gate_b[gate_cols], 0).T
    dtb = pad_small(dt_bias, DT_LANE0)
    alog = pad_small(a_log, DT_LANE0)

    x2 = x.reshape(bsz * seq, D_MODEL)
    qnt, qrt, cm, kk, vvt, za, zs, xbc, sm, smt = _proj_call(
        x2, pre_w[None, :], wqt, wcm, wk, wvt, wza, wzs, wxbc, wsm, wsm.T, tabs, seq)

    w1r = cmp_w1.reshape(2, 2, CMP_STRIDE, HEAD_DIM, CMP_HIDDEN)
    wa, wb = w1r[:, 0], w1r[:, 1]
    z1 = jnp.zeros_like(wa)
    bd1 = jnp.concatenate([jnp.concatenate([wa, z1, wb, z1], axis=-1),
                           jnp.concatenate([z1, wa, z1, wb], axis=-1)], axis=-2).astype(BF16)
    z2 = jnp.zeros_like(cmp_w2)
    bd2 = jnp.concatenate([jnp.concatenate([cmp_w2, z2], axis=-1),
                           jnp.concatenate([z2, cmp_w2], axis=-1)], axis=-2).astype(BF16)
    pos8 = jnp.broadcast_to(cmp_pos.reshape(2, 1, CMP_BLOCK * HEAD_DIM), (2, SUBLANES, CMP_BLOCK * HEAD_DIM))
    b2 = jnp.concatenate([cmp_b2, cmp_b2], axis=-1)[:, None, :]
    cm2 = cm.reshape(bsz * seq // CMP_STRIDE, CMP_STRIDE * 2 * KV_WIDTH)
    kc_std, kc_t = _compress_call(cm2, bd1, cmp_w1, pos8, cmp_b1[:, None, :], bd2, b2,
                                  jnp.swapaxes(bd2, 1, 2), jnp.swapaxes(b2, 1, 2))

    att = _attn_call(qnt, qrt, kc_std, kc_t, kk, vvt, smt, gb, jnp.asarray(egt, BF16), jnp.asarray(ovt, BF16),
                     jnp.asarray(tri), za, bsz, seq)

    dskip = jnp.repeat(d_skip, SSM_HEAD_DIM)[None, :]
    ssm = _ssd_call(xbc, sm, conv_w, conv_b[None, :], dtb, alog, jnp.asarray(e16, BF16), dskip, zs,
                    ssm_norm_w[None, :], bsz, seq)

    wo_a = w_out[0:ATTN_WIDTH][colperm, :].astype(BF16)
    wo_s = w_out[ATTN_WIDTH:].astype(BF16)
    out = _out_call(att, ssm, x2, wo_a, wo_s, post_w[None, :])
    return out.reshape(bsz, seq, D_MODEL)


def kernel(x, w_in, w_out, pre_norm_w, post_norm_w, cmp_pos, cmp_w1, cmp_b1, cmp_w2, cmp_b2, gate_b, conv_w,
           conv_b, dt_bias, a_log, d_skip, ssm_norm_w):
    for l in range(w_in.shape[0]):
        x = _layer(x, w_in[l], w_out[l], pre_norm_w[l], post_norm_w[l], cmp_pos[l], cmp_w1[l], cmp_b1[l],
                   cmp_w2[l], cmp_b2[l], gate_b[l], conv_w[l], conv_b[l], dt_bias[l], a_log[l], d_skip[l],
                   ssm_norm_w[l])
    return x
```

```python
import numpy as np
import jax
import jax.numpy as jnp
from jax import lax
from jax.experimental import pallas as pl
from jax.experimental.pallas import tpu as pltpu

F32 = jnp.float32
BF16 = jnp.bfloat16

D_MODEL = 1024
ATTN_HEADS = 16
HEAD_DIM = 64
ATTN_WIDTH = ATTN_HEADS * HEAD_DIM
KV_HEADS = 4
Q_PER_KV = ATTN_HEADS // KV_HEADS
KV_WIDTH = KV_HEADS * HEAD_DIM
ROPE_DIM = HEAD_DIM // 4
ROPE_HALF = ROPE_DIM // 2
ROPE_THETA = 500000.0
CMP_BLOCK = 32
CMP_STRIDE = 16
CMP_HIDDEN = 256
SLC_BLOCK = 64
SLC_TOPN = 16
WINDOW = 512
N_BRANCH = 3
SSM_HEADS = 16
SSM_HEAD_DIM = 64
SSM_WIDTH = SSM_HEADS * SSM_HEAD_DIM
SSM_GROUPS = 4
SSM_STATE = 128
CONV_WIDTH = 4
CONV_CH = SSM_WIDTH + 2 * SSM_GROUPS * SSM_STATE
MIX_WIDTH = ATTN_WIDTH + SSM_WIDTH
EPS = 1e-6
NEG = -1e30
BIG = 1e30
M_FLOOR = -1e29
LOG2E = 1.4426950408889634

LANES = 128
SUBLANES = 8
N_PAIR_COLS = KV_HEADS // 2
HALF_W = ATTN_WIDTH // N_PAIR_COLS
SMALL_W = LANES
ONES_ROWS = 16
VT_ROWS = HEAD_DIM + ONES_ROWS
DT_LANE0 = ATTN_HEADS * N_BRANCH
VMEM_LIMIT = 56 * 1024 * 1024

PROJ_ROWS = 512
ATTN_Q = 256
ATTN_K = 256
MAX_GROUP = WINDOW // ATTN_K + 1
SCORE_LEAD = 3
SSD_CHUNK = 256
OUT_ROWS = 512


def _dot(a, b):
    return jnp.dot(a, b, preferred_element_type=F32)


def _dot_nt(a, b):
    return lax.dot_general(a, b, (((1,), (1,)), ((), ())), preferred_element_type=F32)


def _split3(x):
    x1 = x.astype(BF16)
    r1 = x - x1.astype(F32)
    x2 = r1.astype(BF16)
    x3 = (r1 - x2.astype(F32)).astype(BF16)
    return x1, x2, x3


def _dot_select(x, sel):
    x1, x2, x3 = _split3(x)
    return _dot(x1, sel) + _dot(x2, sel) + _dot(x3, sel)


def _select_dot(sel, x):
    x1, x2, x3 = _split3(x)
    return _dot(sel, x1) + _dot(sel, x2) + _dot(sel, x3)


def _sigmoid(x):
    return 1.0 / (1.0 + jnp.exp(-x))


def _silu(x):
    return x * _sigmoid(x)


def _rope_chunk(c, cos_t, sin_a, sin_b):
    return c * cos_t + pltpu.roll(c, LANES - ROPE_HALF, 1) * sin_a + pltpu.roll(c, ROPE_HALF, 1) * sin_b


def _proj_kernel(x_ref, pre_w_ref, wqt_ref, wcm_ref, wk_ref, wvt_ref, wza_ref, wzs_ref, wxbc_ref, wsm_ref,
                 wsmt_ref, cos_ref, sina_ref, sinb_ref, cos8_ref, sin8_ref,
                 qnt_ref, qrt_ref, cm_ref, kk_ref, vvt_ref, za_ref, zs_ref, xbc_ref, sm_ref, smt_ref):
    x = x_ref[...]
    h = x * lax.rsqrt(jnp.mean(x * x, axis=-1, keepdims=True) + EPS) * pre_w_ref[...]
    h = h.astype(BF16)

    qt = _dot_nt(wqt_ref[...], h)
    qt_bf = qt.astype(BF16)
    qnt_ref[...] = qt_bf
    qrt_ref[...] = qt_bf
    cos8, sin8 = cos8_ref[...], sin8_ref[...]
    for hs in range(ATTN_HEADS):
        base = hs * HEAD_DIM
        t1, t2 = qt[base:base + ROPE_HALF], qt[base + ROPE_HALF:base + ROPE_DIM]
        rot = jnp.concatenate([t1 * cos8 - t2 * sin8, t2 * cos8 + t1 * sin8], axis=0)
        qrt_ref[base:base + ROPE_DIM, :] = rot.astype(BF16)

    cm = _dot(h, wcm_ref[...]).astype(BF16)
    for k in range(2 * N_PAIR_COLS):
        cm_ref[k] = cm[:, k * LANES:(k + 1) * LANES]

    cos_t, sin_a, sin_b = cos_ref[...], sina_ref[...], sinb_ref[...]
    kk = _dot(h, wk_ref[...])
    for t in range(2):
        for k in range(N_PAIR_COLS):
            c = kk[:, t * KV_WIDTH + k * LANES: t * KV_WIDTH + (k + 1) * LANES]
            kk_ref[t, :, k * LANES:(k + 1) * LANES] = _rope_chunk(c, cos_t, sin_a, sin_b).astype(BF16)

    vvt = _dot_nt(wvt_ref[...], h).astype(BF16)
    ones = jnp.ones((ONES_ROWS, ATTN_K), BF16)
    for t in range(2):
        for j in range(PROJ_ROWS // ATTN_K):
            for g in range(KV_HEADS):
                src = t * KV_WIDTH + g * HEAD_DIM
                vvt_ref[t, j, g * VT_ROWS:g * VT_ROWS + HEAD_DIM, :] = vvt[src:src + HEAD_DIM,
                                                                            j * ATTN_K:(j + 1) * ATTN_K]
                vvt_ref[t, j, g * VT_ROWS + HEAD_DIM:(g + 1) * VT_ROWS, :] = ones

    za_ref[...] = _dot(h, wza_ref[...]).astype(BF16)
    zs_ref[...] = _dot(h, wzs_ref[...]).astype(BF16)
    xbc_ref[...] = _dot(h, wxbc_ref[...]).astype(BF16)
    sm_ref[...] = _dot(h, wsm_ref[...])
    smt_ref[...] = _dot_nt(wsmt_ref[...], h)


def _resident(shape):
    nd = len(shape)
    return pl.BlockSpec(shape, lambda *_: (0,) * nd, pipeline_mode=pl.Buffered(1))


def _proj_call(x2, pre_w, wqt, wcm, wk, wvt, wza, wzs, wxbc, wsm, wsmt, tabs, seq):
    rows = x2.shape[0]
    tm = PROJ_ROWS
    n_seq_tiles = seq // tm
    k_per_tile = tm // ATTN_K
    row_spec = lambda w: pl.BlockSpec((tm, w), lambda i: (i, 0))
    col_spec = lambda h: pl.BlockSpec((h, tm), lambda i: (0, i))
    tab_spec = pl.BlockSpec((tm, LANES), lambda i: (i % n_seq_tiles, 0))
    tab8_spec = pl.BlockSpec((ROPE_HALF, tm), lambda i: (0, i % n_seq_tiles))
    weights = (pre_w, wqt, wcm, wk, wvt, wza, wzs, wxbc, wsm, wsmt)
    return pl.pallas_call(
        _proj_kernel,
        grid=(rows // tm,),
        in_specs=[row_spec(D_MODEL)] + [_resident(w.shape) for w in weights]
                 + [tab_spec, tab_spec, tab_spec, tab8_spec, tab8_spec],
        out_specs=[col_spec(ATTN_WIDTH), col_spec(ATTN_WIDTH),
                   pl.BlockSpec((2 * N_PAIR_COLS, tm, LANES), lambda i: (0, i, 0)),
                   pl.BlockSpec((2, tm, KV_WIDTH), lambda i: (0, i, 0)),
                   pl.BlockSpec((2, k_per_tile, KV_HEADS * VT_ROWS, ATTN_K), lambda i: (0, i, 0, 0)),
                   row_spec(ATTN_WIDTH), row_spec(SSM_WIDTH), row_spec(CONV_CH), row_spec(SMALL_W),
                   col_spec(SMALL_W)],
        out_shape=[jax.ShapeDtypeStruct((ATTN_WIDTH, rows), BF16),
                   jax.ShapeDtypeStruct((ATTN_WIDTH, rows), BF16),
                   jax.ShapeDtypeStruct((2 * N_PAIR_COLS, rows, LANES), BF16),
                   jax.ShapeDtypeStruct((2, rows, KV_WIDTH), BF16),
                   jax.ShapeDtypeStruct((2, rows // ATTN_K, KV_HEADS * VT_ROWS, ATTN_K), BF16),
                   jax.ShapeDtypeStruct((rows, ATTN_WIDTH), BF16),
                   jax.ShapeDtypeStruct((rows, SSM_WIDTH), BF16),
                   jax.ShapeDtypeStruct((rows, CONV_CH), BF16),
                   jax.ShapeDtypeStruct((rows, SMALL_W), F32),
                   jax.ShapeDtypeStruct((SMALL_W, rows), F32)],
        compiler_params=pltpu.CompilerParams(dimension_semantics=("arbitrary",),
                                             vmem_limit_bytes=VMEM_LIMIT),
        name="proj",
    )(x2, *weights, *tabs)


def _compress_kernel(x_ref, bd1_ref, w1_ref, pos_ref, b1_ref, bd2_ref, b2_ref, bd2t_ref, b2t_ref,
                     out_ref, outt_ref, acc_ref):
    n_rows = x_ref.shape[0]
    hid2 = 2 * CMP_HIDDEN
    acc_ref[n_rows:, :] = jnp.zeros((SUBLANES, 2 * hid2), F32)
    acc_ref[0:n_rows, :] = _dot(x_ref[...], bd1_ref[...])
    first = acc_ref[0:n_rows, 0:hid2]
    second = acc_ref[pl.ds(1, n_rows), hid2:2 * hid2]
    posterm = _dot(pos_ref[...].astype(BF16), w1_ref[...].astype(BF16))[0:1, :] + b1_ref[...]
    hcat = first + second + jnp.concatenate([posterm, posterm], axis=1)
    act = _silu(hcat).astype(BF16)
    out_ref[...] = _dot(act, bd2_ref[...]) + b2_ref[...]
    outt_ref[...] = _dot_nt(bd2t_ref[...], act) + b2t_ref[...]


def _compress_call(cm4, bd1, w1, pos8, b1, bd2, b2, bd2t, b2t):
    n_rows = cm4.shape[1]
    per_s = lambda *tail: pl.BlockSpec((None,) + tail, lambda s, c: (s,) + (0,) * len(tail))
    return pl.pallas_call(
        _compress_kernel,
        grid=(2, N_PAIR_COLS),
        in_specs=[pl.BlockSpec((None, n_rows, CMP_STRIDE * LANES), lambda s, c: (s * N_PAIR_COLS + c, 0, 0)),
                  per_s(CMP_STRIDE * LANES, 4 * CMP_HIDDEN),
                  per_s(CMP_BLOCK * HEAD_DIM, CMP_HIDDEN),
                  per_s(SUBLANES, CMP_BLOCK * HEAD_DIM),
                  per_s(1, CMP_HIDDEN),
                  per_s(2 * CMP_HIDDEN, LANES),
                  per_s(1, LANES),
                  per_s(LANES, 2 * CMP_HIDDEN),
                  per_s(LANES, 1)],
        out_specs=[pl.BlockSpec((None, None, n_rows, LANES), lambda s, c: (s, c, 0, 0)),
                   pl.BlockSpec((None, None, LANES, n_rows), lambda s, c: (s, c, 0, 0))],
        out_shape=[jax.ShapeDtypeStruct((2, N_PAIR_COLS, n_rows, LANES), F32),
                   jax.ShapeDtypeStruct((2, N_PAIR_COLS, LANES, n_rows), F32)],
        scratch_shapes=[pltpu.VMEM((n_rows + SUBLANES, 4 * CMP_HIDDEN), F32)],
        compiler_params=pltpu.CompilerParams(dimension_semantics=("arbitrary", "arbitrary"),
                                             vmem_limit_bytes=VMEM_LIMIT),
        name="compress",
    )(cm4, bd1, w1, pos8, b1, bd2, b2, bd2t, b2t)


def _attn_kernel(qnt_ref, qrt_ref, kc_ref, vct_ref, ksl_ref, kwn_ref, vslt_ref, vwnt_ref, smt_ref, gb_ref,
                 egt_ref, ovt_ref, tri_ref, za_ref, out_ref,
                 qm_ref, bias_ref, m_ref, acc_ref, osum_ref, s_ref, kaug_ref):
    tq, tk = ATTN_Q, ATTN_K
    n_r = Q_PER_KV
    qt = pl.program_id(2)
    q0 = qt * tq
    row = lax.broadcasted_iota(jnp.int32, (LANES, 1), 0)
    halves = (row < HEAD_DIM, row >= HEAD_DIM)
    qpos = q0 + lax.broadcasted_iota(jnp.int32, (1, tq), 1)

    gates = _sigmoid(smt_ref[...] + gb_ref[...])
    g1, g2, g3 = _split3(gates)

    def gate(br, r):
        sel = egt_ref[(br * n_r + r) * LANES:(br * n_r + r + 1) * LANES, :]
        return _dot(sel, g1) + _dot(sel, g2) + _dot(sel, g3)

    kc = kc_ref[...].astype(BF16)
    vct = vct_ref[...].astype(BF16)
    cpos = row * CMP_STRIDE + (CMP_BLOCK - 1)
    cbias = jnp.where(cpos <= qpos, 0.0, NEG)
    psum = [jnp.zeros((LANES, tq), F32), jnp.zeros((LANES, tq), F32)]
    for idx in range(2 * n_r):
        q = qnt_ref[(idx // 2) * LANES:(idx // 2 + 1) * LANES, :]
        qm = jnp.where(halves[idx % 2], q, jnp.zeros_like(q))
        s_ref[idx, 0:LANES, :] = _dot(kc, qm) + cbias
    for idx in range(2 * n_r):
        r, hf = divmod(idx, 2)
        s = s_ref[idx, 0:LANES, :]
        m = jnp.maximum(jnp.max(s, axis=0, keepdims=True), M_FLOOR)
        p = jnp.exp2(s - m)
        l = jnp.sum(p, axis=0, keepdims=True)
        p = p * jnp.where(l > 0.0, 1.0 / l, 0.0)
        psum[hf] = psum[hf] + p
        hs = slice(hf * HEAD_DIM, (hf + 1) * HEAD_DIM)
        osum_ref[r, hs, :] = _dot(vct[hs, :], p.astype(BF16))
    for r in range(n_r):
        osum_ref[r] = gate(0, r) * osum_ref[r]

    nb = ksl_ref.shape[0] // SLC_BLOCK
    jio = lax.broadcasted_iota(jnp.int32, (nb, tq), 0)
    cur = (q0 + lax.broadcasted_iota(jnp.int32, (nb, tq), 1)) // SLC_BLOCK
    forced = (jio == 0) | (jio == cur) | (jio == cur - 1)
    for hf in range(2):
        imp = _select_dot(ovt_ref[...], psum[hf])[0:nb, :]
        imp = jnp.where(forced, BIG, imp)
        imp = jnp.where(jio > cur, -BIG, imp)
        cnt = jnp.zeros((nb, tq), F32)
        for i in range(nb):
            other = imp[i:i + 1, :]
            beats = (other > imp) | ((other == imp) & (jio > i))
            cnt = cnt + jnp.where(beats, 1.0, 0.0)
        bias_ref[hf] = jnp.where(cnt < float(SLC_TOPN), 0.0, NEG)

    def load_queries(with_selection_bias):
        spare = jnp.zeros((HEAD_DIM - nb, tq), BF16)
        for r in range(n_r):
            q = qrt_ref[r * LANES:(r + 1) * LANES, :]
            for hf in range(2):
                if with_selection_bias:
                    fill = [bias_ref[hf].astype(BF16), spare]
                else:
                    fill = [jnp.zeros((HEAD_DIM, tq), BF16)]
                pieces = [q[0:HEAD_DIM]] + fill if hf == 0 else fill + [q[HEAD_DIM:]]
                qm_ref[r * 2 + hf] = jnp.concatenate(pieces, axis=0)

    lane_io = lax.broadcasted_iota(jnp.int32, (1, LANES), 1)
    key_blk = lax.broadcasted_iota(jnp.int32, (tk, 1), 0) // SLC_BLOCK

    def keys_with_block_indicator(k_tile, kt, hf):
        first = 0 if hf == 1 else HEAD_DIM
        onehot = (lane_io - first) == (kt * (tk // SLC_BLOCK) + key_blk)
        keep = (lane_io >= HEAD_DIM) if hf == 1 else (lane_io < HEAD_DIM)
        return jnp.where(keep, k_tile, jnp.where(onehot, 1.0, 0.0).astype(BF16))

    def reset():
        m_ref[...] = jnp.full(m_ref.shape, M_FLOOR, F32)
        acc_ref[...] = jnp.zeros(acc_ref.shape, F32)

    def flash_group(k_ref, vt_ref, parts, indicator=False):
        if indicator:
            for pi, (kt, _) in enumerate(parts):
                k_tile = k_ref[pl.ds(pl.multiple_of(kt * tk, tk), tk), :]
                for hf in range(2):
                    kaug_ref[pi, hf] = keys_with_block_indicator(k_tile, kt, hf)

        def scores(idx):
            col_max = None
            for pi, (kt, bias_fn) in enumerate(parts):
                if indicator:
                    k_tile = kaug_ref[pi, idx % 2]
                else:
                    k_tile = k_ref[pl.ds(pl.multiple_of(kt * tk, tk), tk), :]
                s = _dot(k_tile, qm_ref[idx])
                if bias_fn is not None:
                    s = bias_fn(s)
                s_ref[idx, pi * tk:(pi + 1) * tk, :] = s
                part_max = jnp.max(s, axis=0, keepdims=True)
                col_max = part_max if col_max is None else jnp.maximum(col_max, part_max)
            return col_max

        n_units = 2 * n_r
        col_maxes = [scores(idx) for idx in range(min(SCORE_LEAD, n_units))]
        for idx in range(n_units):
            if idx + SCORE_LEAD < n_units:
                col_maxes.append(scores(idx + SCORE_LEAD))
            m_old = m_ref[idx:idx + 1, :]
            m_new = jnp.maximum(m_old, col_maxes[idx])
            m_ref[idx:idx + 1, :] = m_new
            rows = slice(idx * VT_ROWS, (idx + 1) * VT_ROWS)
            vrows = slice((idx % 2) * VT_ROWS, (idx % 2 + 1) * VT_ROWS)
            upd = acc_ref[rows, :] * jnp.exp2(m_old - m_new)
            for pi, (kt, _) in enumerate(parts):
                p = jnp.exp2(s_ref[idx, pi * tk:(pi + 1) * tk, :] - m_new).astype(BF16)
                upd = upd + _dot(vt_ref[kt, vrows, :], p)
            acc_ref[rows, :] = upd

    def finish(branch):
        for r in range(n_r):
            g = gate(branch, r)
            for hf in range(2):
                base = (r * 2 + hf) * VT_ROWS
                l = acc_ref[base + HEAD_DIM:base + HEAD_DIM + 1, :]
                inv = jnp.where(l > 0.0, 1.0 / l, 0.0)
                hs = slice(hf * HEAD_DIM, (hf + 1) * HEAD_DIM)
                osum_ref[r, hs, :] += g[hs, :] * (acc_ref[base:base + HEAD_DIM, :] * inv)

    tri_causal, tri_tail, tri_open, tri_closed = 0, 1, 2, 3

    load_queries(with_selection_bias=True)
    reset()
    causal = lambda s: s + tri_ref[tri_causal]

    def slc_pair(j, carry):
        flash_group(ksl_ref, vslt_ref, [(2 * j, None), (2 * j + 1, None)], indicator=True)
        return carry

    lax.fori_loop(0, qt // 2, slc_pair, 0)

    @pl.when(qt % 2 == 1)
    def _():
        flash_group(ksl_ref, vslt_ref, [(qt - 1, None), (qt, causal)], indicator=True)

    @pl.when(qt % 2 == 0)
    def _():
        flash_group(ksl_ref, vslt_ref, [(qt, causal)], indicator=True)

    finish(1)

    load_queries(with_selection_bias=False)
    reset()
    tiles_back = WINDOW // tk
    win_parts = []
    for back in range(tiles_back, -1, -1):
        kt = jnp.maximum(qt - back, 0)
        inside = tri_causal if back == 0 else (tri_tail if back == tiles_back else tri_open)
        table = inside if back == 0 else jnp.where(qt >= back, inside, tri_closed)
        win_parts.append((kt, lambda s, table=table: s + tri_ref[table]))
    flash_group(kwn_ref, vwnt_ref, win_parts)
    finish(2)

    for r in range(n_r):
        z = za_ref[:, r * LANES:(r + 1) * LANES].astype(F32)
        out_ref[:, r * LANES:(r + 1) * LANES] = (osum_ref[r].T * _silu(z)).astype(BF16)


def _attn_call(qnt, qrt, kc_std, kc_t, kk, vvt, smt, gb, egt, ovt, tri, za, bsz, seq):
    tq, tk = ATTN_Q, ATTN_K
    n_qt = seq // tq
    n_kt = seq // tk
    qt_spec = pl.BlockSpec((HALF_W, tq), lambda b, c, t: (c, b * n_qt + t))
    out_spec = pl.BlockSpec((tq, HALF_W), lambda b, c, t: (b * n_qt + t, c))
    k_spec = lambda s: pl.BlockSpec((None, seq, LANES), lambda b, c, t: (s, b, c))
    vt_spec = lambda s: pl.BlockSpec((None, n_kt, 2 * VT_ROWS, tk), lambda b, c, t: (s, b, c, 0))
    const = lambda a: pl.BlockSpec(a.shape, lambda b, c, t: (0,) * a.ndim)
    return pl.pallas_call(
        _attn_kernel,
        grid=(bsz, N_PAIR_COLS, n_qt),
        in_specs=[qt_spec, qt_spec,
                  pl.BlockSpec((None, None, LANES, LANES), lambda b, c, t: (0, c, b, 0)),
                  pl.BlockSpec((None, None, LANES, LANES), lambda b, c, t: (1, c, 0, b)),
                  k_spec(0), k_spec(1), vt_spec(0), vt_spec(1),
                  pl.BlockSpec((SMALL_W, tq), lambda b, c, t: (0, b * n_qt + t)),
                  const(gb),
                  pl.BlockSpec((None, N_BRANCH * HALF_W, SMALL_W), lambda b, c, t: (c, 0, 0)),
                  const(ovt), const(tri), out_spec],
        out_specs=out_spec,
        out_shape=jax.ShapeDtypeStruct((bsz * seq, ATTN_WIDTH), BF16),
        scratch_shapes=[pltpu.VMEM((2 * Q_PER_KV, LANES, tq), BF16),
                        pltpu.VMEM((2, seq // SLC_BLOCK, tq), F32),
                        pltpu.VMEM((2 * Q_PER_KV, tq), F32),
                        pltpu.VMEM((2 * Q_PER_KV * VT_ROWS, tq), F32),
                        pltpu.VMEM((Q_PER_KV, LANES, tq), F32),
                        pltpu.VMEM((2 * Q_PER_KV, MAX_GROUP * tk, tq), F32),
                        pltpu.VMEM((2, 2, tk, LANES), BF16)],
        compiler_params=pltpu.CompilerParams(dimension_semantics=("arbitrary", "arbitrary", "arbitrary"),
                                             vmem_limit_bytes=VMEM_LIMIT),
        name="attn",
    )(qnt, qrt, kc_std, kc_t, kk, kk, vvt, vvt, smt, gb, egt, ovt, tri, za)


def _ssd_kernel(xbc_ref, sm_ref, cw_ref, cb_ref, dtb_ref, alog_ref, e16_ref, dskip_ref, zs_ref, nw_ref,
                out_ref, ext_ref, state_ref, y_ref):
    L = SSD_CHUNK
    j = pl.program_id(1)

    @pl.when(j == 0)
    def _():
        ext_ref[0:8, :] = jnp.zeros((8, CONV_CH), F32)
        state_ref[...] = jnp.zeros(state_ref.shape, F32)

    ext_ref[8:8 + L, :] = xbc_ref[...].astype(F32)
    conv = cb_ref[...] + sum(ext_ref[pl.ds(8 - (CONV_WIDTH - 1) + w, L), :] * cw_ref[w:w + 1, :]
                             for w in range(CONV_WIDTH))
    ext_ref[0:8, :] = ext_ref[L:L + 8, :]
    act = _silu(conv)

    sm = sm_ref[...] + dtb_ref[...]
    dt = jnp.maximum(sm, 0.0) + jnp.log1p(jnp.exp(-jnp.abs(sm)))
    a = dt * (-jnp.exp(alog_ref[...]))
    tril = lax.broadcasted_iota(jnp.int32, (L, L), 0) >= lax.broadcasted_iota(jnp.int32, (L, L), 1)
    a_cs = _select_dot(jnp.where(tril, 1.0, 0.0).astype(BF16), a)
    a_cs_t = a_cs.T
    e16 = e16_ref[...]
    a_exp = _dot_select(a_cs, e16)
    dt_exp = _dot_select(dt, e16)
    a_last = a_exp[L - 1:L, :]
    x_c = act[:, 0:SSM_WIDTH]
    xdt = x_c * dt_exp
    xw = (xdt * jnp.exp(a_last - a_exp)).astype(BF16)
    ea = jnp.exp(a_exp)
    chunk_decay = jnp.exp(a_last)
    lane = lax.broadcasted_iota(jnp.int32, (1, LANES), 1)
    lo = lane < SSM_HEAD_DIM
    halves = (lo, jnp.logical_not(lo))

    for g in range(SSM_GROUPS):
        bg = act[:, SSM_WIDTH + g * SSM_STATE: SSM_WIDTH + (g + 1) * SSM_STATE]
        cg = act[:, SSM_WIDTH + SSM_GROUPS * SSM_STATE + g * SSM_STATE:
                 SSM_WIDTH + SSM_GROUPS * SSM_STATE + (g + 1) * SSM_STATE].astype(BF16)
        cb = _dot_nt(cg, bg.astype(BF16))
        bg_t = bg.T.astype(BF16)
        for i in range(2 * g, 2 * g + 2):
            sl = slice(i * LANES, (i + 1) * LANES)
            y = jnp.zeros((L, LANES), F32)
            for hh in range(2):
                h = 2 * i + hh
                col = a_cs[:, DT_LANE0 + h:DT_LANE0 + h + 1]
                row = a_cs_t[DT_LANE0 + h:DT_LANE0 + h + 1, :]
                decay = jnp.exp(jnp.where(tril, col - row, NEG))
                xh = jnp.where(halves[hh], xdt[:, sl], 0.0).astype(BF16)
                y = y + _dot((cb * decay).astype(BF16), xh)
            st = state_ref[i]
            y = y + _dot(cg, st.astype(BF16)) * ea[:, sl]
            state_ref[i] = st * chunk_decay[:, sl] + _dot(bg_t, xw[:, sl])
            y_ref[:, sl] = y + dskip_ref[:, sl] * x_c[:, sl]

    y = y_ref[...] * _silu(zs_ref[...].astype(F32))
    y = y * lax.rsqrt(jnp.mean(y * y, axis=-1, keepdims=True) + EPS) * nw_ref[...]
    out_ref[...] = y.astype(BF16)


def _ssd_call(xbc, sm, cw, cb, dtb, alog, e16, dskip, zs, nw, bsz, seq):
    L = SSD_CHUNK
    n_ch = seq // L
    row = lambda w: pl.BlockSpec((L, w), lambda b, j: (b * n_ch + j, 0))
    const = lambda a: pl.BlockSpec(a.shape, lambda b, j: (0,) * a.ndim)
    return pl.pallas_call(
        _ssd_kernel,
        grid=(bsz, n_ch),
        in_specs=[row(CONV_CH), row(SMALL_W), const(cw), const(cb), const(dtb), const(alog), const(e16),
                  const(dskip), row(SSM_WIDTH), const(nw)],
        out_specs=row(SSM_WIDTH),
        out_shape=jax.ShapeDtypeStruct((bsz * seq, SSM_WIDTH), BF16),
        scratch_shapes=[pltpu.VMEM((L + 8, CONV_CH), F32),
                        pltpu.VMEM((SSM_HEADS // 2, SSM_STATE, LANES), F32),
                        pltpu.VMEM((L, SSM_WIDTH), F32)],
        compiler_params=pltpu.CompilerParams(dimension_semantics=("arbitrary", "arbitrary"),
                                             vmem_limit_bytes=VMEM_LIMIT),
        name="ssd",
    )(xbc, sm, cw, cb, dtb, alog, e16, dskip, zs, nw)


def _out_kernel(att_ref, ssm_ref, x_ref, wa_ref, ws_ref, pw_ref, out_ref):
    o = _dot(att_ref[...], wa_ref[...]) + _dot(ssm_ref[...], ws_ref[...])
    o = o * lax.rsqrt(jnp.mean(o * o, axis=-1, keepdims=True) + EPS) * pw_ref[...]
    out_ref[...] = x_ref[...] + o


def _out_call(att, ssm, x2, wa, ws, pw):
    rows = x2.shape[0]
    tm = OUT_ROWS
    row = lambda w: pl.BlockSpec((tm, w), lambda i: (i, 0))
    return pl.pallas_call(
        _out_kernel,
        grid=(rows // tm,),
        in_specs=[row(ATTN_WIDTH), row(SSM_WIDTH), row(D_MODEL), _resident(wa.shape), _resident(ws.shape),
                  _resident(pw.shape)],
        out_specs=row(D_MODEL),
        out_shape=jax.ShapeDtypeStruct((rows, D_MODEL), F32),
        compiler_params=pltpu.CompilerParams(dimension_semantics=("arbitrary",),
                                             vmem_limit_bytes=VMEM_LIMIT),
        name="outproj",
    )(att, ssm, x2, wa, ws, pw)


def _head_slots():
    slots = []
    for c in range(N_PAIR_COLS):
        for r in range(Q_PER_KV):
            slots += [(2 * c) * Q_PER_KV + r, (2 * c + 1) * Q_PER_KV + r]
    return np.asarray(slots)


def _constants(seq):
    slots = _head_slots()
    colperm = (slots[:, None] * HEAD_DIM + np.arange(HEAD_DIM)[None, :]).reshape(-1)
    gate_cols = np.concatenate([slots * N_BRANCH + br for br in range(N_BRANCH)])

    egt = np.zeros((N_PAIR_COLS, N_BRANCH * HALF_W, SMALL_W), np.float32)
    for c in range(N_PAIR_COLS):
        for br in range(N_BRANCH):
            for r in range(Q_PER_KV):
                for hh in range(2):
                    slot = 2 * (Q_PER_KV * c + r) + hh
                    row0 = br * HALF_W + r * LANES + hh * HEAD_DIM
                    egt[c, row0:row0 + HEAD_DIM, br * ATTN_HEADS + slot] = 1.0

    nc = (seq - CMP_BLOCK) // CMP_STRIDE + 1
    nb = seq // SLC_BLOCK
    ci = np.arange(nc)[:, None] * CMP_STRIDE
    bj = np.arange(nb)[None, :] * SLC_BLOCK
    overlap = ((ci <= bj + SLC_BLOCK - 1) & (ci + CMP_BLOCK - 1 >= bj)).astype(np.float32)
    ovt = np.zeros((LANES, LANES), np.float32)
    ovt[:nb, :nc] = overlap.T

    e16 = np.zeros((SMALL_W, SSM_WIDTH), np.float32)
    for h in range(SSM_HEADS):
        e16[DT_LANE0 + h, h * SSM_HEAD_DIM:(h + 1) * SSM_HEAD_DIM] = 1.0

    key = np.arange(ATTN_K)[:, None]
    qry = np.arange(ATTN_Q)[None, :]
    tri = np.stack([np.where(key <= qry, 0.0, NEG), np.where(key > qry, 0.0, NEG),
                    np.zeros((ATTN_K, ATTN_Q)), np.full((ATTN_K, ATTN_Q), NEG)]).astype(np.float32)
    return colperm, gate_cols, egt, ovt, e16, tri


def _rope_tables(seq):
    inv_freq = ROPE_THETA ** (-jnp.arange(ROPE_HALF, dtype=F32) * 2.0 / ROPE_DIM)
    ang = jnp.arange(seq).astype(F32)[:, None] * inv_freq[None, :]
    cos, sin = jnp.cos(ang), jnp.sin(ang)
    ones = jnp.ones((seq, HEAD_DIM - ROPE_DIM), F32)
    zeros_h = jnp.zeros((seq, ROPE_HALF), F32)
    zeros_r = jnp.zeros((seq, HEAD_DIM - ROPE_DIM), F32)
    cos_h = jnp.concatenate([cos, cos, ones], axis=1)
    sina_h = jnp.concatenate([-sin, zeros_h, zeros_r], axis=1)
    sinb_h = jnp.concatenate([zeros_h, sin, zeros_r], axis=1)
    tile2 = lambda t: jnp.concatenate([t, t], axis=1)
    return tile2(cos_h), tile2(sina_h), tile2(sinb_h), cos.T, sin.T


def _layer(x, w_in, w_out, pre_w, post_w, cmp_pos, cmp_w1, cmp_b1, cmp_w2, cmp_b2,
           gate_b, conv_w, conv_b, dt_bias, a_log, d_skip, ssm_norm_w):
    bsz, seq, _ = x.shape
    colperm, gate_cols, egt, ovt, e16, tri = _constants(seq)
    tabs = _rope_tables(seq)

    o_q = 0
    o_kcm = ATTN_WIDTH
    o_ksl = o_kcm + 2 * KV_WIDTH
    o_vsl = o_ksl + KV_WIDTH
    o_kwn = o_vsl + KV_WIDTH
    o_vwn = o_kwn + KV_WIDTH
    o_g = o_vwn + KV_WIDTH
    o_za = o_g + ATTN_HEADS * N_BRANCH
    o_zs = o_za + ATTN_WIDTH
    o_xbc = o_zs + SSM_WIDTH
    o_dt = o_xbc + CONV_CH
    cols = lambda o, n: w_in[:, o:o + n]
    scale = HEAD_DIM ** -0.5 * LOG2E
    wqt = (cols(o_q, ATTN_WIDTH)[:, colperm] * scale).T.astype(BF16)
    wcm = cols(o_kcm, 2 * KV_WIDTH).astype(BF16)
    wk = jnp.concatenate([cols(o_ksl, KV_WIDTH), cols(o_kwn, KV_WIDTH)], axis=1).astype(BF16)
    wvt = jnp.concatenate([cols(o_vsl, KV_WIDTH), cols(o_vwn, KV_WIDTH)], axis=1).T.astype(BF16)
    wza = cols(o_za, ATTN_WIDTH)[:, colperm].astype(BF16)
    wzs = cols(o_zs, SSM_WIDTH).astype(BF16)
    wxbc = cols(o_xbc, CONV_CH).astype(BF16)
    n_gate = ATTN_HEADS * N_BRANCH
    wsm = jnp.concatenate([cols(o_g, n_gate)[:, gate_cols], cols(o_dt, SSM_HEADS),
                           jnp.zeros((D_MODEL, SMALL_W - n_gate - SSM_HEADS), F32)], axis=1).astype(BF16)
    pad_small = lambda v, at: jnp.zeros((1, SMALL_W), F32).at[0, at:at + v.shape[0]].set(v)
    gb = pad_small(gate_b[gate_cols], 0).T
    dtb = pad_small(dt_bias, DT_LANE0)
    alog = pad_small(a_log, DT_LANE0)

    x2 = x.reshape(bsz * seq, D_MODEL)
    qnt, qrt, cm, kk, vvt, za, zs, xbc, sm, smt = _proj_call(
        x2, pre_w[None, :], wqt, wcm, wk, wvt, wza, wzs, wxbc, wsm, wsm.T, tabs, seq)

    w1r = cmp_w1.reshape(2, 2, CMP_STRIDE, HEAD_DIM, CMP_HIDDEN)
    wa, wb = w1r[:, 0], w1r[:, 1]
    z1 = jnp.zeros_like(wa)
    bd1 = jnp.concatenate([jnp.concatenate([wa, z1, wb, z1], axis=-1),
                           jnp.concatenate([z1, wa, z1, wb], axis=-1)], axis=-2).astype(BF16)
    bd1 = bd1.reshape(2, CMP_STRIDE * LANES, 4 * CMP_HIDDEN)
    z2 = jnp.zeros_like(cmp_w2)
    bd2 = jnp.concatenate([jnp.concatenate([cmp_w2, z2], axis=-1),
                           jnp.concatenate([z2, cmp_w2], axis=-1)], axis=-2).astype(BF16)
    pos8 = jnp.broadcast_to(cmp_pos.reshape(2, 1, CMP_BLOCK * HEAD_DIM), (2, SUBLANES, CMP_BLOCK * HEAD_DIM))
    b2 = jnp.concatenate([cmp_b2, cmp_b2], axis=-1)[:, None, :]
    cm4 = cm.reshape(2 * N_PAIR_COLS, bsz * seq // CMP_STRIDE, CMP_STRIDE * LANES)
    kc_std, kc_t = _compress_call(cm4, bd1, cmp_w1, pos8, cmp_b1[:, None, :], bd2, b2,
                                  jnp.swapaxes(bd2, 1, 2), jnp.swapaxes(b2, 1, 2))

    att = _attn_call(qnt, qrt, kc_std, kc_t, kk, vvt, smt, gb, jnp.asarray(egt, BF16), jnp.asarray(ovt, BF16),
                     jnp.asarray(tri), za, bsz, seq)

    dskip = jnp.repeat(d_skip, SSM_HEAD_DIM)[None, :]
    ssm = _ssd_call(xbc, sm, conv_w, conv_b[None, :], dtb, alog, jnp.asarray(e16, BF16), dskip, zs,
                    ssm_norm_w[None, :], bsz, seq)

    wo_a = w_out[0:ATTN_WIDTH][colperm, :].astype(BF16)
    wo_s = w_out[ATTN_WIDTH:].astype(BF16)
    out = _out_call(att, ssm, x2, wo_a, wo_s, post_w[None, :])
    return out.reshape(bsz, seq, D_MODEL)


def kernel(x, w_in, w_out, pre_norm_w, post_norm_w, cmp_pos, cmp_w1, cmp_b1, cmp_w2, cmp_b2, gate_b, conv_w,
           conv_b, dt_bias, a_log, d_skip, ssm_norm_w):
    for l in range(w_in.shape[0]):
        x = _layer(x, w_in[l], w_out[l], pre_norm_w[l], post_norm_w[l], cmp_pos[l], cmp_w1[l], cmp_b1[l],
                   cmp_w2[l], cmp_b2[l], gate_b[l], conv_w[l], conv_b[l], dt_bias[l], a_log[l], d_skip[l],
                   ssm_norm_w[l])
    return x
```

```python
import numpy as np
import jax
import jax.numpy as jnp
from jax import lax
from jax.experimental import pallas as pl
from jax.experimental.pallas import tpu as pltpu

F32 = jnp.float32
BF16 = jnp.bfloat16

D_MODEL = 1024
ATTN_HEADS = 16
HEAD_DIM = 64
ATTN_WIDTH = ATTN_HEADS * HEAD_DIM
KV_HEADS = 4
Q_PER_KV = ATTN_HEADS // KV_HEADS
KV_WIDTH = KV_HEADS * HEAD_DIM
ROPE_DIM = HEAD_DIM // 4
ROPE_HALF = ROPE_DIM // 2
ROPE_THETA = 500000.0
CMP_BLOCK = 32
CMP_STRIDE = 16
CMP_HIDDEN = 256
SLC_BLOCK = 64
SLC_TOPN = 16
WINDOW = 512
N_BRANCH = 3
SSM_HEADS = 16
SSM_HEAD_DIM = 64
SSM_WIDTH = SSM_HEADS * SSM_HEAD_DIM
SSM_GROUPS = 4
SSM_STATE = 128
CONV_WIDTH = 4
CONV_CH = SSM_WIDTH + 2 * SSM_GROUPS * SSM_STATE
MIX_WIDTH = ATTN_WIDTH + SSM_WIDTH
EPS = 1e-6
NEG = -1e30
BIG = 1e30
M_FLOOR = -1e29
LOG2E = 1.4426950408889634

LANES = 128
SUBLANES = 8
N_PAIR_COLS = KV_HEADS // 2
HALF_W = ATTN_WIDTH // N_PAIR_COLS
SMALL_W = LANES
ONES_ROWS = 16
VT_ROWS = HEAD_DIM + ONES_ROWS
DT_LANE0 = ATTN_HEADS * N_BRANCH
VMEM_LIMIT = 56 * 1024 * 1024

PROJ_ROWS = 512
ATTN_Q = 256
ATTN_K = 256
MAX_GROUP = WINDOW // ATTN_K + 1
SCORE_LEAD = 3
SSD_CHUNK = 256
CARRY_ROWS = 16
OUT_ROWS = 512


def _dot(a, b):
    return jnp.dot(a, b, preferred_element_type=F32)


def _dot_nt(a, b):
    return lax.dot_general(a, b, (((1,), (1,)), ((), ())), preferred_element_type=F32)


def _split3(x):
    x1 = x.astype(BF16)
    r1 = x - x1.astype(F32)
    x2 = r1.astype(BF16)
    x3 = (r1 - x2.astype(F32)).astype(BF16)
    return x1, x2, x3


def _dot_select(x, sel):
    x1, x2, x3 = _split3(x)
    return _dot(x1, sel) + _dot(x2, sel) + _dot(x3, sel)


def _select_dot(sel, x):
    x1, x2, x3 = _split3(x)
    return _dot(sel, x1) + _dot(sel, x2) + _dot(sel, x3)


def _sigmoid(x):
    return 1.0 / (1.0 + jnp.exp(-x))


def _silu(x):
    return x * _sigmoid(x)


def _rope_chunk(c, cos_t, sin_a, sin_b):
    return c * cos_t + pltpu.roll(c, LANES - ROPE_HALF, 1) * sin_a + pltpu.roll(c, ROPE_HALF, 1) * sin_b


def _proj_kernel(x_ref, pre_w_ref, wqt_ref, wcm_ref, wk_ref, wvt_ref, wza_ref, wzs_ref, wxbc_ref, wsm_ref,
                 wsmt_ref, cos_ref, sina_ref, sinb_ref, cos8_ref, sin8_ref,
                 qnt_ref, qrt_ref, cm_ref, kk_ref, vvt_ref, za_ref, zs_ref, xbc_ref, sm_ref, smt_ref, cm_scr):
    x = x_ref[...]
    h = x * lax.rsqrt(jnp.mean(x * x, axis=-1, keepdims=True) + EPS) * pre_w_ref[...]
    h = h.astype(BF16)

    qt = _dot_nt(wqt_ref[...], h)
    qt_bf = qt.astype(BF16)
    qnt_ref[...] = qt_bf
    qrt_ref[...] = qt_bf
    cos8, sin8 = cos8_ref[...], sin8_ref[...]
    for hs in range(ATTN_HEADS):
        base = hs * HEAD_DIM
        t1, t2 = qt[base:base + ROPE_HALF], qt[base + ROPE_HALF:base + ROPE_DIM]
        rot = jnp.concatenate([t1 * cos8 - t2 * sin8, t2 * cos8 + t1 * sin8], axis=0)
        qrt_ref[base:base + ROPE_DIM, :] = rot.astype(BF16)

    cm = _dot(h, wcm_ref[...])
    for k in range(2 * N_PAIR_COLS):
        cm_scr[k] = cm[:, k * LANES:(k + 1) * LANES]
        for t in range(CMP_STRIDE):
            piece = cm_scr[k, pl.ds(t, PROJ_ROWS // CMP_STRIDE, stride=CMP_STRIDE), :]
            cm_ref[k, :, t * LANES:(t + 1) * LANES] = piece.astype(BF16)

    cos_t, sin_a, sin_b = cos_ref[...], sina_ref[...], sinb_ref[...]
    kk = _dot(h, wk_ref[...])
    for t in range(2):
        for k in range(N_PAIR_COLS):
            c = kk[:, t * KV_WIDTH + k * LANES: t * KV_WIDTH + (k + 1) * LANES]
            kk_ref[t, :, k * LANES:(k + 1) * LANES] = _rope_chunk(c, cos_t, sin_a, sin_b).astype(BF16)

    vvt = _dot_nt(wvt_ref[...], h).astype(BF16)
    ones = jnp.ones((ONES_ROWS, ATTN_K), BF16)
    for t in range(2):
        for j in range(PROJ_ROWS // ATTN_K):
            for g in range(KV_HEADS):
                src = t * KV_WIDTH + g * HEAD_DIM
                vvt_ref[t, j, g * VT_ROWS:g * VT_ROWS + HEAD_DIM, :] = vvt[src:src + HEAD_DIM,
                                                                            j * ATTN_K:(j + 1) * ATTN_K]
                vvt_ref[t, j, g * VT_ROWS + HEAD_DIM:(g + 1) * VT_ROWS, :] = ones

    za_ref[...] = _dot(h, wza_ref[...]).astype(BF16)
    zs_ref[...] = _dot(h, wzs_ref[...]).astype(BF16)
    xbc_ref[...] = _dot(h, wxbc_ref[...]).astype(BF16)
    sm_ref[...] = _dot(h, wsm_ref[...])
    smt_ref[...] = _dot_nt(wsmt_ref[...], h)


def _resident(shape):
    nd = len(shape)
    return pl.BlockSpec(shape, lambda *_: (0,) * nd, pipeline_mode=pl.Buffered(1))


def _proj_call(x2, pre_w, wqt, wcm, wk, wvt, wza, wzs, wxbc, wsm, wsmt, tabs, seq):
    rows = x2.shape[0]
    tm = PROJ_ROWS
    n_seq_tiles = seq // tm
    k_per_tile = tm // ATTN_K
    row_spec = lambda w: pl.BlockSpec((tm, w), lambda i: (i, 0))
    col_spec = lambda h: pl.BlockSpec((h, tm), lambda i: (0, i))
    tab_spec = pl.BlockSpec((tm, LANES), lambda i: (i % n_seq_tiles, 0))
    tab8_spec = pl.BlockSpec((ROPE_HALF, tm), lambda i: (0, i % n_seq_tiles))
    weights = (pre_w, wqt, wcm, wk, wvt, wza, wzs, wxbc, wsm, wsmt)
    return pl.pallas_call(
        _proj_kernel,
        grid=(rows // tm,),
        in_specs=[row_spec(D_MODEL)] + [_resident(w.shape) for w in weights]
                 + [tab_spec, tab_spec, tab_spec, tab8_spec, tab8_spec],
        out_specs=[col_spec(ATTN_WIDTH), col_spec(ATTN_WIDTH),
                   pl.BlockSpec((2 * N_PAIR_COLS, tm // CMP_STRIDE, CMP_STRIDE * LANES), lambda i: (0, i, 0)),
                   pl.BlockSpec((2, tm, KV_WIDTH), lambda i: (0, i, 0)),
                   pl.BlockSpec((2, k_per_tile, KV_HEADS * VT_ROWS, ATTN_K), lambda i: (0, i, 0, 0)),
                   row_spec(ATTN_WIDTH), row_spec(SSM_WIDTH), row_spec(CONV_CH), row_spec(SMALL_W),
                   col_spec(SMALL_W)],
        out_shape=[jax.ShapeDtypeStruct((ATTN_WIDTH, rows), BF16),
                   jax.ShapeDtypeStruct((ATTN_WIDTH, rows), BF16),
                   jax.ShapeDtypeStruct((2 * N_PAIR_COLS, rows // CMP_STRIDE, CMP_STRIDE * LANES), BF16),
                   jax.ShapeDtypeStruct((2, rows, KV_WIDTH), BF16),
                   jax.ShapeDtypeStruct((2, rows // ATTN_K, KV_HEADS * VT_ROWS, ATTN_K), BF16),
                   jax.ShapeDtypeStruct((rows, ATTN_WIDTH), BF16),
                   jax.ShapeDtypeStruct((rows, SSM_WIDTH), BF16),
                   jax.ShapeDtypeStruct((rows, CONV_CH), BF16),
                   jax.ShapeDtypeStruct((rows, SMALL_W), F32),
                   jax.ShapeDtypeStruct((SMALL_W, rows), F32)],
        compiler_params=pltpu.CompilerParams(dimension_semantics=("arbitrary",),
                                             vmem_limit_bytes=VMEM_LIMIT),
        scratch_shapes=[pltpu.VMEM((2 * N_PAIR_COLS, tm, LANES), F32)],
        name="proj",
    )(x2, *weights, *tabs)


def _compress_kernel(x_ref, bd1_ref, w1_ref, pos_ref, b1_ref, bd2_ref, b2_ref, bd2t_ref, b2t_ref,
                     out_ref, outt_ref, acc_ref):
    n_rows = x_ref.shape[0]
    hid2 = 2 * CMP_HIDDEN
    acc_ref[n_rows:, :] = jnp.zeros((SUBLANES, 2 * hid2), F32)
    acc_ref[0:n_rows, :] = _dot(x_ref[...], bd1_ref[...])
    first = acc_ref[0:n_rows, 0:hid2]
    second = acc_ref[pl.ds(1, n_rows), hid2:2 * hid2]
    posterm = _dot(pos_ref[...].astype(BF16), w1_ref[...].astype(BF16))[0:1, :] + b1_ref[...]
    hcat = first + second + jnp.concatenate([posterm, posterm], axis=1)
    act = _silu(hcat).astype(BF16)
    out_ref[...] = _dot(act, bd2_ref[...]) + b2_ref[...]
    outt_ref[...] = _dot_nt(bd2t_ref[...], act) + b2t_ref[...]


def _compress_call(cm4, bd1, w1, pos8, b1, bd2, b2, bd2t, b2t):
    n_rows = cm4.shape[1]
    per_s = lambda *tail: pl.BlockSpec((None,) + tail, lambda s, c: (s,) + (0,) * len(tail))
    return pl.pallas_call(
        _compress_kernel,
        grid=(2, N_PAIR_COLS),
        in_specs=[pl.BlockSpec((None, n_rows, CMP_STRIDE * LANES), lambda s, c: (s * N_PAIR_COLS + c, 0, 0)),
                  per_s(CMP_STRIDE * LANES, 4 * CMP_HIDDEN),
                  per_s(CMP_BLOCK * HEAD_DIM, CMP_HIDDEN),
                  per_s(SUBLANES, CMP_BLOCK * HEAD_DIM),
                  per_s(1, CMP_HIDDEN),
                  per_s(2 * CMP_HIDDEN, LANES),
                  per_s(1, LANES),
                  per_s(LANES, 2 * CMP_HIDDEN),
                  per_s(LANES, 1)],
        out_specs=[pl.BlockSpec((None, None, n_rows, LANES), lambda s, c: (s, c, 0, 0)),
                   pl.BlockSpec((None, None, LANES, n_rows), lambda s, c: (s, c, 0, 0))],
        out_shape=[jax.ShapeDtypeStruct((2, N_PAIR_COLS, n_rows, LANES), F32),
                   jax.ShapeDtypeStruct((2, N_PAIR_COLS, LANES, n_rows), F32)],
        scratch_shapes=[pltpu.VMEM((n_rows + SUBLANES, 4 * CMP_HIDDEN), F32)],
        compiler_params=pltpu.CompilerParams(dimension_semantics=("arbitrary", "arbitrary"),
                                             vmem_limit_bytes=VMEM_LIMIT),
        name="compress",
    )(cm4, bd1, w1, pos8, b1, bd2, b2, bd2t, b2t)


def _attn_kernel(qnt_ref, qrt_ref, kc_ref, vct_ref, ksl_ref, kwn_ref, vslt_ref, vwnt_ref, smt_ref, gb_ref,
                 egt_ref, ovt_ref, tri_ref, za_ref, out_ref,
                 qw_ref, qs_ref, bias_ref, m_ref, acc_ref, osum_ref, s_ref, kaug_ref):
    tq, tk = ATTN_Q, ATTN_K
    n_r = Q_PER_KV
    qt = pl.program_id(2)
    q0 = qt * tq
    row = lax.broadcasted_iota(jnp.int32, (LANES, 1), 0)
    halves = (row < HEAD_DIM, row >= HEAD_DIM)
    qpos = q0 + lax.broadcasted_iota(jnp.int32, (1, tq), 1)

    gates = _sigmoid(smt_ref[...] + gb_ref[...])
    g1, g2, g3 = _split3(gates)

    def gate(br, r):
        sel = egt_ref[(br * n_r + r) * LANES:(br * n_r + r + 1) * LANES, :]
        return _dot(sel, g1) + _dot(sel, g2) + _dot(sel, g3)

    kc = kc_ref[...].astype(BF16)
    vct = vct_ref[...].astype(BF16)
    cpos = row * CMP_STRIDE + (CMP_BLOCK - 1)
    cbias = jnp.where(cpos <= qpos, 0.0, NEG)
    psum = [jnp.zeros((LANES, tq), F32), jnp.zeros((LANES, tq), F32)]
    for idx in range(2 * n_r):
        q = qnt_ref[(idx // 2) * LANES:(idx // 2 + 1) * LANES, :]
        qm = jnp.where(halves[idx % 2], q, jnp.zeros_like(q))
        s_ref[idx, 0:LANES, :] = _dot(kc, qm) + cbias
    for idx in range(2 * n_r):
        r, hf = divmod(idx, 2)
        s = s_ref[idx, 0:LANES, :]
        m = jnp.maximum(jnp.max(s, axis=0, keepdims=True), M_FLOOR)
        p = jnp.exp2(s - m)
        l = jnp.sum(p, axis=0, keepdims=True)
        p = p * jnp.where(l > 0.0, 1.0 / l, 0.0)
        psum[hf] = psum[hf] + p
        hs = slice(hf * HEAD_DIM, (hf + 1) * HEAD_DIM)
        osum_ref[r, hs, :] = _dot(vct[hs, :], p.astype(BF16))
    for r in range(n_r):
        osum_ref[r] = gate(0, r) * osum_ref[r]

    nb = ksl_ref.shape[0] // SLC_BLOCK
    jio = lax.broadcasted_iota(jnp.int32, (nb, tq), 0)
    cur = (q0 + lax.broadcasted_iota(jnp.int32, (nb, tq), 1)) // SLC_BLOCK
    forced = (jio == 0) | (jio == cur) | (jio == cur - 1)

    def select_blocks():
        for hf in range(2):
            imp = _select_dot(ovt_ref[...], psum[hf])[0:nb, :]
            imp = jnp.where(forced, BIG, imp)
            imp = jnp.where(jio > cur, -BIG, imp)
            cnt = jnp.zeros((nb, tq), F32)
            for i in range(nb):
                other = imp[i:i + 1, :]
                beats = (other > imp) | ((other == imp) & (jio > i))
                cnt = cnt + jnp.where(beats, 1.0, 0.0)
            bias_ref[hf] = jnp.where(cnt < float(SLC_TOPN), 0.0, NEG)

    def load_queries(qm_ref, with_selection_bias):
        spare = jnp.zeros((HEAD_DIM - nb, tq), BF16)
        for r in range(n_r):
            q = qrt_ref[r * LANES:(r + 1) * LANES, :]
            for hf in range(2):
                if with_selection_bias:
                    fill = [bias_ref[hf].astype(BF16), spare]
                else:
                    fill = [jnp.zeros((HEAD_DIM, tq), BF16)]
                pieces = [q[0:HEAD_DIM]] + fill if hf == 0 else fill + [q[HEAD_DIM:]]
                qm_ref[r * 2 + hf] = jnp.concatenate(pieces, axis=0)

    lane_io = lax.broadcasted_iota(jnp.int32, (1, LANES), 1)
    key_blk = lax.broadcasted_iota(jnp.int32, (tk, 1), 0) // SLC_BLOCK

    def keys_with_block_indicator(k_tile, kt, hf):
        first = 0 if hf == 1 else HEAD_DIM
        onehot = (lane_io - first) == (kt * (tk // SLC_BLOCK) + key_blk)
        keep = (lane_io >= HEAD_DIM) if hf == 1 else (lane_io < HEAD_DIM)
        return jnp.where(keep, k_tile, jnp.where(onehot, 1.0, 0.0).astype(BF16))

    def reset():
        m_ref[...] = jnp.full(m_ref.shape, M_FLOOR, F32)
        acc_ref[...] = jnp.zeros(acc_ref.shape, F32)

    def flash_group(qm_ref, k_ref, vt_ref, parts, indicator=False):
        if indicator:
            for pi, (kt, _) in enumerate(parts):
                k_tile = k_ref[pl.ds(pl.multiple_of(kt * tk, tk), tk), :]
                for hf in range(2):
                    kaug_ref[pi, hf] = keys_with_block_indicator(k_tile, kt, hf)

        def scores(idx):
            col_max = None
            for pi, (kt, bias_fn) in enumerate(parts):
                if indicator:
                    k_tile = kaug_ref[pi, idx % 2]
                else:
                    k_tile = k_ref[pl.ds(pl.multiple_of(kt * tk, tk), tk), :]
                s = _dot(k_tile, qm_ref[idx])
                if bias_fn is not None:
                    s = bias_fn(s)
                s_ref[idx, pi * tk:(pi + 1) * tk, :] = s
                part_max = jnp.max(s, axis=0, keepdims=True)
                col_max = part_max if col_max is None else jnp.maximum(col_max, part_max)
            return col_max

        n_units = 2 * n_r
        col_maxes = [scores(idx) for idx in range(min(SCORE_LEAD, n_units))]
        for idx in range(n_units):
            if idx + SCORE_LEAD < n_units:
                col_maxes.append(scores(idx + SCORE_LEAD))
            m_old = m_ref[idx:idx + 1, :]
            m_new = jnp.maximum(m_old, col_maxes[idx])
            m_ref[idx:idx + 1, :] = m_new
            rows = slice(idx * VT_ROWS, (idx + 1) * VT_ROWS)
            vrows = slice((idx % 2) * VT_ROWS, (idx % 2 + 1) * VT_ROWS)
            upd = acc_ref[rows, :] * jnp.exp2(m_old - m_new)
            for pi, (kt, _) in enumerate(parts):
                p = jnp.exp2(s_ref[idx, pi * tk:(pi + 1) * tk, :] - m_new).astype(BF16)
                upd = upd + _dot(vt_ref[kt, vrows, :], p)
            acc_ref[rows, :] = upd

    def finish(branch):
        for r in range(n_r):
            g = gate(branch, r)
            for hf in range(2):
                base = (r * 2 + hf) * VT_ROWS
                l = acc_ref[base + HEAD_DIM:base + HEAD_DIM + 1, :]
                inv = jnp.where(l > 0.0, 1.0 / l, 0.0)
                hs = slice(hf * HEAD_DIM, (hf + 1) * HEAD_DIM)
                osum_ref[r, hs, :] += g[hs, :] * (acc_ref[base:base + HEAD_DIM, :] * inv)

    tri_causal, tri_tail, tri_open, tri_closed = 0, 1, 2, 3

    load_queries(qw_ref, with_selection_bias=False)
    reset()
    tiles_back = WINDOW // tk
    win_parts = []
    for back in range(tiles_back, -1, -1):
        kt = jnp.maximum(qt - back, 0)
        inside = tri_causal if back == 0 else (tri_tail if back == tiles_back else tri_open)
        table = inside if back == 0 else jnp.where(qt >= back, inside, tri_closed)
        win_parts.append((kt, lambda s, table=table: s + tri_ref[table]))
    flash_group(qw_ref, kwn_ref, vwnt_ref, win_parts)
    finish(2)

    select_blocks()
    load_queries(qs_ref, with_selection_bias=True)
    reset()
    causal = lambda s: s + tri_ref[tri_causal]

    def slc_pair(j, carry):
        flash_group(qs_ref, ksl_ref, vslt_ref, [(2 * j, None), (2 * j + 1, None)], indicator=True)
        return carry

    lax.fori_loop(0, qt // 2, slc_pair, 0)

    @pl.when(qt % 2 == 1)
    def _():
        flash_group(qs_ref, ksl_ref, vslt_ref, [(qt - 1, None), (qt, causal)], indicator=True)

    @pl.when(qt % 2 == 0)
    def _():
        flash_group(qs_ref, ksl_ref, vslt_ref, [(qt, causal)], indicator=True)

    finish(1)

    for r in range(n_r):
        z = za_ref[:, r * LANES:(r + 1) * LANES].astype(F32)
        out_ref[:, r * LANES:(r + 1) * LANES] = (osum_ref[r].T * _silu(z)).astype(BF16)


def _attn_call(qnt, qrt, kc_std, kc_t, kk, vvt, smt, gb, egt, ovt, tri, za, bsz, seq):
    tq, tk = ATTN_Q, ATTN_K
    n_qt = seq // tq
    n_kt = seq // tk
    qt_spec = pl.BlockSpec((HALF_W, tq), lambda b, c, t: (c, b * n_qt + t))
    out_spec = pl.BlockSpec((tq, HALF_W), lambda b, c, t: (b * n_qt + t, c))
    k_spec = lambda s: pl.BlockSpec((None, seq, LANES), lambda b, c, t: (s, b, c))
    vt_spec = lambda s: pl.BlockSpec((None, n_kt, 2 * VT_ROWS, tk), lambda b, c, t: (s, b, c, 0))
    const = lambda a: pl.BlockSpec(a.shape, lambda b, c, t: (0,) * a.ndim)
    return pl.pallas_call(
        _attn_kernel,
        grid=(bsz, N_PAIR_COLS, n_qt),
        in_specs=[qt_spec, qt_spec,
                  pl.BlockSpec((None, None, LANES, LANES), lambda b, c, t: (0, c, b, 0)),
                  pl.BlockSpec((None, None, LANES, LANES), lambda b, c, t: (1, c, 0, b)),
                  k_spec(0), k_spec(1), vt_spec(0), vt_spec(1),
                  pl.BlockSpec((SMALL_W, tq), lambda b, c, t: (0, b * n_qt + t)),
                  const(gb),
                  pl.BlockSpec((None, N_BRANCH * HALF_W, SMALL_W), lambda b, c, t: (c, 0, 0)),
                  const(ovt), const(tri), out_spec],
        out_specs=out_spec,
        out_shape=jax.ShapeDtypeStruct((bsz * seq, ATTN_WIDTH), BF16),
        scratch_shapes=[pltpu.VMEM((2 * Q_PER_KV, LANES, tq), BF16),
                        pltpu.VMEM((2 * Q_PER_KV, LANES, tq), BF16),
                        pltpu.VMEM((2, seq // SLC_BLOCK, tq), F32),
                        pltpu.VMEM((2 * Q_PER_KV, tq), F32),
                        pltpu.VMEM((2 * Q_PER_KV * VT_ROWS, tq), F32),
                        pltpu.VMEM((Q_PER_KV, LANES, tq), F32),
                        pltpu.VMEM((2 * Q_PER_KV, MAX_GROUP * tk, tq), F32),
                        pltpu.VMEM((2, 2, tk, LANES), BF16)],
        compiler_params=pltpu.CompilerParams(dimension_semantics=("arbitrary", "arbitrary", "arbitrary"),
                                             vmem_limit_bytes=VMEM_LIMIT),
        name="attn",
    )(qnt, qrt, kc_std, kc_t, kk, kk, vvt, vvt, smt, gb, egt, ovt, tri, za)


def _ssd_kernel(xbc_ref, sm_ref, cw_ref, cb_ref, dtb_ref, alog_ref, e16_ref, dskip_ref, zs_ref, nw_ref,
                shift_ref, head_ref, out_ref, carry_ref, state_ref, y_ref):
    L = SSD_CHUNK
    j = pl.program_id(1)
    n_prev = CONV_WIDTH - 1

    @pl.when(j == 0)
    def _():
        carry_ref[...] = jnp.zeros(carry_ref.shape, BF16)
        state_ref[...] = jnp.zeros(state_ref.shape, F32)

    u = xbc_ref[...]
    shifted = _dot(shift_ref[...], u)
    conv = cb_ref[...] + u.astype(F32) * cw_ref[n_prev:n_prev + 1, :]
    head = jnp.zeros((SUBLANES, CONV_CH), F32)
    for w in range(n_prev):
        conv = conv + shifted[w * L:(w + 1) * L] * cw_ref[w:w + 1, :]
        head = head + _dot(head_ref[w], carry_ref[...]) * cw_ref[w:w + 1, :]
    conv = jnp.concatenate([conv[0:SUBLANES] + head, conv[SUBLANES:]], axis=0)
    carry_ref[...] = u[L - CARRY_ROWS:L]
    act = _silu(conv)

    sm = sm_ref[...] + dtb_ref[...]
    dt = jnp.maximum(sm, 0.0) + jnp.log1p(jnp.exp(-jnp.abs(sm)))
    a = dt * (-jnp.exp(alog_ref[...]) * LOG2E)
    tril = lax.broadcasted_iota(jnp.int32, (L, L), 0) >= lax.broadcasted_iota(jnp.int32, (L, L), 1)
    a_cs = _select_dot(jnp.where(tril, 1.0, 0.0).astype(BF16), a)
    a_cs_t = a_cs.T
    e16 = e16_ref[...]
    a_exp = _dot_select(a_cs, e16)
    dt_exp = _dot_select(dt, e16)
    a_last = a_exp[L - 1:L, :]
    x_c = act[:, 0:SSM_WIDTH]
    xdt = x_c * dt_exp
    xw = (xdt * jnp.exp2(a_last - a_exp)).astype(BF16)
    ea = jnp.exp2(a_exp)
    chunk_decay = jnp.exp2(a_last)
    lane = lax.broadcasted_iota(jnp.int32, (1, LANES), 1)
    lo = lane < SSM_HEAD_DIM
    halves = (lo, jnp.logical_not(lo))

    for g in range(SSM_GROUPS):
        bg = act[:, SSM_WIDTH + g * SSM_STATE: SSM_WIDTH + (g + 1) * SSM_STATE]
        cg = act[:, SSM_WIDTH + SSM_GROUPS * SSM_STATE + g * SSM_STATE:
                 SSM_WIDTH + SSM_GROUPS * SSM_STATE + (g + 1) * SSM_STATE].astype(BF16)
        cb = _dot_nt(cg, bg.astype(BF16))
        bg_t = bg.T.astype(BF16)
        for i in range(2 * g, 2 * g + 2):
            sl = slice(i * LANES, (i + 1) * LANES)
            y = jnp.zeros((L, LANES), F32)
            for hh in range(2):
                h = 2 * i + hh
                col = a_cs[:, DT_LANE0 + h:DT_LANE0 + h + 1]
                row = a_cs_t[DT_LANE0 + h:DT_LANE0 + h + 1, :]
                decay = jnp.exp2(jnp.where(tril, col - row, NEG))
                xh = jnp.where(halves[hh], xdt[:, sl], 0.0).astype(BF16)
                y = y + _dot((cb * decay).astype(BF16), xh)
            st = state_ref[i]
            y = y + _dot(cg, st.astype(BF16)) * ea[:, sl]
            state_ref[i] = st * chunk_decay[:, sl] + _dot(bg_t, xw[:, sl])
            y_ref[:, sl] = y + dskip_ref[:, sl] * x_c[:, sl]

    y = y_ref[...] * _silu(zs_ref[...].astype(F32))
    y = y * lax.rsqrt(jnp.mean(y * y, axis=-1, keepdims=True) + EPS) * nw_ref[...]
    out_ref[...] = y.astype(BF16)


def _ssd_call(xbc, sm, cw, cb, dtb, alog, e16, dskip, zs, nw, shift, head, bsz, seq):
    L = SSD_CHUNK
    n_ch = seq // L
    row = lambda w: pl.BlockSpec((L, w), lambda b, j: (b * n_ch + j, 0))
    const = lambda a: pl.BlockSpec(a.shape, lambda b, j: (0,) * a.ndim)
    return pl.pallas_call(
        _ssd_kernel,
        grid=(bsz, n_ch),
        in_specs=[row(CONV_CH), row(SMALL_W), const(cw), const(cb), const(dtb), const(alog), const(e16),
                  const(dskip), row(SSM_WIDTH), const(nw), const(shift), const(head)],
        out_specs=row(SSM_WIDTH),
        out_shape=jax.ShapeDtypeStruct((bsz * seq, SSM_WIDTH), BF16),
        scratch_shapes=[pltpu.VMEM((CARRY_ROWS, CONV_CH), BF16),
                        pltpu.VMEM((SSM_HEADS // 2, SSM_STATE, LANES), F32),
                        pltpu.VMEM((L, SSM_WIDTH), F32)],
        compiler_params=pltpu.CompilerParams(dimension_semantics=("arbitrary", "arbitrary"),
                                             vmem_limit_bytes=VMEM_LIMIT),
        name="ssd",
    )(xbc, sm, cw, cb, dtb, alog, e16, dskip, zs, nw, shift, head)


def _out_kernel(att_ref, ssm_ref, x_ref, wa_ref, ws_ref, pw_ref, out_ref):
    o = _dot(att_ref[...], wa_ref[...]) + _dot(ssm_ref[...], ws_ref[...])
    o = o * lax.rsqrt(jnp.mean(o * o, axis=-1, keepdims=True) + EPS) * pw_ref[...]
    out_ref[...] = x_ref[...] + o


def _out_call(att, ssm, x2, wa, ws, pw):
    rows = x2.shape[0]
    tm = OUT_ROWS
    row = lambda w: pl.BlockSpec((tm, w), lambda i: (i, 0))
    return pl.pallas_call(
        _out_kernel,
        grid=(rows // tm,),
        in_specs=[row(ATTN_WIDTH), row(SSM_WIDTH), row(D_MODEL), _resident(wa.shape), _resident(ws.shape),
                  _resident(pw.shape)],
        out_specs=row(D_MODEL),
        out_shape=jax.ShapeDtypeStruct((rows, D_MODEL), F32),
        compiler_params=pltpu.CompilerParams(dimension_semantics=("arbitrary",),
                                             vmem_limit_bytes=VMEM_LIMIT),
        name="outproj",
    )(att, ssm, x2, wa, ws, pw)


def _pair_heads(t, axis, per_head):
    shp = t.shape
    t = t.reshape(shp[:axis] + (N_PAIR_COLS, 2, Q_PER_KV, per_head) + shp[axis + 1:])
    return jnp.swapaxes(t, axis + 1, axis + 2).reshape(shp)


def _branch_major(t, axis):
    shp = t.shape
    t = _pair_heads(t, axis, N_BRANCH).reshape(shp[:axis] + (ATTN_HEADS, N_BRANCH) + shp[axis + 1:])
    return jnp.swapaxes(t, axis, axis + 1).reshape(shp)


def _constants(seq):

    egt = np.zeros((N_PAIR_COLS, N_BRANCH * HALF_W, SMALL_W), np.float32)
    for c in range(N_PAIR_COLS):
        for br in range(N_BRANCH):
            for r in range(Q_PER_KV):
                for hh in range(2):
                    slot = 2 * (Q_PER_KV * c + r) + hh
                    row0 = br * HALF_W + r * LANES + hh * HEAD_DIM
                    egt[c, row0:row0 + HEAD_DIM, br * ATTN_HEADS + slot] = 1.0

    nc = (seq - CMP_BLOCK) // CMP_STRIDE + 1
    nb = seq // SLC_BLOCK
    ci = np.arange(nc)[:, None] * CMP_STRIDE
    bj = np.arange(nb)[None, :] * SLC_BLOCK
    overlap = ((ci <= bj + SLC_BLOCK - 1) & (ci + CMP_BLOCK - 1 >= bj)).astype(np.float32)
    ovt = np.zeros((LANES, LANES), np.float32)
    ovt[:nb, :nc] = overlap.T

    e16 = np.zeros((SMALL_W, SSM_WIDTH), np.float32)
    for h in range(SSM_HEADS):
        e16[DT_LANE0 + h, h * SSM_HEAD_DIM:(h + 1) * SSM_HEAD_DIM] = 1.0

    key = np.arange(ATTN_K)[:, None]
    qry = np.arange(ATTN_Q)[None, :]
    tri = np.stack([np.where(key <= qry, 0.0, NEG), np.where(key > qry, 0.0, NEG),
                    np.zeros((ATTN_K, ATTN_Q)), np.full((ATTN_K, ATTN_Q), NEG)]).astype(np.float32)
    n_prev = CONV_WIDTH - 1
    shift = np.zeros((n_prev * SSD_CHUNK, SSD_CHUNK), np.float32)
    head = np.zeros((n_prev, SUBLANES, CARRY_ROWS), np.float32)
    for w in range(n_prev):
        back = n_prev - w
        for t in range(SSD_CHUNK):
            if t - back >= 0:
                shift[w * SSD_CHUNK + t, t - back] = 1.0
            elif t < SUBLANES:
                head[w, t, CARRY_ROWS + t - back] = 1.0
    return egt, ovt, e16, tri, shift, head


def _rope_tables(seq):
    inv_freq = ROPE_THETA ** (-jnp.arange(ROPE_HALF, dtype=F32) * 2.0 / ROPE_DIM)
    ang = jnp.arange(seq).astype(F32)[:, None] * inv_freq[None, :]
    cos, sin = jnp.cos(ang), jnp.sin(ang)
    ones = jnp.ones((seq, HEAD_DIM - ROPE_DIM), F32)
    zeros_h = jnp.zeros((seq, ROPE_HALF), F32)
    zeros_r = jnp.zeros((seq, HEAD_DIM - ROPE_DIM), F32)
    cos_h = jnp.concatenate([cos, cos, ones], axis=1)
    sina_h = jnp.concatenate([-sin, zeros_h, zeros_r], axis=1)
    sinb_h = jnp.concatenate([zeros_h, sin, zeros_r], axis=1)
    tile2 = lambda t: jnp.concatenate([t, t], axis=1)
    return tile2(cos_h), tile2(sina_h), tile2(sinb_h), cos.T, sin.T


def _layer(x, w_in, w_out, pre_w, post_w, cmp_pos, cmp_w1, cmp_b1, cmp_w2, cmp_b2,
           gate_b, conv_w, conv_b, dt_bias, a_log, d_skip, ssm_norm_w):
    bsz, seq, _ = x.shape
    egt, ovt, e16, tri, shift, head = _constants(seq)
    tabs = _rope_tables(seq)

    o_q = 0
    o_kcm = ATTN_WIDTH
    o_ksl = o_kcm + 2 * KV_WIDTH
    o_vsl = o_ksl + KV_WIDTH
    o_kwn = o_vsl + KV_WIDTH
    o_vwn = o_kwn + KV_WIDTH
    o_g = o_vwn + KV_WIDTH
    o_za = o_g + ATTN_HEADS * N_BRANCH
    o_zs = o_za + ATTN_WIDTH
    o_xbc = o_zs + SSM_WIDTH
    o_dt = o_xbc + CONV_CH
    cols = lambda o, n: w_in[:, o:o + n]
    scale = HEAD_DIM ** -0.5 * LOG2E
    wqt = (_pair_heads(cols(o_q, ATTN_WIDTH), 1, HEAD_DIM) * scale).T.astype(BF16)
    wcm = cols(o_kcm, 2 * KV_WIDTH).astype(BF16)
    wk = jnp.concatenate([cols(o_ksl, KV_WIDTH), cols(o_kwn, KV_WIDTH)], axis=1).astype(BF16)
    wvt = jnp.concatenate([cols(o_vsl, KV_WIDTH), cols(o_vwn, KV_WIDTH)], axis=1).T.astype(BF16)
    wza = _pair_heads(cols(o_za, ATTN_WIDTH), 1, HEAD_DIM).astype(BF16)
    wzs = cols(o_zs, SSM_WIDTH).astype(BF16)
    wxbc = cols(o_xbc, CONV_CH).astype(BF16)
    n_gate = ATTN_HEADS * N_BRANCH
    wsm = jnp.concatenate([_branch_major(cols(o_g, n_gate), 1), cols(o_dt, SSM_HEADS),
                           jnp.zeros((D_MODEL, SMALL_W - n_gate - SSM_HEADS), F32)], axis=1).astype(BF16)
    pad_small = lambda v, at: jnp.zeros((1, SMALL_W), F32).at[0, at:at + v.shape[0]].set(v)
    gb = pad_small(_branch_major(gate_b, 0), 0).T
    dtb = pad_small(dt_bias, DT_LANE0)
    alog = pad_small(a_log, DT_LANE0)

    x2 = x.reshape(bsz * seq, D_MODEL)
    qnt, qrt, cm, kk, vvt, za, zs, xbc, sm, smt = _proj_call(
        x2, pre_w[None, :], wqt, wcm, wk, wvt, wza, wzs, wxbc, wsm, wsm.T, tabs, seq)

    w1r = cmp_w1.reshape(2, 2, CMP_STRIDE, HEAD_DIM, CMP_HIDDEN)
    wa, wb = w1r[:, 0], w1r[:, 1]
    z1 = jnp.zeros_like(wa)
    bd1 = jnp.concatenate([jnp.concatenate([wa, z1, wb, z1], axis=-1),
                           jnp.concatenate([z1, wa, z1, wb], axis=-1)], axis=-2).astype(BF16)
    bd1 = bd1.reshape(2, CMP_STRIDE * LANES, 4 * CMP_HIDDEN)
    z2 = jnp.zeros_like(cmp_w2)
    bd2 = jnp.concatenate([jnp.concatenate([cmp_w2, z2], axis=-1),
                           jnp.concatenate([z2, cmp_w2], axis=-1)], axis=-2).astype(BF16)
    pos8 = jnp.broadcast_to(cmp_pos.reshape(2, 1, CMP_BLOCK * HEAD_DIM), (2, SUBLANES, CMP_BLOCK * HEAD_DIM))
    b2 = jnp.concatenate([cmp_b2, cmp_b2], axis=-1)[:, None, :]
    kc_std, kc_t = _compress_call(cm, bd1, cmp_w1, pos8, cmp_b1[:, None, :], bd2, b2,
                                  jnp.swapaxes(bd2, 1, 2), jnp.swapaxes(b2, 1, 2))

    att = _attn_call(qnt, qrt, kc_std, kc_t, kk, vvt, smt, gb, jnp.asarray(egt, BF16), jnp.asarray(ovt, BF16),
                     jnp.asarray(tri), za, bsz, seq)

    dskip = jnp.repeat(d_skip, SSM_HEAD_DIM)[None, :]
    ssm = _ssd_call(xbc, sm, conv_w, conv_b[None, :], dtb, alog, jnp.asarray(e16, BF16), dskip, zs,
                    ssm_norm_w[None, :], jnp.asarray(shift, BF16), jnp.asarray(head, BF16), bsz, seq)

    wo_a = _pair_heads(w_out[0:ATTN_WIDTH], 0, HEAD_DIM).astype(BF16)
    wo_s = w_out[ATTN_WIDTH:].astype(BF16)
    out = _out_call(att, ssm, x2, wo_a, wo_s, post_w[None, :])
    return out.reshape(bsz, seq, D_MODEL)


def kernel(x, w_in, w_out, pre_norm_w, post_norm_w, cmp_pos, cmp_w1, cmp_b1, cmp_w2, cmp_b2, gate_b, conv_w,
           conv_b, dt_bias, a_log, d_skip, ssm_norm_w):
    for l in range(w_in.shape[0]):
        x = _layer(x, w_in[l], w_out[l], pre_norm_w[l], post_norm_w[l], cmp_pos[l], cmp_w1[l], cmp_b1[l],
                   cmp_w2[l], cmp_b2[l], gate_b[l], conv_w[l], conv_b[l], dt_bias[l], a_log[l], d_skip[l],
                   ssm_norm_w[l])
    return x
```

```python
import numpy as np
import jax
import jax.numpy as jnp
from jax import lax
from jax.experimental import pallas as pl
from jax.experimental.pallas import tpu as pltpu

F32 = jnp.float32
BF16 = jnp.bfloat16

D_MODEL = 1024
ATTN_HEADS = 16
HEAD_DIM = 64
ATTN_WIDTH = ATTN_HEADS * HEAD_DIM
KV_HEADS = 4
Q_PER_KV = ATTN_HEADS // KV_HEADS
KV_WIDTH = KV_HEADS * HEAD_DIM
ROPE_DIM = HEAD_DIM // 4
ROPE_HALF = ROPE_DIM // 2
ROPE_THETA = 500000.0
CMP_BLOCK = 32
CMP_STRIDE = 16
CMP_HIDDEN = 256
SLC_BLOCK = 64
SLC_TOPN = 16
WINDOW = 512
N_BRANCH = 3
SSM_HEADS = 16
SSM_HEAD_DIM = 64
SSM_WIDTH = SSM_HEADS * SSM_HEAD_DIM
SSM_GROUPS = 4
SSM_STATE = 128
CONV_WIDTH = 4
CONV_CH = SSM_WIDTH + 2 * SSM_GROUPS * SSM_STATE
MIX_WIDTH = ATTN_WIDTH + SSM_WIDTH
EPS = 1e-6
NEG = -1e30
BIG = 1e30
M_FLOOR = -1e29
LOG2E = 1.4426950408889634

LANES = 128
SUBLANES = 8
N_PAIR_COLS = KV_HEADS // 2
HALF_W = ATTN_WIDTH // N_PAIR_COLS
SMALL_W = LANES
ONES_ROWS = 16
VT_ROWS = HEAD_DIM + ONES_ROWS
DT_LANE0 = ATTN_HEADS * N_BRANCH
VMEM_LIMIT = 56 * 1024 * 1024

PROJ_ROWS = 512

WN_WIDTH = dict(cm=2 * KV_WIDTH, k=2 * KV_WIDTH, za=ATTN_WIDTH, zs=SSM_WIDTH, xbc=CONV_CH, sm=SMALL_W)
WT_WIDTH = dict(q=ATTN_WIDTH, v=2 * KV_WIDTH, sm=SMALL_W)


def _offsets(widths):
    out, at = {}, 0
    for name, width in widths.items():
        out[name] = at
        at += width
    return out


WN_OFF = _offsets(WN_WIDTH)
WT_OFF = _offsets(WT_WIDTH)
ATTN_Q = 256
ATTN_K = 256
SLC_GROUP = 2
MAX_GROUP = max(SLC_GROUP, WINDOW // ATTN_K + 1)
SCORE_LEAD = 3
SSD_CHUNK = 256
CARRY_ROWS = 16
OUT_ROWS = 512


def _dot(a, b):
    return jnp.dot(a, b, preferred_element_type=F32)


def _dot_nt(a, b):
    return lax.dot_general(a, b, (((1,), (1,)), ((), ())), preferred_element_type=F32)


def _split3(x):
    x1 = x.astype(BF16)
    r1 = x - x1.astype(F32)
    x2 = r1.astype(BF16)
    x3 = (r1 - x2.astype(F32)).astype(BF16)
    return x1, x2, x3


def _dot_select(x, sel):
    x1, x2, x3 = _split3(x)
    return _dot(x1, sel) + _dot(x2, sel) + _dot(x3, sel)


def _select_dot(sel, x):
    x1, x2, x3 = _split3(x)
    return _dot(sel, x1) + _dot(sel, x2) + _dot(sel, x3)


def _sigmoid(x):
    return 1.0 / (1.0 + jnp.exp(-x))


def _silu(x):
    return x * _sigmoid(x)


def _rope_chunk(c, cos_t, sin_a, sin_b):
    return c * cos_t + pltpu.roll(c, LANES - ROPE_HALF, 1) * sin_a + pltpu.roll(c, ROPE_HALF, 1) * sin_b


def _proj_kernel(x_ref, pre_w_ref, wt_ref, wn_ref, cos_ref, sina_ref, sinb_ref, cos8_ref, sin8_ref,
                 qnt_ref, qrt_ref, cm_ref, kk_ref, vvt_ref, za_ref, zs_ref, xbc_ref, sm_ref, smt_ref, cm_scr):
    x = x_ref[...]
    h = x * lax.rsqrt(jnp.mean(x * x, axis=-1, keepdims=True) + EPS) * pre_w_ref[...]
    h = h.astype(BF16)

    wn = lambda name: wn_ref[:, WN_OFF[name]:WN_OFF[name] + WN_WIDTH[name]]
    wt = lambda name: wt_ref[WT_OFF[name]:WT_OFF[name] + WT_WIDTH[name], :]
    qt = _dot_nt(wt("q"), h)
    qt_bf = qt.astype(BF16)
    qnt_ref[...] = qt_bf
    qrt_ref[...] = qt_bf
    cos8, sin8 = cos8_ref[...], sin8_ref[...]
    for hs in range(ATTN_HEADS):
        base = hs * HEAD_DIM
        t1, t2 = qt[base:base + ROPE_HALF], qt[base + ROPE_HALF:base + ROPE_DIM]
        rot = jnp.concatenate([t1 * cos8 - t2 * sin8, t2 * cos8 + t1 * sin8], axis=0)
        qrt_ref[base:base + ROPE_DIM, :] = rot.astype(BF16)

    cm = _dot(h, wn("cm"))
    for k in range(2 * N_PAIR_COLS):
        cm_scr[k] = cm[:, k * LANES:(k + 1) * LANES]
        for t in range(CMP_STRIDE):
            piece = cm_scr[k, pl.ds(t, PROJ_ROWS // CMP_STRIDE, stride=CMP_STRIDE), :]
            cm_ref[k, :, t * LANES:(t + 1) * LANES] = piece.astype(BF16)

    cos_t, sin_a, sin_b = cos_ref[...], sina_ref[...], sinb_ref[...]
    kk = _dot(h, wn("k"))
    for t in range(2):
        for k in range(N_PAIR_COLS):
            c = kk[:, t * KV_WIDTH + k * LANES: t * KV_WIDTH + (k + 1) * LANES]
            kk_ref[t, :, k * LANES:(k + 1) * LANES] = _rope_chunk(c, cos_t, sin_a, sin_b).astype(BF16)

    vvt = _dot_nt(wt("v"), h).astype(BF16)
    ones = jnp.ones((ONES_ROWS, ATTN_K), BF16)
    for t in range(2):
        for j in range(PROJ_ROWS // ATTN_K):
            for g in range(KV_HEADS):
                src = t * KV_WIDTH + g * HEAD_DIM
                vvt_ref[t, j, g * VT_ROWS:g * VT_ROWS + HEAD_DIM, :] = vvt[src:src + HEAD_DIM,
                                                                            j * ATTN_K:(j + 1) * ATTN_K]
                vvt_ref[t, j, g * VT_ROWS + HEAD_DIM:(g + 1) * VT_ROWS, :] = ones

    za_ref[...] = _dot(h, wn("za")).astype(BF16)
    zs_ref[...] = _dot(h, wn("zs")).astype(BF16)
    xbc_ref[...] = _dot(h, wn("xbc")).astype(BF16)
    sm_ref[...] = _dot(h, wn("sm"))
    smt_ref[...] = _dot_nt(wt("sm"), h)


def _resident(shape):
    nd = len(shape)
    return pl.BlockSpec(shape, lambda *_: (0,) * nd, pipeline_mode=pl.Buffered(1))


def _proj_call(x2, pre_w, wt, wn, tabs, seq):
    rows = x2.shape[0]
    tm = PROJ_ROWS
    n_seq_tiles = seq // tm
    k_per_tile = tm // ATTN_K
    row_spec = lambda w: pl.BlockSpec((tm, w), lambda i: (i, 0))
    col_spec = lambda h: pl.BlockSpec((h, tm), lambda i: (0, i))
    tab_spec = pl.BlockSpec((tm, LANES), lambda i: (i % n_seq_tiles, 0))
    tab8_spec = pl.BlockSpec((ROPE_HALF, tm), lambda i: (0, i % n_seq_tiles))
    weights = (pre_w, wt, wn)
    return pl.pallas_call(
        _proj_kernel,
        grid=(rows // tm,),
        in_specs=[row_spec(D_MODEL)] + [_resident(w.shape) for w in weights]
                 + [tab_spec, tab_spec, tab_spec, tab8_spec, tab8_spec],
        out_specs=[col_spec(ATTN_WIDTH), col_spec(ATTN_WIDTH),
                   pl.BlockSpec((2 * N_PAIR_COLS, tm // CMP_STRIDE, CMP_STRIDE * LANES), lambda i: (0, i, 0)),
                   pl.BlockSpec((2, tm, KV_WIDTH), lambda i: (0, i, 0)),
                   pl.BlockSpec((2, k_per_tile, KV_HEADS * VT_ROWS, ATTN_K), lambda i: (0, i, 0, 0)),
                   row_spec(ATTN_WIDTH), row_spec(SSM_WIDTH), row_spec(CONV_CH), row_spec(SMALL_W),
                   col_spec(SMALL_W)],
        out_shape=[jax.ShapeDtypeStruct((ATTN_WIDTH, rows), BF16),
                   jax.ShapeDtypeStruct((ATTN_WIDTH, rows), BF16),
                   jax.ShapeDtypeStruct((2 * N_PAIR_COLS, rows // CMP_STRIDE, CMP_STRIDE * LANES), BF16),
                   jax.ShapeDtypeStruct((2, rows, KV_WIDTH), BF16),
                   jax.ShapeDtypeStruct((2, rows // ATTN_K, KV_HEADS * VT_ROWS, ATTN_K), BF16),
                   jax.ShapeDtypeStruct((rows, ATTN_WIDTH), BF16),
                   jax.ShapeDtypeStruct((rows, SSM_WIDTH), BF16),
                   jax.ShapeDtypeStruct((rows, CONV_CH), BF16),
                   jax.ShapeDtypeStruct((rows, SMALL_W), F32),
                   jax.ShapeDtypeStruct((SMALL_W, rows), F32)],
        compiler_params=pltpu.CompilerParams(dimension_semantics=("arbitrary",),
                                             vmem_limit_bytes=VMEM_LIMIT),
        scratch_shapes=[pltpu.VMEM((2 * N_PAIR_COLS, tm, LANES), F32)],
        name="proj",
    )(x2, *weights, *tabs)


def _compress_kernel(x_ref, bd1_ref, w1_ref, pos_ref, b1_ref, bd2_ref, b2_ref, bd2t_ref, b2t_ref,
                     out_ref, outt_ref, acc_ref):
    n_rows = x_ref.shape[0]
    hid2 = 2 * CMP_HIDDEN
    acc_ref[n_rows:, :] = jnp.zeros((SUBLANES, 2 * hid2), F32)
    acc_ref[0:n_rows, :] = _dot(x_ref[...], bd1_ref[...])
    first = acc_ref[0:n_rows, 0:hid2]
    second = acc_ref[pl.ds(1, n_rows), hid2:2 * hid2]
    posterm = _dot(pos_ref[...].astype(BF16), w1_ref[...].astype(BF16))[0:1, :] + b1_ref[...]
    hcat = first + second + jnp.concatenate([posterm, posterm], axis=1)
    act = _silu(hcat).astype(BF16)
    out_ref[...] = _dot(act, bd2_ref[...]) + b2_ref[...]
    outt_ref[...] = _dot_nt(bd2t_ref[...], act) + b2t_ref[...]


def _compress_call(cm4, bd1, w1, pos8, b1, bd2, b2, bd2t, b2t):
    n_rows = cm4.shape[1]
    per_s = lambda *tail: pl.BlockSpec((None,) + tail, lambda s, c: (s,) + (0,) * len(tail))
    return pl.pallas_call(
        _compress_kernel,
        grid=(2, N_PAIR_COLS),
        in_specs=[pl.BlockSpec((None, n_rows, CMP_STRIDE * LANES), lambda s, c: (s * N_PAIR_COLS + c, 0, 0)),
                  per_s(CMP_STRIDE * LANES, 4 * CMP_HIDDEN),
                  per_s(CMP_BLOCK * HEAD_DIM, CMP_HIDDEN),
                  per_s(SUBLANES, CMP_BLOCK * HEAD_DIM),
                  per_s(1, CMP_HIDDEN),
                  per_s(2 * CMP_HIDDEN, LANES),
                  per_s(1, LANES),
                  per_s(LANES, 2 * CMP_HIDDEN),
                  per_s(LANES, 1)],
        out_specs=[pl.BlockSpec((None, None, n_rows, LANES), lambda s, c: (s, c, 0, 0)),
                   pl.BlockSpec((None, None, LANES, n_rows), lambda s, c: (s, c, 0, 0))],
        out_shape=[jax.ShapeDtypeStruct((2, N_PAIR_COLS, n_rows, LANES), F32),
                   jax.ShapeDtypeStruct((2, N_PAIR_COLS, LANES, n_rows), F32)],
        scratch_shapes=[pltpu.VMEM((n_rows + SUBLANES, 4 * CMP_HIDDEN), F32)],
        compiler_params=pltpu.CompilerParams(dimension_semantics=("arbitrary", "arbitrary"),
                                             vmem_limit_bytes=VMEM_LIMIT),
        name="compress",
    )(cm4, bd1, w1, pos8, b1, bd2, b2, bd2t, b2t)


def _attn_kernel(qnt_ref, qrt_ref, kc_ref, vct_ref, ksl_ref, kwn_ref, vslt_ref, vwnt_ref, smt_ref, gb_ref,
                 ovt_ref, tri_ref, za_ref, out_ref,
                 qw_ref, qs_ref, bias_ref, m_ref, acc_ref, osum_ref, s_ref, kaug_ref, gate_ref):
    tq, tk = ATTN_Q, ATTN_K
    n_r = Q_PER_KV
    qt = pl.program_id(2)
    q0 = qt * tq
    row = lax.broadcasted_iota(jnp.int32, (LANES, 1), 0)
    halves = (row < HEAD_DIM, row >= HEAD_DIM)
    qpos = q0 + lax.broadcasted_iota(jnp.int32, (1, tq), 1)

    gate_ref[...] = _sigmoid(smt_ref[...] + gb_ref[...])
    slot0 = pl.program_id(1) * (2 * n_r)

    def gate_row(br, idx):
        return gate_ref[pl.ds(br * ATTN_HEADS + slot0 + idx, 1), :]

    kc = kc_ref[...].astype(BF16)
    vct = vct_ref[...].astype(BF16)
    cpos = row * CMP_STRIDE + (CMP_BLOCK - 1)
    cbias = jnp.where(cpos <= qpos, 0.0, NEG)
    psum = [jnp.zeros((LANES, tq), F32), jnp.zeros((LANES, tq), F32)]
    for idx in range(2 * n_r):
        q = qnt_ref[(idx // 2) * LANES:(idx // 2 + 1) * LANES, :]
        qm = jnp.where(halves[idx % 2], q, jnp.zeros_like(q))
        s_ref[idx, 0:LANES, :] = _dot(kc, qm) + cbias
    for idx in range(2 * n_r):
        r, hf = divmod(idx, 2)
        s = s_ref[idx, 0:LANES, :]
        m = jnp.maximum(jnp.max(s, axis=0, keepdims=True), M_FLOOR)
        p = jnp.exp2(s - m)
        l = jnp.sum(p, axis=0, keepdims=True)
        p = p * jnp.where(l > 0.0, 1.0 / l, 0.0)
        psum[hf] = psum[hf] + p
        hs = slice(hf * HEAD_DIM, (hf + 1) * HEAD_DIM)
        osum_ref[r, hs, :] = gate_row(0, idx) * _dot(vct[hs, :], p.astype(BF16))

    nb = ksl_ref.shape[0] // SLC_BLOCK
    jio = lax.broadcasted_iota(jnp.int32, (nb, tq), 0)
    cur = (q0 + lax.broadcasted_iota(jnp.int32, (nb, tq), 1)) // SLC_BLOCK
    forced = (jio == 0) | (jio == cur) | (jio == cur - 1)

    def select_blocks():
        for hf in range(2):
            imp = _select_dot(ovt_ref[...], psum[hf])[0:nb, :]
            imp = jnp.where(forced, BIG, imp)
            imp = jnp.where(jio > cur, -BIG, imp)
            cnt = jnp.zeros((nb, tq), F32)
            for i in range(nb):
                other = imp[i:i + 1, :]
                beats = (other > imp) | ((other == imp) & (jio > i))
                cnt = cnt + jnp.where(beats, 1.0, 0.0)
            bias_ref[hf] = jnp.where(cnt < float(SLC_TOPN), 0.0, NEG)

    def load_queries(qm_ref, with_selection_bias):
        spare = jnp.zeros((HEAD_DIM - nb, tq), BF16)
        for r in range(n_r):
            q = qrt_ref[r * LANES:(r + 1) * LANES, :]
            for hf in range(2):
                if with_selection_bias:
                    fill = [bias_ref[hf].astype(BF16), spare]
                else:
                    fill = [jnp.zeros((HEAD_DIM, tq), BF16)]
                pieces = [q[0:HEAD_DIM]] + fill if hf == 0 else fill + [q[HEAD_DIM:]]
                qm_ref[r * 2 + hf] = jnp.concatenate(pieces, axis=0)

    lane_io = lax.broadcasted_iota(jnp.int32, (1, LANES), 1)
    key_blk = lax.broadcasted_iota(jnp.int32, (tk, 1), 0) // SLC_BLOCK

    def keys_with_block_indicator(k_tile, kt, hf):
        first = 0 if hf == 1 else HEAD_DIM
        onehot = (lane_io - first) == (kt * (tk // SLC_BLOCK) + key_blk)
        keep = (lane_io >= HEAD_DIM) if hf == 1 else (lane_io < HEAD_DIM)
        return jnp.where(keep, k_tile, jnp.where(onehot, 1.0, 0.0).astype(BF16))

    def reset():
        m_ref[...] = jnp.full(m_ref.shape, M_FLOOR, F32)
        acc_ref[...] = jnp.zeros(acc_ref.shape, F32)

    def flash_group(qm_ref, k_ref, vt_ref, parts, indicator=False):
        if indicator:
            for pi, (kt, _) in enumerate(parts):
                k_tile = k_ref[pl.ds(pl.multiple_of(kt * tk, tk), tk), :]
                for hf in range(2):
                    kaug_ref[pi, hf] = keys_with_block_indicator(k_tile, kt, hf)

        def scores(idx):
            col_max = None
            for pi, (kt, bias_fn) in enumerate(parts):
                if indicator:
                    k_tile = kaug_ref[pi, idx % 2]
                else:
                    k_tile = k_ref[pl.ds(pl.multiple_of(kt * tk, tk), tk), :]
                s = _dot(k_tile, qm_ref[idx])
                if bias_fn is not None:
                    s = bias_fn(s)
                s_ref[idx, pi * tk:(pi + 1) * tk, :] = s
                part_max = jnp.max(s, axis=0, keepdims=True)
                col_max = part_max if col_max is None else jnp.maximum(col_max, part_max)
            return col_max

        n_units = 2 * n_r
        col_maxes = [scores(idx) for idx in range(min(SCORE_LEAD, n_units))]
        for idx in range(n_units):
            if idx + SCORE_LEAD < n_units:
                col_maxes.append(scores(idx + SCORE_LEAD))
            m_old = m_ref[idx:idx + 1, :]
            m_new = jnp.maximum(m_old, col_maxes[idx])
            m_ref[idx:idx + 1, :] = m_new
            rows = slice(idx * VT_ROWS, (idx + 1) * VT_ROWS)
            vrows = slice((idx % 2) * VT_ROWS, (idx % 2 + 1) * VT_ROWS)
            upd = acc_ref[rows, :] * jnp.exp2(m_old - m_new)
            for pi, (kt, _) in enumerate(parts):
                p = jnp.exp2(s_ref[idx, pi * tk:(pi + 1) * tk, :] - m_new).astype(BF16)
                upd = upd + _dot(vt_ref[kt, vrows, :], p)
            acc_ref[rows, :] = upd

    def finish(branch):
        for idx in range(2 * n_r):
            r, hf = divmod(idx, 2)
            base = idx * VT_ROWS
            l = acc_ref[base + HEAD_DIM:base + HEAD_DIM + 1, :]
            scale = gate_row(branch, idx) * jnp.where(l > 0.0, 1.0 / l, 0.0)
            hs = slice(hf * HEAD_DIM, (hf + 1) * HEAD_DIM)
            osum_ref[r, hs, :] += acc_ref[base:base + HEAD_DIM, :] * scale

    tri_causal, tri_tail, tri_open, tri_closed = 0, 1, 2, 3

    load_queries(qw_ref, with_selection_bias=False)
    reset()
    tiles_back = WINDOW // tk
    win_parts = []
    for back in range(tiles_back, -1, -1):
        kt = jnp.maximum(qt - back, 0)
        inside = tri_causal if back == 0 else (tri_tail if back == tiles_back else tri_open)
        table = inside if back == 0 else jnp.where(qt >= back, inside, tri_closed)
        win_parts.append((kt, lambda s, table=table: s + tri_ref[table]))
    flash_group(qw_ref, kwn_ref, vwnt_ref, win_parts)
    finish(2)

    select_blocks()
    load_queries(qs_ref, with_selection_bias=True)
    reset()
    causal = lambda s: s + tri_ref[tri_causal]
    full_groups = qt // SLC_GROUP

    def slc_full(j, carry):
        flash_group(qs_ref, ksl_ref, vslt_ref, [(SLC_GROUP * j + i, None) for i in range(SLC_GROUP)],
                    indicator=True)
        return carry

    lax.fori_loop(0, full_groups, slc_full, 0)

    for n_last in range(1, SLC_GROUP + 1):
        @pl.when(qt % SLC_GROUP == n_last - 1)
        def _(n_last=n_last):
            first = SLC_GROUP * full_groups
            parts = [(first + i, None) for i in range(n_last - 1)] + [(qt, causal)]
            flash_group(qs_ref, ksl_ref, vslt_ref, parts, indicator=True)

    finish(1)

    for r in range(n_r):
        z = za_ref[:, r * LANES:(r + 1) * LANES].astype(F32)
        out_ref[:, r * LANES:(r + 1) * LANES] = (osum_ref[r].T * _silu(z)).astype(BF16)


def _attn_call(qnt, qrt, kc_std, kc_t, kk, vvt, smt, gb, ovt, tri, za, bsz, seq):
    tq, tk = ATTN_Q, ATTN_K
    n_qt = seq // tq
    n_kt = seq // tk
    qt_spec = pl.BlockSpec((HALF_W, tq), lambda b, c, t: (c, b * n_qt + t))
    out_spec = pl.BlockSpec((tq, HALF_W), lambda b, c, t: (b * n_qt + t, c))
    k_spec = lambda s: pl.BlockSpec((None, seq, LANES), lambda b, c, t: (s, b, c))
    vt_spec = lambda s: pl.BlockSpec((None, n_kt, 2 * VT_ROWS, tk), lambda b, c, t: (s, b, c, 0))
    const = lambda a: pl.BlockSpec(a.shape, lambda b, c, t: (0,) * a.ndim)
    return pl.pallas_call(
        _attn_kernel,
        grid=(bsz, N_PAIR_COLS, n_qt),
        in_specs=[qt_spec, qt_spec,
                  pl.BlockSpec((None, None, LANES, LANES), lambda b, c, t: (0, c, b, 0)),
                  pl.BlockSpec((None, None, LANES, LANES), lambda b, c, t: (1, c, 0, b)),
                  k_spec(0), k_spec(1), vt_spec(0), vt_spec(1),
                  pl.BlockSpec((SMALL_W, tq), lambda b, c, t: (0, b * n_qt + t)),
                  const(gb),
                  const(ovt), const(tri), out_spec],
        out_specs=out_spec,
        out_shape=jax.ShapeDtypeStruct((bsz * seq, ATTN_WIDTH), BF16),
        scratch_shapes=[pltpu.VMEM((2 * Q_PER_KV, LANES, tq), BF16),
                        pltpu.VMEM((2 * Q_PER_KV, LANES, tq), BF16),
                        pltpu.VMEM((2, seq // SLC_BLOCK, tq), F32),
                        pltpu.VMEM((2 * Q_PER_KV, tq), F32),
                        pltpu.VMEM((2 * Q_PER_KV * VT_ROWS, tq), F32),
                        pltpu.VMEM((Q_PER_KV, LANES, tq), F32),
                        pltpu.VMEM((2 * Q_PER_KV, MAX_GROUP * tk, tq), F32),
                        pltpu.VMEM((SLC_GROUP, 2, tk, LANES), BF16),
                        pltpu.VMEM((SMALL_W, tq), F32)],
        compiler_params=pltpu.CompilerParams(dimension_semantics=("arbitrary", "arbitrary", "arbitrary"),
                                             vmem_limit_bytes=VMEM_LIMIT),
        name="attn",
    )(qnt, qrt, kc_std, kc_t, kk, kk, vvt, vvt, smt, gb, ovt, tri, za)


def _ssd_kernel(xbc_ref, sm_ref, cw_ref, cb_ref, dtb_ref, alog_ref, e16_ref, dskip_ref, zs_ref, nw_ref,
                shift_ref, head_ref, out_ref, carry_ref, state_ref, y_ref):
    L = SSD_CHUNK
    j = pl.program_id(1)
    n_prev = CONV_WIDTH - 1

    @pl.when(j == 0)
    def _():
        carry_ref[...] = jnp.zeros(carry_ref.shape, BF16)
        state_ref[...] = jnp.zeros(state_ref.shape, F32)

    u = xbc_ref[...]
    shifted = _dot(shift_ref[...], u)
    conv = cb_ref[...] + u.astype(F32) * cw_ref[n_prev:n_prev + 1, :]
    head = jnp.zeros((SUBLANES, CONV_CH), F32)
    for w in range(n_prev):
        conv = conv + shifted[w * L:(w + 1) * L] * cw_ref[w:w + 1, :]
        head = head + _dot(head_ref[w], carry_ref[...]) * cw_ref[w:w + 1, :]
    conv = jnp.concatenate([conv[0:SUBLANES] + head, conv[SUBLANES:]], axis=0)
    carry_ref[...] = u[L - CARRY_ROWS:L]
    act = _silu(conv)

    sm = sm_ref[...] + dtb_ref[...]
    dt = jnp.maximum(sm, 0.0) + jnp.log1p(jnp.exp(-jnp.abs(sm)))
    a = dt * (-jnp.exp(alog_ref[...]) * LOG2E)
    tril = lax.broadcasted_iota(jnp.int32, (L, L), 0) >= lax.broadcasted_iota(jnp.int32, (L, L), 1)
    a_cs = _select_dot(jnp.where(tril, 1.0, 0.0).astype(BF16), a)
    a_cs_t = a_cs.T
    e16 = e16_ref[...]
    a_exp = _dot_select(a_cs, e16)
    dt_exp = _dot_select(dt, e16)
    a_last = a_exp[L - 1:L, :]
    x_c = act[:, 0:SSM_WIDTH]
    xdt = x_c * dt_exp
    xw = (xdt * jnp.exp2(a_last - a_exp)).astype(BF16)
    ea = jnp.exp2(a_exp)
    chunk_decay = jnp.exp2(a_last)
    lane = lax.broadcasted_iota(jnp.int32, (1, LANES), 1)
    lo = lane < SSM_HEAD_DIM
    halves = (lo, jnp.logical_not(lo))

    for g in range(SSM_GROUPS):
        bg = act[:, SSM_WIDTH + g * SSM_STATE: SSM_WIDTH + (g + 1) * SSM_STATE]
        cg = act[:, SSM_WIDTH + SSM_GROUPS * SSM_STATE + g * SSM_STATE:
                 SSM_WIDTH + SSM_GROUPS * SSM_STATE + (g + 1) * SSM_STATE].astype(BF16)
        cb = _dot_nt(cg, bg.astype(BF16))
        bg_t = bg.T.astype(BF16)
        for i in range(2 * g, 2 * g + 2):
            sl = slice(i * LANES, (i + 1) * LANES)
            y = jnp.zeros((L, LANES), F32)
            for hh in range(2):
                h = 2 * i + hh
                col = a_cs[:, DT_LANE0 + h:DT_LANE0 + h + 1]
                row = a_cs_t[DT_LANE0 + h:DT_LANE0 + h + 1, :]
                decay = jnp.exp2(jnp.where(tril, col - row, NEG))
                xh = jnp.where(halves[hh], xdt[:, sl], 0.0).astype(BF16)
                y = y + _dot((cb * decay).astype(BF16), xh)
            st = state_ref[i]
            y = y + _dot(cg, st.astype(BF16)) * ea[:, sl]
            state_ref[i] = st * chunk_decay[:, sl] + _dot(bg_t, xw[:, sl])
            y_ref[:, sl] = y + dskip_ref[:, sl] * x_c[:, sl]

    y = y_ref[...] * _silu(zs_ref[...].astype(F32))
    y = y * lax.rsqrt(jnp.mean(y * y, axis=-1, keepdims=True) + EPS) * nw_ref[...]
    out_ref[...] = y.astype(BF16)


def _ssd_call(xbc, sm, cw, cb, dtb, alog, e16, dskip, zs, nw, shift, head, bsz, seq):
    L = SSD_CHUNK
    n_ch = seq // L
    row = lambda w: pl.BlockSpec((L, w), lambda b, j: (b * n_ch + j, 0))
    const = lambda a: pl.BlockSpec(a.shape, lambda b, j: (0,) * a.ndim)
    return pl.pallas_call(
        _ssd_kernel,
        grid=(bsz, n_ch),
        in_specs=[row(CONV_CH), row(SMALL_W), const(cw), const(cb), const(dtb), const(alog), const(e16),
                  const(dskip), row(SSM_WIDTH), const(nw), const(shift), const(head)],
        out_specs=row(SSM_WIDTH),
        out_shape=jax.ShapeDtypeStruct((bsz * seq, SSM_WIDTH), BF16),
        scratch_shapes=[pltpu.VMEM((CARRY_ROWS, CONV_CH), BF16),
                        pltpu.VMEM((SSM_HEADS // 2, SSM_STATE, LANES), F32),
                        pltpu.VMEM((L, SSM_WIDTH), F32)],
        compiler_params=pltpu.CompilerParams(dimension_semantics=("arbitrary", "arbitrary"),
                                             vmem_limit_bytes=VMEM_LIMIT),
        name="ssd",
    )(xbc, sm, cw, cb, dtb, alog, e16, dskip, zs, nw, shift, head)


def _out_kernel(att_ref, ssm_ref, x_ref, wa_ref, ws_ref, pw_ref, out_ref):
    o = _dot(att_ref[...], wa_ref[...]) + _dot(ssm_ref[...], ws_ref[...])
    o = o * lax.rsqrt(jnp.mean(o * o, axis=-1, keepdims=True) + EPS) * pw_ref[...]
    out_ref[...] = x_ref[...] + o


def _out_call(att, ssm, x2, wa, ws, pw):
    rows = x2.shape[0]
    tm = OUT_ROWS
    row = lambda w: pl.BlockSpec((tm, w), lambda i: (i, 0))
    return pl.pallas_call(
        _out_kernel,
        grid=(rows // tm,),
        in_specs=[row(ATTN_WIDTH), row(SSM_WIDTH), row(D_MODEL), _resident(wa.shape), _resident(ws.shape),
                  _resident(pw.shape)],
        out_specs=row(D_MODEL),
        out_shape=jax.ShapeDtypeStruct((rows, D_MODEL), F32),
        compiler_params=pltpu.CompilerParams(dimension_semantics=("arbitrary",),
                                             vmem_limit_bytes=VMEM_LIMIT),
        name="outproj",
    )(att, ssm, x2, wa, ws, pw)


def _pair_heads(t, axis, per_head):
    shp = t.shape
    t = t.reshape(shp[:axis] + (N_PAIR_COLS, 2, Q_PER_KV, per_head) + shp[axis + 1:])
    return jnp.swapaxes(t, axis + 1, axis + 2).reshape(shp)


def _branch_major(t, axis):
    shp = t.shape
    t = _pair_heads(t, axis, N_BRANCH).reshape(shp[:axis] + (ATTN_HEADS, N_BRANCH) + shp[axis + 1:])
    return jnp.swapaxes(t, axis, axis + 1).reshape(shp)


def _constants(seq):

    nc = (seq - CMP_BLOCK) // CMP_STRIDE + 1
    nb = seq // SLC_BLOCK
    ci = np.arange(nc)[:, None] * CMP_STRIDE
    bj = np.arange(nb)[None, :] * SLC_BLOCK
    overlap = ((ci <= bj + SLC_BLOCK - 1) & (ci + CMP_BLOCK - 1 >= bj)).astype(np.float32)
    ovt = np.zeros((LANES, LANES), np.float32)
    ovt[:nb, :nc] = overlap.T

    e16 = np.zeros((SMALL_W, SSM_WIDTH), np.float32)
    for h in range(SSM_HEADS):
        e16[DT_LANE0 + h, h * SSM_HEAD_DIM:(h + 1) * SSM_HEAD_DIM] = 1.0

    key = np.arange(ATTN_K)[:, None]
    qry = np.arange(ATTN_Q)[None, :]
    tri = np.stack([np.where(key <= qry, 0.0, NEG), np.where(key > qry, 0.0, NEG),
                    np.zeros((ATTN_K, ATTN_Q)), np.full((ATTN_K, ATTN_Q), NEG)]).astype(np.float32)
    n_prev = CONV_WIDTH - 1
    shift = np.zeros((n_prev * SSD_CHUNK, SSD_CHUNK), np.float32)
    head = np.zeros((n_prev, SUBLANES, CARRY_ROWS), np.float32)
    for w in range(n_prev):
        back = n_prev - w
        for t in range(SSD_CHUNK):
            if t - back >= 0:
                shift[w * SSD_CHUNK + t, t - back] = 1.0
            elif t < SUBLANES:
                head[w, t, CARRY_ROWS + t - back] = 1.0
    return ovt, e16, tri, shift, head


def _rope_tables(seq):
    half_freqs = ROPE_THETA ** (-np.arange(ROPE_HALF, dtype=np.float64) * 2.0 / ROPE_DIM)
    ang = np.arange(seq, dtype=np.float64)[:, None] * half_freqs[None, :]
    cos, sin = np.cos(ang), np.sin(ang)
    ones = np.ones((seq, HEAD_DIM - ROPE_DIM))
    zeros_h = np.zeros((seq, ROPE_HALF))
    zeros_r = np.zeros((seq, HEAD_DIM - ROPE_DIM))
    cos_h = np.concatenate([cos, cos, ones], axis=1)
    sina_h = np.concatenate([-sin, zeros_h, zeros_r], axis=1)
    sinb_h = np.concatenate([zeros_h, sin, zeros_r], axis=1)
    tile2 = lambda t: np.concatenate([t, t], axis=1)
    return tuple(jnp.asarray(t, F32) for t in (tile2(cos_h), tile2(sina_h), tile2(sinb_h), cos.T, sin.T))


def _layer(x, w_in, w_out, pre_w, post_w, cmp_pos, cmp_w1, cmp_b1, cmp_w2, cmp_b2,
           gate_b, conv_w, conv_b, dt_bias, a_log, d_skip, ssm_norm_w):
    bsz, seq, _ = x.shape
    ovt, e16, tri, shift, head = _constants(seq)
    tabs = _rope_tables(seq)

    o_q = 0
    o_kcm = ATTN_WIDTH
    o_ksl = o_kcm + 2 * KV_WIDTH
    o_vsl = o_ksl + KV_WIDTH
    o_kwn = o_vsl + KV_WIDTH
    o_vwn = o_kwn + KV_WIDTH
    o_g = o_vwn + KV_WIDTH
    o_za = o_g + ATTN_HEADS * N_BRANCH
    o_zs = o_za + ATTN_WIDTH
    o_xbc = o_zs + SSM_WIDTH
    o_dt = o_xbc + CONV_CH
    cols = lambda o, n: w_in[:, o:o + n]
    scale = HEAD_DIM ** -0.5 * LOG2E
    n_gate = ATTN_HEADS * N_BRANCH
    wsm = jnp.concatenate([_branch_major(cols(o_g, n_gate), 1), cols(o_dt, SSM_HEADS),
                           jnp.zeros((D_MODEL, SMALL_W - n_gate - SSM_HEADS), F32)], axis=1)
    wn = jnp.concatenate([cols(o_kcm, 2 * KV_WIDTH), cols(o_ksl, KV_WIDTH), cols(o_kwn, KV_WIDTH),
                          _pair_heads(cols(o_za, ATTN_WIDTH), 1, HEAD_DIM), cols(o_zs, SSM_WIDTH),
                          cols(o_xbc, CONV_CH), wsm], axis=1).astype(BF16)
    wt = jnp.concatenate([_pair_heads(cols(o_q, ATTN_WIDTH), 1, HEAD_DIM) * scale, cols(o_vsl, KV_WIDTH),
                          cols(o_vwn, KV_WIDTH), wsm], axis=1).T.astype(BF16)
    pad_small = lambda v, at: jnp.zeros((1, SMALL_W), F32).at[0, at:at + v.shape[0]].set(v)
    gb = pad_small(_branch_major(gate_b, 0), 0).T
    dtb = pad_small(dt_bias, DT_LANE0)
    alog = pad_small(a_log, DT_LANE0)

    x2 = x.reshape(bsz * seq, D_MODEL)
    qnt, qrt, cm, kk, vvt, za, zs, xbc, sm, smt = _proj_call(x2, pre_w[None, :], wt, wn, tabs, seq)

    w1r = cmp_w1.reshape(2, 2, CMP_STRIDE, HEAD_DIM, CMP_HIDDEN)
    wa, wb = w1r[:, 0], w1r[:, 1]
    z1 = jnp.zeros_like(wa)
    bd1 = jnp.concatenate([jnp.concatenate([wa, z1, wb, z1], axis=-1),
                           jnp.concatenate([z1, wa, z1, wb], axis=-1)], axis=-2).astype(BF16)
    bd1 = bd1.reshape(2, CMP_STRIDE * LANES, 4 * CMP_HIDDEN)
    z2 = jnp.zeros_like(cmp_w2)
    bd2 = jnp.concatenate([jnp.concatenate([cmp_w2, z2], axis=-1),
                           jnp.concatenate([z2, cmp_w2], axis=-1)], axis=-2).astype(BF16)
    pos8 = jnp.broadcast_to(cmp_pos.reshape(2, 1, CMP_BLOCK * HEAD_DIM), (2, SUBLANES, CMP_BLOCK * HEAD_DIM))
    b2 = jnp.concatenate([cmp_b2, cmp_b2], axis=-1)[:, None, :]
    kc_std, kc_t = _compress_call(cm, bd1, cmp_w1, pos8, cmp_b1[:, None, :], bd2, b2,
                                  jnp.swapaxes(bd2, 1, 2), jnp.swapaxes(b2, 1, 2))

    att = _attn_call(qnt, qrt, kc_std, kc_t, kk, vvt, smt, gb, jnp.asarray(ovt, BF16),
                     jnp.asarray(tri), za, bsz, seq)

    dskip = jnp.repeat(d_skip, SSM_HEAD_DIM)[None, :]
    ssm = _ssd_call(xbc, sm, conv_w, conv_b[None, :], dtb, alog, jnp.asarray(e16, BF16), dskip, zs,
                    ssm_norm_w[None, :], jnp.asarray(shift, BF16), jnp.asarray(head, BF16), bsz, seq)

    wo_a = _pair_heads(w_out[0:ATTN_WIDTH], 0, HEAD_DIM).astype(BF16)
    wo_s = w_out[ATTN_WIDTH:].astype(BF16)
    out = _out_call(att, ssm, x2, wo_a, wo_s, post_w[None, :])
    return out.reshape(bsz, seq, D_MODEL)


def kernel(x, w_in, w_out, pre_norm_w, post_norm_w, cmp_pos, cmp_w1, cmp_b1, cmp_w2, cmp_b2, gate_b, conv_w,
           conv_b, dt_bias, a_log, d_skip, ssm_norm_w):
    for l in range(w_in.shape[0]):
        x = _layer(x, w_in[l], w_out[l], pre_norm_w[l], post_norm_w[l], cmp_pos[l], cmp_w1[l], cmp_b1[l],
                   cmp_w2[l], cmp_b2[l], gate_b[l], conv_w[l], conv_b[l], dt_bias[l], a_log[l], d_skip[l],
                   ssm_norm_w[l])
    return x
```

```python
import numpy as np
import jax
import jax.numpy as jnp
from jax import lax
from jax.experimental import pallas as pl
from jax.experimental.pallas import tpu as pltpu

F32 = jnp.float32
BF16 = jnp.bfloat16

D_MODEL = 1024
ATTN_HEADS = 16
HEAD_DIM = 64
ATTN_WIDTH = ATTN_HEADS * HEAD_DIM
KV_HEADS = 4
Q_PER_KV = ATTN_HEADS // KV_HEADS
KV_WIDTH = KV_HEADS * HEAD_DIM
ROPE_DIM = HEAD_DIM // 4
ROPE_HALF = ROPE_DIM // 2
ROPE_THETA = 500000.0
CMP_BLOCK = 32
CMP_STRIDE = 16
CMP_HIDDEN = 256
SLC_BLOCK = 64
SLC_TOPN = 16
WINDOW = 512
N_BRANCH = 3
SSM_HEADS = 16
SSM_HEAD_DIM = 64
SSM_WIDTH = SSM_HEADS * SSM_HEAD_DIM
SSM_GROUPS = 4
SSM_STATE = 128
CONV_WIDTH = 4
CONV_CH = SSM_WIDTH + 2 * SSM_GROUPS * SSM_STATE
MIX_WIDTH = ATTN_WIDTH + SSM_WIDTH
EPS = 1e-6
NEG = -1e30
BIG = 1e30
M_FLOOR = -1e29
LOG2E = 1.4426950408889634

LANES = 128
SUBLANES = 8
N_PAIR_COLS = KV_HEADS // 2
HALF_W = ATTN_WIDTH // N_PAIR_COLS
SMALL_W = LANES
ONES_ROWS = 16
VT_ROWS = HEAD_DIM + ONES_ROWS
DT_LANE0 = ATTN_HEADS * N_BRANCH
VMEM_LIMIT = 56 * 1024 * 1024

PROJ_ROWS = 512
ATTN_Q = 256
ATTN_K = 256
SLC_GROUP = 2
MAX_GROUP = max(SLC_GROUP, WINDOW // ATTN_K + 1)
SCORE_LEAD = 3
SSD_CHUNK = 256
CARRY_ROWS = 16
OUT_ROWS = 1024


def _dot(a, b):
    return jnp.dot(a, b, preferred_element_type=F32)


def _dot_nt(a, b):
    return lax.dot_general(a, b, (((1,), (1,)), ((), ())), preferred_element_type=F32)


def _split3(x):
    x1 = x.astype(BF16)
    r1 = x - x1.astype(F32)
    x2 = r1.astype(BF16)
    x3 = (r1 - x2.astype(F32)).astype(BF16)
    return x1, x2, x3


def _dot_select(x, sel):
    x1, x2, x3 = _split3(x)
    return _dot(x1, sel) + _dot(x2, sel) + _dot(x3, sel)


def _select_dot(sel, x):
    x1, x2, x3 = _split3(x)
    return _dot(sel, x1) + _dot(sel, x2) + _dot(sel, x3)


def _sigmoid(x):
    return 1.0 / (1.0 + jnp.exp(-x))


def _silu(x):
    return x * _sigmoid(x)


def _rope_chunk(c, cos_t, sin_a, sin_b):
    return c * cos_t + pltpu.roll(c, LANES - ROPE_HALF, 1) * sin_a + pltpu.roll(c, ROPE_HALF, 1) * sin_b


def _proj_kernel(x_ref, pre_w_ref, wqt_ref, wcm_ref, wk_ref, wvt_ref, wza_ref, wzs_ref, wxbc_ref, wsm_ref,
                 cos_ref, sina_ref, sinb_ref, cos8_ref, sin8_ref,
                 qnt_ref, qrt_ref, cm_ref, kk_ref, vvt_ref, za_ref, zs_ref, xbc_ref, sm_ref, smt_ref, cm_scr):
    x = x_ref[...]
    h = x * lax.rsqrt(jnp.mean(x * x, axis=-1, keepdims=True) + EPS) * pre_w_ref[...]
    h = h.astype(BF16)

    qt = _dot_nt(wqt_ref[...], h)
    qt_bf = qt.astype(BF16)
    qnt_ref[...] = qt_bf
    qrt_ref[...] = qt_bf
    cos8, sin8 = cos8_ref[...], sin8_ref[...]
    for hs in range(ATTN_HEADS):
        base = hs * HEAD_DIM
        t1, t2 = qt[base:base + ROPE_HALF], qt[base + ROPE_HALF:base + ROPE_DIM]
        rot = jnp.concatenate([t1 * cos8 - t2 * sin8, t2 * cos8 + t1 * sin8], axis=0)
        qrt_ref[base:base + ROPE_DIM, :] = rot.astype(BF16)

    cm = _dot(h, wcm_ref[...])
    for k in range(2 * N_PAIR_COLS):
        cm_scr[k] = cm[:, k * LANES:(k + 1) * LANES]
        for t in range(CMP_STRIDE):
            piece = cm_scr[k, pl.ds(t, PROJ_ROWS // CMP_STRIDE, stride=CMP_STRIDE), :]
            cm_ref[k, :, t * LANES:(t + 1) * LANES] = piece.astype(BF16)

    cos_t, sin_a, sin_b = cos_ref[...], sina_ref[...], sinb_ref[...]
    kk = _dot(h, wk_ref[...])
    for t in range(2):
        for k in range(N_PAIR_COLS):
            c = kk[:, t * KV_WIDTH + k * LANES: t * KV_WIDTH + (k + 1) * LANES]
            kk_ref[t, :, k * LANES:(k + 1) * LANES] = _rope_chunk(c, cos_t, sin_a, sin_b).astype(BF16)

    vvt = _dot_nt(wvt_ref[...], h).astype(BF16)
    ones = jnp.ones((ONES_ROWS, ATTN_K), BF16)
    for t in range(2):
        for j in range(PROJ_ROWS // ATTN_K):
            for g in range(KV_HEADS):
                src = t * KV_WIDTH + g * HEAD_DIM
                vvt_ref[t, j, g * VT_ROWS:g * VT_ROWS + HEAD_DIM, :] = vvt[src:src + HEAD_DIM,
                                                                            j * ATTN_K:(j + 1) * ATTN_K]
                vvt_ref[t, j, g * VT_ROWS + HEAD_DIM:(g + 1) * VT_ROWS, :] = ones

    za_ref[...] = _dot(h, wza_ref[...]).astype(BF16)
    zs_ref[...] = _dot(h, wzs_ref[...]).astype(BF16)
    xbc_ref[...] = _dot(h, wxbc_ref[...]).astype(BF16)
    sm = _dot(h, wsm_ref[...])
    sm_ref[...] = sm
    smt_ref[...] = sm.T


def _resident(shape):
    nd = len(shape)
    return pl.BlockSpec(shape, lambda *_: (0,) * nd, pipeline_mode=pl.Buffered(1))


def _proj_call(x2, pre_w, wqt, wcm, wk, wvt, wza, wzs, wxbc, wsm, tabs, seq):
    rows = x2.shape[0]
    tm = PROJ_ROWS
    n_seq_tiles = seq // tm
    k_per_tile = tm // ATTN_K
    row_spec = lambda w: pl.BlockSpec((tm, w), lambda i: (i, 0))
    col_spec = lambda h: pl.BlockSpec((h, tm), lambda i: (0, i))
    tab_spec = pl.BlockSpec((tm, LANES), lambda i: (i % n_seq_tiles, 0))
    tab8_spec = pl.BlockSpec((ROPE_HALF, tm), lambda i: (0, i % n_seq_tiles))
    weights = (pre_w, wqt, wcm, wk, wvt, wza, wzs, wxbc, wsm)
    return pl.pallas_call(
        _proj_kernel,
        grid=(rows // tm,),
        in_specs=[row_spec(D_MODEL)] + [_resident(w.shape) for w in weights]
                 + [tab_spec, tab_spec, tab_spec, tab8_spec, tab8_spec],
        out_specs=[col_spec(ATTN_WIDTH), col_spec(ATTN_WIDTH),
                   pl.BlockSpec((2 * N_PAIR_COLS, tm // CMP_STRIDE, CMP_STRIDE * LANES), lambda i: (0, i, 0)),
                   pl.BlockSpec((2, tm, KV_WIDTH), lambda i: (0, i, 0)),
                   pl.BlockSpec((2, k_per_tile, KV_HEADS * VT_ROWS, ATTN_K), lambda i: (0, i, 0, 0)),
                   row_spec(ATTN_WIDTH), row_spec(SSM_WIDTH), row_spec(CONV_CH), row_spec(SMALL_W),
                   col_spec(SMALL_W)],
        out_shape=[jax.ShapeDtypeStruct((ATTN_WIDTH, rows), BF16),
                   jax.ShapeDtypeStruct((ATTN_WIDTH, rows), BF16),
                   jax.ShapeDtypeStruct((2 * N_PAIR_COLS, rows // CMP_STRIDE, CMP_STRIDE * LANES), BF16),
                   jax.ShapeDtypeStruct((2, rows, KV_WIDTH), BF16),
                   jax.ShapeDtypeStruct((2, rows // ATTN_K, KV_HEADS * VT_ROWS, ATTN_K), BF16),
                   jax.ShapeDtypeStruct((rows, ATTN_WIDTH), BF16),
                   jax.ShapeDtypeStruct((rows, SSM_WIDTH), BF16),
                   jax.ShapeDtypeStruct((rows, CONV_CH), BF16),
                   jax.ShapeDtypeStruct((rows, SMALL_W), F32),
                   jax.ShapeDtypeStruct((SMALL_W, rows), F32)],
        compiler_params=pltpu.CompilerParams(dimension_semantics=("arbitrary",),
                                             vmem_limit_bytes=VMEM_LIMIT),
        scratch_shapes=[pltpu.VMEM((2 * N_PAIR_COLS, tm, LANES), F32)],
        name="proj",
    )(x2, *weights, *tabs)


def _compress_kernel(x_ref, bd1_ref, w1_ref, pos_ref, b1_ref, bd2_ref, b2_ref, bd2t_ref, b2t_ref,
                     out_ref, outt_ref, acc_ref):
    n_rows = x_ref.shape[0]
    hid2 = 2 * CMP_HIDDEN
    acc_ref[n_rows:, :] = jnp.zeros((SUBLANES, 2 * hid2), F32)
    acc_ref[0:n_rows, :] = _dot(x_ref[...], bd1_ref[...])
    first = acc_ref[0:n_rows, 0:hid2]
    second = acc_ref[pl.ds(1, n_rows), hid2:2 * hid2]
    posterm = _dot(pos_ref[...].astype(BF16), w1_ref[...].astype(BF16))[0:1, :] + b1_ref[...]
    hcat = first + second + jnp.concatenate([posterm, posterm], axis=1)
    act = _silu(hcat).astype(BF16)
    out_ref[...] = _dot(act, bd2_ref[...]) + b2_ref[...]
    outt_ref[...] = _dot_nt(bd2t_ref[...], act) + b2t_ref[...]


def _compress_call(cm4, bd1, w1, pos8, b1, bd2, b2, bd2t, b2t):
    n_rows = cm4.shape[1]
    per_s = lambda *tail: pl.BlockSpec((None,) + tail, lambda s, c: (s,) + (0,) * len(tail))
    return pl.pallas_call(
        _compress_kernel,
        grid=(2, N_PAIR_COLS),
        in_specs=[pl.BlockSpec((None, n_rows, CMP_STRIDE * LANES), lambda s, c: (s * N_PAIR_COLS + c, 0, 0)),
                  per_s(CMP_STRIDE * LANES, 4 * CMP_HIDDEN),
                  per_s(CMP_BLOCK * HEAD_DIM, CMP_HIDDEN),
                  per_s(SUBLANES, CMP_BLOCK * HEAD_DIM),
                  per_s(1, CMP_HIDDEN),
                  per_s(2 * CMP_HIDDEN, LANES),
                  per_s(1, LANES),
                  per_s(LANES, 2 * CMP_HIDDEN),
                  per_s(LANES, 1)],
        out_specs=[pl.BlockSpec((None, None, n_rows, LANES), lambda s, c: (s, c, 0, 0)),
                   pl.BlockSpec((None, None, LANES, n_rows), lambda s, c: (s, c, 0, 0))],
        out_shape=[jax.ShapeDtypeStruct((2, N_PAIR_COLS, n_rows, LANES), F32),
                   jax.ShapeDtypeStruct((2, N_PAIR_COLS, LANES, n_rows), F32)],
        scratch_shapes=[pltpu.VMEM((n_rows + SUBLANES, 4 * CMP_HIDDEN), F32)],
        compiler_params=pltpu.CompilerParams(dimension_semantics=("arbitrary", "arbitrary"),
                                             vmem_limit_bytes=VMEM_LIMIT),
        name="compress",
    )(cm4, bd1, w1, pos8, b1, bd2, b2, bd2t, b2t)


def _attn_kernel(qnt_ref, qrt_ref, kc_ref, vct_ref, ksl_ref, kwn_ref, vslt_ref, vwnt_ref, smt_ref, gb_ref,
                 ovt_ref, tri_ref, za_ref, out_ref,
                 qw_ref, qs_ref, bias_ref, m_ref, acc_ref, osum_ref, s_ref, kaug_ref, gate_ref):
    tq, tk = ATTN_Q, ATTN_K
    n_r = Q_PER_KV
    qt = pl.program_id(2)
    q0 = qt * tq
    row = lax.broadcasted_iota(jnp.int32, (LANES, 1), 0)
    halves = (row < HEAD_DIM, row >= HEAD_DIM)
    qpos = q0 + lax.broadcasted_iota(jnp.int32, (1, tq), 1)

    gate_ref[...] = _sigmoid(smt_ref[...] + gb_ref[...])
    slot0 = pl.program_id(1) * (2 * n_r)

    def gate_row(br, idx):
        return gate_ref[pl.ds(br * ATTN_HEADS + slot0 + idx, 1), :]

    kc = kc_ref[...].astype(BF16)
    vct = vct_ref[...].astype(BF16)
    cpos = row * CMP_STRIDE + (CMP_BLOCK - 1)
    cbias = jnp.where(cpos <= qpos, 0.0, NEG)
    psum = [jnp.zeros((LANES, tq), F32), jnp.zeros((LANES, tq), F32)]
    for idx in range(2 * n_r):
        q = qnt_ref[(idx // 2) * LANES:(idx // 2 + 1) * LANES, :]
        qm = jnp.where(halves[idx % 2], q, jnp.zeros_like(q))
        s_ref[idx, 0:LANES, :] = _dot(kc, qm) + cbias
    for idx in range(2 * n_r):
        r, hf = divmod(idx, 2)
        s = s_ref[idx, 0:LANES, :]
        m = jnp.maximum(jnp.max(s, axis=0, keepdims=True), M_FLOOR)
        p = jnp.exp2(s - m)
        l = jnp.sum(p, axis=0, keepdims=True)
        p = p * jnp.where(l > 0.0, 1.0 / l, 0.0)
        psum[hf] = psum[hf] + p
        hs = slice(hf * HEAD_DIM, (hf + 1) * HEAD_DIM)
        osum_ref[r, hs, :] = gate_row(0, idx) * _dot(vct[hs, :], p.astype(BF16))

    nb = ksl_ref.shape[0] // SLC_BLOCK
    jio = lax.broadcasted_iota(jnp.int32, (nb, tq), 0)
    cur = (q0 + lax.broadcasted_iota(jnp.int32, (nb, tq), 1)) // SLC_BLOCK
    forced = (jio == 0) | (jio == cur) | (jio == cur - 1)

    def select_blocks():
        for hf in range(2):
            imp = _select_dot(ovt_ref[...], psum[hf])[0:nb, :]
            imp = jnp.where(forced, BIG, imp)
            imp = jnp.where(jio > cur, -BIG, imp)
            cnt = jnp.zeros((nb, tq), F32)
            for i in range(nb):
                other = imp[i:i + 1, :]
                beats = (other > imp) | ((other == imp) & (jio > i))
                cnt = cnt + jnp.where(beats, 1.0, 0.0)
            bias_ref[hf] = jnp.where(cnt < float(SLC_TOPN), 0.0, NEG)

    def load_queries(qm_ref, with_selection_bias):
        spare = jnp.zeros((HEAD_DIM - nb, tq), BF16)
        for r in range(n_r):
            q = qrt_ref[r * LANES:(r + 1) * LANES, :]
            for hf in range(2):
                if with_selection_bias:
                    fill = [bias_ref[hf].astype(BF16), spare]
                else:
                    fill = [jnp.zeros((HEAD_DIM, tq), BF16)]
                pieces = [q[0:HEAD_DIM]] + fill if hf == 0 else fill + [q[HEAD_DIM:]]
                qm_ref[r * 2 + hf] = jnp.concatenate(pieces, axis=0)

    lane_io = lax.broadcasted_iota(jnp.int32, (1, LANES), 1)
    key_blk = lax.broadcasted_iota(jnp.int32, (tk, 1), 0) // SLC_BLOCK

    def keys_with_block_indicator(k_tile, kt, hf):
        first = 0 if hf == 1 else HEAD_DIM
        onehot = (lane_io - first) == (kt * (tk // SLC_BLOCK) + key_blk)
        keep = (lane_io >= HEAD_DIM) if hf == 1 else (lane_io < HEAD_DIM)
        return jnp.where(keep, k_tile, jnp.where(onehot, 1.0, 0.0).astype(BF16))

    def reset():
        m_ref[...] = jnp.full(m_ref.shape, M_FLOOR, F32)
        acc_ref[...] = jnp.zeros(acc_ref.shape, F32)

    def flash_group(qm_ref, k_ref, vt_ref, parts, indicator=False):
        if indicator:
            for pi, (kt, _) in enumerate(parts):
                k_tile = k_ref[pl.ds(pl.multiple_of(kt * tk, tk), tk), :]
                for hf in range(2):
                    kaug_ref[pi, hf] = keys_with_block_indicator(k_tile, kt, hf)

        def scores(idx):
            col_max = None
            for pi, (kt, bias_fn) in enumerate(parts):
                if indicator:
                    k_tile = kaug_ref[pi, idx % 2]
                else:
                    k_tile = k_ref[pl.ds(pl.multiple_of(kt * tk, tk), tk), :]
                s = _dot(k_tile, qm_ref[idx])
                if bias_fn is not None:
                    s = bias_fn(s)
                s_ref[idx, pi * tk:(pi + 1) * tk, :] = s
                part_max = jnp.max(s, axis=0, keepdims=True)
                col_max = part_max if col_max is None else jnp.maximum(col_max, part_max)
            return col_max

        n_units = 2 * n_r
        col_maxes = [scores(idx) for idx in range(min(SCORE_LEAD, n_units))]
        for idx in range(n_units):
            if idx + SCORE_LEAD < n_units:
                col_maxes.append(scores(idx + SCORE_LEAD))
            m_old = m_ref[idx:idx + 1, :]
            m_new = jnp.maximum(m_old, col_maxes[idx])
            m_ref[idx:idx + 1, :] = m_new
            rows = slice(idx * VT_ROWS, (idx + 1) * VT_ROWS)
            vrows = slice((idx % 2) * VT_ROWS, (idx % 2 + 1) * VT_ROWS)
            upd = acc_ref[rows, :] * jnp.exp2(m_old - m_new)
            for pi, (kt, _) in enumerate(parts):
                p = jnp.exp2(s_ref[idx, pi * tk:(pi + 1) * tk, :] - m_new).astype(BF16)
                upd = upd + _dot(vt_ref[kt, vrows, :], p)
            acc_ref[rows, :] = upd

    def finish(branch):
        for idx in range(2 * n_r):
            r, hf = divmod(idx, 2)
            base = idx * VT_ROWS
            l = acc_ref[base + HEAD_DIM:base + HEAD_DIM + 1, :]
            scale = gate_row(branch, idx) * jnp.where(l > 0.0, 1.0 / l, 0.0)
            hs = slice(hf * HEAD_DIM, (hf + 1) * HEAD_DIM)
            osum_ref[r, hs, :] += acc_ref[base:base + HEAD_DIM, :] * scale

    tri_causal, tri_tail, tri_open, tri_closed = 0, 1, 2, 3

    load_queries(qw_ref, with_selection_bias=False)
    reset()
    tiles_back = WINDOW // tk
    win_parts = []
    for back in range(tiles_back, -1, -1):
        kt = jnp.maximum(qt - back, 0)
        inside = tri_causal if back == 0 else (tri_tail if back == tiles_back else tri_open)
        table = inside if back == 0 else jnp.where(qt >= back, inside, tri_closed)
        win_parts.append((kt, lambda s, table=table: s + tri_ref[table]))
    flash_group(qw_ref, kwn_ref, vwnt_ref, win_parts)
    finish(2)

    select_blocks()
    load_queries(qs_ref, with_selection_bias=True)
    reset()
    causal = lambda s: s + tri_ref[tri_causal]
    full_groups = qt // SLC_GROUP

    def slc_full(j, carry):
        flash_group(qs_ref, ksl_ref, vslt_ref, [(SLC_GROUP * j + i, None) for i in range(SLC_GROUP)],
                    indicator=True)
        return carry

    lax.fori_loop(0, full_groups, slc_full, 0)

    for n_last in range(1, SLC_GROUP + 1):
        @pl.when(qt % SLC_GROUP == n_last - 1)
        def _(n_last=n_last):
            first = SLC_GROUP * full_groups
            parts = [(first + i, None) for i in range(n_last - 1)] + [(qt, causal)]
            flash_group(qs_ref, ksl_ref, vslt_ref, parts, indicator=True)

    finish(1)

    for r in range(n_r):
        z = za_ref[:, r * LANES:(r + 1) * LANES].astype(F32)
        out_ref[:, r * LANES:(r + 1) * LANES] = (osum_ref[r].T * _silu(z)).astype(BF16)


def _attn_call(qnt, qrt, kc_std, kc_t, kk, vvt, smt, gb, ovt, tri, za, bsz, seq):
    tq, tk = ATTN_Q, ATTN_K
    n_qt = seq // tq
    n_kt = seq // tk
    qt_spec = pl.BlockSpec((HALF_W, tq), lambda b, c, t: (c, b * n_qt + t))
    out_spec = pl.BlockSpec((tq, HALF_W), lambda b, c, t: (b * n_qt + t, c))
    k_spec = lambda s: pl.BlockSpec((None, seq, LANES), lambda b, c, t: (s, b, c))
    vt_spec = lambda s: pl.BlockSpec((None, n_kt, 2 * VT_ROWS, tk), lambda b, c, t: (s, b, c, 0))
    const = lambda a: pl.BlockSpec(a.shape, lambda b, c, t: (0,) * a.ndim)
    return pl.pallas_call(
        _attn_kernel,
        grid=(bsz, N_PAIR_COLS, n_qt),
        in_specs=[qt_spec, qt_spec,
                  pl.BlockSpec((None, None, LANES, LANES), lambda b, c, t: (0, c, b, 0)),
                  pl.BlockSpec((None, None, LANES, LANES), lambda b, c, t: (1, c, 0, b)),
                  k_spec(0), k_spec(1), vt_spec(0), vt_spec(1),
                  pl.BlockSpec((SMALL_W, tq), lambda b, c, t: (0, b * n_qt + t)),
                  const(gb),
                  const(ovt), const(tri), out_spec],
        out_specs=out_spec,
        out_shape=jax.ShapeDtypeStruct((bsz * seq, ATTN_WIDTH), BF16),
        scratch_shapes=[pltpu.VMEM((2 * Q_PER_KV, LANES, tq), BF16),
                        pltpu.VMEM((2 * Q_PER_KV, LANES, tq), BF16),
                        pltpu.VMEM((2, seq // SLC_BLOCK, tq), F32),
                        pltpu.VMEM((2 * Q_PER_KV, tq), F32),
                        pltpu.VMEM((2 * Q_PER_KV * VT_ROWS, tq), F32),
                        pltpu.VMEM((Q_PER_KV, LANES, tq), F32),
                        pltpu.VMEM((2 * Q_PER_KV, MAX_GROUP * tk, tq), F32),
                        pltpu.VMEM((SLC_GROUP, 2, tk, LANES), BF16),
                        pltpu.VMEM((SMALL_W, tq), F32)],
        compiler_params=pltpu.CompilerParams(dimension_semantics=("arbitrary", "arbitrary", "arbitrary"),
                                             vmem_limit_bytes=VMEM_LIMIT),
        name="attn",
    )(qnt, qrt, kc_std, kc_t, kk, kk, vvt, vvt, smt, gb, ovt, tri, za)


def _ssd_kernel(xbc_ref, sm_ref, cw_ref, cb_ref, dtb_ref, alog_ref, e16_ref, dskip_ref, zs_ref, nw_ref,
                shift_ref, head_ref, out_ref, carry_ref, state_ref, y_ref):
    L = SSD_CHUNK
    j = pl.program_id(1)
    n_prev = CONV_WIDTH - 1

    @pl.when(j == 0)
    def _():
        carry_ref[...] = jnp.zeros(carry_ref.shape, BF16)
        state_ref[...] = jnp.zeros(state_ref.shape, F32)

    u = xbc_ref[...]
    shifted = _dot(shift_ref[...], u)
    conv = cb_ref[...] + u.astype(F32) * cw_ref[n_prev:n_prev + 1, :]
    head = jnp.zeros((SUBLANES, CONV_CH), F32)
    for w in range(n_prev):
        conv = conv + shifted[w * L:(w + 1) * L] * cw_ref[w:w + 1, :]
        head = head + _dot(head_ref[w], carry_ref[...]) * cw_ref[w:w + 1, :]
    conv = jnp.concatenate([conv[0:SUBLANES] + head, conv[SUBLANES:]], axis=0)
    carry_ref[...] = u[L - CARRY_ROWS:L]
    act = _silu(conv)

    sm = sm_ref[...] + dtb_ref[...]
    dt = jnp.maximum(sm, 0.0) + jnp.log1p(jnp.exp(-jnp.abs(sm)))
    a = dt * (-jnp.exp(alog_ref[...]) * LOG2E)
    tril = lax.broadcasted_iota(jnp.int32, (L, L), 0) >= lax.broadcasted_iota(jnp.int32, (L, L), 1)
    a_cs = _select_dot(jnp.where(tril, 1.0, 0.0).astype(BF16), a)
    a_cs_t = a_cs.T
    e16 = e16_ref[...]
    a_exp = _dot_select(a_cs, e16)
    dt_exp = _dot_select(dt, e16)
    a_last = a_exp[L - 1:L, :]
    x_c = act[:, 0:SSM_WIDTH]
    xdt = x_c * dt_exp
    xw = (xdt * jnp.exp2(a_last - a_exp)).astype(BF16)
    ea = jnp.exp2(a_exp)
    chunk_decay = jnp.exp2(a_last)
    lane = lax.broadcasted_iota(jnp.int32, (1, LANES), 1)
    lo = lane < SSM_HEAD_DIM
    halves = (lo, jnp.logical_not(lo))

    for g in range(SSM_GROUPS):
        bg = act[:, SSM_WIDTH + g * SSM_STATE: SSM_WIDTH + (g + 1) * SSM_STATE]
        cg = act[:, SSM_WIDTH + SSM_GROUPS * SSM_STATE + g * SSM_STATE:
                 SSM_WIDTH + SSM_GROUPS * SSM_STATE + (g + 1) * SSM_STATE].astype(BF16)
        cb = _dot_nt(cg, bg.astype(BF16))
        bg_t = bg.T.astype(BF16)
        for i in range(2 * g, 2 * g + 2):
            sl = slice(i * LANES, (i + 1) * LANES)
            y = jnp.zeros((L, LANES), F32)
            for hh in range(2):
                h = 2 * i + hh
                col = a_cs[:, DT_LANE0 + h:DT_LANE0 + h + 1]
                row = a_cs_t[DT_LANE0 + h:DT_LANE0 + h + 1, :]
                decay = jnp.exp2(jnp.where(tril, col - row, NEG))
                xh = jnp.where(halves[hh], xdt[:, sl], 0.0).astype(BF16)
                y = y + _dot((cb * decay).astype(BF16), xh)
            st = state_ref[i]
            y = y + _dot(cg, st.astype(BF16)) * ea[:, sl]
            state_ref[i] = st * chunk_decay[:, sl] + _dot(bg_t, xw[:, sl])
            y_ref[:, sl] = y + dskip_ref[:, sl] * x_c[:, sl]

    y = y_ref[...] * _silu(zs_ref[...].astype(F32))
    y = y * lax.rsqrt(jnp.mean(y * y, axis=-1, keepdims=True) + EPS) * nw_ref[...]
    out_ref[...] = y.astype(BF16)


def _ssd_call(xbc, sm, cw, cb, dtb, alog, e16, dskip, zs, nw, shift, head, bsz, seq):
    L = SSD_CHUNK
    n_ch = seq // L
    row = lambda w: pl.BlockSpec((L, w), lambda b, j: (b * n_ch + j, 0))
    const = lambda a: pl.BlockSpec(a.shape, lambda b, j: (0,) * a.ndim)
    return pl.pallas_call(
        _ssd_kernel,
        grid=(bsz, n_ch),
        in_specs=[row(CONV_CH), row(SMALL_W), const(cw), const(cb), const(dtb), const(alog), const(e16),
                  const(dskip), row(SSM_WIDTH), const(nw), const(shift), const(head)],
        out_specs=row(SSM_WIDTH),
        out_shape=jax.ShapeDtypeStruct((bsz * seq, SSM_WIDTH), BF16),
        scratch_shapes=[pltpu.VMEM((CARRY_ROWS, CONV_CH), BF16),
                        pltpu.VMEM((SSM_HEADS // 2, SSM_STATE, LANES), F32),
                        pltpu.VMEM((L, SSM_WIDTH), F32)],
        compiler_params=pltpu.CompilerParams(dimension_semantics=("arbitrary", "arbitrary"),
                                             vmem_limit_bytes=VMEM_LIMIT),
        name="ssd",
    )(xbc, sm, cw, cb, dtb, alog, e16, dskip, zs, nw, shift, head)


def _out_kernel(att_ref, ssm_ref, x_ref, wa_ref, ws_ref, pw_ref, out_ref):
    o = _dot(att_ref[...], wa_ref[...]) + _dot(ssm_ref[...], ws_ref[...])
    o = o * lax.rsqrt(jnp.mean(o * o, axis=-1, keepdims=True) + EPS) * pw_ref[...]
    out_ref[...] = x_ref[...] + o


def _out_call(att, ssm, x2, wa, ws, pw):
    rows = x2.shape[0]
    tm = OUT_ROWS
    row = lambda w: pl.BlockSpec((tm, w), lambda i: (i, 0))
    return pl.pallas_call(
        _out_kernel,
        grid=(rows // tm,),
        in_specs=[row(ATTN_WIDTH), row(SSM_WIDTH), row(D_MODEL), _resident(wa.shape), _resident(ws.shape),
                  _resident(pw.shape)],
        out_specs=row(D_MODEL),
        out_shape=jax.ShapeDtypeStruct((rows, D_MODEL), F32),
        compiler_params=pltpu.CompilerParams(dimension_semantics=("arbitrary",),
                                             vmem_limit_bytes=VMEM_LIMIT),
        name="outproj",
    )(att, ssm, x2, wa, ws, pw)


def _pair_heads(t, axis, per_head):
    shp = t.shape
    t = t.reshape(shp[:axis] + (N_PAIR_COLS, 2, Q_PER_KV, per_head) + shp[axis + 1:])
    return jnp.swapaxes(t, axis + 1, axis + 2).reshape(shp)


def _branch_major(t, axis):
    shp = t.shape
    t = _pair_heads(t, axis, N_BRANCH).reshape(shp[:axis] + (ATTN_HEADS, N_BRANCH) + shp[axis + 1:])
    return jnp.swapaxes(t, axis, axis + 1).reshape(shp)


def _constants(seq):

    nc = (seq - CMP_BLOCK) // CMP_STRIDE + 1
    nb = seq // SLC_BLOCK
    ci = np.arange(nc)[:, None] * CMP_STRIDE
    bj = np.arange(nb)[None, :] * SLC_BLOCK
    overlap = ((ci <= bj + SLC_BLOCK - 1) & (ci + CMP_BLOCK - 1 >= bj)).astype(np.float32)
    ovt = np.zeros((LANES, LANES), np.float32)
    ovt[:nb, :nc] = overlap.T

    e16 = np.zeros((SMALL_W, SSM_WIDTH), np.float32)
    for h in range(SSM_HEADS):
        e16[DT_LANE0 + h, h * SSM_HEAD_DIM:(h + 1) * SSM_HEAD_DIM] = 1.0

    key = np.arange(ATTN_K)[:, None]
    qry = np.arange(ATTN_Q)[None, :]
    tri = np.stack([np.where(key <= qry, 0.0, NEG), np.where(key > qry, 0.0, NEG),
                    np.zeros((ATTN_K, ATTN_Q)), np.full((ATTN_K, ATTN_Q), NEG)]).astype(np.float32)
    n_prev = CONV_WIDTH - 1
    shift = np.zeros((n_prev * SSD_CHUNK, SSD_CHUNK), np.float32)
    head = np.zeros((n_prev, SUBLANES, CARRY_ROWS), np.float32)
    for w in range(n_prev):
        back = n_prev - w
        for t in range(SSD_CHUNK):
            if t - back >= 0:
                shift[w * SSD_CHUNK + t, t - back] = 1.0
            elif t < SUBLANES:
                head[w, t, CARRY_ROWS + t - back] = 1.0
    return ovt, e16, tri, shift, head


def _rope_tables(seq):
    half_freqs = ROPE_THETA ** (-np.arange(ROPE_HALF, dtype=np.float64) * 2.0 / ROPE_DIM)
    ang = np.arange(seq, dtype=np.float64)[:, None] * half_freqs[None, :]
    cos, sin = np.cos(ang), np.sin(ang)
    ones = np.ones((seq, HEAD_DIM - ROPE_DIM))
    zeros_h = np.zeros((seq, ROPE_HALF))
    zeros_r = np.zeros((seq, HEAD_DIM - ROPE_DIM))
    cos_h = np.concatenate([cos, cos, ones], axis=1)
    sina_h = np.concatenate([-sin, zeros_h, zeros_r], axis=1)
    sinb_h = np.concatenate([zeros_h, sin, zeros_r], axis=1)
    tile2 = lambda t: np.concatenate([t, t], axis=1)
    return tuple(jnp.asarray(t, F32) for t in (tile2(cos_h), tile2(sina_h), tile2(sinb_h), cos.T, sin.T))


def _layer(x, w_in, w_out, pre_w, post_w, cmp_pos, cmp_w1, cmp_b1, cmp_w2, cmp_b2,
           gate_b, conv_w, conv_b, dt_bias, a_log, d_skip, ssm_norm_w):
    bsz, seq, _ = x.shape
    ovt, e16, tri, shift, head = _constants(seq)
    tabs = _rope_tables(seq)

    o_q = 0
    o_kcm = ATTN_WIDTH
    o_ksl = o_kcm + 2 * KV_WIDTH
    o_vsl = o_ksl + KV_WIDTH
    o_kwn = o_vsl + KV_WIDTH
    o_vwn = o_kwn + KV_WIDTH
    o_g = o_vwn + KV_WIDTH
    o_za = o_g + ATTN_HEADS * N_BRANCH
    o_zs = o_za + ATTN_WIDTH
    o_xbc = o_zs + SSM_WIDTH
    o_dt = o_xbc + CONV_CH
    cols = lambda o, n: w_in[:, o:o + n]
    scale = HEAD_DIM ** -0.5 * LOG2E
    bf = lambda t: t.astype(BF16)
    wqt = bf(_pair_heads(cols(o_q, ATTN_WIDTH), 1, HEAD_DIM) * scale).T
    wcm = bf(cols(o_kcm, 2 * KV_WIDTH))
    wk = jnp.concatenate([bf(cols(o_ksl, KV_WIDTH)), bf(cols(o_kwn, KV_WIDTH))], axis=1)
    wvt = jnp.concatenate([bf(cols(o_vsl, KV_WIDTH)), bf(cols(o_vwn, KV_WIDTH))], axis=1).T
    wza = bf(_pair_heads(cols(o_za, ATTN_WIDTH), 1, HEAD_DIM))
    wzs = bf(cols(o_zs, SSM_WIDTH))
    wxbc = bf(cols(o_xbc, CONV_CH))
    n_gate = ATTN_HEADS * N_BRANCH
    wsm = bf(jnp.concatenate([_branch_major(cols(o_g, n_gate), 1), cols(o_dt, SSM_HEADS),
                              jnp.zeros((D_MODEL, SMALL_W - n_gate - SSM_HEADS), F32)], axis=1))
    pad_small = lambda v, at: jnp.zeros((1, SMALL_W), F32).at[0, at:at + v.shape[0]].set(v)
    gb = pad_small(_branch_major(gate_b, 0), 0).T
    dtb = pad_small(dt_bias, DT_LANE0)
    alog = pad_small(a_log, DT_LANE0)

    x2 = x.reshape(bsz * seq, D_MODEL)
    qnt, qrt, cm, kk, vvt, za, zs, xbc, sm, smt = _proj_call(
        x2, pre_w[None, :], wqt, wcm, wk, wvt, wza, wzs, wxbc, wsm, tabs, seq)

    w1r = cmp_w1.reshape(2, 2, CMP_STRIDE, HEAD_DIM, CMP_HIDDEN)
    wa, wb = w1r[:, 0], w1r[:, 1]
    z1 = jnp.zeros_like(wa)
    bd1 = jnp.concatenate([jnp.concatenate([wa, z1, wb, z1], axis=-1),
                           jnp.concatenate([z1, wa, z1, wb], axis=-1)], axis=-2).astype(BF16)
    bd1 = bd1.reshape(2, CMP_STRIDE * LANES, 4 * CMP_HIDDEN)
    z2 = jnp.zeros_like(cmp_w2)
    bd2 = jnp.concatenate([jnp.concatenate([cmp_w2, z2], axis=-1),
                           jnp.concatenate([z2, cmp_w2], axis=-1)], axis=-2).astype(BF16)
    pos8 = jnp.broadcast_to(cmp_pos.reshape(2, 1, CMP_BLOCK * HEAD_DIM), (2, SUBLANES, CMP_BLOCK * HEAD_DIM))
    b2 = jnp.concatenate([cmp_b2, cmp_b2], axis=-1)[:, None, :]
    kc_std, kc_t = _compress_call(cm, bd1, cmp_w1, pos8, cmp_b1[:, None, :], bd2, b2,
                                  jnp.swapaxes(bd2, 1, 2), jnp.swapaxes(b2, 1, 2))

    att = _attn_call(qnt, qrt, kc_std, kc_t, kk, vvt, smt, gb, jnp.asarray(ovt, BF16),
                     jnp.asarray(tri), za, bsz, seq)

    dskip = jnp.repeat(d_skip, SSM_HEAD_DIM)[None, :]
    ssm = _ssd_call(xbc, sm, conv_w, conv_b[None, :], dtb, alog, jnp.asarray(e16, BF16), dskip, zs,
                    ssm_norm_w[None, :], jnp.asarray(shift, BF16), jnp.asarray(head, BF16), bsz, seq)

    wo_a = _pair_heads(w_out[0:ATTN_WIDTH], 0, HEAD_DIM).astype(BF16)
    wo_s = w_out[ATTN_WIDTH:].astype(BF16)
    out = _out_call(att, ssm, x2, wo_a, wo_s, post_w[None, :])
    return out.reshape(bsz, seq, D_MODEL)


def kernel(x, w_in, w_out, pre_norm_w, post_norm_w, cmp_pos, cmp_w1, cmp_b1, cmp_w2, cmp_b2, gate_b, conv_w,
           conv_b, dt_bias, a_log, d_skip, ssm_norm_w):
    for l in range(w_in.shape[0]):
        x = _layer(x, w_in[l], w_out[l], pre_norm_w[l], post_norm_w[l], cmp_pos[l], cmp_w1[l], cmp_b1[l],
                   cmp_w2[l], cmp_b2[l], gate_b[l], conv_w[l], conv_b[l], dt_bias[l], a_log[l], d_skip[l],
                   ssm_norm_w[l])
    return x
```

```python
import numpy as np
import jax
import jax.numpy as jnp
from jax import lax
from jax.experimental import pallas as pl
from jax.experimental.pallas import tpu as pltpu

F32 = jnp.float32
BF16 = jnp.bfloat16

D_MODEL = 1024
ATTN_HEADS = 16
HEAD_DIM = 64
ATTN_WIDTH = ATTN_HEADS * HEAD_DIM
KV_HEADS = 4
Q_PER_KV = ATTN_HEADS // KV_HEADS
KV_WIDTH = KV_HEADS * HEAD_DIM
ROPE_DIM = HEAD_DIM // 4
ROPE_HALF = ROPE_DIM // 2
ROPE_THETA = 500000.0
CMP_BLOCK = 32
CMP_STRIDE = 16
CMP_HIDDEN = 256
SLC_BLOCK = 64
SLC_TOPN = 16
WINDOW = 512
N_BRANCH = 3
SSM_HEADS = 16
SSM_HEAD_DIM = 64
SSM_WIDTH = SSM_HEADS * SSM_HEAD_DIM
SSM_GROUPS = 4
SSM_STATE = 128
CONV_WIDTH = 4
CONV_CH = SSM_WIDTH + 2 * SSM_GROUPS * SSM_STATE
MIX_WIDTH = ATTN_WIDTH + SSM_WIDTH
EPS = 1e-6
NEG = -1e30
BIG = 1e30
M_FLOOR = -1e29
LOG2E = 1.4426950408889634

LANES = 128
SUBLANES = 8
N_PAIR_COLS = KV_HEADS // 2
HALF_W = ATTN_WIDTH // N_PAIR_COLS
SMALL_W = LANES
ONES_ROWS = 16
VT_ROWS = HEAD_DIM + ONES_ROWS
DT_LANE0 = ATTN_HEADS * N_BRANCH
VMEM_LIMIT = 56 * 1024 * 1024

PROJ_ROWS = 512
ATTN_Q = 256
ATTN_K = 256
SLC_GROUP = 4
MAX_GROUP = max(SLC_GROUP, WINDOW // ATTN_K + 1)
SCORE_LEAD = 3
SSD_CHUNK = 256
CARRY_ROWS = 16
OUT_ROWS = 1024


def _dot(a, b):
    return jnp.dot(a, b, preferred_element_type=F32)


def _dot_nt(a, b):
    return lax.dot_general(a, b, (((1,), (1,)), ((), ())), preferred_element_type=F32)


def _split3(x):
    x1 = x.astype(BF16)
    r1 = x - x1.astype(F32)
    x2 = r1.astype(BF16)
    x3 = (r1 - x2.astype(F32)).astype(BF16)
    return x1, x2, x3


def _dot_select(x, sel):
    x1, x2, x3 = _split3(x)
    return _dot(x1, sel) + _dot(x2, sel) + _dot(x3, sel)


def _select_dot(sel, x):
    x1, x2, x3 = _split3(x)
    return _dot(sel, x1) + _dot(sel, x2) + _dot(sel, x3)


def _sigmoid(x):
    return 1.0 / (1.0 + jnp.exp2(x * -LOG2E))


def _silu(x):
    return x * _sigmoid(x)


def _rope_chunk(c, cos_t, sin_a, sin_b):
    return c * cos_t + pltpu.roll(c, LANES - ROPE_HALF, 1) * sin_a + pltpu.roll(c, ROPE_HALF, 1) * sin_b


def _proj_kernel(x_ref, pre_w_ref, wqt_ref, wcm_ref, wk_ref, wvt_ref, wza_ref, wzs_ref, wxbc_ref, wsm_ref,
                 cos_ref, sina_ref, sinb_ref, cos8_ref, sin8_ref,
                 qnt_ref, qrt_ref, cm_ref, kk_ref, vvt_ref, za_ref, zs_ref, xbc_ref, sm_ref, smt_ref, cm_scr):
    x = x_ref[...]
    h = x * lax.rsqrt(jnp.mean(x * x, axis=-1, keepdims=True) + EPS) * pre_w_ref[...]
    h = h.astype(BF16)

    qt = _dot_nt(wqt_ref[...], h)
    qt_bf = qt.astype(BF16)
    qnt_ref[...] = qt_bf
    qrt_ref[...] = qt_bf
    cos8, sin8 = cos8_ref[...], sin8_ref[...]
    for hs in range(ATTN_HEADS):
        base = hs * HEAD_DIM
        t1, t2 = qt[base:base + ROPE_HALF], qt[base + ROPE_HALF:base + ROPE_DIM]
        rot = jnp.concatenate([t1 * cos8 - t2 * sin8, t2 * cos8 + t1 * sin8], axis=0)
        qrt_ref[base:base + ROPE_DIM, :] = rot.astype(BF16)

    cm = _dot(h, wcm_ref[...])
    for k in range(2 * N_PAIR_COLS):
        cm_scr[k] = cm[:, k * LANES:(k + 1) * LANES]
        for t in range(CMP_STRIDE):
            piece = cm_scr[k, pl.ds(t, PROJ_ROWS // CMP_STRIDE, stride=CMP_STRIDE), :]
            cm_ref[k, :, t * LANES:(t + 1) * LANES] = piece.astype(BF16)

    cos_t, sin_a, sin_b = cos_ref[...], sina_ref[...], sinb_ref[...]
    kk = _dot(h, wk_ref[...])
    for t in range(2):
        for k in range(N_PAIR_COLS):
            c = kk[:, t * KV_WIDTH + k * LANES: t * KV_WIDTH + (k + 1) * LANES]
            kk_ref[t, :, k * LANES:(k + 1) * LANES] = _rope_chunk(c, cos_t, sin_a, sin_b).astype(BF16)

    vvt = _dot_nt(wvt_ref[...], h).astype(BF16)
    ones = jnp.ones((ONES_ROWS, ATTN_K), BF16)
    for t in range(2):
        for j in range(PROJ_ROWS // ATTN_K):
            for g in range(KV_HEADS):
                src = t * KV_WIDTH + g * HEAD_DIM
                vvt_ref[t, j, g * VT_ROWS:g * VT_ROWS + HEAD_DIM, :] = vvt[src:src + HEAD_DIM,
                                                                            j * ATTN_K:(j + 1) * ATTN_K]
                vvt_ref[t, j, g * VT_ROWS + HEAD_DIM:(g + 1) * VT_ROWS, :] = ones

    za_ref[...] = _dot(h, wza_ref[...]).astype(BF16)
    zs_ref[...] = _dot(h, wzs_ref[...]).astype(BF16)
    xbc_ref[...] = _dot(h, wxbc_ref[...]).astype(BF16)
    sm = _dot(h, wsm_ref[...])
    sm_ref[...] = sm
    smt_ref[...] = sm.T


def _resident(shape):
    nd = len(shape)
    return pl.BlockSpec(shape, lambda *_: (0,) * nd, pipeline_mode=pl.Buffered(1))


def _proj_call(x2, pre_w, wqt, wcm, wk, wvt, wza, wzs, wxbc, wsm, tabs, seq):
    rows = x2.shape[0]
    tm = PROJ_ROWS
    n_seq_tiles = seq // tm
    k_per_tile = tm // ATTN_K
    row_spec = lambda w: pl.BlockSpec((tm, w), lambda i: (i, 0))
    col_spec = lambda h: pl.BlockSpec((h, tm), lambda i: (0, i))
    tab_spec = pl.BlockSpec((tm, LANES), lambda i: (i % n_seq_tiles, 0))
    tab8_spec = pl.BlockSpec((ROPE_HALF, tm), lambda i: (0, i % n_seq_tiles))
    weights = (pre_w, wqt, wcm, wk, wvt, wza, wzs, wxbc, wsm)
    return pl.pallas_call(
        _proj_kernel,
        grid=(rows // tm,),
        in_specs=[row_spec(D_MODEL)] + [_resident(w.shape) for w in weights]
                 + [tab_spec, tab_spec, tab_spec, tab8_spec, tab8_spec],
        out_specs=[col_spec(ATTN_WIDTH), col_spec(ATTN_WIDTH),
                   pl.BlockSpec((2 * N_PAIR_COLS, tm // CMP_STRIDE, CMP_STRIDE * LANES), lambda i: (0, i, 0)),
                   pl.BlockSpec((2, tm, KV_WIDTH), lambda i: (0, i, 0)),
                   pl.BlockSpec((2, k_per_tile, KV_HEADS * VT_ROWS, ATTN_K), lambda i: (0, i, 0, 0)),
                   row_spec(ATTN_WIDTH), row_spec(SSM_WIDTH), row_spec(CONV_CH), row_spec(SMALL_W),
                   col_spec(SMALL_W)],
        out_shape=[jax.ShapeDtypeStruct((ATTN_WIDTH, rows), BF16),
                   jax.ShapeDtypeStruct((ATTN_WIDTH, rows), BF16),
                   jax.ShapeDtypeStruct((2 * N_PAIR_COLS, rows // CMP_STRIDE, CMP_STRIDE * LANES), BF16),
                   jax.ShapeDtypeStruct((2, rows, KV_WIDTH), BF16),
                   jax.ShapeDtypeStruct((2, rows // ATTN_K, KV_HEADS * VT_ROWS, ATTN_K), BF16),
                   jax.ShapeDtypeStruct((rows, ATTN_WIDTH), BF16),
                   jax.ShapeDtypeStruct((rows, SSM_WIDTH), BF16),
                   jax.ShapeDtypeStruct((rows, CONV_CH), BF16),
                   jax.ShapeDtypeStruct((rows, SMALL_W), F32),
                   jax.ShapeDtypeStruct((SMALL_W, rows), F32)],
        compiler_params=pltpu.CompilerParams(dimension_semantics=("arbitrary",),
                                             vmem_limit_bytes=VMEM_LIMIT),
        scratch_shapes=[pltpu.VMEM((2 * N_PAIR_COLS, tm, LANES), F32)],
        name="proj",
    )(x2, *weights, *tabs)


def _compress_kernel(x_ref, bd1_ref, w1_ref, pos_ref, b1_ref, bd2_ref, b2_ref, bd2t_ref, b2t_ref,
                     out_ref, outt_ref, acc_ref):
    n_rows = x_ref.shape[0]
    hid2 = 2 * CMP_HIDDEN
    acc_ref[n_rows:, :] = jnp.zeros((SUBLANES, 2 * hid2), F32)
    acc_ref[0:n_rows, :] = _dot(x_ref[...], bd1_ref[...])
    first = acc_ref[0:n_rows, 0:hid2]
    second = acc_ref[pl.ds(1, n_rows), hid2:2 * hid2]
    posterm = _dot(pos_ref[...].astype(BF16), w1_ref[...].astype(BF16))[0:1, :] + b1_ref[...]
    hcat = first + second + jnp.concatenate([posterm, posterm], axis=1)
    act = _silu(hcat).astype(BF16)
    out_ref[...] = _dot(act, bd2_ref[...]) + b2_ref[...]
    outt_ref[...] = _dot_nt(bd2t_ref[...], act) + b2t_ref[...]


def _compress_call(cm4, bd1, w1, pos8, b1, bd2, b2, bd2t, b2t):
    n_rows = cm4.shape[1]
    per_s = lambda *tail: pl.BlockSpec((None,) + tail, lambda s, c: (s,) + (0,) * len(tail))
    return pl.pallas_call(
        _compress_kernel,
        grid=(2, N_PAIR_COLS),
        in_specs=[pl.BlockSpec((None, n_rows, CMP_STRIDE * LANES), lambda s, c: (s * N_PAIR_COLS + c, 0, 0)),
                  per_s(CMP_STRIDE * LANES, 4 * CMP_HIDDEN),
                  per_s(CMP_BLOCK * HEAD_DIM, CMP_HIDDEN),
                  per_s(SUBLANES, CMP_BLOCK * HEAD_DIM),
                  per_s(1, CMP_HIDDEN),
                  per_s(2 * CMP_HIDDEN, LANES),
                  per_s(1, LANES),
                  per_s(LANES, 2 * CMP_HIDDEN),
                  per_s(LANES, 1)],
        out_specs=[pl.BlockSpec((None, None, n_rows, LANES), lambda s, c: (s, c, 0, 0)),
                   pl.BlockSpec((None, None, LANES, n_rows), lambda s, c: (s, c, 0, 0))],
        out_shape=[jax.ShapeDtypeStruct((2, N_PAIR_COLS, n_rows, LANES), F32),
                   jax.ShapeDtypeStruct((2, N_PAIR_COLS, LANES, n_rows), F32)],
        scratch_shapes=[pltpu.VMEM((n_rows + SUBLANES, 4 * CMP_HIDDEN), F32)],
        compiler_params=pltpu.CompilerParams(dimension_semantics=("arbitrary", "arbitrary"),
                                             vmem_limit_bytes=VMEM_LIMIT),
        name="compress",
    )(cm4, bd1, w1, pos8, b1, bd2, b2, bd2t, b2t)


def _attn_kernel(qnt_ref, qrt_ref, kc_ref, vct_ref, ksl_ref, kwn_ref, vslt_ref, vwnt_ref, smt_ref, gb_ref,
                 ovt_ref, tri_ref, za_ref, out_ref,
                 qw_ref, qs_ref, bias_ref, m_ref, acc_ref, osum_ref, s_ref, kaug_ref, gate_ref):
    tq, tk = ATTN_Q, ATTN_K
    n_r = Q_PER_KV
    qt = pl.program_id(2)
    q0 = qt * tq
    row = lax.broadcasted_iota(jnp.int32, (LANES, 1), 0)
    halves = (row < HEAD_DIM, row >= HEAD_DIM)
    qpos = q0 + lax.broadcasted_iota(jnp.int32, (1, tq), 1)

    gate_ref[...] = _sigmoid(smt_ref[...] + gb_ref[...])
    slot0 = pl.program_id(1) * (2 * n_r)

    def gate_row(br, idx):
        return gate_ref[pl.ds(br * ATTN_HEADS + slot0 + idx, 1), :]

    kc = kc_ref[...].astype(BF16)
    vct = vct_ref[...].astype(BF16)
    cpos = row * CMP_STRIDE + (CMP_BLOCK - 1)
    cbias = jnp.where(cpos <= qpos, 0.0, NEG)
    psum = [jnp.zeros((LANES, tq), F32), jnp.zeros((LANES, tq), F32)]
    for idx in range(2 * n_r):
        q = qnt_ref[(idx // 2) * LANES:(idx // 2 + 1) * LANES, :]
        qm = jnp.where(halves[idx % 2], q, jnp.zeros_like(q))
        s_ref[idx, 0:LANES, :] = _dot(kc, qm) + cbias
    for idx in range(2 * n_r):
        r, hf = divmod(idx, 2)
        s = s_ref[idx, 0:LANES, :]
        m = jnp.maximum(jnp.max(s, axis=0, keepdims=True), M_FLOOR)
        p = jnp.exp2(s - m)
        l = jnp.sum(p, axis=0, keepdims=True)
        p = p * jnp.where(l > 0.0, 1.0 / l, 0.0)
        psum[hf] = psum[hf] + p
        hs = slice(hf * HEAD_DIM, (hf + 1) * HEAD_DIM)
        osum_ref[r, hs, :] = gate_row(0, idx) * _dot(vct[hs, :], p.astype(BF16))

    nb = ksl_ref.shape[0] // SLC_BLOCK
    jio = lax.broadcasted_iota(jnp.int32, (nb, tq), 0)
    cur = (q0 + lax.broadcasted_iota(jnp.int32, (nb, tq), 1)) // SLC_BLOCK
    forced = (jio == 0) | (jio == cur) | (jio == cur - 1)

    def select_blocks():
        for hf in range(2):
            imp = _select_dot(ovt_ref[...], psum[hf])[0:nb, :]
            imp = jnp.where(forced, BIG, imp)
            imp = jnp.where(jio > cur, -BIG, imp)
            cnt = jnp.zeros((nb, tq), F32)
            for i in range(nb):
                other = imp[i:i + 1, :]
                beats = (other > imp) | ((other == imp) & (jio > i))
                cnt = cnt + jnp.where(beats, 1.0, 0.0)
            bias_ref[hf] = jnp.where(cnt < float(SLC_TOPN), 0.0, NEG)

    def load_queries(qm_ref, with_selection_bias):
        spare = jnp.zeros((HEAD_DIM - nb, tq), BF16)
        for r in range(n_r):
            q = qrt_ref[r * LANES:(r + 1) * LANES, :]
            for hf in range(2):
                if with_selection_bias:
                    fill = [bias_ref[hf].astype(BF16), spare]
                else:
                    fill = [jnp.zeros((HEAD_DIM, tq), BF16)]
                pieces = [q[0:HEAD_DIM]] + fill if hf == 0 else fill + [q[HEAD_DIM:]]
                qm_ref[r * 2 + hf] = jnp.concatenate(pieces, axis=0)

    lane_io = lax.broadcasted_iota(jnp.int32, (1, LANES), 1)
    key_blk = lax.broadcasted_iota(jnp.int32, (tk, 1), 0) // SLC_BLOCK

    def keys_with_block_indicator(k_tile, kt, hf):
        first = 0 if hf == 1 else HEAD_DIM
        onehot = (lane_io - first) == (kt * (tk // SLC_BLOCK) + key_blk)
        keep = (lane_io >= HEAD_DIM) if hf == 1 else (lane_io < HEAD_DIM)
        return jnp.where(keep, k_tile, jnp.where(onehot, 1.0, 0.0).astype(BF16))

    def reset():
        m_ref[...] = jnp.full(m_ref.shape, M_FLOOR, F32)
        acc_ref[...] = jnp.zeros(acc_ref.shape, F32)

    def flash_group(qm_ref, k_ref, vt_ref, parts, indicator=False):
        if indicator:
            for pi, (kt, _) in enumerate(parts):
                k_tile = k_ref[pl.ds(pl.multiple_of(kt * tk, tk), tk), :]
                for hf in range(2):
                    kaug_ref[pi, hf] = keys_with_block_indicator(k_tile, kt, hf)

        def scores(idx):
            col_max = None
            for pi, (kt, bias_fn) in enumerate(parts):
                if indicator:
                    k_tile = kaug_ref[pi, idx % 2]
                else:
                    k_tile = k_ref[pl.ds(pl.multiple_of(kt * tk, tk), tk), :]
                s = _dot(k_tile, qm_ref[idx])
                if bias_fn is not None:
                    s = bias_fn(s)
                s_ref[idx, pi * tk:(pi + 1) * tk, :] = s
                part_max = jnp.max(s, axis=0, keepdims=True)
                col_max = part_max if col_max is None else jnp.maximum(col_max, part_max)
            return col_max

        n_units = 2 * n_r
        col_maxes = [scores(idx) for idx in range(min(SCORE_LEAD, n_units))]
        for idx in range(n_units):
            if idx + SCORE_LEAD < n_units:
                col_maxes.append(scores(idx + SCORE_LEAD))
            m_old = m_ref[idx:idx + 1, :]
            m_new = jnp.maximum(m_old, col_maxes[idx])
            m_ref[idx:idx + 1, :] = m_new
            rows = slice(idx * VT_ROWS, (idx + 1) * VT_ROWS)
            vrows = slice((idx % 2) * VT_ROWS, (idx % 2 + 1) * VT_ROWS)
            upd = acc_ref[rows, :] * jnp.exp2(m_old - m_new)
            for pi, (kt, _) in enumerate(parts):
                p = jnp.exp2(s_ref[idx, pi * tk:(pi + 1) * tk, :] - m_new).astype(BF16)
                upd = upd + _dot(vt_ref[kt, vrows, :], p)
            acc_ref[rows, :] = upd

    def finish(branch):
        for idx in range(2 * n_r):
            r, hf = divmod(idx, 2)
            base = idx * VT_ROWS
            l = acc_ref[base + HEAD_DIM:base + HEAD_DIM + 1, :]
            scale = gate_row(branch, idx) * jnp.where(l > 0.0, 1.0 / l, 0.0)
            hs = slice(hf * HEAD_DIM, (hf + 1) * HEAD_DIM)
            osum_ref[r, hs, :] += acc_ref[base:base + HEAD_DIM, :] * scale

    tri_causal, tri_tail, tri_open, tri_closed = 0, 1, 2, 3
    causal = lambda s: s + tri_ref[tri_causal]

    load_queries(qw_ref, with_selection_bias=False)
    reset()
    tiles_back = WINDOW // tk
    win_parts = []
    for back in range(tiles_back, -1, -1):
        kt = jnp.maximum(qt - back, 0)
        inside = tri_causal if back == 0 else (tri_tail if back == tiles_back else tri_open)
        table = inside if back == 0 else jnp.where(qt >= back, inside, tri_closed)
        win_parts.append((kt, lambda s, table=table: s + tri_ref[table]))
    flash_group(qw_ref, kwn_ref, vwnt_ref, win_parts)
    finish(2)

    select_blocks()
    load_queries(qs_ref, with_selection_bias=True)
    reset()
    full_groups = qt // SLC_GROUP

    def slc_full(j, carry):
        flash_group(qs_ref, ksl_ref, vslt_ref, [(SLC_GROUP * j + i, None) for i in range(SLC_GROUP)],
                    indicator=True)
        return carry

    lax.fori_loop(0, full_groups, slc_full, 0)

    for n_last in range(1, SLC_GROUP + 1):
        @pl.when(qt % SLC_GROUP == n_last - 1)
        def _(n_last=n_last):
            first = SLC_GROUP * full_groups
            parts = [(first + i, None) for i in range(n_last - 1)] + [(qt, causal)]
            flash_group(qs_ref, ksl_ref, vslt_ref, parts, indicator=True)

    finish(1)

    for r in range(n_r):
        z = za_ref[:, r * LANES:(r + 1) * LANES].astype(F32)
        out_ref[:, r * LANES:(r + 1) * LANES] = (osum_ref[r].T * _silu(z)).astype(BF16)


def _attn_call(qnt, qrt, kc_std, kc_t, kk, vvt, smt, gb, ovt, tri, za, bsz, seq):
    tq, tk = ATTN_Q, ATTN_K
    n_qt = seq // tq
    n_kt = seq // tk
    qt_spec = pl.BlockSpec((HALF_W, tq), lambda b, c, t: (c, b * n_qt + t))
    out_spec = pl.BlockSpec((tq, HALF_W), lambda b, c, t: (b * n_qt + t, c))
    k_spec = lambda s: pl.BlockSpec((None, seq, LANES), lambda b, c, t: (s, b, c))
    vt_spec = lambda s: pl.BlockSpec((None, n_kt, 2 * VT_ROWS, tk), lambda b, c, t: (s, b, c, 0))
    const = lambda a: pl.BlockSpec(a.shape, lambda b, c, t: (0,) * a.ndim)
    return pl.pallas_call(
        _attn_kernel,
        grid=(bsz, N_PAIR_COLS, n_qt),
        in_specs=[qt_spec, qt_spec,
                  pl.BlockSpec((None, None, LANES, LANES), lambda b, c, t: (0, c, b, 0)),
                  pl.BlockSpec((None, None, LANES, LANES), lambda b, c, t: (1, c, 0, b)),
                  k_spec(0), k_spec(1), vt_spec(0), vt_spec(1),
                  pl.BlockSpec((SMALL_W, tq), lambda b, c, t: (0, b * n_qt + t)),
                  const(gb),
                  const(ovt), const(tri), out_spec],
        out_specs=out_spec,
        out_shape=jax.ShapeDtypeStruct((bsz * seq, ATTN_WIDTH), BF16),
        scratch_shapes=[pltpu.VMEM((2 * Q_PER_KV, LANES, tq), BF16),
                        pltpu.VMEM((2 * Q_PER_KV, LANES, tq), BF16),
                        pltpu.VMEM((2, seq // SLC_BLOCK, tq), F32),
                        pltpu.VMEM((2 * Q_PER_KV, tq), F32),
                        pltpu.VMEM((2 * Q_PER_KV * VT_ROWS, tq), F32),
                        pltpu.VMEM((Q_PER_KV, LANES, tq), F32),
                        pltpu.VMEM((2 * Q_PER_KV, MAX_GROUP * tk, tq), F32),
                        pltpu.VMEM((SLC_GROUP, 2, tk, LANES), BF16),
                        pltpu.VMEM((SMALL_W, tq), F32)],
        compiler_params=pltpu.CompilerParams(dimension_semantics=("arbitrary", "arbitrary", "arbitrary"),
                                             vmem_limit_bytes=VMEM_LIMIT),
        name="attn",
    )(qnt, qrt, kc_std, kc_t, kk, kk, vvt, vvt, smt, gb, ovt, tri, za)


def _ssd_kernel(xbc_ref, sm_ref, cw_ref, cb_ref, dtb_ref, alog_ref, e16_ref, dskip_ref, zs_ref, nw_ref,
                shift_ref, head_ref, out_ref, carry_ref, state_ref, y_ref):
    L = SSD_CHUNK
    j = pl.program_id(1)
    n_prev = CONV_WIDTH - 1

    @pl.when(j == 0)
    def _():
        carry_ref[...] = jnp.zeros(carry_ref.shape, BF16)
        state_ref[...] = jnp.zeros(state_ref.shape, F32)

    u = xbc_ref[...]
    shifted = _dot(shift_ref[...], u)
    conv = cb_ref[...] + u.astype(F32) * cw_ref[n_prev:n_prev + 1, :]
    head = jnp.zeros((SUBLANES, CONV_CH), F32)
    for w in range(n_prev):
        conv = conv + shifted[w * L:(w + 1) * L] * cw_ref[w:w + 1, :]
        head = head + _dot(head_ref[w], carry_ref[...]) * cw_ref[w:w + 1, :]
    conv = jnp.concatenate([conv[0:SUBLANES] + head, conv[SUBLANES:]], axis=0)
    carry_ref[...] = u[L - CARRY_ROWS:L]
    act = _silu(conv)

    sm = sm_ref[...] + dtb_ref[...]
    dt = jnp.maximum(sm, 0.0) + jnp.log1p(jnp.exp(-jnp.abs(sm)))
    a = dt * (-jnp.exp(alog_ref[...]) * LOG2E)
    tril = lax.broadcasted_iota(jnp.int32, (L, L), 0) >= lax.broadcasted_iota(jnp.int32, (L, L), 1)
    a_cs = _select_dot(jnp.where(tril, 1.0, 0.0).astype(BF16), a)
    a_cs_t = a_cs.T
    e16 = e16_ref[...]
    a_exp = _dot_select(a_cs, e16)
    dt_exp = _dot_select(dt, e16)
    a_last = a_exp[L - 1:L, :]
    x_c = act[:, 0:SSM_WIDTH]
    xdt = x_c * dt_exp
    xdt_bf = xdt.astype(BF16)
    xw = (xdt * jnp.exp2(a_last - a_exp)).astype(BF16)
    ea = jnp.exp2(a_exp)
    chunk_decay = jnp.exp2(a_last)
    lane = lax.broadcasted_iota(jnp.int32, (1, LANES), 1)
    lo = lane < SSM_HEAD_DIM
    halves = (lo, jnp.logical_not(lo))

    for g in range(SSM_GROUPS):
        bg = act[:, SSM_WIDTH + g * SSM_STATE: SSM_WIDTH + (g + 1) * SSM_STATE]
        cg = act[:, SSM_WIDTH + SSM_GROUPS * SSM_STATE + g * SSM_STATE:
                 SSM_WIDTH + SSM_GROUPS * SSM_STATE + (g + 1) * SSM_STATE].astype(BF16)
        cb = _dot_nt(cg, bg.astype(BF16))
        bg_t = bg.T.astype(BF16)
        for i in range(2 * g, 2 * g + 2):
            sl = slice(i * LANES, (i + 1) * LANES)
            y = jnp.zeros((L, LANES), F32)
            for hh in range(2):
                h = 2 * i + hh
                col = a_cs[:, DT_LANE0 + h:DT_LANE0 + h + 1]
                row = a_cs_t[DT_LANE0 + h:DT_LANE0 + h + 1, :]
                decay = jnp.exp2(jnp.where(tril, col - row, NEG))
                xh = jnp.where(halves[hh], xdt_bf[:, sl], jnp.zeros((), BF16))
                y = y + _dot((cb * decay).astype(BF16), xh)
            st = state_ref[i]
            y = y + _dot(cg, st.astype(BF16)) * ea[:, sl]
            state_ref[i] = st * chunk_decay[:, sl] + _dot(bg_t, xw[:, sl])
            y_ref[:, sl] = y + dskip_ref[:, sl] * x_c[:, sl]

    y = y_ref[...] * _silu(zs_ref[...].astype(F32))
    y = y * lax.rsqrt(jnp.mean(y * y, axis=-1, keepdims=True) + EPS) * nw_ref[...]
    out_ref[...] = y.astype(BF16)


def _ssd_call(xbc, sm, cw, cb, dtb, alog, e16, dskip, zs, nw, shift, head, bsz, seq):
    L = SSD_CHUNK
    n_ch = seq // L
    row = lambda w: pl.BlockSpec((L, w), lambda b, j: (b * n_ch + j, 0))
    const = lambda a: pl.BlockSpec(a.shape, lambda b, j: (0,) * a.ndim)
    return pl.pallas_call(
        _ssd_kernel,
        grid=(bsz, n_ch),
        in_specs=[row(CONV_CH), row(SMALL_W), const(cw), const(cb), const(dtb), const(alog), const(e16),
                  const(dskip), row(SSM_WIDTH), const(nw), const(shift), const(head)],
        out_specs=row(SSM_WIDTH),
        out_shape=jax.ShapeDtypeStruct((bsz * seq, SSM_WIDTH), BF16),
        scratch_shapes=[pltpu.VMEM((CARRY_ROWS, CONV_CH), BF16),
                        pltpu.VMEM((SSM_HEADS // 2, SSM_STATE, LANES), F32),
                        pltpu.VMEM((L, SSM_WIDTH), F32)],
        compiler_params=pltpu.CompilerParams(dimension_semantics=("arbitrary", "arbitrary"),
                                             vmem_limit_bytes=VMEM_LIMIT),
        name="ssd",
    )(xbc, sm, cw, cb, dtb, alog, e16, dskip, zs, nw, shift, head)


def _out_kernel(att_ref, ssm_ref, x_ref, wa_ref, ws_ref, pw_ref, out_ref):
    o = _dot(att_ref[...], wa_ref[...]) + _dot(ssm_ref[...], ws_ref[...])
    o = o * lax.rsqrt(jnp.mean(o * o, axis=-1, keepdims=True) + EPS) * pw_ref[...]
    out_ref[...] = x_ref[...] + o


def _out_call(att, ssm, x2, wa, ws, pw):
    rows = x2.shape[0]
    tm = OUT_ROWS
    row = lambda w: pl.BlockSpec((tm, w), lambda i: (i, 0))
    return pl.pallas_call(
        _out_kernel,
        grid=(rows // tm,),
        in_specs=[row(ATTN_WIDTH), row(SSM_WIDTH), row(D_MODEL), _resident(wa.shape), _resident(ws.shape),
                  _resident(pw.shape)],
        out_specs=row(D_MODEL),
        out_shape=jax.ShapeDtypeStruct((rows, D_MODEL), F32),
        compiler_params=pltpu.CompilerParams(dimension_semantics=("arbitrary",),
                                             vmem_limit_bytes=VMEM_LIMIT),
        name="outproj",
    )(att, ssm, x2, wa, ws, pw)


def _pair_heads(t, axis, per_head):
    shp = t.shape
    t = t.reshape(shp[:axis] + (N_PAIR_COLS, 2, Q_PER_KV, per_head) + shp[axis + 1:])
    return jnp.swapaxes(t, axis + 1, axis + 2).reshape(shp)


def _branch_major(t, axis):
    shp = t.shape
    t = _pair_heads(t, axis, N_BRANCH).reshape(shp[:axis] + (ATTN_HEADS, N_BRANCH) + shp[axis + 1:])
    return jnp.swapaxes(t, axis, axis + 1).reshape(shp)


def _constants(seq):

    nc = (seq - CMP_BLOCK) // CMP_STRIDE + 1
    nb = seq // SLC_BLOCK
    ci = np.arange(nc)[:, None] * CMP_STRIDE
    bj = np.arange(nb)[None, :] * SLC_BLOCK
    overlap = ((ci <= bj + SLC_BLOCK - 1) & (ci + CMP_BLOCK - 1 >= bj)).astype(np.float32)
    ovt = np.zeros((LANES, LANES), np.float32)
    ovt[:nb, :nc] = overlap.T

    e16 = np.zeros((SMALL_W, SSM_WIDTH), np.float32)
    for h in range(SSM_HEADS):
        e16[DT_LANE0 + h, h * SSM_HEAD_DIM:(h + 1) * SSM_HEAD_DIM] = 1.0

    key = np.arange(ATTN_K)[:, None]
    qry = np.arange(ATTN_Q)[None, :]
    tri = np.stack([np.where(key <= qry, 0.0, NEG), np.where(key > qry, 0.0, NEG),
                    np.zeros((ATTN_K, ATTN_Q)), np.full((ATTN_K, ATTN_Q), NEG)]).astype(np.float32)
    n_prev = CONV_WIDTH - 1
    shift = np.zeros((n_prev * SSD_CHUNK, SSD_CHUNK), np.float32)
    head = np.zeros((n_prev, SUBLANES, CARRY_ROWS), np.float32)
    for w in range(n_prev):
        back = n_prev - w
        for t in range(SSD_CHUNK):
            if t - back >= 0:
                shift[w * SSD_CHUNK + t, t - back] = 1.0
            elif t < SUBLANES:
                head[w, t, CARRY_ROWS + t - back] = 1.0
    return ovt, e16, tri, shift, head


def _rope_tables(seq):
    half_freqs = ROPE_THETA ** (-np.arange(ROPE_HALF, dtype=np.float64) * 2.0 / ROPE_DIM)
    ang = np.arange(seq, dtype=np.float64)[:, None] * half_freqs[None, :]
    cos, sin = np.cos(ang), np.sin(ang)
    ones = np.ones((seq, HEAD_DIM - ROPE_DIM))
    zeros_h = np.zeros((seq, ROPE_HALF))
    zeros_r = np.zeros((seq, HEAD_DIM - ROPE_DIM))
    cos_h = np.concatenate([cos, cos, ones], axis=1)
    sina_h = np.concatenate([-sin, zeros_h, zeros_r], axis=1)
    sinb_h = np.concatenate([zeros_h, sin, zeros_r], axis=1)
    tile2 = lambda t: np.concatenate([t, t], axis=1)
    return tuple(jnp.asarray(t, F32) for t in (tile2(cos_h), tile2(sina_h), tile2(sinb_h), cos.T, sin.T))


def _layer(x, w_in, w_out, pre_w, post_w, cmp_pos, cmp_w1, cmp_b1, cmp_w2, cmp_b2,
           gate_b, conv_w, conv_b, dt_bias, a_log, d_skip, ssm_norm_w):
    bsz, seq, _ = x.shape
    ovt, e16, tri, shift, head = _constants(seq)
    tabs = _rope_tables(seq)

    o_q = 0
    o_kcm = ATTN_WIDTH
    o_ksl = o_kcm + 2 * KV_WIDTH
    o_vsl = o_ksl + KV_WIDTH
    o_kwn = o_vsl + KV_WIDTH
    o_vwn = o_kwn + KV_WIDTH
    o_g = o_vwn + KV_WIDTH
    o_za = o_g + ATTN_HEADS * N_BRANCH
    o_zs = o_za + ATTN_WIDTH
    o_xbc = o_zs + SSM_WIDTH
    o_dt = o_xbc + CONV_CH
    cols = lambda o, n: w_in[:, o:o + n]
    scale = HEAD_DIM ** -0.5 * LOG2E
    bf = lambda t: t.astype(BF16)
    wqt = bf((_pair_heads(cols(o_q, ATTN_WIDTH), 1, HEAD_DIM) * scale).T)
    wcm = bf(cols(o_kcm, 2 * KV_WIDTH))
    wk = bf(jnp.concatenate([cols(o_ksl, KV_WIDTH), cols(o_kwn, KV_WIDTH)], axis=1))
    wvt = bf(jnp.concatenate([cols(o_vsl, KV_WIDTH), cols(o_vwn, KV_WIDTH)], axis=1).T)
    wza = bf(_pair_heads(cols(o_za, ATTN_WIDTH), 1, HEAD_DIM))
    wzs = bf(cols(o_zs, SSM_WIDTH))
    wxbc = bf(cols(o_xbc, CONV_CH))
    n_gate = ATTN_HEADS * N_BRANCH
    wsm = bf(jnp.concatenate([_branch_major(cols(o_g, n_gate), 1), cols(o_dt, SSM_HEADS),
                              jnp.zeros((D_MODEL, SMALL_W - n_gate - SSM_HEADS), F32)], axis=1))
    pad_small = lambda v, at: jnp.zeros((1, SMALL_W), F32).at[0, at:at + v.shape[0]].set(v)
    gb = pad_small(_branch_major(gate_b, 0), 0).T
    dtb = pad_small(dt_bias, DT_LANE0)
    alog = pad_small(a_log, DT_LANE0)

    x2 = x.reshape(bsz * seq, D_MODEL)
    qnt, qrt, cm, kk, vvt, za, zs, xbc, sm, smt = _proj_call(
        x2, pre_w[None, :], wqt, wcm, wk, wvt, wza, wzs, wxbc, wsm, tabs, seq)

    w1r = cmp_w1.reshape(2, 2, CMP_STRIDE, HEAD_DIM, CMP_HIDDEN)
    wa, wb = w1r[:, 0], w1r[:, 1]
    z1 = jnp.zeros_like(wa)
    bd1 = jnp.concatenate([jnp.concatenate([wa, z1, wb, z1], axis=-1),
                           jnp.concatenate([z1, wa, z1, wb], axis=-1)], axis=-2).astype(BF16)
    bd1 = bd1.reshape(2, CMP_STRIDE * LANES, 4 * CMP_HIDDEN)
    z2 = jnp.zeros_like(cmp_w2)
    bd2 = jnp.concatenate([jnp.concatenate([cmp_w2, z2], axis=-1),
                           jnp.concatenate([z2, cmp_w2], axis=-1)], axis=-2).astype(BF16)
    pos8 = jnp.broadcast_to(cmp_pos.reshape(2, 1, CMP_BLOCK * HEAD_DIM), (2, SUBLANES, CMP_BLOCK * HEAD_DIM))
    b2 = jnp.concatenate([cmp_b2, cmp_b2], axis=-1)[:, None, :]
    kc_std, kc_t = _compress_call(cm, bd1, cmp_w1, pos8, cmp_b1[:, None, :], bd2, b2,
                                  jnp.swapaxes(bd2, 1, 2), jnp.swapaxes(b2, 1, 2))

    att = _attn_call(qnt, qrt, kc_std, kc_t, kk, vvt, smt, gb, jnp.asarray(ovt, BF16),
                     jnp.asarray(tri), za, bsz, seq)

    dskip = jnp.repeat(d_skip, SSM_HEAD_DIM)[None, :]
    ssm = _ssd_call(xbc, sm, conv_w, conv_b[None, :], dtb, alog, jnp.asarray(e16, BF16), dskip, zs,
                    ssm_norm_w[None, :], jnp.asarray(shift, BF16), jnp.asarray(head, BF16), bsz, seq)

    wo_a = _pair_heads(w_out[0:ATTN_WIDTH], 0, HEAD_DIM).astype(BF16)
    wo_s = w_out[ATTN_WIDTH:].astype(BF16)
    out = _out_call(att, ssm, x2, wo_a, wo_s, post_w[None, :])
    return out.reshape(bsz, seq, D_MODEL)


def kernel(x, w_in, w_out, pre_norm_w, post_norm_w, cmp_pos, cmp_w1, cmp_b1, cmp_w2, cmp_b2, gate_b, conv_w,
           conv_b, dt_bias, a_log, d_skip, ssm_norm_w):
    for l in range(w_in.shape[0]):
        x = _layer(x, w_in[l], w_out[l], pre_norm_w[l], post_norm_w[l], cmp_pos[l], cmp_w1[l], cmp_b1[l],
                   cmp_w2[l], cmp_b2[l], gate_b[l], conv_w[l], conv_b[l], dt_bias[l], a_log[l], d_skip[l],
                   ssm_norm_w[l])
    return x
```

```python
import numpy as np
import jax
import jax.numpy as jnp
from jax import lax
from jax.experimental import pallas as pl
from jax.experimental.pallas import tpu as pltpu

F32 = jnp.float32
BF16 = jnp.bfloat16

D_MODEL = 1024
ATTN_HEADS = 16
HEAD_DIM = 64
ATTN_WIDTH = ATTN_HEADS * HEAD_DIM
KV_HEADS = 4
Q_PER_KV = ATTN_HEADS // KV_HEADS
KV_WIDTH = KV_HEADS * HEAD_DIM
ROPE_DIM = HEAD_DIM // 4
ROPE_HALF = ROPE_DIM // 2
ROPE_THETA = 500000.0
CMP_BLOCK = 32
CMP_STRIDE = 16
CMP_HIDDEN = 256
SLC_BLOCK = 64
SLC_TOPN = 16
WINDOW = 512
N_BRANCH = 3
SSM_HEADS = 16
SSM_HEAD_DIM = 64
SSM_WIDTH = SSM_HEADS * SSM_HEAD_DIM
SSM_GROUPS = 4
SSM_STATE = 128
CONV_WIDTH = 4
CONV_CH = SSM_WIDTH + 2 * SSM_GROUPS * SSM_STATE
MIX_WIDTH = ATTN_WIDTH + SSM_WIDTH
EPS = 1e-6
NEG = -1e30
BIG = 1e30
M_FLOOR = -1e29
LOG2E = 1.4426950408889634

LANES = 128
SUBLANES = 8
MXU_WIDTH = 256
N_PAIR_COLS = KV_HEADS // 2
HALF_W = ATTN_WIDTH // N_PAIR_COLS
SMALL_W = LANES
ONES_ROWS = 16
VT_ROWS = HEAD_DIM + ONES_ROWS
DT_LANE0 = ATTN_HEADS * N_BRANCH
VMEM_LIMIT = 56 * 1024 * 1024

PROJ_ROWS = 512
ATTN_Q = 256
ATTN_K = 256
SLC_GROUP = 4
MAX_GROUP = max(SLC_GROUP, WINDOW // ATTN_K + 1)
SCORE_LEAD = 3
SSD_CHUNK = 256
CARRY_ROWS = 16
OUT_ROWS = 1024


def _dot(a, b):
    return jnp.dot(a, b, preferred_element_type=F32)


def _dot_nt(a, b):
    return lax.dot_general(a, b, (((1,), (1,)), ((), ())), preferred_element_type=F32)


def _split3(x):
    x1 = x.astype(BF16)
    r1 = x - x1.astype(F32)
    x2 = r1.astype(BF16)
    x3 = (r1 - x2.astype(F32)).astype(BF16)
    return x1, x2, x3


def _dot_select(x, sel):
    x1, x2, x3 = _split3(x)
    return _dot(x1, sel) + _dot(x2, sel) + _dot(x3, sel)


def _select_dot(sel, x):
    x1, x2, x3 = _split3(x)
    return _dot(sel, x1) + _dot(sel, x2) + _dot(sel, x3)


def _sigmoid(x):
    return 1.0 / (1.0 + jnp.exp2(x * -LOG2E))


def _silu(x):
    return x * _sigmoid(x)


def _rope_chunk(c, cos_t, sin_a, sin_b):
    return c * cos_t + pltpu.roll(c, LANES - ROPE_HALF, 1) * sin_a + pltpu.roll(c, ROPE_HALF, 1) * sin_b


def _proj_kernel(x_ref, pre_w_ref, wqt_ref, wcm_ref, wk_ref, wvt_ref, wza_ref, wzs_ref, wxbc_ref, wsm_ref,
                 cos_ref, sina_ref, sinb_ref, cos8_ref, sin8_ref,
                 qnt_ref, qrt_ref, cm_ref, kk_ref, vvt_ref, za_ref, zs_ref, xbc_ref, sm_ref, smt_ref, cm_scr):
    x = x_ref[...]
    h = x * lax.rsqrt(jnp.mean(x * x, axis=-1, keepdims=True) + EPS) * pre_w_ref[...]
    h = h.astype(BF16)

    qt = _dot_nt(wqt_ref[...], h)
    qt_bf = qt.astype(BF16)
    qnt_ref[...] = qt_bf
    qrt_ref[...] = qt_bf
    cos8, sin8 = cos8_ref[...], sin8_ref[...]
    for hs in range(ATTN_HEADS):
        base = hs * HEAD_DIM
        t1, t2 = qt[base:base + ROPE_HALF], qt[base + ROPE_HALF:base + ROPE_DIM]
        rot = jnp.concatenate([t1 * cos8 - t2 * sin8, t2 * cos8 + t1 * sin8], axis=0)
        qrt_ref[base:base + ROPE_DIM, :] = rot.astype(BF16)

    cm = _dot(h, wcm_ref[...])
    for k in range(2 * N_PAIR_COLS):
        cm_scr[k] = cm[:, k * LANES:(k + 1) * LANES]
        for t in range(CMP_STRIDE):
            piece = cm_scr[k, pl.ds(t, PROJ_ROWS // CMP_STRIDE, stride=CMP_STRIDE), :]
            cm_ref[k, :, t * LANES:(t + 1) * LANES] = piece.astype(BF16)

    cos_t, sin_a, sin_b = cos_ref[...], sina_ref[...], sinb_ref[...]
    kk = _dot(h, wk_ref[...])
    for t in range(2):
        for k in range(N_PAIR_COLS):
            c = kk[:, t * KV_WIDTH + k * LANES: t * KV_WIDTH + (k + 1) * LANES]
            kk_ref[t, :, k * LANES:(k + 1) * LANES] = _rope_chunk(c, cos_t, sin_a, sin_b).astype(BF16)

    vvt = _dot_nt(wvt_ref[...], h).astype(BF16)
    ones = jnp.ones((ONES_ROWS, ATTN_K), BF16)
    for t in range(2):
        for j in range(PROJ_ROWS // ATTN_K):
            for g in range(KV_HEADS):
                src = t * KV_WIDTH + g * HEAD_DIM
                vvt_ref[t, j, g * VT_ROWS:g * VT_ROWS + HEAD_DIM, :] = vvt[src:src + HEAD_DIM,
                                                                            j * ATTN_K:(j + 1) * ATTN_K]
                vvt_ref[t, j, g * VT_ROWS + HEAD_DIM:(g + 1) * VT_ROWS, :] = ones

    za_ref[...] = _dot(h, wza_ref[...]).astype(BF16)
    zs_ref[...] = _dot(h, wzs_ref[...]).astype(BF16)
    xbc_ref[...] = _dot(h, wxbc_ref[...]).astype(BF16)
    sm = _dot(h, wsm_ref[...])
    sm_ref[...] = sm
    smt_ref[...] = sm.T


def _transpose_kernel(w_ref, out_ref):
    out_ref[...] = w_ref[...].T.astype(BF16)


def _transpose_bf16(w):
    k, n = w.shape
    tn = MXU_WIDTH
    return pl.pallas_call(
        _transpose_kernel,
        grid=(n // tn,),
        in_specs=[pl.BlockSpec((k, tn), lambda i: (0, i))],
        out_specs=pl.BlockSpec((tn, k), lambda i: (i, 0)),
        out_shape=jax.ShapeDtypeStruct((n, k), BF16),
        compiler_params=pltpu.CompilerParams(dimension_semantics=("arbitrary",)),
        name="wtranspose",
    )(w)


def _resident(shape):
    nd = len(shape)
    return pl.BlockSpec(shape, lambda *_: (0,) * nd, pipeline_mode=pl.Buffered(1))


def _proj_call(x2, pre_w, wqt, wcm, wk, wvt, wza, wzs, wxbc, wsm, tabs, seq):
    rows = x2.shape[0]
    tm = PROJ_ROWS
    n_seq_tiles = seq // tm
    k_per_tile = tm // ATTN_K
    row_spec = lambda w: pl.BlockSpec((tm, w), lambda i: (i, 0))
    col_spec = lambda h: pl.BlockSpec((h, tm), lambda i: (0, i))
    tab_spec = pl.BlockSpec((tm, LANES), lambda i: (i % n_seq_tiles, 0))
    tab8_spec = pl.BlockSpec((ROPE_HALF, tm), lambda i: (0, i % n_seq_tiles))
    weights = (pre_w, wqt, wcm, wk, wvt, wza, wzs, wxbc, wsm)
    return pl.pallas_call(
        _proj_kernel,
        grid=(rows // tm,),
        in_specs=[row_spec(D_MODEL)] + [_resident(w.shape) for w in weights]
                 + [tab_spec, tab_spec, tab_spec, tab8_spec, tab8_spec],
        out_specs=[col_spec(ATTN_WIDTH), col_spec(ATTN_WIDTH),
                   pl.BlockSpec((2 * N_PAIR_COLS, tm // CMP_STRIDE, CMP_STRIDE * LANES), lambda i: (0, i, 0)),
                   pl.BlockSpec((2, tm, KV_WIDTH), lambda i: (0, i, 0)),
                   pl.BlockSpec((2, k_per_tile, KV_HEADS * VT_ROWS, ATTN_K), lambda i: (0, i, 0, 0)),
                   row_spec(ATTN_WIDTH), row_spec(SSM_WIDTH), row_spec(CONV_CH), row_spec(SMALL_W),
                   col_spec(SMALL_W)],
        out_shape=[jax.ShapeDtypeStruct((ATTN_WIDTH, rows), BF16),
                   jax.ShapeDtypeStruct((ATTN_WIDTH, rows), BF16),
                   jax.ShapeDtypeStruct((2 * N_PAIR_COLS, rows // CMP_STRIDE, CMP_STRIDE * LANES), BF16),
                   jax.ShapeDtypeStruct((2, rows, KV_WIDTH), BF16),
                   jax.ShapeDtypeStruct((2, rows // ATTN_K, KV_HEADS * VT_ROWS, ATTN_K), BF16),
                   jax.ShapeDtypeStruct((rows, ATTN_WIDTH), BF16),
                   jax.ShapeDtypeStruct((rows, SSM_WIDTH), BF16),
                   jax.ShapeDtypeStruct((rows, CONV_CH), BF16),
                   jax.ShapeDtypeStruct((rows, SMALL_W), F32),
                   jax.ShapeDtypeStruct((SMALL_W, rows), F32)],
        compiler_params=pltpu.CompilerParams(dimension_semantics=("arbitrary",),
                                             vmem_limit_bytes=VMEM_LIMIT),
        scratch_shapes=[pltpu.VMEM((2 * N_PAIR_COLS, tm, LANES), F32)],
        name="proj",
    )(x2, *weights, *tabs)


def _compress_kernel(x_ref, bd1_ref, w1_ref, pos_ref, b1_ref, bd2_ref, b2_ref, bd2t_ref, b2t_ref,
                     out_ref, outt_ref, acc_ref):
    n_rows = x_ref.shape[0]
    hid2 = 2 * CMP_HIDDEN
    acc_ref[n_rows:, :] = jnp.zeros((SUBLANES, 2 * hid2), F32)
    acc_ref[0:n_rows, :] = _dot(x_ref[...], bd1_ref[...])
    first = acc_ref[0:n_rows, 0:hid2]
    second = acc_ref[pl.ds(1, n_rows), hid2:2 * hid2]
    posterm = _dot(pos_ref[...].astype(BF16), w1_ref[...].astype(BF16))[0:1, :] + b1_ref[...]
    hcat = first + second + jnp.concatenate([posterm, posterm], axis=1)
    act = _silu(hcat).astype(BF16)
    out_ref[...] = _dot(act, bd2_ref[...]) + b2_ref[...]
    outt_ref[...] = _dot_nt(bd2t_ref[...], act) + b2t_ref[...]


def _compress_call(cm4, bd1, w1, pos8, b1, bd2, b2, bd2t, b2t):
    n_rows = cm4.shape[1]
    per_s = lambda *tail: pl.BlockSpec((None,) + tail, lambda s, c: (s,) + (0,) * len(tail))
    return pl.pallas_call(
        _compress_kernel,
        grid=(2, N_PAIR_COLS),
        in_specs=[pl.BlockSpec((None, n_rows, CMP_STRIDE * LANES), lambda s, c: (s * N_PAIR_COLS + c, 0, 0)),
                  per_s(CMP_STRIDE * LANES, 4 * CMP_HIDDEN),
                  per_s(CMP_BLOCK * HEAD_DIM, CMP_HIDDEN),
                  per_s(SUBLANES, CMP_BLOCK * HEAD_DIM),
                  per_s(1, CMP_HIDDEN),
                  per_s(2 * CMP_HIDDEN, LANES),
                  per_s(1, LANES),
                  per_s(LANES, 2 * CMP_HIDDEN),
                  per_s(LANES, 1)],
        out_specs=[pl.BlockSpec((None, None, n_rows, LANES), lambda s, c: (s, c, 0, 0)),
                   pl.BlockSpec((None, None, LANES, n_rows), lambda s, c: (s, c, 0, 0))],
        out_shape=[jax.ShapeDtypeStruct((2, N_PAIR_COLS, n_rows, LANES), F32),
                   jax.ShapeDtypeStruct((2, N_PAIR_COLS, LANES, n_rows), F32)],
        scratch_shapes=[pltpu.VMEM((n_rows + SUBLANES, 4 * CMP_HIDDEN), F32)],
        compiler_params=pltpu.CompilerParams(dimension_semantics=("arbitrary", "arbitrary"),
                                             vmem_limit_bytes=VMEM_LIMIT),
        name="compress",
    )(cm4, bd1, w1, pos8, b1, bd2, b2, bd2t, b2t)


def _attn_kernel(qnt_ref, qrt_ref, kc_ref, vct_ref, ksl_ref, kwn_ref, vslt_ref, vwnt_ref, smt_ref, gb_ref,
                 ovt_ref, tri_ref, za_ref, out_ref,
                 qw_ref, qs_ref, bias_ref, m_ref, acc_ref, osum_ref, s_ref, kaug_ref, gate_ref):
    tq, tk = ATTN_Q, ATTN_K
    n_r = Q_PER_KV
    qt = pl.program_id(2)
    q0 = qt * tq
    row = lax.broadcasted_iota(jnp.int32, (LANES, 1), 0)
    halves = (row < HEAD_DIM, row >= HEAD_DIM)
    qpos = q0 + lax.broadcasted_iota(jnp.int32, (1, tq), 1)

    gate_ref[...] = _sigmoid(smt_ref[...] + gb_ref[...])
    slot0 = pl.program_id(1) * (2 * n_r)

    def gate_row(br, idx):
        return gate_ref[pl.ds(br * ATTN_HEADS + slot0 + idx, 1), :]

    kc = kc_ref[...].astype(BF16)
    vct = vct_ref[...].astype(BF16)
    cpos = row * CMP_STRIDE + (CMP_BLOCK - 1)
    cbias = jnp.where(cpos <= qpos, 0.0, NEG)
    psum = [jnp.zeros((LANES, tq), F32), jnp.zeros((LANES, tq), F32)]
    for idx in range(2 * n_r):
        q = qnt_ref[(idx // 2) * LANES:(idx // 2 + 1) * LANES, :]
        qm = jnp.where(halves[idx % 2], q, jnp.zeros_like(q))
        s_ref[idx, 0:LANES, :] = _dot(kc, qm) + cbias
    for idx in range(2 * n_r):
        r, hf = divmod(idx, 2)
        s = s_ref[idx, 0:LANES, :]
        m = jnp.maximum(jnp.max(s, axis=0, keepdims=True), M_FLOOR)
        p = jnp.exp2(s - m)
        l = jnp.sum(p, axis=0, keepdims=True)
        p = p * jnp.where(l > 0.0, 1.0 / l, 0.0)
        psum[hf] = psum[hf] + p
        hs = slice(hf * HEAD_DIM, (hf + 1) * HEAD_DIM)
        osum_ref[r, hs, :] = gate_row(0, idx) * _dot(vct[hs, :], p.astype(BF16))

    nb = ksl_ref.shape[0] // SLC_BLOCK
    jio = lax.broadcasted_iota(jnp.int32, (nb, tq), 0)
    cur = (q0 + lax.broadcasted_iota(jnp.int32, (nb, tq), 1)) // SLC_BLOCK
    forced = (jio == 0) | (jio == cur) | (jio == cur - 1)

    def select_blocks():
        for hf in range(2):
            imp = _select_dot(ovt_ref[...], psum[hf])[0:nb, :]
            imp = jnp.where(forced, BIG, imp)
            imp = jnp.where(jio > cur, -BIG, imp)
            cnt = jnp.zeros((nb, tq), F32)
            for i in range(nb):
                other = imp[i:i + 1, :]
                beats = (other > imp) | ((other == imp) & (jio > i))
                cnt = cnt + jnp.where(beats, 1.0, 0.0)
            bias_ref[hf] = jnp.where(cnt < float(SLC_TOPN), 0.0, NEG)

    def load_queries(qm_ref, with_selection_bias):
        spare = jnp.zeros((HEAD_DIM - nb, tq), BF16)
        for r in range(n_r):
            q = qrt_ref[r * LANES:(r + 1) * LANES, :]
            for hf in range(2):
                if with_selection_bias:
                    fill = [bias_ref[hf].astype(BF16), spare]
                else:
                    fill = [jnp.zeros((HEAD_DIM, tq), BF16)]
                pieces = [q[0:HEAD_DIM]] + fill if hf == 0 else fill + [q[HEAD_DIM:]]
                qm_ref[r * 2 + hf] = jnp.concatenate(pieces, axis=0)

    lane_io = lax.broadcasted_iota(jnp.int32, (1, LANES), 1)
    key_blk = lax.broadcasted_iota(jnp.int32, (tk, 1), 0) // SLC_BLOCK

    def keys_with_block_indicator(k_tile, kt, hf):
        first = 0 if hf == 1 else HEAD_DIM
        onehot = (lane_io - first) == (kt * (tk // SLC_BLOCK) + key_blk)
        keep = (lane_io >= HEAD_DIM) if hf == 1 else (lane_io < HEAD_DIM)
        return jnp.where(keep, k_tile, jnp.where(onehot, 1.0, 0.0).astype(BF16))

    def reset():
        m_ref[...] = jnp.full(m_ref.shape, M_FLOOR, F32)
        acc_ref[...] = jnp.zeros(acc_ref.shape, F32)

    def flash_group(qm_ref, k_ref, vt_ref, parts, indicator=False):
        if indicator:
            for pi, (kt, _) in enumerate(parts):
                k_tile = k_ref[pl.ds(pl.multiple_of(kt * tk, tk), tk), :]
                for hf in range(2):
                    kaug_ref[pi, hf] = keys_with_block_indicator(k_tile, kt, hf)

        def scores(idx):
            col_max = None
            for pi, (kt, bias_fn) in enumerate(parts):
                if indicator:
                    k_tile = kaug_ref[pi, idx % 2]
                else:
                    k_tile = k_ref[pl.ds(pl.multiple_of(kt * tk, tk), tk), :]
                s = _dot(k_tile, qm_ref[idx])
                if bias_fn is not None:
                    s = bias_fn(s)
                s_ref[idx, pi * tk:(pi + 1) * tk, :] = s
                part_max = jnp.max(s, axis=0, keepdims=True)
                col_max = part_max if col_max is None else jnp.maximum(col_max, part_max)
            return col_max

        n_units = 2 * n_r
        col_maxes = [scores(idx) for idx in range(min(SCORE_LEAD, n_units))]
        for idx in range(n_units):
            if idx + SCORE_LEAD < n_units:
                col_maxes.append(scores(idx + SCORE_LEAD))
            m_old = m_ref[idx:idx + 1, :]
            m_new = jnp.maximum(m_old, col_maxes[idx])
            m_ref[idx:idx + 1, :] = m_new
            rows = slice(idx * VT_ROWS, (idx + 1) * VT_ROWS)
            vrows = slice((idx % 2) * VT_ROWS, (idx % 2 + 1) * VT_ROWS)
            upd = acc_ref[rows, :] * jnp.exp2(m_old - m_new)
            for pi, (kt, _) in enumerate(parts):
                p = jnp.exp2(s_ref[idx, pi * tk:(pi + 1) * tk, :] - m_new).astype(BF16)
                upd = upd + _dot(vt_ref[kt, vrows, :], p)
            acc_ref[rows, :] = upd

    def finish(branch):
        for idx in range(2 * n_r):
            r, hf = divmod(idx, 2)
            base = idx * VT_ROWS
            l = acc_ref[base + HEAD_DIM:base + HEAD_DIM + 1, :]
            scale = gate_row(branch, idx) * jnp.where(l > 0.0, 1.0 / l, 0.0)
            hs = slice(hf * HEAD_DIM, (hf + 1) * HEAD_DIM)
            osum_ref[r, hs, :] += acc_ref[base:base + HEAD_DIM, :] * scale

    tri_causal, tri_tail, tri_open, tri_closed = 0, 1, 2, 3
    causal = lambda s: s + tri_ref[tri_causal]

    load_queries(qw_ref, with_selection_bias=False)
    reset()
    tiles_back = WINDOW // tk
    win_parts = []
    for back in range(tiles_back, -1, -1):
        kt = jnp.maximum(qt - back, 0)
        inside = tri_causal if back == 0 else (tri_tail if back == tiles_back else tri_open)
        table = inside if back == 0 else jnp.where(qt >= back, inside, tri_closed)
        win_parts.append((kt, lambda s, table=table: s + tri_ref[table]))
    flash_group(qw_ref, kwn_ref, vwnt_ref, win_parts)
    finish(2)

    select_blocks()
    load_queries(qs_ref, with_selection_bias=True)
    reset()
    full_groups = qt // SLC_GROUP

    def slc_full(j, carry):
        flash_group(qs_ref, ksl_ref, vslt_ref, [(SLC_GROUP * j + i, None) for i in range(SLC_GROUP)],
                    indicator=True)
        return carry

    lax.fori_loop(0, full_groups, slc_full, 0)

    for n_last in range(1, SLC_GROUP + 1):
        @pl.when(qt % SLC_GROUP == n_last - 1)
        def _(n_last=n_last):
            first = SLC_GROUP * full_groups
            parts = [(first + i, None) for i in range(n_last - 1)] + [(qt, causal)]
            flash_group(qs_ref, ksl_ref, vslt_ref, parts, indicator=True)

    finish(1)

    for r in range(n_r):
        z = za_ref[:, r * LANES:(r + 1) * LANES].astype(F32)
        out_ref[:, r * LANES:(r + 1) * LANES] = (osum_ref[r].T * _silu(z)).astype(BF16)


def _attn_call(qnt, qrt, kc_std, kc_t, kk, vvt, smt, gb, ovt, tri, za, bsz, seq):
    tq, tk = ATTN_Q, ATTN_K
    n_qt = seq // tq
    n_kt = seq // tk
    qt_spec = pl.BlockSpec((HALF_W, tq), lambda b, c, t: (c, b * n_qt + t))
    out_spec = pl.BlockSpec((tq, HALF_W), lambda b, c, t: (b * n_qt + t, c))
    k_spec = lambda s: pl.BlockSpec((None, seq, LANES), lambda b, c, t: (s, b, c))
    vt_spec = lambda s: pl.BlockSpec((None, n_kt, 2 * VT_ROWS, tk), lambda b, c, t: (s, b, c, 0))
    const = lambda a: pl.BlockSpec(a.shape, lambda b, c, t: (0,) * a.ndim)
    return pl.pallas_call(
        _attn_kernel,
        grid=(bsz, N_PAIR_COLS, n_qt),
        in_specs=[qt_spec, qt_spec,
                  pl.BlockSpec((None, None, LANES, LANES), lambda b, c, t: (0, c, b, 0)),
                  pl.BlockSpec((None, None, LANES, LANES), lambda b, c, t: (1, c, 0, b)),
                  k_spec(0), k_spec(1), vt_spec(0), vt_spec(1),
                  pl.BlockSpec((SMALL_W, tq), lambda b, c, t: (0, b * n_qt + t)),
                  const(gb),
                  const(ovt), const(tri), out_spec],
        out_specs=out_spec,
        out_shape=jax.ShapeDtypeStruct((bsz * seq, ATTN_WIDTH), BF16),
        scratch_shapes=[pltpu.VMEM((2 * Q_PER_KV, LANES, tq), BF16),
                        pltpu.VMEM((2 * Q_PER_KV, LANES, tq), BF16),
                        pltpu.VMEM((2, seq // SLC_BLOCK, tq), F32),
                        pltpu.VMEM((2 * Q_PER_KV, tq), F32),
                        pltpu.VMEM((2 * Q_PER_KV * VT_ROWS, tq), F32),
                        pltpu.VMEM((Q_PER_KV, LANES, tq), F32),
                        pltpu.VMEM((2 * Q_PER_KV, MAX_GROUP * tk, tq), F32),
                        pltpu.VMEM((SLC_GROUP, 2, tk, LANES), BF16),
                        pltpu.VMEM((SMALL_W, tq), F32)],
        compiler_params=pltpu.CompilerParams(dimension_semantics=("arbitrary", "arbitrary", "arbitrary"),
                                             vmem_limit_bytes=VMEM_LIMIT),
        name="attn",
    )(qnt, qrt, kc_std, kc_t, kk, kk, vvt, vvt, smt, gb, ovt, tri, za)


def _ssd_kernel(xbc_ref, sm_ref, cw_ref, cb_ref, dtb_ref, alog_ref, e16_ref, dskip_ref, zs_ref, nw_ref,
                shift_ref, head_ref, out_ref, carry_ref, state_ref, y_ref):
    L = SSD_CHUNK
    j = pl.program_id(1)
    n_prev = CONV_WIDTH - 1

    @pl.when(j == 0)
    def _():
        carry_ref[...] = jnp.zeros(carry_ref.shape, BF16)
        state_ref[...] = jnp.zeros(state_ref.shape, F32)

    u = xbc_ref[...]
    shifted = _dot(shift_ref[...], u)
    conv = cb_ref[...] + u.astype(F32) * cw_ref[n_prev:n_prev + 1, :]
    head = jnp.zeros((SUBLANES, CONV_CH), F32)
    for w in range(n_prev):
        conv = conv + shifted[w * L:(w + 1) * L] * cw_ref[w:w + 1, :]
        head = head + _dot(head_ref[w], carry_ref[...]) * cw_ref[w:w + 1, :]
    conv = jnp.concatenate([conv[0:SUBLANES] + head, conv[SUBLANES:]], axis=0)
    carry_ref[...] = u[L - CARRY_ROWS:L]
    act = _silu(conv)

    sm = sm_ref[...] + dtb_ref[...]
    dt = jnp.maximum(sm, 0.0) + jnp.log1p(jnp.exp(-jnp.abs(sm)))
    a = dt * (-jnp.exp(alog_ref[...]) * LOG2E)
    tril = lax.broadcasted_iota(jnp.int32, (L, L), 0) >= lax.broadcasted_iota(jnp.int32, (L, L), 1)
    a_cs = _select_dot(jnp.where(tril, 1.0, 0.0).astype(BF16), a)
    a_cs_t = a_cs.T
    e16 = e16_ref[...]
    a_exp = _dot_select(a_cs, e16)
    dt_exp = _dot_select(dt, e16)
    a_last = a_exp[L - 1:L, :]
    x_c = act[:, 0:SSM_WIDTH]
    xdt = x_c * dt_exp
    xdt_bf = xdt.astype(BF16)
    xw = (xdt * jnp.exp2(a_last - a_exp)).astype(BF16)
    ea = jnp.exp2(a_exp)
    chunk_decay = jnp.exp2(a_last)
    lane = lax.broadcasted_iota(jnp.int32, (1, LANES), 1)
    lo = lane < SSM_HEAD_DIM
    halves = (lo, jnp.logical_not(lo))

    for g in range(SSM_GROUPS):
        bg = act[:, SSM_WIDTH + g * SSM_STATE: SSM_WIDTH + (g + 1) * SSM_STATE]
        cg = act[:, SSM_WIDTH + SSM_GROUPS * SSM_STATE + g * SSM_STATE:
                 SSM_WIDTH + SSM_GROUPS * SSM_STATE + (g + 1) * SSM_STATE].astype(BF16)
        cb = _dot_nt(cg, bg.astype(BF16))
        bg_t = bg.T.astype(BF16)
        for i in range(2 * g, 2 * g + 2):
            sl = slice(i * LANES, (i + 1) * LANES)
            y = jnp.zeros((L, LANES), F32)
            for hh in range(2):
                h = 2 * i + hh
                col = a_cs[:, DT_LANE0 + h:DT_LANE0 + h + 1]
                row = a_cs_t[DT_LANE0 + h:DT_LANE0 + h + 1, :]
                decay = jnp.exp2(jnp.where(tril, col - row, NEG))
                xh = jnp.where(halves[hh], xdt_bf[:, sl], jnp.zeros((), BF16))
                y = y + _dot((cb * decay).astype(BF16), xh)
            st = state_ref[i]
            y = y + _dot(cg, st.astype(BF16)) * ea[:, sl]
            state_ref[i] = st * chunk_decay[:, sl] + _dot(bg_t, xw[:, sl])
            y_ref[:, sl] = y + dskip_ref[:, sl] * x_c[:, sl]

    y = y_ref[...] * _silu(zs_ref[...].astype(F32))
    y = y * lax.rsqrt(jnp.mean(y * y, axis=-1, keepdims=True) + EPS) * nw_ref[...]
    out_ref[...] = y.astype(BF16)


def _ssd_call(xbc, sm, cw, cb, dtb, alog, e16, dskip, zs, nw, shift, head, bsz, seq):
    L = SSD_CHUNK
    n_ch = seq // L
    row = lambda w: pl.BlockSpec((L, w), lambda b, j: (b * n_ch + j, 0))
    const = lambda a: pl.BlockSpec(a.shape, lambda b, j: (0,) * a.ndim)
    return pl.pallas_call(
        _ssd_kernel,
        grid=(bsz, n_ch),
        in_specs=[row(CONV_CH), row(SMALL_W), const(cw), const(cb), const(dtb), const(alog), const(e16),
                  const(dskip), row(SSM_WIDTH), const(nw), const(shift), const(head)],
        out_specs=row(SSM_WIDTH),
        out_shape=jax.ShapeDtypeStruct((bsz * seq, SSM_WIDTH), BF16),
        scratch_shapes=[pltpu.VMEM((CARRY_ROWS, CONV_CH), BF16),
                        pltpu.VMEM((SSM_HEADS // 2, SSM_STATE, LANES), F32),
                        pltpu.VMEM((L, SSM_WIDTH), F32)],
        compiler_params=pltpu.CompilerParams(dimension_semantics=("arbitrary", "arbitrary"),
                                             vmem_limit_bytes=VMEM_LIMIT),
        name="ssd",
    )(xbc, sm, cw, cb, dtb, alog, e16, dskip, zs, nw, shift, head)


def _out_kernel(att_ref, ssm_ref, x_ref, wa_ref, ws_ref, pw_ref, out_ref):
    o = _dot(att_ref[...], wa_ref[...]) + _dot(ssm_ref[...], ws_ref[...])
    o = o * lax.rsqrt(jnp.mean(o * o, axis=-1, keepdims=True) + EPS) * pw_ref[...]
    out_ref[...] = x_ref[...] + o


def _out_call(att, ssm, x2, wa, ws, pw):
    rows = x2.shape[0]
    tm = OUT_ROWS
    row = lambda w: pl.BlockSpec((tm, w), lambda i: (i, 0))
    return pl.pallas_call(
        _out_kernel,
        grid=(rows // tm,),
        in_specs=[row(ATTN_WIDTH), row(SSM_WIDTH), row(D_MODEL), _resident(wa.shape), _resident(ws.shape),
                  _resident(pw.shape)],
        out_specs=row(D_MODEL),
        out_shape=jax.ShapeDtypeStruct((rows, D_MODEL), F32),
        compiler_params=pltpu.CompilerParams(dimension_semantics=("arbitrary",),
                                             vmem_limit_bytes=VMEM_LIMIT),
        name="outproj",
    )(att, ssm, x2, wa, ws, pw)


def _pair_heads(t, axis, per_head):
    shp = t.shape
    t = t.reshape(shp[:axis] + (N_PAIR_COLS, 2, Q_PER_KV, per_head) + shp[axis + 1:])
    return jnp.swapaxes(t, axis + 1, axis + 2).reshape(shp)


def _branch_major(t, axis):
    shp = t.shape
    t = _pair_heads(t, axis, N_BRANCH).reshape(shp[:axis] + (ATTN_HEADS, N_BRANCH) + shp[axis + 1:])
    return jnp.swapaxes(t, axis, axis + 1).reshape(shp)


def _constants(seq):

    nc = (seq - CMP_BLOCK) // CMP_STRIDE + 1
    nb = seq // SLC_BLOCK
    ci = np.arange(nc)[:, None] * CMP_STRIDE
    bj = np.arange(nb)[None, :] * SLC_BLOCK
    overlap = ((ci <= bj + SLC_BLOCK - 1) & (ci + CMP_BLOCK - 1 >= bj)).astype(np.float32)
    ovt = np.zeros((LANES, LANES), np.float32)
    ovt[:nb, :nc] = overlap.T

    e16 = np.zeros((SMALL_W, SSM_WIDTH), np.float32)
    for h in range(SSM_HEADS):
        e16[DT_LANE0 + h, h * SSM_HEAD_DIM:(h + 1) * SSM_HEAD_DIM] = 1.0

    key = np.arange(ATTN_K)[:, None]
    qry = np.arange(ATTN_Q)[None, :]
    tri = np.stack([np.where(key <= qry, 0.0, NEG), np.where(key > qry, 0.0, NEG),
                    np.zeros((ATTN_K, ATTN_Q)), np.full((ATTN_K, ATTN_Q), NEG)]).astype(np.float32)
    n_prev = CONV_WIDTH - 1
    shift = np.zeros((n_prev * SSD_CHUNK, SSD_CHUNK), np.float32)
    head = np.zeros((n_prev, SUBLANES, CARRY_ROWS), np.float32)
    for w in range(n_prev):
        back = n_prev - w
        for t in range(SSD_CHUNK):
            if t - back >= 0:
                shift[w * SSD_CHUNK + t, t - back] = 1.0
            elif t < SUBLANES:
                head[w, t, CARRY_ROWS + t - back] = 1.0
    return ovt, e16, tri, shift, head


def _rope_tables(seq):
    half_freqs = ROPE_THETA ** (-np.arange(ROPE_HALF, dtype=np.float64) * 2.0 / ROPE_DIM)
    ang = np.arange(seq, dtype=np.float64)[:, None] * half_freqs[None, :]
    cos, sin = np.cos(ang), np.sin(ang)
    ones = np.ones((seq, HEAD_DIM - ROPE_DIM))
    zeros_h = np.zeros((seq, ROPE_HALF))
    zeros_r = np.zeros((seq, HEAD_DIM - ROPE_DIM))
    cos_h = np.concatenate([cos, cos, ones], axis=1)
    sina_h = np.concatenate([-sin, zeros_h, zeros_r], axis=1)
    sinb_h = np.concatenate([zeros_h, sin, zeros_r], axis=1)
    tile2 = lambda t: np.concatenate([t, t], axis=1)
    return tuple(jnp.asarray(t, F32) for t in (tile2(cos_h), tile2(sina_h), tile2(sinb_h), cos.T, sin.T))


def _layer(x, w_in, w_out, pre_w, post_w, cmp_pos, cmp_w1, cmp_b1, cmp_w2, cmp_b2,
           gate_b, conv_w, conv_b, dt_bias, a_log, d_skip, ssm_norm_w):
    bsz, seq, _ = x.shape
    ovt, e16, tri, shift, head = _constants(seq)
    tabs = _rope_tables(seq)

    o_q = 0
    o_kcm = ATTN_WIDTH
    o_ksl = o_kcm + 2 * KV_WIDTH
    o_vsl = o_ksl + KV_WIDTH
    o_kwn = o_vsl + KV_WIDTH
    o_vwn = o_kwn + KV_WIDTH
    o_g = o_vwn + KV_WIDTH
    o_za = o_g + ATTN_HEADS * N_BRANCH
    o_zs = o_za + ATTN_WIDTH
    o_xbc = o_zs + SSM_WIDTH
    o_dt = o_xbc + CONV_CH
    cols = lambda o, n: w_in[:, o:o + n]
    scale = HEAD_DIM ** -0.5 * LOG2E
    bf = lambda t: t.astype(BF16)
    wqt = _transpose_bf16(_pair_heads(cols(o_q, ATTN_WIDTH), 1, HEAD_DIM) * scale)
    wcm = bf(cols(o_kcm, 2 * KV_WIDTH))
    wk = bf(jnp.concatenate([cols(o_ksl, KV_WIDTH), cols(o_kwn, KV_WIDTH)], axis=1))
    wvt = _transpose_bf16(jnp.concatenate([cols(o_vsl, KV_WIDTH), cols(o_vwn, KV_WIDTH)], axis=1))
    wza = bf(_pair_heads(cols(o_za, ATTN_WIDTH), 1, HEAD_DIM))
    wzs = bf(cols(o_zs, SSM_WIDTH))
    wxbc = bf(cols(o_xbc, CONV_CH))
    n_gate = ATTN_HEADS * N_BRANCH
    wsm = bf(jnp.concatenate([_branch_major(cols(o_g, n_gate), 1), cols(o_dt, SSM_HEADS),
                              jnp.zeros((D_MODEL, SMALL_W - n_gate - SSM_HEADS), F32)], axis=1))
    pad_small = lambda v, at: jnp.zeros((1, SMALL_W), F32).at[0, at:at + v.shape[0]].set(v)
    gb = pad_small(_branch_major(gate_b, 0), 0).T
    dtb = pad_small(dt_bias, DT_LANE0)
    alog = pad_small(a_log, DT_LANE0)

    x2 = x.reshape(bsz * seq, D_MODEL)
    qnt, qrt, cm, kk, vvt, za, zs, xbc, sm, smt = _proj_call(
        x2, pre_w[None, :], wqt, wcm, wk, wvt, wza, wzs, wxbc, wsm, tabs, seq)

    w1r = cmp_w1.reshape(2, 2, CMP_STRIDE, HEAD_DIM, CMP_HIDDEN)
    wa, wb = w1r[:, 0], w1r[:, 1]
    z1 = jnp.zeros_like(wa)
    bd1 = jnp.concatenate([jnp.concatenate([wa, z1, wb, z1], axis=-1),
                           jnp.concatenate([z1, wa, z1, wb], axis=-1)], axis=-2).astype(BF16)
    bd1 = bd1.reshape(2, CMP_STRIDE * LANES, 4 * CMP_HIDDEN)
    z2 = jnp.zeros_like(cmp_w2)
    bd2 = jnp.concatenate([jnp.concatenate([cmp_w2, z2], axis=-1),
                           jnp.concatenate([z2, cmp_w2], axis=-1)], axis=-2).astype(BF16)
    pos8 = jnp.broadcast_to(cmp_pos.reshape(2, 1, CMP_BLOCK * HEAD_DIM), (2, SUBLANES, CMP_BLOCK * HEAD_DIM))
    b2 = jnp.concatenate([cmp_b2, cmp_b2], axis=-1)[:, None, :]
    kc_std, kc_t = _compress_call(cm, bd1, cmp_w1, pos8, cmp_b1[:, None, :], bd2, b2,
                                  jnp.swapaxes(bd2, 1, 2), jnp.swapaxes(b2, 1, 2))

    att = _attn_call(qnt, qrt, kc_std, kc_t, kk, vvt, smt, gb, jnp.asarray(ovt, BF16),
                     jnp.asarray(tri), za, bsz, seq)

    dskip = jnp.repeat(d_skip, SSM_HEAD_DIM)[None, :]
    ssm = _ssd_call(xbc, sm, conv_w, conv_b[None, :], dtb, alog, jnp.asarray(e16, BF16), dskip, zs,
                    ssm_norm_w[None, :], jnp.asarray(shift, BF16), jnp.asarray(head, BF16), bsz, seq)

    wo_a = _pair_heads(w_out[0:ATTN_WIDTH], 0, HEAD_DIM).astype(BF16)
    wo_s = w_out[ATTN_WIDTH:].astype(BF16)
    out = _out_call(att, ssm, x2, wo_a, wo_s, post_w[None, :])
    return out.reshape(bsz, seq, D_MODEL)


def kernel(x, w_in, w_out, pre_norm_w, post_norm_w, cmp_pos, cmp_w1, cmp_b1, cmp_w2, cmp_b2, gate_b, conv_w,
           conv_b, dt_bias, a_log, d_skip, ssm_norm_w):
    for l in range(w_in.shape[0]):
        x = _layer(x, w_in[l], w_out[l], pre_norm_w[l], post_norm_w[l], cmp_pos[l], cmp_w1[l], cmp_b1[l],
                   cmp_w2[l], cmp_b2[l], gate_b[l], conv_w[l], conv_b[l], dt_bias[l], a_log[l], d_skip[l],
                   ssm_norm_w[l])
    return x
```

```python
import numpy as np
import jax
import jax.numpy as jnp
from jax import lax
from jax.experimental import pallas as pl
from jax.experimental.pallas import tpu as pltpu

F32 = jnp.float32
BF16 = jnp.bfloat16

D_MODEL = 1024
ATTN_HEADS = 16
HEAD_DIM = 64
ATTN_WIDTH = ATTN_HEADS * HEAD_DIM
KV_HEADS = 4
Q_PER_KV = ATTN_HEADS // KV_HEADS
KV_WIDTH = KV_HEADS * HEAD_DIM
ROPE_DIM = HEAD_DIM // 4
ROPE_HALF = ROPE_DIM // 2
ROPE_THETA = 500000.0
CMP_BLOCK = 32
CMP_STRIDE = 16
CMP_HIDDEN = 256
SLC_BLOCK = 64
SLC_TOPN = 16
WINDOW = 512
N_BRANCH = 3
SSM_HEADS = 16
SSM_HEAD_DIM = 64
SSM_WIDTH = SSM_HEADS * SSM_HEAD_DIM
SSM_GROUPS = 4
SSM_STATE = 128
CONV_WIDTH = 4
CONV_CH = SSM_WIDTH + 2 * SSM_GROUPS * SSM_STATE
MIX_WIDTH = ATTN_WIDTH + SSM_WIDTH
EPS = 1e-6
NEG = -1e30
BIG = 1e30
M_FLOOR = -1e29
LOG2E = 1.4426950408889634
Q_SCALE = HEAD_DIM ** -0.5 * LOG2E

LANES = 128
SUBLANES = 8
N_PAIR_COLS = KV_HEADS // 2
HALF_W = ATTN_WIDTH // N_PAIR_COLS
SMALL_W = LANES
ONES_ROWS = 16
VT_ROWS = HEAD_DIM + ONES_ROWS
DT_LANE0 = ATTN_HEADS * N_BRANCH
VMEM_LIMIT = 56 * 1024 * 1024

PROJ_ROWS = 512
ATTN_Q = 256
ATTN_K = 256
SLC_GROUP = 4
MAX_GROUP = max(SLC_GROUP, WINDOW // ATTN_K + 1)
SCORE_LEAD = 3
SSD_CHUNK = 256
CARRY_ROWS = 16
OUT_ROWS = 1024


def _dot(a, b):
    return jnp.dot(a, b, preferred_element_type=F32)


def _dot_nt(a, b):
    return lax.dot_general(a, b, (((1,), (1,)), ((), ())), preferred_element_type=F32)


def _split3(x):
    x1 = x.astype(BF16)
    r1 = x - x1.astype(F32)
    x2 = r1.astype(BF16)
    x3 = (r1 - x2.astype(F32)).astype(BF16)
    return x1, x2, x3


def _dot_select(x, sel):
    x1, x2, x3 = _split3(x)
    return _dot(x1, sel) + _dot(x2, sel) + _dot(x3, sel)


def _select_dot(sel, x):
    x1, x2, x3 = _split3(x)
    return _dot(sel, x1) + _dot(sel, x2) + _dot(sel, x3)


def _sigmoid(x):
    return 1.0 / (1.0 + jnp.exp2(x * -LOG2E))


def _silu(x):
    return x * _sigmoid(x)


def _rope_chunk(c, cos_t, sin_a, sin_b):
    return c * cos_t + pltpu.roll(c, LANES - ROPE_HALF, 1) * sin_a + pltpu.roll(c, ROPE_HALF, 1) * sin_b


def _proj_kernel(x_ref, pre_w_ref, wqt_ref, wcm_ref, wk_ref, wvt_ref, wza_ref, wzs_ref, wxbc_ref, wsm_ref,
                 cos_ref, sina_ref, sinb_ref, cos8_ref, sin8_ref,
                 qnt_ref, qrt_ref, cm_ref, kk_ref, vvt_ref, za_ref, zs_ref, xbc_ref, sm_ref, smt_ref, cm_scr):
    x = x_ref[...]
    h = x * lax.rsqrt(jnp.mean(x * x, axis=-1, keepdims=True) + EPS) * pre_w_ref[...]
    h = h.astype(BF16)

    qt = _dot_nt(wqt_ref[...], h)
    qt_bf = qt.astype(BF16)
    qnt_ref[...] = qt_bf
    qrt_ref[...] = qt_bf
    cos8, sin8 = cos8_ref[...], sin8_ref[...]
    for hs in range(ATTN_HEADS):
        base = hs * HEAD_DIM
        t1, t2 = qt[base:base + ROPE_HALF], qt[base + ROPE_HALF:base + ROPE_DIM]
        rot = jnp.concatenate([t1 * cos8 - t2 * sin8, t2 * cos8 + t1 * sin8], axis=0)
        qrt_ref[base:base + ROPE_DIM, :] = rot.astype(BF16)

    cm = _dot(h, wcm_ref[...])
    for k in range(2 * N_PAIR_COLS):
        cm_scr[k] = cm[:, k * LANES:(k + 1) * LANES]
        for t in range(CMP_STRIDE):
            piece = cm_scr[k, pl.ds(t, PROJ_ROWS // CMP_STRIDE, stride=CMP_STRIDE), :]
            cm_ref[k, :, t * LANES:(t + 1) * LANES] = piece.astype(BF16)

    cos_t, sin_a, sin_b = cos_ref[...], sina_ref[...], sinb_ref[...]
    kk = _dot(h, wk_ref[...])
    for t in range(2):
        for k in range(N_PAIR_COLS):
            c = kk[:, t * KV_WIDTH + k * LANES: t * KV_WIDTH + (k + 1) * LANES]
            kk_ref[t, :, k * LANES:(k + 1) * LANES] = _rope_chunk(c, cos_t, sin_a, sin_b).astype(BF16)

    vvt = _dot_nt(wvt_ref[...], h).astype(BF16)
    ones = jnp.ones((ONES_ROWS, ATTN_K), BF16)
    for t in range(2):
        for j in range(PROJ_ROWS // ATTN_K):
            for g in range(KV_HEADS):
                src = t * KV_WIDTH + g * HEAD_DIM
                vvt_ref[t, j, g * VT_ROWS:g * VT_ROWS + HEAD_DIM, :] = vvt[src:src + HEAD_DIM,
                                                                            j * ATTN_K:(j + 1) * ATTN_K]
                vvt_ref[t, j, g * VT_ROWS + HEAD_DIM:(g + 1) * VT_ROWS, :] = ones

    za_ref[...] = _dot(h, wza_ref[...]).astype(BF16)
    zs_ref[...] = _dot(h, wzs_ref[...]).astype(BF16)
    xbc_ref[...] = _dot(h, wxbc_ref[...]).astype(BF16)
    sm = _dot(h, wsm_ref[...])
    sm_ref[...] = sm
    smt_ref[...] = sm.T


def _cast_kernel(w_ref, out_ref):
    out_ref[:, 0:ATTN_WIDTH] = (w_ref[:, 0:ATTN_WIDTH] * Q_SCALE).astype(BF16)
    out_ref[:, ATTN_WIDTH:] = w_ref[:, ATTN_WIDTH:].astype(BF16)


def _cast_call(w_in):
    rows, width = w_in.shape
    tr = LANES
    spec = pl.BlockSpec((tr, width), lambda i: (i, 0))
    return pl.pallas_call(
        _cast_kernel,
        grid=(rows // tr,),
        in_specs=[spec],
        out_specs=spec,
        out_shape=jax.ShapeDtypeStruct((rows, width), BF16),
        compiler_params=pltpu.CompilerParams(dimension_semantics=("arbitrary",)),
        name="wcast",
    )(w_in)


def _resident(shape):
    nd = len(shape)
    return pl.BlockSpec(shape, lambda *_: (0,) * nd, pipeline_mode=pl.Buffered(1))


def _proj_call(x2, pre_w, wqt, wcm, wk, wvt, wza, wzs, wxbc, wsm, tabs, seq):
    rows = x2.shape[0]
    tm = PROJ_ROWS
    n_seq_tiles = seq // tm
    k_per_tile = tm // ATTN_K
    row_spec = lambda w: pl.BlockSpec((tm, w), lambda i: (i, 0))
    col_spec = lambda h: pl.BlockSpec((h, tm), lambda i: (0, i))
    tab_spec = pl.BlockSpec((tm, LANES), lambda i: (i % n_seq_tiles, 0))
    tab8_spec = pl.BlockSpec((ROPE_HALF, tm), lambda i: (0, i % n_seq_tiles))
    weights = (pre_w, wqt, wcm, wk, wvt, wza, wzs, wxbc, wsm)
    return pl.pallas_call(
        _proj_kernel,
        grid=(rows // tm,),
        in_specs=[row_spec(D_MODEL)] + [_resident(w.shape) for w in weights]
                 + [tab_spec, tab_spec, tab_spec, tab8_spec, tab8_spec],
        out_specs=[col_spec(ATTN_WIDTH), col_spec(ATTN_WIDTH),
                   pl.BlockSpec((2 * N_PAIR_COLS, tm // CMP_STRIDE, CMP_STRIDE * LANES), lambda i: (0, i, 0)),
                   pl.BlockSpec((2, tm, KV_WIDTH), lambda i: (0, i, 0)),
                   pl.BlockSpec((2, k_per_tile, KV_HEADS * VT_ROWS, ATTN_K), lambda i: (0, i, 0, 0)),
                   row_spec(ATTN_WIDTH), row_spec(SSM_WIDTH), row_spec(CONV_CH), row_spec(SMALL_W),
                   col_spec(SMALL_W)],
        out_shape=[jax.ShapeDtypeStruct((ATTN_WIDTH, rows), BF16),
                   jax.ShapeDtypeStruct((ATTN_WIDTH, rows), BF16),
                   jax.ShapeDtypeStruct((2 * N_PAIR_COLS, rows // CMP_STRIDE, CMP_STRIDE * LANES), BF16),
                   jax.ShapeDtypeStruct((2, rows, KV_WIDTH), BF16),
                   jax.ShapeDtypeStruct((2, rows // ATTN_K, KV_HEADS * VT_ROWS, ATTN_K), BF16),
                   jax.ShapeDtypeStruct((rows, ATTN_WIDTH), BF16),
                   jax.ShapeDtypeStruct((rows, SSM_WIDTH), BF16),
                   jax.ShapeDtypeStruct((rows, CONV_CH), BF16),
                   jax.ShapeDtypeStruct((rows, SMALL_W), F32),
                   jax.ShapeDtypeStruct((SMALL_W, rows), F32)],
        compiler_params=pltpu.CompilerParams(dimension_semantics=("arbitrary",),
                                             vmem_limit_bytes=VMEM_LIMIT),
        scratch_shapes=[pltpu.VMEM((2 * N_PAIR_COLS, tm, LANES), F32)],
        name="proj",
    )(x2, *weights, *tabs)


def _compress_kernel(x_ref, bd1_ref, w1_ref, pos_ref, b1_ref, bd2_ref, b2_ref, bd2t_ref, b2t_ref,
                     out_ref, outt_ref, acc_ref):
    n_rows = x_ref.shape[0]
    hid2 = 2 * CMP_HIDDEN
    acc_ref[n_rows:, :] = jnp.zeros((SUBLANES, 2 * hid2), F32)
    acc_ref[0:n_rows, :] = _dot(x_ref[...], bd1_ref[...])
    first = acc_ref[0:n_rows, 0:hid2]
    second = acc_ref[pl.ds(1, n_rows), hid2:2 * hid2]
    posterm = _dot(pos_ref[...].astype(BF16), w1_ref[...].astype(BF16))[0:1, :] + b1_ref[...]
    hcat = first + second + jnp.concatenate([posterm, posterm], axis=1)
    act = _silu(hcat).astype(BF16)
    out_ref[...] = _dot(act, bd2_ref[...]) + b2_ref[...]
    outt_ref[...] = _dot_nt(bd2t_ref[...], act) + b2t_ref[...]


def _compress_call(cm4, bd1, w1, pos8, b1, bd2, b2, bd2t, b2t):
    n_rows = cm4.shape[1]
    per_s = lambda *tail: pl.BlockSpec((None,) + tail, lambda s, c: (s,) + (0,) * len(tail))
    return pl.pallas_call(
        _compress_kernel,
        grid=(2, N_PAIR_COLS),
        in_specs=[pl.BlockSpec((None, n_rows, CMP_STRIDE * LANES), lambda s, c: (s * N_PAIR_COLS + c, 0, 0)),
                  per_s(CMP_STRIDE * LANES, 4 * CMP_HIDDEN),
                  per_s(CMP_BLOCK * HEAD_DIM, CMP_HIDDEN),
                  per_s(SUBLANES, CMP_BLOCK * HEAD_DIM),
                  per_s(1, CMP_HIDDEN),
                  per_s(2 * CMP_HIDDEN, LANES),
                  per_s(1, LANES),
                  per_s(LANES, 2 * CMP_HIDDEN),
                  per_s(LANES, 1)],
        out_specs=[pl.BlockSpec((None, None, n_rows, LANES), lambda s, c: (s, c, 0, 0)),
                   pl.BlockSpec((None, None, LANES, n_rows), lambda s, c: (s, c, 0, 0))],
        out_shape=[jax.ShapeDtypeStruct((2, N_PAIR_COLS, n_rows, LANES), F32),
                   jax.ShapeDtypeStruct((2, N_PAIR_COLS, LANES, n_rows), F32)],
        scratch_shapes=[pltpu.VMEM((n_rows + SUBLANES, 4 * CMP_HIDDEN), F32)],
        compiler_params=pltpu.CompilerParams(dimension_semantics=("arbitrary", "arbitrary"),
                                             vmem_limit_bytes=VMEM_LIMIT),
        name="compress",
    )(cm4, bd1, w1, pos8, b1, bd2, b2, bd2t, b2t)


def _attn_kernel(qnt_ref, qrt_ref, kc_ref, vct_ref, ksl_ref, kwn_ref, vslt_ref, vwnt_ref, smt_ref, gb_ref,
                 ovt_ref, tri_ref, za_ref, out_ref,
                 qw_ref, qs_ref, bias_ref, m_ref, acc_ref, osum_ref, s_ref, kaug_ref, gate_ref):
    tq, tk = ATTN_Q, ATTN_K
    n_r = Q_PER_KV
    qt = pl.program_id(2)
    q0 = qt * tq
    row = lax.broadcasted_iota(jnp.int32, (LANES, 1), 0)
    halves = (row < HEAD_DIM, row >= HEAD_DIM)
    qpos = q0 + lax.broadcasted_iota(jnp.int32, (1, tq), 1)

    gate_ref[...] = _sigmoid(smt_ref[...] + gb_ref[...])
    slot0 = pl.program_id(1) * (2 * n_r)

    def gate_row(br, idx):
        return gate_ref[pl.ds(br * ATTN_HEADS + slot0 + idx, 1), :]

    kc = kc_ref[...].astype(BF16)
    vct = vct_ref[...].astype(BF16)
    cpos = row * CMP_STRIDE + (CMP_BLOCK - 1)
    cbias = jnp.where(cpos <= qpos, 0.0, NEG)
    psum = [jnp.zeros((LANES, tq), F32), jnp.zeros((LANES, tq), F32)]
    for idx in range(2 * n_r):
        q = qnt_ref[(idx // 2) * LANES:(idx // 2 + 1) * LANES, :]
        qm = jnp.where(halves[idx % 2], q, jnp.zeros_like(q))
        s_ref[idx, 0:LANES, :] = _dot(kc, qm) + cbias
    for idx in range(2 * n_r):
        r, hf = divmod(idx, 2)
        s = s_ref[idx, 0:LANES, :]
        m = jnp.maximum(jnp.max(s, axis=0, keepdims=True), M_FLOOR)
        p = jnp.exp2(s - m)
        l = jnp.sum(p, axis=0, keepdims=True)
        p = p * jnp.where(l > 0.0, 1.0 / l, 0.0)
        psum[hf] = psum[hf] + p
        hs = slice(hf * HEAD_DIM, (hf + 1) * HEAD_DIM)
        osum_ref[r, hs, :] = gate_row(0, idx) * _dot(vct[hs, :], p.astype(BF16))

    nb = ksl_ref.shape[0] // SLC_BLOCK
    jio = lax.broadcasted_iota(jnp.int32, (nb, tq), 0)
    cur = (q0 + lax.broadcasted_iota(jnp.int32, (nb, tq), 1)) // SLC_BLOCK
    forced = (jio == 0) | (jio == cur) | (jio == cur - 1)

    def select_blocks():
        for hf in range(2):
            imp = _select_dot(ovt_ref[...], psum[hf])[0:nb, :]
            imp = jnp.where(forced, BIG, imp)
            imp = jnp.where(jio > cur, -BIG, imp)
            cnt = jnp.zeros((nb, tq), F32)
            for i in range(nb):
                other = imp[i:i + 1, :]
                beats = (other > imp) | ((other == imp) & (jio > i))
                cnt = cnt + jnp.where(beats, 1.0, 0.0)
            bias_ref[hf] = jnp.where(cnt < float(SLC_TOPN), 0.0, NEG)

    def load_queries(qm_ref, with_selection_bias):
        spare = jnp.zeros((HEAD_DIM - nb, tq), BF16)
        for r in range(n_r):
            q = qrt_ref[r * LANES:(r + 1) * LANES, :]
            for hf in range(2):
                if with_selection_bias:
                    fill = [bias_ref[hf].astype(BF16), spare]
                else:
                    fill = [jnp.zeros((HEAD_DIM, tq), BF16)]
                pieces = [q[0:HEAD_DIM]] + fill if hf == 0 else fill + [q[HEAD_DIM:]]
                qm_ref[r * 2 + hf] = jnp.concatenate(pieces, axis=0)

    lane_io = lax.broadcasted_iota(jnp.int32, (1, LANES), 1)
    key_blk = lax.broadcasted_iota(jnp.int32, (tk, 1), 0) // SLC_BLOCK

    def keys_with_block_indicator(k_tile, kt, hf):
        first = 0 if hf == 1 else HEAD_DIM
        onehot = (lane_io - first) == (kt * (tk // SLC_BLOCK) + key_blk)
        keep = (lane_io >= HEAD_DIM) if hf == 1 else (lane_io < HEAD_DIM)
        return jnp.where(keep, k_tile, jnp.where(onehot, 1.0, 0.0).astype(BF16))

    def reset():
        m_ref[...] = jnp.full(m_ref.shape, M_FLOOR, F32)
        acc_ref[...] = jnp.zeros(acc_ref.shape, F32)

    def flash_group(qm_ref, k_ref, vt_ref, parts, indicator=False):
        if indicator:
            for pi, (kt, _) in enumerate(parts):
                k_tile = k_ref[pl.ds(pl.multiple_of(kt * tk, tk), tk), :]
                for hf in range(2):
                    kaug_ref[pi, hf] = keys_with_block_indicator(k_tile, kt, hf)

        def scores(idx):
            col_max = None
            for pi, (kt, bias_fn) in enumerate(parts):
                if indicator:
                    k_tile = kaug_ref[pi, idx % 2]
                else:
                    k_tile = k_ref[pl.ds(pl.multiple_of(kt * tk, tk), tk), :]
                s = _dot(k_tile, qm_ref[idx])
                if bias_fn is not None:
                    s = bias_fn(s)
                s_ref[idx, pi * tk:(pi + 1) * tk, :] = s
                part_max = jnp.max(s, axis=0, keepdims=True)
                col_max = part_max if col_max is None else jnp.maximum(col_max, part_max)
            return col_max

        n_units = 2 * n_r
        col_maxes = [scores(idx) for idx in range(min(SCORE_LEAD, n_units))]
        for idx in range(n_units):
            if idx + SCORE_LEAD < n_units:
                col_maxes.append(scores(idx + SCORE_LEAD))
            m_old = m_ref[idx:idx + 1, :]
            m_new = jnp.maximum(m_old, col_maxes[idx])
            m_ref[idx:idx + 1, :] = m_new
            rows = slice(idx * VT_ROWS, (idx + 1) * VT_ROWS)
            vrows = slice((idx % 2) * VT_ROWS, (idx % 2 + 1) * VT_ROWS)
            upd = acc_ref[rows, :] * jnp.exp2(m_old - m_new)
            for pi, (kt, _) in enumerate(parts):
                p = jnp.exp2(s_ref[idx, pi * tk:(pi + 1) * tk, :] - m_new).astype(BF16)
                upd = upd + _dot(vt_ref[kt, vrows, :], p)
            acc_ref[rows, :] = upd

    def finish(branch):
        for idx in range(2 * n_r):
            r, hf = divmod(idx, 2)
            base = idx * VT_ROWS
            l = acc_ref[base + HEAD_DIM:base + HEAD_DIM + 1, :]
            scale = gate_row(branch, idx) * jnp.where(l > 0.0, 1.0 / l, 0.0)
            hs = slice(hf * HEAD_DIM, (hf + 1) * HEAD_DIM)
            osum_ref[r, hs, :] += acc_ref[base:base + HEAD_DIM, :] * scale

    tri_causal, tri_tail, tri_open, tri_closed = 0, 1, 2, 3
    causal = lambda s: s + tri_ref[tri_causal]

    load_queries(qw_ref, with_selection_bias=False)
    reset()
    tiles_back = WINDOW // tk
    win_parts = []
    for back in range(tiles_back, -1, -1):
        kt = jnp.maximum(qt - back, 0)
        inside = tri_causal if back == 0 else (tri_tail if back == tiles_back else tri_open)
        table = inside if back == 0 else jnp.where(qt >= back, inside, tri_closed)
        win_parts.append((kt, lambda s, table=table: s + tri_ref[table]))
    flash_group(qw_ref, kwn_ref, vwnt_ref, win_parts)
    finish(2)

    select_blocks()
    load_queries(qs_ref, with_selection_bias=True)
    reset()
    full_groups = qt // SLC_GROUP

    def slc_full(j, carry):
        flash_group(qs_ref, ksl_ref, vslt_ref, [(SLC_GROUP * j + i, None) for i in range(SLC_GROUP)],
                    indicator=True)
        return carry

    lax.fori_loop(0, full_groups, slc_full, 0)

    for n_last in range(1, SLC_GROUP + 1):
        @pl.when(qt % SLC_GROUP == n_last - 1)
        def _(n_last=n_last):
            first = SLC_GROUP * full_groups
            parts = [(first + i, None) for i in range(n_last - 1)] + [(qt, causal)]
            flash_group(qs_ref, ksl_ref, vslt_ref, parts, indicator=True)

    finish(1)

    for r in range(n_r):
        z = za_ref[:, r * LANES:(r + 1) * LANES].astype(F32)
        out_ref[:, r * LANES:(r + 1) * LANES] = (osum_ref[r].T * _silu(z)).astype(BF16)


def _attn_call(qnt, qrt, kc_std, kc_t, kk, vvt, smt, gb, ovt, tri, za, bsz, seq):
    tq, tk = ATTN_Q, ATTN_K
    n_qt = seq // tq
    n_kt = seq // tk
    qt_spec = pl.BlockSpec((HALF_W, tq), lambda b, c, t: (c, b * n_qt + t))
    out_spec = pl.BlockSpec((tq, HALF_W), lambda b, c, t: (b * n_qt + t, c))
    k_spec = lambda s: pl.BlockSpec((None, seq, LANES), lambda b, c, t: (s, b, c))
    vt_spec = lambda s: pl.BlockSpec((None, n_kt, 2 * VT_ROWS, tk), lambda b, c, t: (s, b, c, 0))
    const = lambda a: pl.BlockSpec(a.shape, lambda b, c, t: (0,) * a.ndim)
    return pl.pallas_call(
        _attn_kernel,
        grid=(bsz, N_PAIR_COLS, n_qt),
        in_specs=[qt_spec, qt_spec,
                  pl.BlockSpec((None, None, LANES, LANES), lambda b, c, t: (0, c, b, 0)),
                  pl.BlockSpec((None, None, LANES, LANES), lambda b, c, t: (1, c, 0, b)),
                  k_spec(0), k_spec(1), vt_spec(0), vt_spec(1),
                  pl.BlockSpec((SMALL_W, tq), lambda b, c, t: (0, b * n_qt + t)),
                  const(gb),
                  const(ovt), const(tri), out_spec],
        out_specs=out_spec,
        out_shape=jax.ShapeDtypeStruct((bsz * seq, ATTN_WIDTH), BF16),
        scratch_shapes=[pltpu.VMEM((2 * Q_PER_KV, LANES, tq), BF16),
                        pltpu.VMEM((2 * Q_PER_KV, LANES, tq), BF16),
                        pltpu.VMEM((2, seq // SLC_BLOCK, tq), F32),
                        pltpu.VMEM((2 * Q_PER_KV, tq), F32),
                        pltpu.VMEM((2 * Q_PER_KV * VT_ROWS, tq), F32),
                        pltpu.VMEM((Q_PER_KV, LANES, tq), F32),
                        pltpu.VMEM((2 * Q_PER_KV, MAX_GROUP * tk, tq), F32),
                        pltpu.VMEM((SLC_GROUP, 2, tk, LANES), BF16),
                        pltpu.VMEM((SMALL_W, tq), F32)],
        compiler_params=pltpu.CompilerParams(dimension_semantics=("arbitrary", "arbitrary", "arbitrary"),
                                             vmem_limit_bytes=VMEM_LIMIT),
        name="attn",
    )(qnt, qrt, kc_std, kc_t, kk, kk, vvt, vvt, smt, gb, ovt, tri, za)


def _ssd_kernel(xbc_ref, sm_ref, cw_ref, cb_ref, dtb_ref, alog_ref, e16_ref, dskip_ref, zs_ref, nw_ref,
                shift_ref, head_ref, out_ref, carry_ref, state_ref, y_ref):
    L = SSD_CHUNK
    j = pl.program_id(1)
    n_prev = CONV_WIDTH - 1

    @pl.when(j == 0)
    def _():
        carry_ref[...] = jnp.zeros(carry_ref.shape, BF16)
        state_ref[...] = jnp.zeros(state_ref.shape, F32)

    u = xbc_ref[...]
    shifted = _dot(shift_ref[...], u)
    conv = cb_ref[...] + u.astype(F32) * cw_ref[n_prev:n_prev + 1, :]
    head = jnp.zeros((SUBLANES, CONV_CH), F32)
    for w in range(n_prev):
        conv = conv + shifted[w * L:(w + 1) * L] * cw_ref[w:w + 1, :]
        head = head + _dot(head_ref[w], carry_ref[...]) * cw_ref[w:w + 1, :]
    conv = jnp.concatenate([conv[0:SUBLANES] + head, conv[SUBLANES:]], axis=0)
    carry_ref[...] = u[L - CARRY_ROWS:L]
    act = _silu(conv)

    sm = sm_ref[...] + dtb_ref[...]
    dt = jnp.maximum(sm, 0.0) + jnp.log1p(jnp.exp(-jnp.abs(sm)))
    a = dt * (-jnp.exp(alog_ref[...]) * LOG2E)
    tril = lax.broadcasted_iota(jnp.int32, (L, L), 0) >= lax.broadcasted_iota(jnp.int32, (L, L), 1)
    a_cs = _select_dot(jnp.where(tril, 1.0, 0.0).astype(BF16), a)
    a_cs_t = a_cs.T
    e16 = e16_ref[...]
    a_exp = _dot_select(a_cs, e16)
    dt_exp = _dot_select(dt, e16)
    a_last = a_exp[L - 1:L, :]
    x_c = act[:, 0:SSM_WIDTH]
    xdt = x_c * dt_exp
    xdt_bf = xdt.astype(BF16)
    xw = (xdt * jnp.exp2(a_last - a_exp)).astype(BF16)
    ea = jnp.exp2(a_exp)
    chunk_decay = jnp.exp2(a_last)
    lane = lax.broadcasted_iota(jnp.int32, (1, LANES), 1)
    lo = lane < SSM_HEAD_DIM
    halves = (lo, jnp.logical_not(lo))

    for g in range(SSM_GROUPS):
        bg = act[:, SSM_WIDTH + g * SSM_STATE: SSM_WIDTH + (g + 1) * SSM_STATE]
        cg = act[:, SSM_WIDTH + SSM_GROUPS * SSM_STATE + g * SSM_STATE:
                 SSM_WIDTH + SSM_GROUPS * SSM_STATE + (g + 1) * SSM_STATE].astype(BF16)
        cb = _dot_nt(cg, bg.astype(BF16))
        bg_t = bg.T.astype(BF16)
        for i in range(2 * g, 2 * g + 2):
            sl = slice(i * LANES, (i + 1) * LANES)
            y = jnp.zeros((L, LANES), F32)
            for hh in range(2):
                h = 2 * i + hh
                col = a_cs[:, DT_LANE0 + h:DT_LANE0 + h + 1]
                row = a_cs_t[DT_LANE0 + h:DT_LANE0 + h + 1, :]
                decay = jnp.exp2(jnp.where(tril, col - row, NEG))
                xh = jnp.where(halves[hh], xdt_bf[:, sl], jnp.zeros((), BF16))
                y = y + _dot((cb * decay).astype(BF16), xh)
            st = state_ref[i]
            y = y + _dot(cg, st.astype(BF16)) * ea[:, sl]
            state_ref[i] = st * chunk_decay[:, sl] + _dot(bg_t, xw[:, sl])
            y_ref[:, sl] = y + dskip_ref[:, sl] * x_c[:, sl]

    y = y_ref[...] * _silu(zs_ref[...].astype(F32))
    y = y * lax.rsqrt(jnp.mean(y * y, axis=-1, keepdims=True) + EPS) * nw_ref[...]
    out_ref[...] = y.astype(BF16)


def _ssd_call(xbc, sm, cw, cb, dtb, alog, e16, dskip, zs, nw, shift, head, bsz, seq):
    L = SSD_CHUNK
    n_ch = seq // L
    row = lambda w: pl.BlockSpec((L, w), lambda b, j: (b * n_ch + j, 0))
    const = lambda a: pl.BlockSpec(a.shape, lambda b, j: (0,) * a.ndim)
    return pl.pallas_call(
        _ssd_kernel,
        grid=(bsz, n_ch),
        in_specs=[row(CONV_CH), row(SMALL_W), const(cw), const(cb), const(dtb), const(alog), const(e16),
                  const(dskip), row(SSM_WIDTH), const(nw), const(shift), const(head)],
        out_specs=row(SSM_WIDTH),
        out_shape=jax.ShapeDtypeStruct((bsz * seq, SSM_WIDTH), BF16),
        scratch_shapes=[pltpu.VMEM((CARRY_ROWS, CONV_CH), BF16),
                        pltpu.VMEM((SSM_HEADS // 2, SSM_STATE, LANES), F32),
                        pltpu.VMEM((L, SSM_WIDTH), F32)],
        compiler_params=pltpu.CompilerParams(dimension_semantics=("arbitrary", "arbitrary"),
                                             vmem_limit_bytes=VMEM_LIMIT),
        name="ssd",
    )(xbc, sm, cw, cb, dtb, alog, e16, dskip, zs, nw, shift, head)


def _out_kernel(att_ref, ssm_ref, x_ref, wa_ref, ws_ref, pw_ref, out_ref):
    o = _dot(att_ref[...], wa_ref[...]) + _dot(ssm_ref[...], ws_ref[...])
    o = o * lax.rsqrt(jnp.mean(o * o, axis=-1, keepdims=True) + EPS) * pw_ref[...]
    out_ref[...] = x_ref[...] + o


def _out_call(att, ssm, x2, wa, ws, pw):
    rows = x2.shape[0]
    tm = OUT_ROWS
    row = lambda w: pl.BlockSpec((tm, w), lambda i: (i, 0))
    return pl.pallas_call(
        _out_kernel,
        grid=(rows // tm,),
        in_specs=[row(ATTN_WIDTH), row(SSM_WIDTH), row(D_MODEL), _resident(wa.shape), _resident(ws.shape),
                  _resident(pw.shape)],
        out_specs=row(D_MODEL),
        out_shape=jax.ShapeDtypeStruct((rows, D_MODEL), F32),
        compiler_params=pltpu.CompilerParams(dimension_semantics=("arbitrary",),
                                             vmem_limit_bytes=VMEM_LIMIT),
        name="outproj",
    )(att, ssm, x2, wa, ws, pw)


def _pair_heads(t, axis, per_head):
    shp = t.shape
    t = t.reshape(shp[:axis] + (N_PAIR_COLS, 2, Q_PER_KV, per_head) + shp[axis + 1:])
    return jnp.swapaxes(t, axis + 1, axis + 2).reshape(shp)


def _branch_major(t, axis):
    shp = t.shape
    t = _pair_heads(t, axis, N_BRANCH).reshape(shp[:axis] + (ATTN_HEADS, N_BRANCH) + shp[axis + 1:])
    return jnp.swapaxes(t, axis, axis + 1).reshape(shp)


def _constants(seq):

    nc = (seq - CMP_BLOCK) // CMP_STRIDE + 1
    nb = seq // SLC_BLOCK
    ci = np.arange(nc)[:, None] * CMP_STRIDE
    bj = np.arange(nb)[None, :] * SLC_BLOCK
    overlap = ((ci <= bj + SLC_BLOCK - 1) & (ci + CMP_BLOCK - 1 >= bj)).astype(np.float32)
    ovt = np.zeros((LANES, LANES), np.float32)
    ovt[:nb, :nc] = overlap.T

    e16 = np.zeros((SMALL_W, SSM_WIDTH), np.float32)
    for h in range(SSM_HEADS):
        e16[DT_LANE0 + h, h * SSM_HEAD_DIM:(h + 1) * SSM_HEAD_DIM] = 1.0

    key = np.arange(ATTN_K)[:, None]
    qry = np.arange(ATTN_Q)[None, :]
    tri = np.stack([np.where(key <= qry, 0.0, NEG), np.where(key > qry, 0.0, NEG),
                    np.zeros((ATTN_K, ATTN_Q)), np.full((ATTN_K, ATTN_Q), NEG)]).astype(np.float32)
    n_prev = CONV_WIDTH - 1
    shift = np.zeros((n_prev * SSD_CHUNK, SSD_CHUNK), np.float32)
    head = np.zeros((n_prev, SUBLANES, CARRY_ROWS), np.float32)
    for w in range(n_prev):
        back = n_prev - w
        for t in range(SSD_CHUNK):
            if t - back >= 0:
                shift[w * SSD_CHUNK + t, t - back] = 1.0
            elif t < SUBLANES:
                head[w, t, CARRY_ROWS + t - back] = 1.0
    return ovt, e16, tri, shift, head


def _rope_tables(seq):
    half_freqs = ROPE_THETA ** (-np.arange(ROPE_HALF, dtype=np.float64) * 2.0 / ROPE_DIM)
    ang = np.arange(seq, dtype=np.float64)[:, None] * half_freqs[None, :]
    cos, sin = np.cos(ang), np.sin(ang)
    ones = np.ones((seq, HEAD_DIM - ROPE_DIM))
    zeros_h = np.zeros((seq, ROPE_HALF))
    zeros_r = np.zeros((seq, HEAD_DIM - ROPE_DIM))
    cos_h = np.concatenate([cos, cos, ones], axis=1)
    sina_h = np.concatenate([-sin, zeros_h, zeros_r], axis=1)
    sinb_h = np.concatenate([zeros_h, sin, zeros_r], axis=1)
    tile2 = lambda t: np.concatenate([t, t], axis=1)
    return tuple(jnp.asarray(t, F32) for t in (tile2(cos_h), tile2(sina_h), tile2(sinb_h), cos.T, sin.T))


def _layer(x, w_in, w_out, pre_w, post_w, cmp_pos, cmp_w1, cmp_b1, cmp_w2, cmp_b2,
           gate_b, conv_w, conv_b, dt_bias, a_log, d_skip, ssm_norm_w):
    bsz, seq, _ = x.shape
    ovt, e16, tri, shift, head = _constants(seq)
    tabs = _rope_tables(seq)

    o_q = 0
    o_kcm = ATTN_WIDTH
    o_ksl = o_kcm + 2 * KV_WIDTH
    o_vsl = o_ksl + KV_WIDTH
    o_kwn = o_vsl + KV_WIDTH
    o_vwn = o_kwn + KV_WIDTH
    o_g = o_vwn + KV_WIDTH
    o_za = o_g + ATTN_HEADS * N_BRANCH
    o_zs = o_za + ATTN_WIDTH
    o_xbc = o_zs + SSM_WIDTH
    o_dt = o_xbc + CONV_CH
    w_bf = _cast_call(w_in)
    cols = lambda o, n: w_bf[:, o:o + n]
    wqt = _pair_heads(cols(o_q, ATTN_WIDTH), 1, HEAD_DIM).T
    wcm = cols(o_kcm, 2 * KV_WIDTH)
    wk = jnp.concatenate([cols(o_ksl, KV_WIDTH), cols(o_kwn, KV_WIDTH)], axis=1)
    wvt = jnp.concatenate([cols(o_vsl, KV_WIDTH), cols(o_vwn, KV_WIDTH)], axis=1).T
    wza = _pair_heads(cols(o_za, ATTN_WIDTH), 1, HEAD_DIM)
    wzs = cols(o_zs, SSM_WIDTH)
    wxbc = cols(o_xbc, CONV_CH)
    n_gate = ATTN_HEADS * N_BRANCH
    wsm = jnp.concatenate([_branch_major(cols(o_g, n_gate), 1), cols(o_dt, SSM_HEADS),
                           jnp.zeros((D_MODEL, SMALL_W - n_gate - SSM_HEADS), BF16)], axis=1)
    pad_small = lambda v, at: jnp.zeros((1, SMALL_W), F32).at[0, at:at + v.shape[0]].set(v)
    gb = pad_small(_branch_major(gate_b, 0), 0).T
    dtb = pad_small(dt_bias, DT_LANE0)
    alog = pad_small(a_log, DT_LANE0)

    x2 = x.reshape(bsz * seq, D_MODEL)
    qnt, qrt, cm, kk, vvt, za, zs, xbc, sm, smt = _proj_call(
        x2, pre_w[None, :], wqt, wcm, wk, wvt, wza, wzs, wxbc, wsm, tabs, seq)

    w1r = cmp_w1.reshape(2, 2, CMP_STRIDE, HEAD_DIM, CMP_HIDDEN)
    wa, wb = w1r[:, 0], w1r[:, 1]
    z1 = jnp.zeros_like(wa)
    bd1 = jnp.concatenate([jnp.concatenate([wa, z1, wb, z1], axis=-1),
                           jnp.concatenate([z1, wa, z1, wb], axis=-1)], axis=-2).astype(BF16)
    bd1 = bd1.reshape(2, CMP_STRIDE * LANES, 4 * CMP_HIDDEN)
    z2 = jnp.zeros_like(cmp_w2)
    bd2 = jnp.concatenate([jnp.concatenate([cmp_w2, z2], axis=-1),
                           jnp.concatenate([z2, cmp_w2], axis=-1)], axis=-2).astype(BF16)
    pos8 = jnp.broadcast_to(cmp_pos.reshape(2, 1, CMP_BLOCK * HEAD_DIM), (2, SUBLANES, CMP_BLOCK * HEAD_DIM))
    b2 = jnp.concatenate([cmp_b2, cmp_b2], axis=-1)[:, None, :]
    kc_std, kc_t = _compress_call(cm, bd1, cmp_w1, pos8, cmp_b1[:, None, :], bd2, b2,
                                  jnp.swapaxes(bd2, 1, 2), jnp.swapaxes(b2, 1, 2))

    att = _attn_call(qnt, qrt, kc_std, kc_t, kk, vvt, smt, gb, jnp.asarray(ovt, BF16),
                     jnp.asarray(tri), za, bsz, seq)

    dskip = jnp.repeat(d_skip, SSM_HEAD_DIM)[None, :]
    ssm = _ssd_call(xbc, sm, conv_w, conv_b[None, :], dtb, alog, jnp.asarray(e16, BF16), dskip, zs,
                    ssm_norm_w[None, :], jnp.asarray(shift, BF16), jnp.asarray(head, BF16), bsz, seq)

    wo_a = _pair_heads(w_out[0:ATTN_WIDTH], 0, HEAD_DIM).astype(BF16)
    wo_s = w_out[ATTN_WIDTH:].astype(BF16)
    out = _out_call(att, ssm, x2, wo_a, wo_s, post_w[None, :])
    return out.reshape(bsz, seq, D_MODEL)


def kernel(x, w_in, w_out, pre_norm_w, post_norm_w, cmp_pos, cmp_w1, cmp_b1, cmp_w2, cmp_b2, gate_b, conv_w,
           conv_b, dt_bias, a_log, d_skip, ssm_norm_w):
    for l in range(w_in.shape[0]):
        x = _layer(x, w_in[l], w_out[l], pre_norm_w[l], post_norm_w[l], cmp_pos[l], cmp_w1[l], cmp_b1[l],
                   cmp_w2[l], cmp_b2[l], gate_b[l], conv_w[l], conv_b[l], dt_bias[l], a_log[l], d_skip[l],
                   ssm_norm_w[l])
    return x
```

```python
import numpy as np
import jax
import jax.numpy as jnp
from jax import lax
from jax.experimental import pallas as pl
from jax.experimental.pallas import tpu as pltpu

F32 = jnp.float32
BF16 = jnp.bfloat16

D_MODEL = 1024
ATTN_HEADS = 16
HEAD_DIM = 64
ATTN_WIDTH = ATTN_HEADS * HEAD_DIM
KV_HEADS = 4
Q_PER_KV = ATTN_HEADS // KV_HEADS
KV_WIDTH = KV_HEADS * HEAD_DIM
ROPE_DIM = HEAD_DIM // 4
ROPE_HALF = ROPE_DIM // 2
ROPE_THETA = 500000.0
CMP_BLOCK = 32
CMP_STRIDE = 16
CMP_HIDDEN = 256
SLC_BLOCK = 64
SLC_TOPN = 16
WINDOW = 512
N_BRANCH = 3
SSM_HEADS = 16
SSM_HEAD_DIM = 64
SSM_WIDTH = SSM_HEADS * SSM_HEAD_DIM
SSM_GROUPS = 4
SSM_STATE = 128
CONV_WIDTH = 4
CONV_CH = SSM_WIDTH + 2 * SSM_GROUPS * SSM_STATE
MIX_WIDTH = ATTN_WIDTH + SSM_WIDTH
EPS = 1e-6
NEG = -1e30
BIG = 1e30
M_FLOOR = -1e29
LOG2E = 1.4426950408889634
Q_SCALE = HEAD_DIM ** -0.5 * LOG2E

LANES = 128
SUBLANES = 8
N_PAIR_COLS = KV_HEADS // 2
HALF_W = ATTN_WIDTH // N_PAIR_COLS
SMALL_W = LANES
ONES_ROWS = 16
VT_ROWS = HEAD_DIM + ONES_ROWS
DT_LANE0 = ATTN_HEADS * N_BRANCH
VMEM_LIMIT = 56 * 1024 * 1024

PROJ_ROWS = 512
W_CAST_ROWS = 1344

W_ROWS = dict(q=ATTN_WIDTH, cm=2 * KV_WIDTH, k_sl=KV_WIDTH, v_sl=KV_WIDTH, k_wn=KV_WIDTH, v_wn=KV_WIDTH,
              gate=ATTN_HEADS * N_BRANCH, z_att=ATTN_WIDTH, z_ssm=SSM_WIDTH, xbc=CONV_CH, dt=SSM_HEADS)
W_OFF = dict(zip(W_ROWS, np.cumsum([0] + list(W_ROWS.values())[:-1]).tolist()))
HEAD_OF_SLOT = [(2 * c + gp) * Q_PER_KV + r for c in range(N_PAIR_COLS) for r in range(Q_PER_KV) for gp in range(2)]
ATTN_Q = 256
ATTN_K = 256
SLC_GROUP = 4
MAX_GROUP = max(SLC_GROUP, WINDOW // ATTN_K + 1)
SCORE_LEAD = 3
SSD_CHUNK = 256
CARRY_ROWS = 16
OUT_ROWS = 1024


def _dot(a, b):
    return jnp.dot(a, b, preferred_element_type=F32)


def _dot_nt(a, b):
    return lax.dot_general(a, b, (((1,), (1,)), ((), ())), preferred_element_type=F32)


def _split3(x):
    x1 = x.astype(BF16)
    r1 = x - x1.astype(F32)
    x2 = r1.astype(BF16)
    x3 = (r1 - x2.astype(F32)).astype(BF16)
    return x1, x2, x3


def _dot_select(x, sel):
    x1, x2, x3 = _split3(x)
    return _dot(x1, sel) + _dot(x2, sel) + _dot(x3, sel)


def _select_dot(sel, x):
    x1, x2, x3 = _split3(x)
    return _dot(sel, x1) + _dot(sel, x2) + _dot(sel, x3)


def _sigmoid(x):
    return 1.0 / (1.0 + jnp.exp2(x * -LOG2E))


def _silu(x):
    return x * _sigmoid(x)


def _rope_chunk(c, cos_t, sin_a, sin_b):
    return c * cos_t + pltpu.roll(c, LANES - ROPE_HALF, 1) * sin_a + pltpu.roll(c, ROPE_HALF, 1) * sin_b


def _proj_kernel(x_ref, pre_w_ref, wt_ref, cos_ref, sina_ref, sinb_ref, cos8_ref, sin8_ref,
                 qnt_ref, qrt_ref, cm_ref, kk_ref, vvt_ref, zat_ref, zs_ref, xbc_ref, sm_ref, smt_ref, cm_scr):
    x = x_ref[...]
    h = x * lax.rsqrt(jnp.mean(x * x, axis=-1, keepdims=True) + EPS) * pre_w_ref[...]
    h = h.astype(BF16)
    seg = lambda name: wt_ref[W_OFF[name]:W_OFF[name] + W_ROWS[name], :]

    qt = _dot_nt(seg("q"), h)
    zat = _dot_nt(seg("z_att"), h).astype(BF16)
    cos8, sin8 = cos8_ref[...], sin8_ref[...]
    for slot, head in enumerate(HEAD_OF_SLOT):
        src = slice(head * HEAD_DIM, (head + 1) * HEAD_DIM)
        dst = slice(slot * HEAD_DIM, (slot + 1) * HEAD_DIM)
        q_head = qt[src]
        t1, t2 = q_head[0:ROPE_HALF], q_head[ROPE_HALF:ROPE_DIM]
        rot = jnp.concatenate([t1 * cos8 - t2 * sin8, t2 * cos8 + t1 * sin8, q_head[ROPE_DIM:]], axis=0)
        qnt_ref[dst, :] = q_head.astype(BF16)
        qrt_ref[dst, :] = rot.astype(BF16)
        zat_ref[dst, :] = zat[src]

    cm = _dot_nt(h, seg("cm"))
    for k in range(2 * N_PAIR_COLS):
        cm_scr[k] = cm[:, k * LANES:(k + 1) * LANES]
        for t in range(CMP_STRIDE):
            piece = cm_scr[k, pl.ds(t, PROJ_ROWS // CMP_STRIDE, stride=CMP_STRIDE), :]
            cm_ref[k, :, t * LANES:(t + 1) * LANES] = piece.astype(BF16)

    cos_t, sin_a, sin_b = cos_ref[...], sina_ref[...], sinb_ref[...]
    ones = jnp.ones((ONES_ROWS, ATTN_K), BF16)
    for t, (k_name, v_name) in enumerate((("k_sl", "v_sl"), ("k_wn", "v_wn"))):
        kk = _dot_nt(h, seg(k_name))
        for k in range(N_PAIR_COLS):
            c = kk[:, k * LANES:(k + 1) * LANES]
            kk_ref[t, :, k * LANES:(k + 1) * LANES] = _rope_chunk(c, cos_t, sin_a, sin_b).astype(BF16)
        vt = _dot_nt(seg(v_name), h).astype(BF16)
        for j in range(PROJ_ROWS // ATTN_K):
            for g in range(KV_HEADS):
                vvt_ref[t, j, g * VT_ROWS:g * VT_ROWS + HEAD_DIM, :] = vt[g * HEAD_DIM:(g + 1) * HEAD_DIM,
                                                                          j * ATTN_K:(j + 1) * ATTN_K]
                vvt_ref[t, j, g * VT_ROWS + HEAD_DIM:(g + 1) * VT_ROWS, :] = ones

    zs_ref[...] = _dot_nt(h, seg("z_ssm")).astype(BF16)
    xbc_ref[...] = _dot_nt(h, seg("xbc")).astype(BF16)
    smt = jnp.concatenate([_dot_nt(seg("gate"), h), _dot_nt(seg("dt"), h),
                           jnp.zeros((SMALL_W - DT_LANE0 - SSM_HEADS, PROJ_ROWS), F32)], axis=0)
    smt_ref[...] = smt
    sm_ref[...] = smt.T


def _cast_kernel(w_ref, out_ref):
    row = pl.program_id(0) * W_CAST_ROWS + lax.broadcasted_iota(jnp.int32, (W_CAST_ROWS, 1), 0)
    out_ref[...] = (w_ref[...] * jnp.where(row < ATTN_WIDTH, Q_SCALE, 1.0)).astype(BF16)


def _cast_call(w_t):
    rows, width = w_t.shape
    spec = pl.BlockSpec((W_CAST_ROWS, width), lambda i: (i, 0))
    return pl.pallas_call(
        _cast_kernel,
        grid=(rows // W_CAST_ROWS,),
        in_specs=[spec],
        out_specs=spec,
        out_shape=jax.ShapeDtypeStruct((rows, width), BF16),
        compiler_params=pltpu.CompilerParams(dimension_semantics=("arbitrary",)),
        name="wcast",
    )(w_t)


def _resident(shape):
    nd = len(shape)
    return pl.BlockSpec(shape, lambda *_: (0,) * nd, pipeline_mode=pl.Buffered(1))


def _proj_call(x2, pre_w, wt, tabs, seq):
    rows = x2.shape[0]
    tm = PROJ_ROWS
    n_seq_tiles = seq // tm
    k_per_tile = tm // ATTN_K
    row_spec = lambda w: pl.BlockSpec((tm, w), lambda i: (i, 0))
    col_spec = lambda h: pl.BlockSpec((h, tm), lambda i: (0, i))
    tab_spec = pl.BlockSpec((tm, LANES), lambda i: (i % n_seq_tiles, 0))
    tab8_spec = pl.BlockSpec((ROPE_HALF, tm), lambda i: (0, i % n_seq_tiles))
    return pl.pallas_call(
        _proj_kernel,
        grid=(rows // tm,),
        in_specs=[row_spec(D_MODEL), _resident(pre_w.shape), _resident(wt.shape),
                  tab_spec, tab_spec, tab_spec, tab8_spec, tab8_spec],
        out_specs=[col_spec(ATTN_WIDTH), col_spec(ATTN_WIDTH),
                   pl.BlockSpec((2 * N_PAIR_COLS, tm // CMP_STRIDE, CMP_STRIDE * LANES), lambda i: (0, i, 0)),
                   pl.BlockSpec((2, tm, KV_WIDTH), lambda i: (0, i, 0)),
                   pl.BlockSpec((2, k_per_tile, KV_HEADS * VT_ROWS, ATTN_K), lambda i: (0, i, 0, 0)),
                   col_spec(ATTN_WIDTH), row_spec(SSM_WIDTH), row_spec(CONV_CH), row_spec(SMALL_W),
                   col_spec(SMALL_W)],
        out_shape=[jax.ShapeDtypeStruct((ATTN_WIDTH, rows), BF16),
                   jax.ShapeDtypeStruct((ATTN_WIDTH, rows), BF16),
                   jax.ShapeDtypeStruct((2 * N_PAIR_COLS, rows // CMP_STRIDE, CMP_STRIDE * LANES), BF16),
                   jax.ShapeDtypeStruct((2, rows, KV_WIDTH), BF16),
                   jax.ShapeDtypeStruct((2, rows // ATTN_K, KV_HEADS * VT_ROWS, ATTN_K), BF16),
                   jax.ShapeDtypeStruct((ATTN_WIDTH, rows), BF16),
                   jax.ShapeDtypeStruct((rows, SSM_WIDTH), BF16),
                   jax.ShapeDtypeStruct((rows, CONV_CH), BF16),
                   jax.ShapeDtypeStruct((rows, SMALL_W), F32),
                   jax.ShapeDtypeStruct((SMALL_W, rows), F32)],
        compiler_params=pltpu.CompilerParams(dimension_semantics=("arbitrary",),
                                             vmem_limit_bytes=VMEM_LIMIT),
        scratch_shapes=[pltpu.VMEM((2 * N_PAIR_COLS, tm, LANES), F32)],
        name="proj",
    )(x2, pre_w, wt, *tabs)


def _compress_kernel(x_ref, bd1_ref, w1_ref, pos_ref, b1_ref, bd2_ref, b2_ref, bd2t_ref, b2t_ref,
                     out_ref, outt_ref, acc_ref):
    n_rows = x_ref.shape[0]
    hid2 = 2 * CMP_HIDDEN
    acc_ref[n_rows:, :] = jnp.zeros((SUBLANES, 2 * hid2), F32)
    acc_ref[0:n_rows, :] = _dot(x_ref[...], bd1_ref[...])
    first = acc_ref[0:n_rows, 0:hid2]
    second = acc_ref[pl.ds(1, n_rows), hid2:2 * hid2]
    posterm = _dot(pos_ref[...].astype(BF16), w1_ref[...].astype(BF16))[0:1, :] + b1_ref[...]
    hcat = first + second + jnp.concatenate([posterm, posterm], axis=1)
    act = _silu(hcat).astype(BF16)
    out_ref[...] = _dot(act, bd2_ref[...]) + b2_ref[...]
    outt_ref[...] = _dot_nt(bd2t_ref[...], act) + b2t_ref[...]


def _compress_call(cm4, bd1, w1, pos8, b1, bd2, b2, bd2t, b2t):
    n_rows = cm4.shape[1]
    per_s = lambda *tail: pl.BlockSpec((None,) + tail, lambda s, c: (s,) + (0,) * len(tail))
    return pl.pallas_call(
        _compress_kernel,
        grid=(2, N_PAIR_COLS),
        in_specs=[pl.BlockSpec((None, n_rows, CMP_STRIDE * LANES), lambda s, c: (s * N_PAIR_COLS + c, 0, 0)),
                  per_s(CMP_STRIDE * LANES, 4 * CMP_HIDDEN),
                  per_s(CMP_BLOCK * HEAD_DIM, CMP_HIDDEN),
                  per_s(SUBLANES, CMP_BLOCK * HEAD_DIM),
                  per_s(1, CMP_HIDDEN),
                  per_s(2 * CMP_HIDDEN, LANES),
                  per_s(1, LANES),
                  per_s(LANES, 2 * CMP_HIDDEN),
                  per_s(LANES, 1)],
        out_specs=[pl.BlockSpec((None, None, n_rows, LANES), lambda s, c: (s, c, 0, 0)),
                   pl.BlockSpec((None, None, LANES, n_rows), lambda s, c: (s, c, 0, 0))],
        out_shape=[jax.ShapeDtypeStruct((2, N_PAIR_COLS, n_rows, LANES), F32),
                   jax.ShapeDtypeStruct((2, N_PAIR_COLS, LANES, n_rows), F32)],
        scratch_shapes=[pltpu.VMEM((n_rows + SUBLANES, 4 * CMP_HIDDEN), F32)],
        compiler_params=pltpu.CompilerParams(dimension_semantics=("arbitrary", "arbitrary"),
                                             vmem_limit_bytes=VMEM_LIMIT),
        name="compress",
    )(cm4, bd1, w1, pos8, b1, bd2, b2, bd2t, b2t)


def _attn_kernel(qnt_ref, qrt_ref, kc_ref, vct_ref, ksl_ref, kwn_ref, vslt_ref, vwnt_ref, smt_ref, gb_ref,
                 ovt_ref, tri_ref, zat_ref, out_ref,
                 qw_ref, qs_ref, bias_ref, m_ref, acc_ref, osum_ref, s_ref, kaug_ref, gate_ref):
    tq, tk = ATTN_Q, ATTN_K
    n_r = Q_PER_KV
    qt = pl.program_id(2)
    q0 = qt * tq
    row = lax.broadcasted_iota(jnp.int32, (LANES, 1), 0)
    halves = (row < HEAD_DIM, row >= HEAD_DIM)
    qpos = q0 + lax.broadcasted_iota(jnp.int32, (1, tq), 1)

    gate_ref[...] = _sigmoid(smt_ref[...] + gb_ref[...])
    first_group = pl.program_id(1) * 2

    def gate_row(br, idx):
        head = (first_group + idx % 2) * n_r + idx // 2
        return gate_ref[pl.ds(head * N_BRANCH + br, 1), :]

    kc = kc_ref[...].astype(BF16)
    vct = vct_ref[...].astype(BF16)
    cpos = row * CMP_STRIDE + (CMP_BLOCK - 1)
    cbias = jnp.where(cpos <= qpos, 0.0, NEG)
    psum = [jnp.zeros((LANES, tq), F32), jnp.zeros((LANES, tq), F32)]
    for idx in range(2 * n_r):
        q = qnt_ref[(idx // 2) * LANES:(idx // 2 + 1) * LANES, :]
        qm = jnp.where(halves[idx % 2], q, jnp.zeros_like(q))
        s_ref[idx, 0:LANES, :] = _dot(kc, qm) + cbias
    for idx in range(2 * n_r):
        r, hf = divmod(idx, 2)
        s = s_ref[idx, 0:LANES, :]
        m = jnp.maximum(jnp.max(s, axis=0, keepdims=True), M_FLOOR)
        p = jnp.exp2(s - m)
        l = jnp.sum(p, axis=0, keepdims=True)
        p = p * jnp.where(l > 0.0, 1.0 / l, 0.0)
        psum[hf] = psum[hf] + p
        hs = slice(hf * HEAD_DIM, (hf + 1) * HEAD_DIM)
        osum_ref[r, hs, :] = gate_row(0, idx) * _dot(vct[hs, :], p.astype(BF16))

    nb = ksl_ref.shape[0] // SLC_BLOCK
    jio = lax.broadcasted_iota(jnp.int32, (nb, tq), 0)
    cur = (q0 + lax.broadcasted_iota(jnp.int32, (nb, tq), 1)) // SLC_BLOCK
    forced = (jio == 0) | (jio == cur) | (jio == cur - 1)

    def select_blocks():
        for hf in range(2):
            imp = _select_dot(ovt_ref[...], psum[hf])[0:nb, :]
            imp = jnp.where(forced, BIG, imp)
            imp = jnp.where(jio > cur, -BIG, imp)
            cnt = jnp.zeros((nb, tq), F32)
            for i in range(nb):
                other = imp[i:i + 1, :]
                beats = (other > imp) | ((other == imp) & (jio > i))
                cnt = cnt + jnp.where(beats, 1.0, 0.0)
            bias_ref[hf] = jnp.where(cnt < float(SLC_TOPN), 0.0, NEG)

    def load_queries(qm_ref, with_selection_bias):
        spare = jnp.zeros((HEAD_DIM - nb, tq), BF16)
        for r in range(n_r):
            q = qrt_ref[r * LANES:(r + 1) * LANES, :]
            for hf in range(2):
                if with_selection_bias:
                    fill = [bias_ref[hf].astype(BF16), spare]
                else:
                    fill = [jnp.zeros((HEAD_DIM, tq), BF16)]
                pieces = [q[0:HEAD_DIM]] + fill if hf == 0 else fill + [q[HEAD_DIM:]]
                qm_ref[r * 2 + hf] = jnp.concatenate(pieces, axis=0)

    lane_io = lax.broadcasted_iota(jnp.int32, (1, LANES), 1)
    key_blk = lax.broadcasted_iota(jnp.int32, (tk, 1), 0) // SLC_BLOCK

    def keys_with_block_indicator(k_tile, kt, hf):
        first = 0 if hf == 1 else HEAD_DIM
        onehot = (lane_io - first) == (kt * (tk // SLC_BLOCK) + key_blk)
        keep = (lane_io >= HEAD_DIM) if hf == 1 else (lane_io < HEAD_DIM)
        return jnp.where(keep, k_tile, jnp.where(onehot, 1.0, 0.0).astype(BF16))

    def reset():
        m_ref[...] = jnp.full(m_ref.shape, M_FLOOR, F32)
        acc_ref[...] = jnp.zeros(acc_ref.shape, F32)

    def flash_group(qm_ref, k_ref, vt_ref, parts, indicator=False):
        if indicator:
            for pi, (kt, _) in enumerate(parts):
                k_tile = k_ref[pl.ds(pl.multiple_of(kt * tk, tk), tk), :]
                for hf in range(2):
                    kaug_ref[pi, hf] = keys_with_block_indicator(k_tile, kt, hf)

        def scores(idx):
            col_max = None
            for pi, (kt, bias_fn) in enumerate(parts):
                if indicator:
                    k_tile = kaug_ref[pi, idx % 2]
                else:
                    k_tile = k_ref[pl.ds(pl.multiple_of(kt * tk, tk), tk), :]
                s = _dot(k_tile, qm_ref[idx])
                if bias_fn is not None:
                    s = bias_fn(s)
                s_ref[idx, pi * tk:(pi + 1) * tk, :] = s
                part_max = jnp.max(s, axis=0, keepdims=True)
                col_max = part_max if col_max is None else jnp.maximum(col_max, part_max)
            return col_max

        n_units = 2 * n_r
        col_maxes = [scores(idx) for idx in range(min(SCORE_LEAD, n_units))]
        for idx in range(n_units):
            if idx + SCORE_LEAD < n_units:
                col_maxes.append(scores(idx + SCORE_LEAD))
            m_old = m_ref[idx:idx + 1, :]
            m_new = jnp.maximum(m_old, col_maxes[idx])
            m_ref[idx:idx + 1, :] = m_new
            rows = slice(idx * VT_ROWS, (idx + 1) * VT_ROWS)
            vrows = slice((idx % 2) * VT_ROWS, (idx % 2 + 1) * VT_ROWS)
            upd = acc_ref[rows, :] * jnp.exp2(m_old - m_new)
            for pi, (kt, _) in enumerate(parts):
                p = jnp.exp2(s_ref[idx, pi * tk:(pi + 1) * tk, :] - m_new).astype(BF16)
                upd = upd + _dot(vt_ref[kt, vrows, :], p)
            acc_ref[rows, :] = upd

    def finish(branch):
        for idx in range(2 * n_r):
            r, hf = divmod(idx, 2)
            base = idx * VT_ROWS
            l = acc_ref[base + HEAD_DIM:base + HEAD_DIM + 1, :]
            scale = gate_row(branch, idx) * jnp.where(l > 0.0, 1.0 / l, 0.0)
            hs = slice(hf * HEAD_DIM, (hf + 1) * HEAD_DIM)
            osum_ref[r, hs, :] += acc_ref[base:base + HEAD_DIM, :] * scale

    tri_causal, tri_tail, tri_open, tri_closed = 0, 1, 2, 3
    causal = lambda s: s + tri_ref[tri_causal]

    load_queries(qw_ref, with_selection_bias=False)
    reset()
    tiles_back = WINDOW // tk
    win_parts = []
    for back in range(tiles_back, -1, -1):
        kt = jnp.maximum(qt - back, 0)
        inside = tri_causal if back == 0 else (tri_tail if back == tiles_back else tri_open)
        table = inside if back == 0 else jnp.where(qt >= back, inside, tri_closed)
        win_parts.append((kt, lambda s, table=table: s + tri_ref[table]))
    flash_group(qw_ref, kwn_ref, vwnt_ref, win_parts)
    finish(2)

    select_blocks()
    load_queries(qs_ref, with_selection_bias=True)
    reset()
    full_groups = qt // SLC_GROUP

    def slc_full(j, carry):
        flash_group(qs_ref, ksl_ref, vslt_ref, [(SLC_GROUP * j + i, None) for i in range(SLC_GROUP)],
                    indicator=True)
        return carry

    lax.fori_loop(0, full_groups, slc_full, 0)

    for n_last in range(1, SLC_GROUP + 1):
        @pl.when(qt % SLC_GROUP == n_last - 1)
        def _(n_last=n_last):
            first = SLC_GROUP * full_groups
            parts = [(first + i, None) for i in range(n_last - 1)] + [(qt, causal)]
            flash_group(qs_ref, ksl_ref, vslt_ref, parts, indicator=True)

    finish(1)

    for r in range(n_r):
        z = zat_ref[r * LANES:(r + 1) * LANES, :].astype(F32)
        out_ref[:, r * LANES:(r + 1) * LANES] = (osum_ref[r] * _silu(z)).T.astype(BF16)


def _attn_call(qnt, qrt, kc_std, kc_t, kk, vvt, smt, gb, ovt, tri, zat, bsz, seq):
    tq, tk = ATTN_Q, ATTN_K
    n_qt = seq // tq
    n_kt = seq // tk
    qt_spec = pl.BlockSpec((HALF_W, tq), lambda b, c, t: (c, b * n_qt + t))
    out_spec = pl.BlockSpec((tq, HALF_W), lambda b, c, t: (b * n_qt + t, c))
    k_spec = lambda s: pl.BlockSpec((None, seq, LANES), lambda b, c, t: (s, b, c))
    vt_spec = lambda s: pl.BlockSpec((None, n_kt, 2 * VT_ROWS, tk), lambda b, c, t: (s, b, c, 0))
    const = lambda a: pl.BlockSpec(a.shape, lambda b, c, t: (0,) * a.ndim)
    return pl.pallas_call(
        _attn_kernel,
        grid=(bsz, N_PAIR_COLS, n_qt),
        in_specs=[qt_spec, qt_spec,
                  pl.BlockSpec((None, None, LANES, LANES), lambda b, c, t: (0, c, b, 0)),
                  pl.BlockSpec((None, None, LANES, LANES), lambda b, c, t: (1, c, 0, b)),
                  k_spec(0), k_spec(1), vt_spec(0), vt_spec(1),
                  pl.BlockSpec((SMALL_W, tq), lambda b, c, t: (0, b * n_qt + t)),
                  const(gb),
                  const(ovt), const(tri), qt_spec],
        out_specs=out_spec,
        out_shape=jax.ShapeDtypeStruct((bsz * seq, ATTN_WIDTH), BF16),
        scratch_shapes=[pltpu.VMEM((2 * Q_PER_KV, LANES, tq), BF16),
                        pltpu.VMEM((2 * Q_PER_KV, LANES, tq), BF16),
                        pltpu.VMEM((2, seq // SLC_BLOCK, tq), F32),
                        pltpu.VMEM((2 * Q_PER_KV, tq), F32),
                        pltpu.VMEM((2 * Q_PER_KV * VT_ROWS, tq), F32),
                        pltpu.VMEM((Q_PER_KV, LANES, tq), F32),
                        pltpu.VMEM((2 * Q_PER_KV, MAX_GROUP * tk, tq), F32),
                        pltpu.VMEM((SLC_GROUP, 2, tk, LANES), BF16),
                        pltpu.VMEM((SMALL_W, tq), F32)],
        compiler_params=pltpu.CompilerParams(dimension_semantics=("arbitrary", "arbitrary", "arbitrary"),
                                             vmem_limit_bytes=VMEM_LIMIT),
        name="attn",
    )(qnt, qrt, kc_std, kc_t, kk, kk, vvt, vvt, smt, gb, ovt, tri, zat)


def _ssd_kernel(xbc_ref, sm_ref, cw_ref, cb_ref, dtb_ref, alog_ref, e16_ref, dskip_ref, zs_ref, nw_ref,
                shift_ref, head_ref, out_ref, carry_ref, state_ref, y_ref):
    L = SSD_CHUNK
    j = pl.program_id(1)
    n_prev = CONV_WIDTH - 1

    @pl.when(j == 0)
    def _():
        carry_ref[...] = jnp.zeros(carry_ref.shape, BF16)
        state_ref[...] = jnp.zeros(state_ref.shape, F32)

    u = xbc_ref[...]
    shifted = _dot(shift_ref[...], u)
    conv = cb_ref[...] + u.astype(F32) * cw_ref[n_prev:n_prev + 1, :]
    head = jnp.zeros((SUBLANES, CONV_CH), F32)
    for w in range(n_prev):
        conv = conv + shifted[w * L:(w + 1) * L] * cw_ref[w:w + 1, :]
        head = head + _dot(head_ref[w], carry_ref[...]) * cw_ref[w:w + 1, :]
    conv = jnp.concatenate([conv[0:SUBLANES] + head, conv[SUBLANES:]], axis=0)
    carry_ref[...] = u[L - CARRY_ROWS:L]
    act = _silu(conv)

    sm = sm_ref[...] + dtb_ref[...]
    dt = jnp.maximum(sm, 0.0) + jnp.log1p(jnp.exp(-jnp.abs(sm)))
    a = dt * (-jnp.exp(alog_ref[...]) * LOG2E)
    tril = lax.broadcasted_iota(jnp.int32, (L, L), 0) >= lax.broadcasted_iota(jnp.int32, (L, L), 1)
    a_cs = _select_dot(jnp.where(tril, 1.0, 0.0).astype(BF16), a)
    a_cs_t = a_cs.T
    e16 = e16_ref[...]
    a_exp = _dot_select(a_cs, e16)
    dt_exp = _dot_select(dt, e16)
    a_last = a_exp[L - 1:L, :]
    x_c = act[:, 0:SSM_WIDTH]
    xdt = x_c * dt_exp
    xdt_bf = xdt.astype(BF16)
    xw = (xdt * jnp.exp2(a_last - a_exp)).astype(BF16)
    ea = jnp.exp2(a_exp)
    chunk_decay = jnp.exp2(a_last)
    lane = lax.broadcasted_iota(jnp.int32, (1, LANES), 1)
    lo = lane < SSM_HEAD_DIM
    halves = (lo, jnp.logical_not(lo))

    for g in range(SSM_GROUPS):
        bg = act[:, SSM_WIDTH + g * SSM_STATE: SSM_WIDTH + (g + 1) * SSM_STATE]
        cg = act[:, SSM_WIDTH + SSM_GROUPS * SSM_STATE + g * SSM_STATE:
                 SSM_WIDTH + SSM_GROUPS * SSM_STATE + (g + 1) * SSM_STATE].astype(BF16)
        cb = _dot_nt(cg, bg.astype(BF16))
        bg_t = bg.T.astype(BF16)
        for i in range(2 * g, 2 * g + 2):
            sl = slice(i * LANES, (i + 1) * LANES)
            y = jnp.zeros((L, LANES), F32)
            for hh in range(2):
                h = 2 * i + hh
                col = a_cs[:, DT_LANE0 + h:DT_LANE0 + h + 1]
                row = a_cs_t[DT_LANE0 + h:DT_LANE0 + h + 1, :]
                decay = jnp.exp2(jnp.where(tril, col - row, NEG))
                xh = jnp.where(halves[hh], xdt_bf[:, sl], jnp.zeros((), BF16))
                y = y + _dot((cb * decay).astype(BF16), xh)
            st = state_ref[i]
            y = y + _dot(cg, st.astype(BF16)) * ea[:, sl]
            state_ref[i] = st * chunk_decay[:, sl] + _dot(bg_t, xw[:, sl])
            y_ref[:, sl] = y + dskip_ref[:, sl] * x_c[:, sl]

    y = y_ref[...] * _silu(zs_ref[...].astype(F32))
    y = y * lax.rsqrt(jnp.mean(y * y, axis=-1, keepdims=True) + EPS) * nw_ref[...]
    out_ref[...] = y.astype(BF16)


def _ssd_call(xbc, sm, cw, cb, dtb, alog, e16, dskip, zs, nw, shift, head, bsz, seq):
    L = SSD_CHUNK
    n_ch = seq // L
    row = lambda w: pl.BlockSpec((L, w), lambda b, j: (b * n_ch + j, 0))
    const = lambda a: pl.BlockSpec(a.shape, lambda b, j: (0,) * a.ndim)
    return pl.pallas_call(
        _ssd_kernel,
        grid=(bsz, n_ch),
        in_specs=[row(CONV_CH), row(SMALL_W), const(cw), const(cb), const(dtb), const(alog), const(e16),
                  const(dskip), row(SSM_WIDTH), const(nw), const(shift), const(head)],
        out_specs=row(SSM_WIDTH),
        out_shape=jax.ShapeDtypeStruct((bsz * seq, SSM_WIDTH), BF16),
        scratch_shapes=[pltpu.VMEM((CARRY_ROWS, CONV_CH), BF16),
                        pltpu.VMEM((SSM_HEADS // 2, SSM_STATE, LANES), F32),
                        pltpu.VMEM((L, SSM_WIDTH), F32)],
        compiler_params=pltpu.CompilerParams(dimension_semantics=("arbitrary", "arbitrary"),
                                             vmem_limit_bytes=VMEM_LIMIT),
        name="ssd",
    )(xbc, sm, cw, cb, dtb, alog, e16, dskip, zs, nw, shift, head)


def _out_kernel(att_ref, ssm_ref, x_ref, wa_ref, ws_ref, pw_ref, out_ref):
    o = _dot(att_ref[...], wa_ref[...]) + _dot(ssm_ref[...], ws_ref[...])
    o = o * lax.rsqrt(jnp.mean(o * o, axis=-1, keepdims=True) + EPS) * pw_ref[...]
    out_ref[...] = x_ref[...] + o


def _out_call(att, ssm, x2, wa, ws, pw):
    rows = x2.shape[0]
    tm = OUT_ROWS
    row = lambda w: pl.BlockSpec((tm, w), lambda i: (i, 0))
    return pl.pallas_call(
        _out_kernel,
        grid=(rows // tm,),
        in_specs=[row(ATTN_WIDTH), row(SSM_WIDTH), row(D_MODEL), _resident(wa.shape), _resident(ws.shape),
                  _resident(pw.shape)],
        out_specs=row(D_MODEL),
        out_shape=jax.ShapeDtypeStruct((rows, D_MODEL), F32),
        compiler_params=pltpu.CompilerParams(dimension_semantics=("arbitrary",),
                                             vmem_limit_bytes=VMEM_LIMIT),
        name="outproj",
    )(att, ssm, x2, wa, ws, pw)


def _pair_heads(t, axis, per_head):
    shp = t.shape
    t = t.reshape(shp[:axis] + (N_PAIR_COLS, 2, Q_PER_KV, per_head) + shp[axis + 1:])
    return jnp.swapaxes(t, axis + 1, axis + 2).reshape(shp)


def _constants(seq):

    nc = (seq - CMP_BLOCK) // CMP_STRIDE + 1
    nb = seq // SLC_BLOCK
    ci = np.arange(nc)[:, None] * CMP_STRIDE
    bj = np.arange(nb)[None, :] * SLC_BLOCK
    overlap = ((ci <= bj + SLC_BLOCK - 1) & (ci + CMP_BLOCK - 1 >= bj)).astype(np.float32)
    ovt = np.zeros((LANES, LANES), np.float32)
    ovt[:nb, :nc] = overlap.T

    e16 = np.zeros((SMALL_W, SSM_WIDTH), np.float32)
    for h in range(SSM_HEADS):
        e16[DT_LANE0 + h, h * SSM_HEAD_DIM:(h + 1) * SSM_HEAD_DIM] = 1.0

    key = np.arange(ATTN_K)[:, None]
    qry = np.arange(ATTN_Q)[None, :]
    tri = np.stack([np.where(key <= qry, 0.0, NEG), np.where(key > qry, 0.0, NEG),
                    np.zeros((ATTN_K, ATTN_Q)), np.full((ATTN_K, ATTN_Q), NEG)]).astype(np.float32)
    n_prev = CONV_WIDTH - 1
    shift = np.zeros((n_prev * SSD_CHUNK, SSD_CHUNK), np.float32)
    head = np.zeros((n_prev, SUBLANES, CARRY_ROWS), np.float32)
    for w in range(n_prev):
        back = n_prev - w
        for t in range(SSD_CHUNK):
            if t - back >= 0:
                shift[w * SSD_CHUNK + t, t - back] = 1.0
            elif t < SUBLANES:
                head[w, t, CARRY_ROWS + t - back] = 1.0
    return ovt, e16, tri, shift, head


def _rope_tables(seq):
    half_freqs = ROPE_THETA ** (-np.arange(ROPE_HALF, dtype=np.float64) * 2.0 / ROPE_DIM)
    ang = np.arange(seq, dtype=np.float64)[:, None] * half_freqs[None, :]
    cos, sin = np.cos(ang), np.sin(ang)
    ones = np.ones((seq, HEAD_DIM - ROPE_DIM))
    zeros_h = np.zeros((seq, ROPE_HALF))
    zeros_r = np.zeros((seq, HEAD_DIM - ROPE_DIM))
    cos_h = np.concatenate([cos, cos, ones], axis=1)
    sina_h = np.concatenate([-sin, zeros_h, zeros_r], axis=1)
    sinb_h = np.concatenate([zeros_h, sin, zeros_r], axis=1)
    tile2 = lambda t: np.concatenate([t, t], axis=1)
    return tuple(jnp.asarray(t, F32) for t in (tile2(cos_h), tile2(sina_h), tile2(sinb_h), cos.T, sin.T))


def _layer(x, w_in, w_out, pre_w, post_w, cmp_pos, cmp_w1, cmp_b1, cmp_w2, cmp_b2,
           gate_b, conv_w, conv_b, dt_bias, a_log, d_skip, ssm_norm_w):
    bsz, seq, _ = x.shape
    ovt, e16, tri, shift, head = _constants(seq)
    tabs = _rope_tables(seq)

    w_t = _cast_call(jnp.swapaxes(w_in, 0, 1))
    pad_small = lambda v, at: jnp.zeros((1, SMALL_W), F32).at[0, at:at + v.shape[0]].set(v)
    gb = pad_small(gate_b, 0).T
    dtb = pad_small(dt_bias, DT_LANE0)
    alog = pad_small(a_log, DT_LANE0)

    x2 = x.reshape(bsz * seq, D_MODEL)
    qnt, qrt, cm, kk, vvt, zat, zs, xbc, sm, smt = _proj_call(x2, pre_w[None, :], w_t, tabs, seq)

    w1r = cmp_w1.reshape(2, 2, CMP_STRIDE, HEAD_DIM, CMP_HIDDEN)
    wa, wb = w1r[:, 0], w1r[:, 1]
    z1 = jnp.zeros_like(wa)
    bd1 = jnp.concatenate([jnp.concatenate([wa, z1, wb, z1], axis=-1),
                           jnp.concatenate([z1, wa, z1, wb], axis=-1)], axis=-2).astype(BF16)
    bd1 = bd1.reshape(2, CMP_STRIDE * LANES, 4 * CMP_HIDDEN)
    z2 = jnp.zeros_like(cmp_w2)
    bd2 = jnp.concatenate([jnp.concatenate([cmp_w2, z2], axis=-1),
                           jnp.concatenate([z2, cmp_w2], axis=-1)], axis=-2).astype(BF16)
    pos8 = jnp.broadcast_to(cmp_pos.reshape(2, 1, CMP_BLOCK * HEAD_DIM), (2, SUBLANES, CMP_BLOCK * HEAD_DIM))
    b2 = jnp.concatenate([cmp_b2, cmp_b2], axis=-1)[:, None, :]
    kc_std, kc_t = _compress_call(cm, bd1, cmp_w1, pos8, cmp_b1[:, None, :], bd2, b2,
                                  jnp.swapaxes(bd2, 1, 2), jnp.swapaxes(b2, 1, 2))

    att = _attn_call(qnt, qrt, kc_std, kc_t, kk, vvt, smt, gb, jnp.asarray(ovt, BF16),
                     jnp.asarray(tri), zat, bsz, seq)

    dskip = jnp.repeat(d_skip, SSM_HEAD_DIM)[None, :]
    ssm = _ssd_call(xbc, sm, conv_w, conv_b[None, :], dtb, alog, jnp.asarray(e16, BF16), dskip, zs,
                    ssm_norm_w[None, :], jnp.asarray(shift, BF16), jnp.asarray(head, BF16), bsz, seq)

    wo_a = _pair_heads(w_out[0:ATTN_WIDTH], 0, HEAD_DIM).astype(BF16)
    wo_s = w_out[ATTN_WIDTH:].astype(BF16)
    out = _out_call(att, ssm, x2, wo_a, wo_s, post_w[None, :])
    return out.reshape(bsz, seq, D_MODEL)


def kernel(x, w_in, w_out, pre_norm_w, post_norm_w, cmp_pos, cmp_w1, cmp_b1, cmp_w2, cmp_b2, gate_b, conv_w,
           conv_b, dt_bias, a_log, d_skip, ssm_norm_w):
    for l in range(w_in.shape[0]):
        x = _layer(x, w_in[l], w_out[l], pre_norm_w[l], post_norm_w[l], cmp_pos[l], cmp_w1[l], cmp_b1[l],
                   cmp_w2[l], cmp_b2[l], gate_b[l], conv_w[l], conv_b[l], dt_bias[l], a_log[l], d_skip[l],
                   ssm_norm_w[l])
    return x
```

```python
import numpy as np
import jax
import jax.numpy as jnp
from jax import lax
from jax.experimental import pallas as pl
from jax.experimental.pallas import tpu as pltpu

F32 = jnp.float32
BF16 = jnp.bfloat16

D_MODEL = 1024
ATTN_HEADS = 16
HEAD_DIM = 64
ATTN_WIDTH = ATTN_HEADS * HEAD_DIM
KV_HEADS = 4
Q_PER_KV = ATTN_HEADS // KV_HEADS
KV_WIDTH = KV_HEADS * HEAD_DIM
ROPE_DIM = HEAD_DIM // 4
ROPE_HALF = ROPE_DIM // 2
ROPE_THETA = 500000.0
CMP_BLOCK = 32
CMP_STRIDE = 16
CMP_HIDDEN = 256
SLC_BLOCK = 64
SLC_TOPN = 16
WINDOW = 512
N_BRANCH = 3
SSM_HEADS = 16
SSM_HEAD_DIM = 64
SSM_WIDTH = SSM_HEADS * SSM_HEAD_DIM
SSM_GROUPS = 4
SSM_STATE = 128
CONV_WIDTH = 4
CONV_CH = SSM_WIDTH + 2 * SSM_GROUPS * SSM_STATE
MIX_WIDTH = ATTN_WIDTH + SSM_WIDTH
EPS = 1e-6
NEG = -1e30
BIG = 1e30
M_FLOOR = -1e29
LOG2E = 1.4426950408889634
Q_SCALE = HEAD_DIM ** -0.5 * LOG2E

LANES = 128
SUBLANES = 8
N_PAIR_COLS = KV_HEADS // 2
HALF_W = ATTN_WIDTH // N_PAIR_COLS
SMALL_W = LANES
ONES_ROWS = 16
VT_ROWS = HEAD_DIM + ONES_ROWS
DT_LANE0 = ATTN_HEADS * N_BRANCH
VMEM_LIMIT = 56 * 1024 * 1024

PROJ_ROWS = 512
W_CAST_ROWS = 1344

W_ROWS = dict(q=ATTN_WIDTH, cm=2 * KV_WIDTH, k_sl=KV_WIDTH, v_sl=KV_WIDTH, k_wn=KV_WIDTH, v_wn=KV_WIDTH,
              gate=ATTN_HEADS * N_BRANCH, z_att=ATTN_WIDTH, z_ssm=SSM_WIDTH, xbc=CONV_CH, dt=SSM_HEADS)
W_OFF = dict(zip(W_ROWS, np.cumsum([0] + list(W_ROWS.values())[:-1]).tolist()))
HEAD_OF_SLOT = [(2 * c + gp) * Q_PER_KV + r for c in range(N_PAIR_COLS) for r in range(Q_PER_KV) for gp in range(2)]
ATTN_Q = 256
ATTN_K = 256
SLC_GROUP = 4
MAX_GROUP = max(SLC_GROUP, WINDOW // ATTN_K + 1)
SCORE_LEAD = 3
SSD_CHUNK = 256
CARRY_ROWS = 16
OUT_ROWS = 1024


def _dot(a, b):
    return jnp.dot(a, b, preferred_element_type=F32)


def _dot_nt(a, b):
    return lax.dot_general(a, b, (((1,), (1,)), ((), ())), preferred_element_type=F32)


def _split3(x):
    x1 = x.astype(BF16)
    r1 = x - x1.astype(F32)
    x2 = r1.astype(BF16)
    x3 = (r1 - x2.astype(F32)).astype(BF16)
    return x1, x2, x3


def _dot_select(x, sel):
    x1, x2, x3 = _split3(x)
    return _dot(x1, sel) + _dot(x2, sel) + _dot(x3, sel)


def _select_dot(sel, x):
    x1, x2, x3 = _split3(x)
    return _dot(sel, x1) + _dot(sel, x2) + _dot(sel, x3)


def _sigmoid(x):
    return 1.0 / (1.0 + jnp.exp2(x * -LOG2E))


def _silu(x):
    return x * _sigmoid(x)


def _rope_chunk(c, cos_t, sin_a, sin_b):
    return c * cos_t + pltpu.roll(c, LANES - ROPE_HALF, 1) * sin_a + pltpu.roll(c, ROPE_HALF, 1) * sin_b


def _proj_kernel(x_ref, x_next_ref, pre_w_ref, wt_ref, pick_ref, cos_ref, sina_ref, sinb_ref, cos8_ref, sin8_ref,
                 qnt_ref, qrt_ref, cm_ref, kk_ref, vvt_ref, zat_ref, zs_ref, xbc_ref, sm_ref, smt_ref, h_scr):
    def normed(ref):
        x = ref[...]
        return (x * lax.rsqrt(jnp.mean(x * x, axis=-1, keepdims=True) + EPS) * pre_w_ref[...]).astype(BF16)

    @pl.when(pl.program_id(0) == 0)
    def _():
        h_scr[...] = normed(x_ref)

    h = h_scr[...]
    seg = lambda name: wt_ref[W_OFF[name]:W_OFF[name] + W_ROWS[name], :]

    qt = _dot_nt(seg("q"), h)
    zat = _dot_nt(seg("z_att"), h).astype(BF16)
    cos8, sin8 = cos8_ref[...], sin8_ref[...]
    for slot, head in enumerate(HEAD_OF_SLOT):
        src = slice(head * HEAD_DIM, (head + 1) * HEAD_DIM)
        dst = slice(slot * HEAD_DIM, (slot + 1) * HEAD_DIM)
        q_head = qt[src]
        t1, t2 = q_head[0:ROPE_HALF], q_head[ROPE_HALF:ROPE_DIM]
        rot = jnp.concatenate([t1 * cos8 - t2 * sin8, t2 * cos8 + t1 * sin8, q_head[ROPE_DIM:]], axis=0)
        qnt_ref[dst, :] = q_head.astype(BF16)
        qrt_ref[dst, :] = rot.astype(BF16)
        zat_ref[dst, :] = zat[src]

    smt = jnp.concatenate([_dot_nt(seg("gate"), h), _dot_nt(seg("dt"), h),
                           jnp.zeros((SMALL_W - DT_LANE0 - SSM_HEADS, PROJ_ROWS), F32)], axis=0)
    smt_ref[...] = smt
    sm_ref[...] = smt.T

    cm = _dot_nt(h, seg("cm")).astype(BF16)
    by_offset = _dot(pick_ref[...], cm).astype(BF16)
    n_grp = PROJ_ROWS // CMP_STRIDE
    for t in range(CMP_STRIDE):
        for k in range(2 * N_PAIR_COLS):
            cm_ref[k, :, t * LANES:(t + 1) * LANES] = by_offset[t * n_grp:(t + 1) * n_grp, k * LANES:(k + 1) * LANES]

    cos_t, sin_a, sin_b = cos_ref[...], sina_ref[...], sinb_ref[...]
    ones = jnp.ones((ONES_ROWS, ATTN_K), BF16)
    for t, (k_name, v_name) in enumerate((("k_sl", "v_sl"), ("k_wn", "v_wn"))):
        kk = _dot_nt(h, seg(k_name))
        for k in range(N_PAIR_COLS):
            c = kk[:, k * LANES:(k + 1) * LANES]
            kk_ref[t, :, k * LANES:(k + 1) * LANES] = _rope_chunk(c, cos_t, sin_a, sin_b).astype(BF16)
        vt = _dot_nt(seg(v_name), h).astype(BF16)
        for j in range(PROJ_ROWS // ATTN_K):
            for g in range(KV_HEADS):
                vvt_ref[t, j, g * VT_ROWS:g * VT_ROWS + HEAD_DIM, :] = vt[g * HEAD_DIM:(g + 1) * HEAD_DIM,
                                                                          j * ATTN_K:(j + 1) * ATTN_K]
                vvt_ref[t, j, g * VT_ROWS + HEAD_DIM:(g + 1) * VT_ROWS, :] = ones

    zs_ref[...] = _dot_nt(h, seg("z_ssm")).astype(BF16)
    xbc_ref[...] = _dot_nt(h, seg("xbc")).astype(BF16)
    h_scr[...] = normed(x_next_ref)


def _cast_kernel(w_ref, out_ref):
    row = pl.program_id(0) * W_CAST_ROWS + lax.broadcasted_iota(jnp.int32, (W_CAST_ROWS, 1), 0)
    out_ref[...] = (w_ref[...] * jnp.where(row < ATTN_WIDTH, Q_SCALE, 1.0)).astype(BF16)


def _cast_call(w_t):
    rows, width = w_t.shape
    spec = pl.BlockSpec((W_CAST_ROWS, width), lambda i: (i, 0))
    return pl.pallas_call(
        _cast_kernel,
        grid=(rows // W_CAST_ROWS,),
        in_specs=[spec],
        out_specs=spec,
        out_shape=jax.ShapeDtypeStruct((rows, width), BF16),
        compiler_params=pltpu.CompilerParams(dimension_semantics=("arbitrary",)),
        name="wcast",
    )(w_t)


def _resident(shape):
    nd = len(shape)
    return pl.BlockSpec(shape, lambda *_: (0,) * nd, pipeline_mode=pl.Buffered(1))


def _proj_call(x2, pre_w, wt, pick, tabs, seq):
    rows = x2.shape[0]
    tm = PROJ_ROWS
    n_seq_tiles = seq // tm
    k_per_tile = tm // ATTN_K
    row_spec = lambda w: pl.BlockSpec((tm, w), lambda i: (i, 0))
    col_spec = lambda h: pl.BlockSpec((h, tm), lambda i: (0, i))
    tab_spec = pl.BlockSpec((tm, LANES), lambda i: (i % n_seq_tiles, 0))
    tab8_spec = pl.BlockSpec((ROPE_HALF, tm), lambda i: (0, i % n_seq_tiles))
    return pl.pallas_call(
        _proj_kernel,
        grid=(rows // tm,),
        in_specs=[row_spec(D_MODEL), pl.BlockSpec((tm, D_MODEL), lambda i: (jnp.minimum(i + 1, rows // tm - 1), 0)),
                  _resident(pre_w.shape), _resident(wt.shape), _resident(pick.shape),
                  tab_spec, tab_spec, tab_spec, tab8_spec, tab8_spec],
        out_specs=[col_spec(ATTN_WIDTH), col_spec(ATTN_WIDTH),
                   pl.BlockSpec((2 * N_PAIR_COLS, tm // CMP_STRIDE, CMP_STRIDE * LANES), lambda i: (0, i, 0)),
                   pl.BlockSpec((2, tm, KV_WIDTH), lambda i: (0, i, 0)),
                   pl.BlockSpec((2, k_per_tile, KV_HEADS * VT_ROWS, ATTN_K), lambda i: (0, i, 0, 0)),
                   col_spec(ATTN_WIDTH), row_spec(SSM_WIDTH), row_spec(CONV_CH), row_spec(SMALL_W),
                   col_spec(SMALL_W)],
        out_shape=[jax.ShapeDtypeStruct((ATTN_WIDTH, rows), BF16),
                   jax.ShapeDtypeStruct((ATTN_WIDTH, rows), BF16),
                   jax.ShapeDtypeStruct((2 * N_PAIR_COLS, rows // CMP_STRIDE, CMP_STRIDE * LANES), BF16),
                   jax.ShapeDtypeStruct((2, rows, KV_WIDTH), BF16),
                   jax.ShapeDtypeStruct((2, rows // ATTN_K, KV_HEADS * VT_ROWS, ATTN_K), BF16),
                   jax.ShapeDtypeStruct((ATTN_WIDTH, rows), BF16),
                   jax.ShapeDtypeStruct((rows, SSM_WIDTH), BF16),
                   jax.ShapeDtypeStruct((rows, CONV_CH), BF16),
                   jax.ShapeDtypeStruct((rows, SMALL_W), F32),
                   jax.ShapeDtypeStruct((SMALL_W, rows), F32)],
        compiler_params=pltpu.CompilerParams(dimension_semantics=("arbitrary",),
                                             vmem_limit_bytes=VMEM_LIMIT),
        scratch_shapes=[pltpu.VMEM((tm, D_MODEL), BF16)],
        name="proj",
    )(x2, x2, pre_w, wt, pick, *tabs)


def _compress_kernel(x_ref, bd1_ref, w1_ref, pos_ref, b1_ref, bd2_ref, b2_ref, bd2t_ref, b2t_ref,
                     out_ref, outt_ref, acc_ref):
    n_rows = x_ref.shape[0]
    hid2 = 2 * CMP_HIDDEN
    acc_ref[n_rows:, :] = jnp.zeros((SUBLANES, 2 * hid2), F32)
    acc_ref[0:n_rows, :] = _dot(x_ref[...], bd1_ref[...])
    first = acc_ref[0:n_rows, 0:hid2]
    second = acc_ref[pl.ds(1, n_rows), hid2:2 * hid2]
    posterm = _dot(pos_ref[...].astype(BF16), w1_ref[...].astype(BF16))[0:1, :] + b1_ref[...]
    hcat = first + second + jnp.concatenate([posterm, posterm], axis=1)
    act = _silu(hcat).astype(BF16)
    out_ref[...] = _dot(act, bd2_ref[...]) + b2_ref[...]
    outt_ref[...] = _dot_nt(bd2t_ref[...], act) + b2t_ref[...]


def _compress_call(cm4, bd1, w1, pos8, b1, bd2, b2, bd2t, b2t):
    n_rows = cm4.shape[1]
    per_s = lambda *tail: pl.BlockSpec((None,) + tail, lambda s, c: (s,) + (0,) * len(tail))
    return pl.pallas_call(
        _compress_kernel,
        grid=(2, N_PAIR_COLS),
        in_specs=[pl.BlockSpec((None, n_rows, CMP_STRIDE * LANES), lambda s, c: (s * N_PAIR_COLS + c, 0, 0)),
                  per_s(CMP_STRIDE * LANES, 4 * CMP_HIDDEN),
                  per_s(CMP_BLOCK * HEAD_DIM, CMP_HIDDEN),
                  per_s(SUBLANES, CMP_BLOCK * HEAD_DIM),
                  per_s(1, CMP_HIDDEN),
                  per_s(2 * CMP_HIDDEN, LANES),
                  per_s(1, LANES),
                  per_s(LANES, 2 * CMP_HIDDEN),
                  per_s(LANES, 1)],
        out_specs=[pl.BlockSpec((None, None, n_rows, LANES), lambda s, c: (s, c, 0, 0)),
                   pl.BlockSpec((None, None, LANES, n_rows), lambda s, c: (s, c, 0, 0))],
        out_shape=[jax.ShapeDtypeStruct((2, N_PAIR_COLS, n_rows, LANES), F32),
                   jax.ShapeDtypeStruct((2, N_PAIR_COLS, LANES, n_rows), F32)],
        scratch_shapes=[pltpu.VMEM((n_rows + SUBLANES, 4 * CMP_HIDDEN), F32)],
        compiler_params=pltpu.CompilerParams(dimension_semantics=("arbitrary", "arbitrary"),
                                             vmem_limit_bytes=VMEM_LIMIT),
        name="compress",
    )(cm4, bd1, w1, pos8, b1, bd2, b2, bd2t, b2t)


def _attn_kernel(qnt_ref, qrt_ref, kc_ref, vct_ref, ksl_ref, kwn_ref, vslt_ref, vwnt_ref, smt_ref, gb_ref,
                 ovt_ref, tri_ref, zat_ref, out_ref,
                 qw_ref, qs_ref, bias_ref, m_ref, acc_ref, osum_ref, s_ref, kaug_ref, gate_ref, cs_ref):
    tq, tk = ATTN_Q, ATTN_K
    n_r = Q_PER_KV
    qt = pl.program_id(2)
    q0 = qt * tq
    row = lax.broadcasted_iota(jnp.int32, (LANES, 1), 0)
    halves = (row < HEAD_DIM, row >= HEAD_DIM)
    qpos = q0 + lax.broadcasted_iota(jnp.int32, (1, tq), 1)

    gate_ref[...] = _sigmoid(smt_ref[...] + gb_ref[...])
    first_group = pl.program_id(1) * 2

    def gate_row(br, idx):
        head = (first_group + idx % 2) * n_r + idx // 2
        return gate_ref[pl.ds(head * N_BRANCH + br, 1), :]

    kc = kc_ref[...].astype(BF16)
    vct = vct_ref[...].astype(BF16)
    cpos = row * CMP_STRIDE + (CMP_BLOCK - 1)
    cbias = jnp.where(cpos <= qpos, 0.0, NEG)
    psum = [jnp.zeros((LANES, tq), F32), jnp.zeros((LANES, tq), F32)]
    osum_ref[...] = jnp.zeros(osum_ref.shape, F32)
    for idx in range(2 * n_r):
        q = qnt_ref[(idx // 2) * LANES:(idx // 2 + 1) * LANES, :]
        qm = jnp.where(halves[idx % 2], q, jnp.zeros_like(q))
        cs_ref[idx] = _dot(kc, qm) + cbias

    def compressed_softmax():
        for idx in range(2 * n_r):
            r, hf = divmod(idx, 2)
            s = cs_ref[idx]
            m = jnp.maximum(jnp.max(s, axis=0, keepdims=True), M_FLOOR)
            p = jnp.exp2(s - m)
            l = jnp.sum(p, axis=0, keepdims=True)
            p = p * jnp.where(l > 0.0, 1.0 / l, 0.0)
            psum[hf] = psum[hf] + p
            hs = slice(hf * HEAD_DIM, (hf + 1) * HEAD_DIM)
            osum_ref[r, hs, :] += gate_row(0, idx) * _dot(vct[hs, :], p.astype(BF16))

    nb = ksl_ref.shape[0] // SLC_BLOCK
    jio = lax.broadcasted_iota(jnp.int32, (nb, tq), 0)
    cur = (q0 + lax.broadcasted_iota(jnp.int32, (nb, tq), 1)) // SLC_BLOCK
    forced = (jio == 0) | (jio == cur) | (jio == cur - 1)

    def select_blocks():
        for hf in range(2):
            imp = _select_dot(ovt_ref[...], psum[hf])[0:nb, :]
            imp = jnp.where(forced, BIG, imp)
            imp = jnp.where(jio > cur, -BIG, imp)
            cnt = jnp.zeros((nb, tq), F32)
            for i in range(nb):
                other = imp[i:i + 1, :]
                beats = (other > imp) | ((other == imp) & (jio > i))
                cnt = cnt + jnp.where(beats, 1.0, 0.0)
            bias_ref[hf] = jnp.where(cnt < float(SLC_TOPN), 0.0, NEG)

    def load_queries(qm_ref, with_selection_bias):
        spare = jnp.zeros((HEAD_DIM - nb, tq), BF16)
        for r in range(n_r):
            q = qrt_ref[r * LANES:(r + 1) * LANES, :]
            for hf in range(2):
                if with_selection_bias:
                    fill = [bias_ref[hf].astype(BF16), spare]
                else:
                    fill = [jnp.zeros((HEAD_DIM, tq), BF16)]
                pieces = [q[0:HEAD_DIM]] + fill if hf == 0 else fill + [q[HEAD_DIM:]]
                qm_ref[r * 2 + hf] = jnp.concatenate(pieces, axis=0)

    lane_io = lax.broadcasted_iota(jnp.int32, (1, LANES), 1)
    key_blk = lax.broadcasted_iota(jnp.int32, (tk, 1), 0) // SLC_BLOCK

    def keys_with_block_indicator(k_tile, kt, hf):
        first = 0 if hf == 1 else HEAD_DIM
        onehot = (lane_io - first) == (kt * (tk // SLC_BLOCK) + key_blk)
        keep = (lane_io >= HEAD_DIM) if hf == 1 else (lane_io < HEAD_DIM)
        return jnp.where(keep, k_tile, jnp.where(onehot, 1.0, 0.0).astype(BF16))

    def reset():
        m_ref[...] = jnp.full(m_ref.shape, M_FLOOR, F32)
        acc_ref[...] = jnp.zeros(acc_ref.shape, F32)

    def flash_group(qm_ref, k_ref, vt_ref, parts, indicator=False):
        if indicator:
            for pi, (kt, _) in enumerate(parts):
                k_tile = k_ref[pl.ds(pl.multiple_of(kt * tk, tk), tk), :]
                for hf in range(2):
                    kaug_ref[pi, hf] = keys_with_block_indicator(k_tile, kt, hf)

        def scores(idx):
            col_max = None
            for pi, (kt, bias_fn) in enumerate(parts):
                if indicator:
                    k_tile = kaug_ref[pi, idx % 2]
                else:
                    k_tile = k_ref[pl.ds(pl.multiple_of(kt * tk, tk), tk), :]
                s = _dot(k_tile, qm_ref[idx])
                if bias_fn is not None:
                    s = bias_fn(s)
                s_ref[idx, pi * tk:(pi + 1) * tk, :] = s
                part_max = jnp.max(s, axis=0, keepdims=True)
                col_max = part_max if col_max is None else jnp.maximum(col_max, part_max)
            return col_max

        n_units = 2 * n_r
        col_maxes = [scores(idx) for idx in range(min(SCORE_LEAD, n_units))]
        for idx in range(n_units):
            if idx + SCORE_LEAD < n_units:
                col_maxes.append(scores(idx + SCORE_LEAD))
            m_old = m_ref[idx:idx + 1, :]
            m_new = jnp.maximum(m_old, col_maxes[idx])
            m_ref[idx:idx + 1, :] = m_new
            rows = slice(idx * VT_ROWS, (idx + 1) * VT_ROWS)
            vrows = slice((idx % 2) * VT_ROWS, (idx % 2 + 1) * VT_ROWS)
            upd = acc_ref[rows, :] * jnp.exp2(m_old - m_new)
            for pi, (kt, _) in enumerate(parts):
                p = jnp.exp2(s_ref[idx, pi * tk:(pi + 1) * tk, :] - m_new).astype(BF16)
                upd = upd + _dot(vt_ref[kt, vrows, :], p)
            acc_ref[rows, :] = upd

    def finish(branch):
        for idx in range(2 * n_r):
            r, hf = divmod(idx, 2)
            base = idx * VT_ROWS
            l = acc_ref[base + HEAD_DIM:base + HEAD_DIM + 1, :]
            scale = gate_row(branch, idx) * jnp.where(l > 0.0, 1.0 / l, 0.0)
            hs = slice(hf * HEAD_DIM, (hf + 1) * HEAD_DIM)
            osum_ref[r, hs, :] += acc_ref[base:base + HEAD_DIM, :] * scale

    tri_causal, tri_tail, tri_open, tri_closed = 0, 1, 2, 3
    causal = lambda s: s + tri_ref[tri_causal]

    load_queries(qw_ref, with_selection_bias=False)
    reset()
    tiles_back = WINDOW // tk
    win_parts = []
    for back in range(tiles_back, -1, -1):
        kt = jnp.maximum(qt - back, 0)
        inside = tri_causal if back == 0 else (tri_tail if back == tiles_back else tri_open)
        table = inside if back == 0 else jnp.where(qt >= back, inside, tri_closed)
        win_parts.append((kt, lambda s, table=table: s + tri_ref[table]))
    flash_group(qw_ref, kwn_ref, vwnt_ref, win_parts)
    finish(2)
    compressed_softmax()

    select_blocks()
    load_queries(qs_ref, with_selection_bias=True)
    reset()
    full_groups = qt // SLC_GROUP

    def slc_full(j, carry):
        flash_group(qs_ref, ksl_ref, vslt_ref, [(SLC_GROUP * j + i, None) for i in range(SLC_GROUP)],
                    indicator=True)
        return carry

    lax.fori_loop(0, full_groups, slc_full, 0)

    for n_last in range(1, SLC_GROUP + 1):
        @pl.when(qt % SLC_GROUP == n_last - 1)
        def _(n_last=n_last):
            first = SLC_GROUP * full_groups
            parts = [(first + i, None) for i in range(n_last - 1)] + [(qt, causal)]
            flash_group(qs_ref, ksl_ref, vslt_ref, parts, indicator=True)

    finish(1)

    for r in range(n_r):
        z = zat_ref[r * LANES:(r + 1) * LANES, :].astype(F32)
        out_ref[:, r * LANES:(r + 1) * LANES] = (osum_ref[r] * _silu(z)).T.astype(BF16)


def _attn_call(qnt, qrt, kc_std, kc_t, kk, vvt, smt, gb, ovt, tri, zat, bsz, seq):
    tq, tk = ATTN_Q, ATTN_K
    n_qt = seq // tq
    n_kt = seq // tk
    qt_spec = pl.BlockSpec((HALF_W, tq), lambda b, c, t: (c, b * n_qt + t))
    out_spec = pl.BlockSpec((tq, HALF_W), lambda b, c, t: (b * n_qt + t, c))
    k_spec = lambda s: pl.BlockSpec((None, seq, LANES), lambda b, c, t: (s, b, c))
    vt_spec = lambda s: pl.BlockSpec((None, n_kt, 2 * VT_ROWS, tk), lambda b, c, t: (s, b, c, 0))
    const = lambda a: pl.BlockSpec(a.shape, lambda b, c, t: (0,) * a.ndim)
    return pl.pallas_call(
        _attn_kernel,
        grid=(bsz, N_PAIR_COLS, n_qt),
        in_specs=[qt_spec, qt_spec,
                  pl.BlockSpec((None, None, LANES, LANES), lambda b, c, t: (0, c, b, 0)),
                  pl.BlockSpec((None, None, LANES, LANES), lambda b, c, t: (1, c, 0, b)),
                  k_spec(0), k_spec(1), vt_spec(0), vt_spec(1),
                  pl.BlockSpec((SMALL_W, tq), lambda b, c, t: (0, b * n_qt + t)),
                  const(gb),
                  const(ovt), const(tri), qt_spec],
        out_specs=out_spec,
        out_shape=jax.ShapeDtypeStruct((bsz * seq, ATTN_WIDTH), BF16),
        scratch_shapes=[pltpu.VMEM((2 * Q_PER_KV, LANES, tq), BF16),
                        pltpu.VMEM((2 * Q_PER_KV, LANES, tq), BF16),
                        pltpu.VMEM((2, seq // SLC_BLOCK, tq), F32),
                        pltpu.VMEM((2 * Q_PER_KV, tq), F32),
                        pltpu.VMEM((2 * Q_PER_KV * VT_ROWS, tq), F32),
                        pltpu.VMEM((Q_PER_KV, LANES, tq), F32),
                        pltpu.VMEM((2 * Q_PER_KV, MAX_GROUP * tk, tq), F32),
                        pltpu.VMEM((SLC_GROUP, 2, tk, LANES), BF16),
                        pltpu.VMEM((SMALL_W, tq), F32),
                        pltpu.VMEM((2 * Q_PER_KV, LANES, tq), F32)],
        compiler_params=pltpu.CompilerParams(dimension_semantics=("arbitrary", "arbitrary", "arbitrary"),
                                             vmem_limit_bytes=VMEM_LIMIT),
        name="attn",
    )(qnt, qrt, kc_std, kc_t, kk, kk, vvt, vvt, smt, gb, ovt, tri, zat)


def _ssd_kernel(xbc_ref, sm_ref, cw_ref, cb_ref, dtb_ref, alog_ref, e16_ref, dskip_ref, zs_ref, nw_ref,
                shift_ref, head_ref, out_ref, carry_ref, state_ref, y_ref):
    L = SSD_CHUNK
    j = pl.program_id(1)
    n_prev = CONV_WIDTH - 1

    @pl.when(j == 0)
    def _():
        carry_ref[...] = jnp.zeros(carry_ref.shape, BF16)
        state_ref[...] = jnp.zeros(state_ref.shape, F32)

    u = xbc_ref[...]
    shifted = _dot(shift_ref[...], u)
    conv = cb_ref[...] + u.astype(F32) * cw_ref[n_prev:n_prev + 1, :]
    head = jnp.zeros((SUBLANES, CONV_CH), F32)
    for w in range(n_prev):
        conv = conv + shifted[w * L:(w + 1) * L] * cw_ref[w:w + 1, :]
        head = head + _dot(head_ref[w], carry_ref[...]) * cw_ref[w:w + 1, :]
    conv = jnp.concatenate([conv[0:SUBLANES] + head, conv[SUBLANES:]], axis=0)
    carry_ref[...] = u[L - CARRY_ROWS:L]
    act = _silu(conv)

    sm = sm_ref[...] + dtb_ref[...]
    dt = jnp.maximum(sm, 0.0) + jnp.log1p(jnp.exp(-jnp.abs(sm)))
    a = dt * (-jnp.exp(alog_ref[...]) * LOG2E)
    tril = lax.broadcasted_iota(jnp.int32, (L, L), 0) >= lax.broadcasted_iota(jnp.int32, (L, L), 1)
    a_cs = _select_dot(jnp.where(tril, 1.0, 0.0).astype(BF16), a)
    a_cs_t = a_cs.T
    e16 = e16_ref[...]
    a_exp = _dot_select(a_cs, e16)
    dt_exp = _dot_select(dt, e16)
    a_last = a_exp[L - 1:L, :]
    x_c = act[:, 0:SSM_WIDTH]
    xdt = x_c * dt_exp
    xdt_bf = xdt.astype(BF16)
    xw = (xdt * jnp.exp2(a_last - a_exp)).astype(BF16)
    ea = jnp.exp2(a_exp)
    chunk_decay = jnp.exp2(a_last)
    lane = lax.broadcasted_iota(jnp.int32, (1, LANES), 1)
    lo = lane < SSM_HEAD_DIM
    halves = (lo, jnp.logical_not(lo))

    for g in range(SSM_GROUPS):
        bg = act[:, SSM_WIDTH + g * SSM_STATE: SSM_WIDTH + (g + 1) * SSM_STATE]
        cg = act[:, SSM_WIDTH + SSM_GROUPS * SSM_STATE + g * SSM_STATE:
                 SSM_WIDTH + SSM_GROUPS * SSM_STATE + (g + 1) * SSM_STATE].astype(BF16)
        cb = _dot_nt(cg, bg.astype(BF16))
        bg_t = bg.T.astype(BF16)
        for i in range(2 * g, 2 * g + 2):
            sl = slice(i * LANES, (i + 1) * LANES)
            y = jnp.zeros((L, LANES), F32)
            for hh in range(2):
                h = 2 * i + hh
                col = a_cs[:, DT_LANE0 + h:DT_LANE0 + h + 1]
                row = a_cs_t[DT_LANE0 + h:DT_LANE0 + h + 1, :]
                decay = jnp.exp2(jnp.where(tril, col - row, NEG))
                xh = jnp.where(halves[hh], xdt_bf[:, sl], jnp.zeros((), BF16))
                y = y + _dot((cb * decay).astype(BF16), xh)
            st = state_ref[i]
            y = y + _dot(cg, st.astype(BF16)) * ea[:, sl]
            state_ref[i] = st * chunk_decay[:, sl] + _dot(bg_t, xw[:, sl])
            y_ref[:, sl] = y + dskip_ref[:, sl] * x_c[:, sl]

    y = y_ref[...] * _silu(zs_ref[...].astype(F32))
    y = y * lax.rsqrt(jnp.mean(y * y, axis=-1, keepdims=True) + EPS) * nw_ref[...]
    out_ref[...] = y.astype(BF16)


def _ssd_call(xbc, sm, cw, cb, dtb, alog, e16, dskip, zs, nw, shift, head, bsz, seq):
    L = SSD_CHUNK
    n_ch = seq // L
    row = lambda w: pl.BlockSpec((L, w), lambda b, j: (b * n_ch + j, 0))
    const = lambda a: pl.BlockSpec(a.shape, lambda b, j: (0,) * a.ndim)
    return pl.pallas_call(
        _ssd_kernel,
        grid=(bsz, n_ch),
        in_specs=[row(CONV_CH), row(SMALL_W), const(cw), const(cb), const(dtb), const(alog), const(e16),
                  const(dskip), row(SSM_WIDTH), const(nw), const(shift), const(head)],
        out_specs=row(SSM_WIDTH),
        out_shape=jax.ShapeDtypeStruct((bsz * seq, SSM_WIDTH), BF16),
        scratch_shapes=[pltpu.VMEM((CARRY_ROWS, CONV_CH), BF16),
                        pltpu.VMEM((SSM_HEADS // 2, SSM_STATE, LANES), F32),
                        pltpu.VMEM((L, SSM_WIDTH), F32)],
        compiler_params=pltpu.CompilerParams(dimension_semantics=("arbitrary", "arbitrary"),
                                             vmem_limit_bytes=VMEM_LIMIT),
        name="ssd",
    )(xbc, sm, cw, cb, dtb, alog, e16, dskip, zs, nw, shift, head)


def _out_kernel(att_ref, ssm_ref, x_ref, wa_ref, ws_ref, pw_ref, out_ref):
    o = _dot(att_ref[...], wa_ref[...]) + _dot(ssm_ref[...], ws_ref[...])
    o = o * lax.rsqrt(jnp.mean(o * o, axis=-1, keepdims=True) + EPS) * pw_ref[...]
    out_ref[...] = x_ref[...] + o


def _out_call(att, ssm, x2, wa, ws, pw):
    rows = x2.shape[0]
    tm = OUT_ROWS
    row = lambda w: pl.BlockSpec((tm, w), lambda i: (i, 0))
    return pl.pallas_call(
        _out_kernel,
        grid=(rows // tm,),
        in_specs=[row(ATTN_WIDTH), row(SSM_WIDTH), row(D_MODEL), _resident(wa.shape), _resident(ws.shape),
                  _resident(pw.shape)],
        out_specs=row(D_MODEL),
        out_shape=jax.ShapeDtypeStruct((rows, D_MODEL), F32),
        compiler_params=pltpu.CompilerParams(dimension_semantics=("arbitrary",),
                                             vmem_limit_bytes=VMEM_LIMIT),
        name="outproj",
    )(att, ssm, x2, wa, ws, pw)


def _pair_heads(t, axis, per_head):
    shp = t.shape
    t = t.reshape(shp[:axis] + (N_PAIR_COLS, 2, Q_PER_KV, per_head) + shp[axis + 1:])
    return jnp.swapaxes(t, axis + 1, axis + 2).reshape(shp)


def _constants(seq):

    nc = (seq - CMP_BLOCK) // CMP_STRIDE + 1
    nb = seq // SLC_BLOCK
    ci = np.arange(nc)[:, None] * CMP_STRIDE
    bj = np.arange(nb)[None, :] * SLC_BLOCK
    overlap = ((ci <= bj + SLC_BLOCK - 1) & (ci + CMP_BLOCK - 1 >= bj)).astype(np.float32)
    ovt = np.zeros((LANES, LANES), np.float32)
    ovt[:nb, :nc] = overlap.T

    e16 = np.zeros((SMALL_W, SSM_WIDTH), np.float32)
    for h in range(SSM_HEADS):
        e16[DT_LANE0 + h, h * SSM_HEAD_DIM:(h + 1) * SSM_HEAD_DIM] = 1.0

    key = np.arange(ATTN_K)[:, None]
    qry = np.arange(ATTN_Q)[None, :]
    tri = np.stack([np.where(key <= qry, 0.0, NEG), np.where(key > qry, 0.0, NEG),
                    np.zeros((ATTN_K, ATTN_Q)), np.full((ATTN_K, ATTN_Q), NEG)]).astype(np.float32)
    n_prev = CONV_WIDTH - 1
    shift = np.zeros((n_prev * SSD_CHUNK, SSD_CHUNK), np.float32)
    head = np.zeros((n_prev, SUBLANES, CARRY_ROWS), np.float32)
    for w in range(n_prev):
        back = n_prev - w
        for t in range(SSD_CHUNK):
            if t - back >= 0:
                shift[w * SSD_CHUNK + t, t - back] = 1.0
            elif t < SUBLANES:
                head[w, t, CARRY_ROWS + t - back] = 1.0
    n_grp = PROJ_ROWS // CMP_STRIDE
    pick = np.zeros((PROJ_ROWS, PROJ_ROWS), np.float32)
    for t in range(CMP_STRIDE):
        pick[t * n_grp + np.arange(n_grp), np.arange(n_grp) * CMP_STRIDE + t] = 1.0
    return ovt, e16, tri, shift, head, pick


def _rope_tables(seq):
    half_freqs = ROPE_THETA ** (-np.arange(ROPE_HALF, dtype=np.float64) * 2.0 / ROPE_DIM)
    ang = np.arange(seq, dtype=np.float64)[:, None] * half_freqs[None, :]
    cos, sin = np.cos(ang), np.sin(ang)
    ones = np.ones((seq, HEAD_DIM - ROPE_DIM))
    zeros_h = np.zeros((seq, ROPE_HALF))
    zeros_r = np.zeros((seq, HEAD_DIM - ROPE_DIM))
    cos_h = np.concatenate([cos, cos, ones], axis=1)
    sina_h = np.concatenate([-sin, zeros_h, zeros_r], axis=1)
    sinb_h = np.concatenate([zeros_h, sin, zeros_r], axis=1)
    tile2 = lambda t: np.concatenate([t, t], axis=1)
    return tuple(jnp.asarray(t, F32) for t in (tile2(cos_h), tile2(sina_h), tile2(sinb_h), cos.T, sin.T))


def _layer(x, w_in, w_out, pre_w, post_w, cmp_pos, cmp_w1, cmp_b1, cmp_w2, cmp_b2,
           gate_b, conv_w, conv_b, dt_bias, a_log, d_skip, ssm_norm_w):
    bsz, seq, _ = x.shape
    ovt, e16, tri, shift, head, pick = _constants(seq)
    tabs = _rope_tables(seq)

    w_t = _cast_call(jnp.swapaxes(w_in, 0, 1))
    pad_small = lambda v, at: jnp.zeros((1, SMALL_W), F32).at[0, at:at + v.shape[0]].set(v)
    gb = pad_small(gate_b, 0).T
    dtb = pad_small(dt_bias, DT_LANE0)
    alog = pad_small(a_log, DT_LANE0)

    x2 = x.reshape(bsz * seq, D_MODEL)
    qnt, qrt, cm, kk, vvt, zat, zs, xbc, sm, smt = _proj_call(x2, pre_w[None, :], w_t, jnp.asarray(pick, BF16), tabs, seq)

    w1r = cmp_w1.reshape(2, 2, CMP_STRIDE, HEAD_DIM, CMP_HIDDEN)
    wa, wb = w1r[:, 0], w1r[:, 1]
    z1 = jnp.zeros_like(wa)
    bd1 = jnp.concatenate([jnp.concatenate([wa, z1, wb, z1], axis=-1),
                           jnp.concatenate([z1, wa, z1, wb], axis=-1)], axis=-2).astype(BF16)
    bd1 = bd1.reshape(2, CMP_STRIDE * LANES, 4 * CMP_HIDDEN)
    z2 = jnp.zeros_like(cmp_w2)
    bd2 = jnp.concatenate([jnp.concatenate([cmp_w2, z2], axis=-1),
                           jnp.concatenate([z2, cmp_w2], axis=-1)], axis=-2).astype(BF16)
    pos8 = jnp.broadcast_to(cmp_pos.reshape(2, 1, CMP_BLOCK * HEAD_DIM), (2, SUBLANES, CMP_BLOCK * HEAD_DIM))
    b2 = jnp.concatenate([cmp_b2, cmp_b2], axis=-1)[:, None, :]
    kc_std, kc_t = _compress_call(cm, bd1, cmp_w1, pos8, cmp_b1[:, None, :], bd2, b2,
                                  jnp.swapaxes(bd2, 1, 2), jnp.swapaxes(b2, 1, 2))

    att = _attn_call(qnt, qrt, kc_std, kc_t, kk, vvt, smt, gb, jnp.asarray(ovt, BF16),
                     jnp.asarray(tri), zat, bsz, seq)

    dskip = jnp.repeat(d_skip, SSM_HEAD_DIM)[None, :]
    ssm = _ssd_call(xbc, sm, conv_w, conv_b[None, :], dtb, alog, jnp.asarray(e16, BF16), dskip, zs,
                    ssm_norm_w[None, :], jnp.asarray(shift, BF16), jnp.asarray(head, BF16), bsz, seq)

    wo_a = _pair_heads(w_out[0:ATTN_WIDTH], 0, HEAD_DIM).astype(BF16)
    wo_s = w_out[ATTN_WIDTH:].astype(BF16)
    out = _out_call(att, ssm, x2, wo_a, wo_s, post_w[None, :])
    return out.reshape(bsz, seq, D_MODEL)


def kernel(x, w_in, w_out, pre_norm_w, post_norm_w, cmp_pos, cmp_w1, cmp_b1, cmp_w2, cmp_b2, gate_b, conv_w,
           conv_b, dt_bias, a_log, d_skip, ssm_norm_w):
    for l in range(w_in.shape[0]):
        x = _layer(x, w_in[l], w_out[l], pre_norm_w[l], post_norm_w[l], cmp_pos[l], cmp_w1[l], cmp_b1[l],
                   cmp_w2[l], cmp_b2[l], gate_b[l], conv_w[l], conv_b[l], dt_bias[l], a_log[l], d_skip[l],
                   ssm_norm_w[l])
    return x
```

```python
import numpy as np
import jax
import jax.numpy as jnp
from jax import lax
from jax.experimental import pallas as pl
from jax.experimental.pallas import tpu as pltpu

F32 = jnp.float32
BF16 = jnp.bfloat16

D_MODEL = 1024
ATTN_HEADS = 16
HEAD_DIM = 64
ATTN_WIDTH = ATTN_HEADS * HEAD_DIM
KV_HEADS = 4
Q_PER_KV = ATTN_HEADS // KV_HEADS
KV_WIDTH = KV_HEADS * HEAD_DIM
ROPE_DIM = HEAD_DIM // 4
ROPE_HALF = ROPE_DIM // 2
ROPE_THETA = 500000.0
CMP_BLOCK = 32
CMP_STRIDE = 16
CMP_HIDDEN = 256
SLC_BLOCK = 64
SLC_TOPN = 16
WINDOW = 512
N_BRANCH = 3
SSM_HEADS = 16
SSM_HEAD_DIM = 64
SSM_WIDTH = SSM_HEADS * SSM_HEAD_DIM
SSM_GROUPS = 4
SSM_STATE = 128
CONV_WIDTH = 4
CONV_CH = SSM_WIDTH + 2 * SSM_GROUPS * SSM_STATE
MIX_WIDTH = ATTN_WIDTH + SSM_WIDTH
EPS = 1e-6
NEG = -1e30
BIG = 1e30
M_FLOOR = -1e29
LOG2E = 1.4426950408889634
Q_SCALE = HEAD_DIM ** -0.5 * LOG2E

LANES = 128
SUBLANES = 8
N_PAIR_COLS = KV_HEADS // 2
HALF_W = ATTN_WIDTH // N_PAIR_COLS
SMALL_W = LANES
ONES_ROWS = 16
VT_ROWS = HEAD_DIM + ONES_ROWS
DT_LANE0 = ATTN_HEADS * N_BRANCH
VMEM_LIMIT = 56 * 1024 * 1024

PROJ_ROWS = 512
W_CAST_ROWS = 1344

W_ROWS = dict(q=ATTN_WIDTH, cm=2 * KV_WIDTH, k_sl=KV_WIDTH, v_sl=KV_WIDTH, k_wn=KV_WIDTH, v_wn=KV_WIDTH,
              gate=ATTN_HEADS * N_BRANCH, z_att=ATTN_WIDTH, z_ssm=SSM_WIDTH, xbc=CONV_CH, dt=SSM_HEADS)
W_OFF = dict(zip(W_ROWS, np.cumsum([0] + list(W_ROWS.values())[:-1]).tolist()))
HEAD_OF_SLOT = [(2 * c + gp) * Q_PER_KV + r for c in range(N_PAIR_COLS) for r in range(Q_PER_KV) for gp in range(2)]
ATTN_Q = 256
ATTN_K = 256
SLC_GROUP = 4
MAX_GROUP = max(SLC_GROUP, WINDOW // ATTN_K + 1)
SCORE_LEAD = 3
SSD_CHUNK = 256
CARRY_ROWS = 16
OUT_ROWS = 1024


def _dot(a, b):
    return jnp.dot(a, b, preferred_element_type=F32)


def _dot_nt(a, b):
    return lax.dot_general(a, b, (((1,), (1,)), ((), ())), preferred_element_type=F32)


def _split3(x):
    x1 = x.astype(BF16)
    r1 = x - x1.astype(F32)
    x2 = r1.astype(BF16)
    x3 = (r1 - x2.astype(F32)).astype(BF16)
    return x1, x2, x3


def _dot_select(x, sel):
    x1, x2, x3 = _split3(x)
    return _dot(x1, sel) + _dot(x2, sel) + _dot(x3, sel)


def _select_dot(sel, x):
    x1, x2, x3 = _split3(x)
    return _dot(sel, x1) + _dot(sel, x2) + _dot(sel, x3)


def _sigmoid(x):
    return 1.0 / (1.0 + jnp.exp2(x * -LOG2E))


def _silu(x):
    return x * _sigmoid(x)


def _rope_chunk(c, cos_t, sin_a, sin_b):
    return c * cos_t + pltpu.roll(c, LANES - ROPE_HALF, 1) * sin_a + pltpu.roll(c, ROPE_HALF, 1) * sin_b


def _proj_kernel(x_ref, x_next_ref, pre_w_ref, wt_ref, pick_ref, cos_ref, sina_ref, sinb_ref, cos8_ref, sin8_ref,
                 qnt_ref, qrt_ref, cm_ref, kk_ref, vvt_ref, zat_ref, zs_ref, xbc_ref, sm_ref, smt_ref, h_scr):
    def normed(ref):
        x = ref[...]
        return (x * lax.rsqrt(jnp.mean(x * x, axis=-1, keepdims=True) + EPS) * pre_w_ref[...]).astype(BF16)

    @pl.when(pl.program_id(0) == 0)
    def _():
        h_scr[...] = normed(x_ref)

    h = h_scr[...]
    seg = lambda name: wt_ref[W_OFF[name]:W_OFF[name] + W_ROWS[name], :]

    qt = _dot_nt(seg("q"), h)
    zat = _dot_nt(seg("z_att"), h).astype(BF16)
    cos8, sin8 = cos8_ref[...], sin8_ref[...]
    for slot, head in enumerate(HEAD_OF_SLOT):
        src = slice(head * HEAD_DIM, (head + 1) * HEAD_DIM)
        dst = slice(slot * HEAD_DIM, (slot + 1) * HEAD_DIM)
        q_head = qt[src]
        t1, t2 = q_head[0:ROPE_HALF], q_head[ROPE_HALF:ROPE_DIM]
        rot = jnp.concatenate([t1 * cos8 - t2 * sin8, t2 * cos8 + t1 * sin8, q_head[ROPE_DIM:]], axis=0)
        qnt_ref[dst, :] = q_head.astype(BF16)
        qrt_ref[dst, :] = rot.astype(BF16)
        zat_ref[dst, :] = zat[src]

    smt = jnp.concatenate([_dot_nt(seg("gate"), h), _dot_nt(seg("dt"), h),
                           jnp.zeros((SMALL_W - DT_LANE0 - SSM_HEADS, PROJ_ROWS), F32)], axis=0)
    smt_ref[...] = smt
    sm_ref[...] = smt.T

    cm = _dot_nt(h, seg("cm")).astype(BF16)
    by_offset = _dot(pick_ref[...], cm).astype(BF16)
    n_grp = PROJ_ROWS // CMP_STRIDE
    for t in range(CMP_STRIDE):
        for k in range(2 * N_PAIR_COLS):
            cm_ref[k, :, t * LANES:(t + 1) * LANES] = by_offset[t * n_grp:(t + 1) * n_grp, k * LANES:(k + 1) * LANES]

    cos_t, sin_a, sin_b = cos_ref[...], sina_ref[...], sinb_ref[...]
    ones = jnp.ones((ONES_ROWS, ATTN_K), BF16)
    for t, (k_name, v_name) in enumerate((("k_sl", "v_sl"), ("k_wn", "v_wn"))):
        kk = _dot_nt(h, seg(k_name))
        for k in range(N_PAIR_COLS):
            c = kk[:, k * LANES:(k + 1) * LANES]
            kk_ref[t, :, k * LANES:(k + 1) * LANES] = _rope_chunk(c, cos_t, sin_a, sin_b).astype(BF16)
        vt = _dot_nt(seg(v_name), h).astype(BF16)
        for j in range(PROJ_ROWS // ATTN_K):
            for g in range(KV_HEADS):
                vvt_ref[t, j, g * VT_ROWS:g * VT_ROWS + HEAD_DIM, :] = vt[g * HEAD_DIM:(g + 1) * HEAD_DIM,
                                                                          j * ATTN_K:(j + 1) * ATTN_K]
                vvt_ref[t, j, g * VT_ROWS + HEAD_DIM:(g + 1) * VT_ROWS, :] = ones

    zs_ref[...] = _dot_nt(h, seg("z_ssm")).astype(BF16)
    xbc_ref[...] = _dot_nt(h, seg("xbc")).astype(BF16)
    h_scr[...] = normed(x_next_ref)


def _cast_kernel(w_ref, out_ref):
    row = pl.program_id(0) * W_CAST_ROWS + lax.broadcasted_iota(jnp.int32, (W_CAST_ROWS, 1), 0)
    out_ref[...] = (w_ref[...] * jnp.where(row < ATTN_WIDTH, Q_SCALE, 1.0)).astype(BF16)


def _cast_call(w_t):
    rows, width = w_t.shape
    spec = pl.BlockSpec((W_CAST_ROWS, width), lambda i: (i, 0))
    return pl.pallas_call(
        _cast_kernel,
        grid=(rows // W_CAST_ROWS,),
        in_specs=[spec],
        out_specs=spec,
        out_shape=jax.ShapeDtypeStruct((rows, width), BF16),
        compiler_params=pltpu.CompilerParams(dimension_semantics=("arbitrary",)),
        name="wcast",
    )(w_t)


def _resident(shape):
    nd = len(shape)
    return pl.BlockSpec(shape, lambda *_: (0,) * nd, pipeline_mode=pl.Buffered(1))


def _proj_call(x2, pre_w, wt, pick, tabs, seq):
    rows = x2.shape[0]
    tm = PROJ_ROWS
    n_seq_tiles = seq // tm
    k_per_tile = tm // ATTN_K
    row_spec = lambda w: pl.BlockSpec((tm, w), lambda i: (i, 0))
    col_spec = lambda h: pl.BlockSpec((h, tm), lambda i: (0, i))
    tab_spec = pl.BlockSpec((tm, LANES), lambda i: (i % n_seq_tiles, 0))
    tab8_spec = pl.BlockSpec((ROPE_HALF, tm), lambda i: (0, i % n_seq_tiles))
    return pl.pallas_call(
        _proj_kernel,
        grid=(rows // tm,),
        in_specs=[row_spec(D_MODEL), pl.BlockSpec((tm, D_MODEL), lambda i: (jnp.minimum(i + 1, rows // tm - 1), 0)),
                  _resident(pre_w.shape), _resident(wt.shape), _resident(pick.shape),
                  tab_spec, tab_spec, tab_spec, tab8_spec, tab8_spec],
        out_specs=[col_spec(ATTN_WIDTH), col_spec(ATTN_WIDTH),
                   pl.BlockSpec((2 * N_PAIR_COLS, tm // CMP_STRIDE, CMP_STRIDE * LANES), lambda i: (0, i, 0)),
                   pl.BlockSpec((2, tm, KV_WIDTH), lambda i: (0, i, 0)),
                   pl.BlockSpec((2, k_per_tile, KV_HEADS * VT_ROWS, ATTN_K), lambda i: (0, i, 0, 0)),
                   col_spec(ATTN_WIDTH), row_spec(SSM_WIDTH), row_spec(CONV_CH), row_spec(SMALL_W),
                   col_spec(SMALL_W)],
        out_shape=[jax.ShapeDtypeStruct((ATTN_WIDTH, rows), BF16),
                   jax.ShapeDtypeStruct((ATTN_WIDTH, rows), BF16),
                   jax.ShapeDtypeStruct((2 * N_PAIR_COLS, rows // CMP_STRIDE, CMP_STRIDE * LANES), BF16),
                   jax.ShapeDtypeStruct((2, rows, KV_WIDTH), BF16),
                   jax.ShapeDtypeStruct((2, rows // ATTN_K, KV_HEADS * VT_ROWS, ATTN_K), BF16),
                   jax.ShapeDtypeStruct((ATTN_WIDTH, rows), BF16),
                   jax.ShapeDtypeStruct((rows, SSM_WIDTH), BF16),
                   jax.ShapeDtypeStruct((rows, CONV_CH), BF16),
                   jax.ShapeDtypeStruct((rows, SMALL_W), F32),
                   jax.ShapeDtypeStruct((SMALL_W, rows), F32)],
        compiler_params=pltpu.CompilerParams(dimension_semantics=("arbitrary",),
                                             vmem_limit_bytes=VMEM_LIMIT),
        scratch_shapes=[pltpu.VMEM((tm, D_MODEL), BF16)],
        name="proj",
    )(x2, x2, pre_w, wt, pick, *tabs)


def _compress_kernel(x_ref, bd1_ref, w1_ref, pos_ref, b1_ref, bd2_ref, b2_ref, bd2t_ref, b2t_ref,
                     out_ref, outt_ref, acc_ref):
    n_rows = x_ref.shape[0]
    hid2 = 2 * CMP_HIDDEN
    acc_ref[n_rows:, :] = jnp.zeros((SUBLANES, 2 * hid2), F32)
    acc_ref[0:n_rows, :] = _dot(x_ref[...], bd1_ref[...])
    first = acc_ref[0:n_rows, 0:hid2]
    second = acc_ref[pl.ds(1, n_rows), hid2:2 * hid2]
    posterm = _dot(pos_ref[...].astype(BF16), w1_ref[...].astype(BF16))[0:1, :] + b1_ref[...]
    hcat = first + second + jnp.concatenate([posterm, posterm], axis=1)
    act = _silu(hcat).astype(BF16)
    out_ref[...] = _dot(act, bd2_ref[...]) + b2_ref[...]
    outt_ref[...] = _dot_nt(bd2t_ref[...], act) + b2t_ref[...]


def _compress_call(cm4, bd1, w1, pos8, b1, bd2, b2, bd2t, b2t):
    n_rows = cm4.shape[1]
    per_s = lambda *tail: pl.BlockSpec((None,) + tail, lambda s, c: (s,) + (0,) * len(tail))
    return pl.pallas_call(
        _compress_kernel,
        grid=(2, N_PAIR_COLS),
        in_specs=[pl.BlockSpec((None, n_rows, CMP_STRIDE * LANES), lambda s, c: (s * N_PAIR_COLS + c, 0, 0)),
                  per_s(CMP_STRIDE * LANES, 4 * CMP_HIDDEN),
                  per_s(CMP_BLOCK * HEAD_DIM, CMP_HIDDEN),
                  per_s(SUBLANES, CMP_BLOCK * HEAD_DIM),
                  per_s(1, CMP_HIDDEN),
                  per_s(2 * CMP_HIDDEN, LANES),
                  per_s(1, LANES),
                  per_s(LANES, 2 * CMP_HIDDEN),
                  per_s(LANES, 1)],
        out_specs=[pl.BlockSpec((None, None, n_rows, LANES), lambda s, c: (s, c, 0, 0)),
                   pl.BlockSpec((None, None, LANES, n_rows), lambda s, c: (s, c, 0, 0))],
        out_shape=[jax.ShapeDtypeStruct((2, N_PAIR_COLS, n_rows, LANES), F32),
                   jax.ShapeDtypeStruct((2, N_PAIR_COLS, LANES, n_rows), F32)],
        scratch_shapes=[pltpu.VMEM((n_rows + SUBLANES, 4 * CMP_HIDDEN), F32)],
        compiler_params=pltpu.CompilerParams(dimension_semantics=("arbitrary", "arbitrary"),
                                             vmem_limit_bytes=VMEM_LIMIT),
        name="compress",
    )(cm4, bd1, w1, pos8, b1, bd2, b2, bd2t, b2t)


def _attn_kernel(qnt_ref, qrt_ref, kc_ref, vct_ref, ksl_ref, kwn_ref, vslt_ref, vwnt_ref, smt_ref, gb_ref,
                 ovt_ref, tri_ref, zat_ref, out_ref,
                 qw_ref, qs_ref, bias_ref, m_ref, acc_ref, osum_ref, s_ref, kaug_ref, gate_ref, cs_ref):
    tq, tk = ATTN_Q, ATTN_K
    n_r = Q_PER_KV
    qt = pl.program_id(2)
    q0 = qt * tq
    row = lax.broadcasted_iota(jnp.int32, (LANES, 1), 0)
    halves = (row < HEAD_DIM, row >= HEAD_DIM)
    qpos = q0 + lax.broadcasted_iota(jnp.int32, (1, tq), 1)

    gate_ref[...] = _sigmoid(smt_ref[...] + gb_ref[...])
    first_group = pl.program_id(1) * 2

    def gate_row(br, idx):
        head = (first_group + idx % 2) * n_r + idx // 2
        return gate_ref[pl.ds(head * N_BRANCH + br, 1), :]

    kc = kc_ref[...].astype(BF16)
    vct = vct_ref[...].astype(BF16)
    cpos = row * CMP_STRIDE + (CMP_BLOCK - 1)
    cbias = jnp.where(cpos <= qpos, 0.0, NEG)
    psum = [jnp.zeros((LANES, tq), F32), jnp.zeros((LANES, tq), F32)]
    osum_ref[...] = jnp.zeros(osum_ref.shape, F32)
    for idx in range(2 * n_r):
        q = qnt_ref[(idx // 2) * LANES:(idx // 2 + 1) * LANES, :]
        qm = jnp.where(halves[idx % 2], q, jnp.zeros_like(q))
        cs_ref[idx] = _dot(kc, qm) + cbias

    def compressed_softmax():
        for idx in range(2 * n_r):
            r, hf = divmod(idx, 2)
            s = cs_ref[idx]
            m = jnp.maximum(jnp.max(s, axis=0, keepdims=True), M_FLOOR)
            p = jnp.exp2(s - m)
            l = jnp.sum(p, axis=0, keepdims=True)
            p = p * jnp.where(l > 0.0, 1.0 / l, 0.0)
            psum[hf] = psum[hf] + p
            hs = slice(hf * HEAD_DIM, (hf + 1) * HEAD_DIM)
            osum_ref[r, hs, :] += gate_row(0, idx) * _dot(vct[hs, :], p.astype(BF16))

    nb = ksl_ref.shape[0] // SLC_BLOCK
    jio = lax.broadcasted_iota(jnp.int32, (nb, tq), 0)
    cur = (q0 + lax.broadcasted_iota(jnp.int32, (nb, tq), 1)) // SLC_BLOCK
    forced = (jio == 0) | (jio == cur) | (jio == cur - 1)

    def select_blocks():
        for hf in range(2):
            imp = _select_dot(ovt_ref[...], psum[hf])[0:nb, :]
            imp = jnp.where(forced, BIG, imp)
            imp = jnp.where(jio > cur, -BIG, imp)
            cnt = jnp.zeros((nb, tq), F32)
            for i in range(nb):
                other = imp[i:i + 1, :]
                beats = (other > imp) | ((other == imp) & (jio > i))
                cnt = cnt + jnp.where(beats, 1.0, 0.0)
            bias_ref[hf] = jnp.where(cnt < float(SLC_TOPN), 0.0, NEG)

    def load_queries(qm_ref, with_selection_bias):
        spare = jnp.zeros((HEAD_DIM - nb, tq), BF16)
        for r in range(n_r):
            q = qrt_ref[r * LANES:(r + 1) * LANES, :]
            for hf in range(2):
                if with_selection_bias:
                    fill = [bias_ref[hf].astype(BF16), spare]
                else:
                    fill = [jnp.zeros((HEAD_DIM, tq), BF16)]
                pieces = [q[0:HEAD_DIM]] + fill if hf == 0 else fill + [q[HEAD_DIM:]]
                qm_ref[r * 2 + hf] = jnp.concatenate(pieces, axis=0)

    lane_io = lax.broadcasted_iota(jnp.int32, (1, LANES), 1)
    key_blk = lax.broadcasted_iota(jnp.int32, (tk, 1), 0) // SLC_BLOCK

    def keys_with_block_indicator(k_tile, kt, hf):
        first = 0 if hf == 1 else HEAD_DIM
        onehot = (lane_io - first) == (kt * (tk // SLC_BLOCK) + key_blk)
        keep = (lane_io >= HEAD_DIM) if hf == 1 else (lane_io < HEAD_DIM)
        return jnp.where(keep, k_tile, jnp.where(onehot, 1.0, 0.0).astype(BF16))

    def reset():
        m_ref[...] = jnp.full(m_ref.shape, M_FLOOR, F32)
        acc_ref[...] = jnp.zeros(acc_ref.shape, F32)

    def flash_group(qm_ref, k_ref, vt_ref, parts, indicator=False):
        if indicator:
            for pi, (kt, _) in enumerate(parts):
                k_tile = k_ref[pl.ds(pl.multiple_of(kt * tk, tk), tk), :]
                for hf in range(2):
                    kaug_ref[pi, hf] = keys_with_block_indicator(k_tile, kt, hf)

        def scores(idx):
            col_max = None
            for pi, (kt, bias_fn) in enumerate(parts):
                if indicator:
                    k_tile = kaug_ref[pi, idx % 2]
                else:
                    k_tile = k_ref[pl.ds(pl.multiple_of(kt * tk, tk), tk), :]
                s = _dot(k_tile, qm_ref[idx])
                if bias_fn is not None:
                    s = bias_fn(s)
                s_ref[idx, pi * tk:(pi + 1) * tk, :] = s
                part_max = jnp.max(s, axis=0, keepdims=True)
                col_max = part_max if col_max is None else jnp.maximum(col_max, part_max)
            return col_max

        n_units = 2 * n_r
        col_maxes = [scores(idx) for idx in range(min(SCORE_LEAD, n_units))]
        for idx in range(n_units):
            if idx + SCORE_LEAD < n_units:
                col_maxes.append(scores(idx + SCORE_LEAD))
            m_old = m_ref[idx:idx + 1, :]
            m_new = jnp.maximum(m_old, col_maxes[idx])
            m_ref[idx:idx + 1, :] = m_new
            rows = slice(idx * VT_ROWS, (idx + 1) * VT_ROWS)
            vrows = slice((idx % 2) * VT_ROWS, (idx % 2 + 1) * VT_ROWS)
            upd = acc_ref[rows, :] * jnp.exp2(m_old - m_new)
            for pi, (kt, _) in enumerate(parts):
                p = jnp.exp2(s_ref[idx, pi * tk:(pi + 1) * tk, :] - m_new).astype(BF16)
                upd = upd + _dot(vt_ref[kt, vrows, :], p)
            acc_ref[rows, :] = upd

    def finish(branch):
        for idx in range(2 * n_r):
            r, hf = divmod(idx, 2)
            base = idx * VT_ROWS
            l = acc_ref[base + HEAD_DIM:base + HEAD_DIM + 1, :]
            scale = gate_row(branch, idx) * jnp.where(l > 0.0, 1.0 / l, 0.0)
            hs = slice(hf * HEAD_DIM, (hf + 1) * HEAD_DIM)
            osum_ref[r, hs, :] += acc_ref[base:base + HEAD_DIM, :] * scale

    tri_causal, tri_tail, tri_open, tri_closed = 0, 1, 2, 3
    causal = lambda s: s + tri_ref[tri_causal]

    load_queries(qw_ref, with_selection_bias=False)
    reset()
    tiles_back = WINDOW // tk
    win_parts = []
    for back in range(tiles_back, -1, -1):
        kt = jnp.maximum(qt - back, 0)
        inside = tri_causal if back == 0 else (tri_tail if back == tiles_back else tri_open)
        table = inside if back == 0 else jnp.where(qt >= back, inside, tri_closed)
        win_parts.append((kt, lambda s, table=table: s + tri_ref[table]))
    flash_group(qw_ref, kwn_ref, vwnt_ref, win_parts)
    finish(2)
    compressed_softmax()

    select_blocks()
    load_queries(qs_ref, with_selection_bias=True)
    reset()
    full_groups = qt // SLC_GROUP

    def slc_full(j, carry):
        flash_group(qs_ref, ksl_ref, vslt_ref, [(SLC_GROUP * j + i, None) for i in range(SLC_GROUP)],
                    indicator=True)
        return carry

    lax.fori_loop(0, full_groups, slc_full, 0)

    for n_last in range(1, SLC_GROUP + 1):
        @pl.when(qt % SLC_GROUP == n_last - 1)
        def _(n_last=n_last):
            first = SLC_GROUP * full_groups
            parts = [(first + i, None) for i in range(n_last - 1)] + [(qt, causal)]
            flash_group(qs_ref, ksl_ref, vslt_ref, parts, indicator=True)

    finish(1)

    for r in range(n_r):
        z = zat_ref[r * LANES:(r + 1) * LANES, :].astype(F32)
        out_ref[:, r * LANES:(r + 1) * LANES] = (osum_ref[r] * _silu(z)).T.astype(BF16)


def _attn_call(qnt, qrt, kc_std, kc_t, kk, vvt, smt, gb, ovt, tri, zat, bsz, seq):
    tq, tk = ATTN_Q, ATTN_K
    n_qt = seq // tq
    n_kt = seq // tk
    qt_spec = pl.BlockSpec((HALF_W, tq), lambda b, c, t: (c, b * n_qt + t))
    out_spec = pl.BlockSpec((tq, HALF_W), lambda b, c, t: (b * n_qt + t, c))
    k_spec = lambda s: pl.BlockSpec((None, seq, LANES), lambda b, c, t: (s, b, c))
    vt_spec = lambda s: pl.BlockSpec((None, n_kt, 2 * VT_ROWS, tk), lambda b, c, t: (s, b, c, 0))
    const = lambda a: pl.BlockSpec(a.shape, lambda b, c, t: (0,) * a.ndim)
    return pl.pallas_call(
        _attn_kernel,
        grid=(bsz, N_PAIR_COLS, n_qt),
        in_specs=[qt_spec, qt_spec,
                  pl.BlockSpec((None, None, LANES, LANES), lambda b, c, t: (0, c, b, 0)),
                  pl.BlockSpec((None, None, LANES, LANES), lambda b, c, t: (1, c, 0, b)),
                  k_spec(0), k_spec(1), vt_spec(0), vt_spec(1),
                  pl.BlockSpec((SMALL_W, tq), lambda b, c, t: (0, b * n_qt + t)),
                  const(gb),
                  const(ovt), const(tri), qt_spec],
        out_specs=out_spec,
        out_shape=jax.ShapeDtypeStruct((bsz * seq, ATTN_WIDTH), BF16),
        scratch_shapes=[pltpu.VMEM((2 * Q_PER_KV, LANES, tq), BF16),
                        pltpu.VMEM((2 * Q_PER_KV, LANES, tq), BF16),
                        pltpu.VMEM((2, seq // SLC_BLOCK, tq), F32),
                        pltpu.VMEM((2 * Q_PER_KV, tq), F32),
                        pltpu.VMEM((2 * Q_PER_KV * VT_ROWS, tq), F32),
                        pltpu.VMEM((Q_PER_KV, LANES, tq), F32),
                        pltpu.VMEM((2 * Q_PER_KV, MAX_GROUP * tk, tq), F32),
                        pltpu.VMEM((SLC_GROUP, 2, tk, LANES), BF16),
                        pltpu.VMEM((SMALL_W, tq), F32),
                        pltpu.VMEM((2 * Q_PER_KV, LANES, tq), F32)],
        compiler_params=pltpu.CompilerParams(dimension_semantics=("arbitrary", "arbitrary", "arbitrary"),
                                             vmem_limit_bytes=VMEM_LIMIT),
        name="attn",
    )(qnt, qrt, kc_std, kc_t, kk, kk, vvt, vvt, smt, gb, ovt, tri, zat)


def _ssd_kernel(xbc_ref, sm_ref, cw_ref, cb_ref, dtb_ref, alog_ref, e16_ref, dskip_ref, zs_ref, nw_ref,
                shift_ref, head_ref, out_ref, carry_ref, state_ref, y_ref):
    L = SSD_CHUNK
    j = pl.program_id(1)
    n_prev = CONV_WIDTH - 1

    @pl.when(j == 0)
    def _():
        carry_ref[...] = jnp.zeros(carry_ref.shape, BF16)
        state_ref[...] = jnp.zeros(state_ref.shape, F32)

    u = xbc_ref[...]
    shifted = _dot(shift_ref[...], u)
    conv = cb_ref[...] + u.astype(F32) * cw_ref[n_prev:n_prev + 1, :]
    head = jnp.zeros((SUBLANES, CONV_CH), F32)
    for w in range(n_prev):
        conv = conv + shifted[w * L:(w + 1) * L] * cw_ref[w:w + 1, :]
        head = head + _dot(head_ref[w], carry_ref[...]) * cw_ref[w:w + 1, :]
    conv = jnp.concatenate([conv[0:SUBLANES] + head, conv[SUBLANES:]], axis=0)
    carry_ref[...] = u[L - CARRY_ROWS:L]
    act = _silu(conv)

    sm = sm_ref[...] + dtb_ref[...]
    dt = jnp.maximum(sm, 0.0) + jnp.log1p(jnp.exp(-jnp.abs(sm)))
    a = dt * (-jnp.exp(alog_ref[...]) * LOG2E)
    tril = lax.broadcasted_iota(jnp.int32, (L, L), 0) >= lax.broadcasted_iota(jnp.int32, (L, L), 1)
    a_cs = _select_dot(jnp.where(tril, 1.0, 0.0).astype(BF16), a)
    a_cs_t = a_cs.T
    e16 = e16_ref[...]
    a_exp = _dot_select(a_cs, e16)
    dt_exp = _dot_select(dt, e16)
    a_last = a_exp[L - 1:L, :]
    x_c = act[:, 0:SSM_WIDTH]
    xdt = x_c * dt_exp
    xdt_bf = xdt.astype(BF16)
    xw = (xdt * jnp.exp2(a_last - a_exp)).astype(BF16)
    ea = jnp.exp2(a_exp)
    chunk_decay = jnp.exp2(a_last)
    lane = lax.broadcasted_iota(jnp.int32, (1, LANES), 1)
    lo = lane < SSM_HEAD_DIM
    halves = (lo, jnp.logical_not(lo))

    for g in range(SSM_GROUPS):
        bg = act[:, SSM_WIDTH + g * SSM_STATE: SSM_WIDTH + (g + 1) * SSM_STATE]
        cg = act[:, SSM_WIDTH + SSM_GROUPS * SSM_STATE + g * SSM_STATE:
                 SSM_WIDTH + SSM_GROUPS * SSM_STATE + (g + 1) * SSM_STATE].astype(BF16)
        cb = _dot_nt(cg, bg.astype(BF16))
        bg_t = bg.T.astype(BF16)
        for i in range(2 * g, 2 * g + 2):
            sl = slice(i * LANES, (i + 1) * LANES)
            y = jnp.zeros((L, LANES), F32)
            for hh in range(2):
                h = 2 * i + hh
                col = a_cs[:, DT_LANE0 + h:DT_LANE0 + h + 1]
                row = a_cs_t[DT_LANE0 + h:DT_LANE0 + h + 1, :]
                decay = jnp.exp2(jnp.where(tril, col - row, NEG))
                xh = jnp.where(halves[hh], xdt_bf[:, sl], jnp.zeros((), BF16))
                y = y + _dot((cb * decay).astype(BF16), xh)
            st = state_ref[i]
            y = y + _dot(cg, st.astype(BF16)) * ea[:, sl]
            state_ref[i] = st * chunk_decay[:, sl] + _dot(bg_t, xw[:, sl])
            y_ref[:, sl] = y + dskip_ref[:, sl] * x_c[:, sl]

    y = y_ref[...] * _silu(zs_ref[...].astype(F32))
    y = y * lax.rsqrt(jnp.mean(y * y, axis=-1, keepdims=True) + EPS) * nw_ref[...]
    out_ref[...] = y.astype(BF16)


def _ssd_call(xbc, sm, cw, cb, dtb, alog, e16, dskip, zs, nw, shift, head, bsz, seq):
    L = SSD_CHUNK
    n_ch = seq // L
    row = lambda w: pl.BlockSpec((L, w), lambda b, j: (b * n_ch + j, 0))
    const = lambda a: pl.BlockSpec(a.shape, lambda b, j: (0,) * a.ndim)
    return pl.pallas_call(
        _ssd_kernel,
        grid=(bsz, n_ch),
        in_specs=[row(CONV_CH), row(SMALL_W), const(cw), const(cb), const(dtb), const(alog), const(e16),
                  const(dskip), row(SSM_WIDTH), const(nw), const(shift), const(head)],
        out_specs=row(SSM_WIDTH),
        out_shape=jax.ShapeDtypeStruct((bsz * seq, SSM_WIDTH), BF16),
        scratch_shapes=[pltpu.VMEM((CARRY_ROWS, CONV_CH), BF16),
                        pltpu.VMEM((SSM_HEADS // 2, SSM_STATE, LANES), F32),
                        pltpu.VMEM((L, SSM_WIDTH), F32)],
        compiler_params=pltpu.CompilerParams(dimension_semantics=("arbitrary", "arbitrary"),
                                             vmem_limit_bytes=VMEM_LIMIT),
        name="ssd",
    )(xbc, sm, cw, cb, dtb, alog, e16, dskip, zs, nw, shift, head)


def _out_kernel(att_ref, ssm_ref, x_ref, w_ref, pw_ref, out_ref, wa_scr, ws_scr):
    @pl.when(pl.program_id(0) == 0)
    def _():
        for slot, head in enumerate(HEAD_OF_SLOT):
            wa_scr[slot * HEAD_DIM:(slot + 1) * HEAD_DIM, :] = w_ref[head * HEAD_DIM:(head + 1) * HEAD_DIM,
                                                                     :].astype(BF16)
        ws_scr[...] = w_ref[ATTN_WIDTH:, :].astype(BF16)

    o = _dot(att_ref[...], wa_scr[...]) + _dot(ssm_ref[...], ws_scr[...])
    o = o * lax.rsqrt(jnp.mean(o * o, axis=-1, keepdims=True) + EPS) * pw_ref[...]
    out_ref[...] = x_ref[...] + o


def _out_call(att, ssm, x2, w_out, pw):
    rows = x2.shape[0]
    tm = OUT_ROWS
    row = lambda w: pl.BlockSpec((tm, w), lambda i: (i, 0))
    return pl.pallas_call(
        _out_kernel,
        grid=(rows // tm,),
        in_specs=[row(ATTN_WIDTH), row(SSM_WIDTH), row(D_MODEL), _resident(w_out.shape), _resident(pw.shape)],
        out_specs=row(D_MODEL),
        out_shape=jax.ShapeDtypeStruct((rows, D_MODEL), F32),
        compiler_params=pltpu.CompilerParams(dimension_semantics=("arbitrary",),
                                             vmem_limit_bytes=VMEM_LIMIT),
        scratch_shapes=[pltpu.VMEM((ATTN_WIDTH, D_MODEL), BF16), pltpu.VMEM((SSM_WIDTH, D_MODEL), BF16)],
        name="outproj",
    )(att, ssm, x2, w_out, pw)


def _constants(seq):

    nc = (seq - CMP_BLOCK) // CMP_STRIDE + 1
    nb = seq // SLC_BLOCK
    ci = np.arange(nc)[:, None] * CMP_STRIDE
    bj = np.arange(nb)[None, :] * SLC_BLOCK
    overlap = ((ci <= bj + SLC_BLOCK - 1) & (ci + CMP_BLOCK - 1 >= bj)).astype(np.float32)
    ovt = np.zeros((LANES, LANES), np.float32)
    ovt[:nb, :nc] = overlap.T

    e16 = np.zeros((SMALL_W, SSM_WIDTH), np.float32)
    for h in range(SSM_HEADS):
        e16[DT_LANE0 + h, h * SSM_HEAD_DIM:(h + 1) * SSM_HEAD_DIM] = 1.0

    key = np.arange(ATTN_K)[:, None]
    qry = np.arange(ATTN_Q)[None, :]
    tri = np.stack([np.where(key <= qry, 0.0, NEG), np.where(key > qry, 0.0, NEG),
                    np.zeros((ATTN_K, ATTN_Q)), np.full((ATTN_K, ATTN_Q), NEG)]).astype(np.float32)
    n_prev = CONV_WIDTH - 1
    shift = np.zeros((n_prev * SSD_CHUNK, SSD_CHUNK), np.float32)
    head = np.zeros((n_prev, SUBLANES, CARRY_ROWS), np.float32)
    for w in range(n_prev):
        back = n_prev - w
        for t in range(SSD_CHUNK):
            if t - back >= 0:
                shift[w * SSD_CHUNK + t, t - back] = 1.0
            elif t < SUBLANES:
                head[w, t, CARRY_ROWS + t - back] = 1.0
    n_grp = PROJ_ROWS // CMP_STRIDE
    pick = np.zeros((PROJ_ROWS, PROJ_ROWS), np.float32)
    for t in range(CMP_STRIDE):
        pick[t * n_grp + np.arange(n_grp), np.arange(n_grp) * CMP_STRIDE + t] = 1.0
    return ovt, e16, tri, shift, head, pick


def _rope_tables(seq):
    half_freqs = ROPE_THETA ** (-np.arange(ROPE_HALF, dtype=np.float64) * 2.0 / ROPE_DIM)
    ang = np.arange(seq, dtype=np.float64)[:, None] * half_freqs[None, :]
    cos, sin = np.cos(ang), np.sin(ang)
    ones = np.ones((seq, HEAD_DIM - ROPE_DIM))
    zeros_h = np.zeros((seq, ROPE_HALF))
    zeros_r = np.zeros((seq, HEAD_DIM - ROPE_DIM))
    cos_h = np.concatenate([cos, cos, ones], axis=1)
    sina_h = np.concatenate([-sin, zeros_h, zeros_r], axis=1)
    sinb_h = np.concatenate([zeros_h, sin, zeros_r], axis=1)
    tile2 = lambda t: np.concatenate([t, t], axis=1)
    return tuple(jnp.asarray(t, F32) for t in (tile2(cos_h), tile2(sina_h), tile2(sinb_h), cos.T, sin.T))


def _layer(x, w_in, w_out, pre_w, post_w, cmp_pos, cmp_w1, cmp_b1, cmp_w2, cmp_b2,
           gate_b, conv_w, conv_b, dt_bias, a_log, d_skip, ssm_norm_w):
    bsz, seq, _ = x.shape
    ovt, e16, tri, shift, head, pick = _constants(seq)
    tabs = _rope_tables(seq)

    w_t = _cast_call(jnp.swapaxes(w_in, 0, 1))
    pad_small = lambda v, at: jnp.zeros((1, SMALL_W), F32).at[0, at:at + v.shape[0]].set(v)
    gb = pad_small(gate_b, 0).T
    dtb = pad_small(dt_bias, DT_LANE0)
    alog = pad_small(a_log, DT_LANE0)

    x2 = x.reshape(bsz * seq, D_MODEL)
    qnt, qrt, cm, kk, vvt, zat, zs, xbc, sm, smt = _proj_call(x2, pre_w[None, :], w_t, jnp.asarray(pick, BF16), tabs, seq)

    w1r = cmp_w1.reshape(2, 2, CMP_STRIDE, HEAD_DIM, CMP_HIDDEN)
    wa, wb = w1r[:, 0], w1r[:, 1]
    z1 = jnp.zeros_like(wa)
    bd1 = jnp.concatenate([jnp.concatenate([wa, z1, wb, z1], axis=-1),
                           jnp.concatenate([z1, wa, z1, wb], axis=-1)], axis=-2).astype(BF16)
    bd1 = bd1.reshape(2, CMP_STRIDE * LANES, 4 * CMP_HIDDEN)
    z2 = jnp.zeros_like(cmp_w2)
    bd2 = jnp.concatenate([jnp.concatenate([cmp_w2, z2], axis=-1),
                           jnp.concatenate([z2, cmp_w2], axis=-1)], axis=-2).astype(BF16)
    pos8 = jnp.broadcast_to(cmp_pos.reshape(2, 1, CMP_BLOCK * HEAD_DIM), (2, SUBLANES, CMP_BLOCK * HEAD_DIM))
    b2 = jnp.concatenate([cmp_b2, cmp_b2], axis=-1)[:, None, :]
    kc_std, kc_t = _compress_call(cm, bd1, cmp_w1, pos8, cmp_b1[:, None, :], bd2, b2,
                                  jnp.swapaxes(bd2, 1, 2), jnp.swapaxes(b2, 1, 2))

    att = _attn_call(qnt, qrt, kc_std, kc_t, kk, vvt, smt, gb, jnp.asarray(ovt, BF16),
                     jnp.asarray(tri), zat, bsz, seq)

    dskip = jnp.repeat(d_skip, SSM_HEAD_DIM)[None, :]
    ssm = _ssd_call(xbc, sm, conv_w, conv_b[None, :], dtb, alog, jnp.asarray(e16, BF16), dskip, zs,
                    ssm_norm_w[None, :], jnp.asarray(shift, BF16), jnp.asarray(head, BF16), bsz, seq)

    out = _out_call(att, ssm, x2, w_out, post_w[None, :])
    return out.reshape(bsz, seq, D_MODEL)


def kernel(x, w_in, w_out, pre_norm_w, post_norm_w, cmp_pos, cmp_w1, cmp_b1, cmp_w2, cmp_b2, gate_b, conv_w,
           conv_b, dt_bias, a_log, d_skip, ssm_norm_w):
    for l in range(w_in.shape[0]):
        x = _layer(x, w_in[l], w_out[l], pre_norm_w[l], post_norm_w[l], cmp_pos[l], cmp_w1[l], cmp_b1[l],
                   cmp_w2[l], cmp_b2[l], gate_b[l], conv_w[l], conv_b[l], dt_bias[l], a_log[l], d_skip[l],
                   ssm_norm_w[l])
    return x
```

```python
import numpy as np
import jax
import jax.numpy as jnp
from jax import lax
from jax.experimental import pallas as pl
from jax.experimental.pallas import tpu as pltpu

F32 = jnp.float32
BF16 = jnp.bfloat16

D_MODEL = 1024
ATTN_HEADS = 16
HEAD_DIM = 64
ATTN_WIDTH = ATTN_HEADS * HEAD_DIM
KV_HEADS = 4
Q_PER_KV = ATTN_HEADS // KV_HEADS
KV_WIDTH = KV_HEADS * HEAD_DIM
ROPE_DIM = HEAD_DIM // 4
ROPE_HALF = ROPE_DIM // 2
ROPE_THETA = 500000.0
CMP_BLOCK = 32
CMP_STRIDE = 16
CMP_HIDDEN = 256
SLC_BLOCK = 64
SLC_TOPN = 16
WINDOW = 512
N_BRANCH = 3
SSM_HEADS = 16
SSM_HEAD_DIM = 64
SSM_WIDTH = SSM_HEADS * SSM_HEAD_DIM
SSM_GROUPS = 4
SSM_STATE = 128
CONV_WIDTH = 4
CONV_CH = SSM_WIDTH + 2 * SSM_GROUPS * SSM_STATE
MIX_WIDTH = ATTN_WIDTH + SSM_WIDTH
EPS = 1e-6
NEG = -1e30
BIG = 1e30
M_FLOOR = -1e29
LOG2E = 1.4426950408889634
Q_SCALE = HEAD_DIM ** -0.5 * LOG2E

LANES = 128
SUBLANES = 8
N_PAIR_COLS = KV_HEADS // 2
HALF_W = ATTN_WIDTH // N_PAIR_COLS
SMALL_W = LANES
ONES_ROWS = 16
VT_ROWS = HEAD_DIM + ONES_ROWS
DT_LANE0 = ATTN_HEADS * N_BRANCH
VMEM_LIMIT = 56 * 1024 * 1024

PROJ_ROWS = 512
W_CAST_ROWS = 1344

W_ROWS = dict(q=ATTN_WIDTH, cm=2 * KV_WIDTH, k_sl=KV_WIDTH, v_sl=KV_WIDTH, k_wn=KV_WIDTH, v_wn=KV_WIDTH,
              gate=ATTN_HEADS * N_BRANCH, z_att=ATTN_WIDTH, z_ssm=SSM_WIDTH, xbc=CONV_CH, dt=SSM_HEADS)
W_OFF = dict(zip(W_ROWS, np.cumsum([0] + list(W_ROWS.values())[:-1]).tolist()))
HEAD_OF_SLOT = [(2 * c + gp) * Q_PER_KV + r for c in range(N_PAIR_COLS) for r in range(Q_PER_KV) for gp in range(2)]
ATTN_Q = 256
ATTN_K = 256
SLC_GROUP = 4
MAX_GROUP = max(SLC_GROUP, WINDOW // ATTN_K + 1)
SCORE_LEAD = 3
SSD_CHUNK = 256
CARRY_ROWS = 16
OUT_ROWS = 1024


def _dot(a, b):
    return jnp.dot(a, b, preferred_element_type=F32)


def _dot_nt(a, b):
    return lax.dot_general(a, b, (((1,), (1,)), ((), ())), preferred_element_type=F32)


def _split3(x):
    x1 = x.astype(BF16)
    r1 = x - x1.astype(F32)
    x2 = r1.astype(BF16)
    x3 = (r1 - x2.astype(F32)).astype(BF16)
    return x1, x2, x3


def _select_dot(sel, x):
    x1, x2, x3 = _split3(x)
    return _dot(sel, x1) + _dot(sel, x2) + _dot(sel, x3)


def _sigmoid(x):
    return 1.0 / (1.0 + jnp.exp2(x * -LOG2E))


def _silu(x):
    return x * _sigmoid(x)


def _rope_chunk(c, cos_t, sin_a, sin_b):
    return c * cos_t + pltpu.roll(c, LANES - ROPE_HALF, 1) * sin_a + pltpu.roll(c, ROPE_HALF, 1) * sin_b


def _proj_kernel(x_ref, x_next_ref, pre_w_ref, wt_ref, pick_ref, cos_ref, sina_ref, sinb_ref, cos8_ref, sin8_ref,
                 qnt_ref, qrt_ref, cm_ref, kk_ref, vvt_ref, zat_ref, zs_ref, xbc_ref, sm_ref, smt_ref, h_scr):
    def normed(ref):
        x = ref[...]
        return (x * lax.rsqrt(jnp.mean(x * x, axis=-1, keepdims=True) + EPS) * pre_w_ref[...]).astype(BF16)

    @pl.when(pl.program_id(0) == 0)
    def _():
        h_scr[...] = normed(x_ref)

    h = h_scr[...]
    seg = lambda name: wt_ref[W_OFF[name]:W_OFF[name] + W_ROWS[name], :]

    qt = _dot_nt(seg("q"), h)
    zat = _dot_nt(seg("z_att"), h).astype(BF16)
    cos8, sin8 = cos8_ref[...], sin8_ref[...]
    for slot, head in enumerate(HEAD_OF_SLOT):
        src = slice(head * HEAD_DIM, (head + 1) * HEAD_DIM)
        dst = slice(slot * HEAD_DIM, (slot + 1) * HEAD_DIM)
        q_head = qt[src]
        t1, t2 = q_head[0:ROPE_HALF], q_head[ROPE_HALF:ROPE_DIM]
        rot = jnp.concatenate([t1 * cos8 - t2 * sin8, t2 * cos8 + t1 * sin8, q_head[ROPE_DIM:]], axis=0)
        qnt_ref[dst, :] = q_head.astype(BF16)
        qrt_ref[dst, :] = rot.astype(BF16)
        zat_ref[dst, :] = zat[src]

    smt = jnp.concatenate([_dot_nt(seg("gate"), h), _dot_nt(seg("dt"), h),
                           jnp.zeros((SMALL_W - DT_LANE0 - SSM_HEADS, PROJ_ROWS), F32)], axis=0)
    smt_ref[...] = smt
    sm_ref[...] = smt.T

    cm = _dot_nt(h, seg("cm")).astype(BF16)
    by_offset = _dot(pick_ref[...], cm).astype(BF16)
    n_grp = PROJ_ROWS // CMP_STRIDE
    for t in range(CMP_STRIDE):
        for k in range(2 * N_PAIR_COLS):
            cm_ref[k, :, t * LANES:(t + 1) * LANES] = by_offset[t * n_grp:(t + 1) * n_grp, k * LANES:(k + 1) * LANES]

    cos_t, sin_a, sin_b = cos_ref[...], sina_ref[...], sinb_ref[...]
    ones = jnp.ones((ONES_ROWS, ATTN_K), BF16)
    for t, (k_name, v_name) in enumerate((("k_sl", "v_sl"), ("k_wn", "v_wn"))):
        kk = _dot_nt(h, seg(k_name))
        for k in range(N_PAIR_COLS):
            c = kk[:, k * LANES:(k + 1) * LANES]
            kk_ref[t, :, k * LANES:(k + 1) * LANES] = _rope_chunk(c, cos_t, sin_a, sin_b).astype(BF16)
        vt = _dot_nt(seg(v_name), h).astype(BF16)
        for j in range(PROJ_ROWS // ATTN_K):
            for g in range(KV_HEADS):
                vvt_ref[t, j, g * VT_ROWS:g * VT_ROWS + HEAD_DIM, :] = vt[g * HEAD_DIM:(g + 1) * HEAD_DIM,
                                                                          j * ATTN_K:(j + 1) * ATTN_K]
                vvt_ref[t, j, g * VT_ROWS + HEAD_DIM:(g + 1) * VT_ROWS, :] = ones

    zs_ref[...] = _dot_nt(h, seg("z_ssm")).astype(BF16)
    xbc_ref[...] = _dot_nt(h, seg("xbc")).astype(BF16)
    h_scr[...] = normed(x_next_ref)


def _cast_kernel(w_ref, out_ref):
    row = pl.program_id(0) * W_CAST_ROWS + lax.broadcasted_iota(jnp.int32, (W_CAST_ROWS, 1), 0)
    out_ref[...] = (w_ref[...] * jnp.where(row < ATTN_WIDTH, Q_SCALE, 1.0)).astype(BF16)


def _cast_call(w_t):
    rows, width = w_t.shape
    spec = pl.BlockSpec((W_CAST_ROWS, width), lambda i: (i, 0))
    return pl.pallas_call(
        _cast_kernel,
        grid=(rows // W_CAST_ROWS,),
        in_specs=[spec],
        out_specs=spec,
        out_shape=jax.ShapeDtypeStruct((rows, width), BF16),
        compiler_params=pltpu.CompilerParams(dimension_semantics=("arbitrary",)),
        name="wcast",
    )(w_t)


def _resident(shape):
    nd = len(shape)
    return pl.BlockSpec(shape, lambda *_: (0,) * nd, pipeline_mode=pl.Buffered(1))


def _proj_call(x2, pre_w, wt, pick, tabs, seq):
    rows = x2.shape[0]
    tm = PROJ_ROWS
    n_seq_tiles = seq // tm
    k_per_tile = tm // ATTN_K
    row_spec = lambda w: pl.BlockSpec((tm, w), lambda i: (i, 0))
    col_spec = lambda h: pl.BlockSpec((h, tm), lambda i: (0, i))
    tab_spec = pl.BlockSpec((tm, LANES), lambda i: (i % n_seq_tiles, 0))
    tab8_spec = pl.BlockSpec((ROPE_HALF, tm), lambda i: (0, i % n_seq_tiles))
    return pl.pallas_call(
        _proj_kernel,
        grid=(rows // tm,),
        in_specs=[row_spec(D_MODEL), pl.BlockSpec((tm, D_MODEL), lambda i: (jnp.minimum(i + 1, rows // tm - 1), 0)),
                  _resident(pre_w.shape), _resident(wt.shape), _resident(pick.shape),
                  tab_spec, tab_spec, tab_spec, tab8_spec, tab8_spec],
        out_specs=[col_spec(ATTN_WIDTH), col_spec(ATTN_WIDTH),
                   pl.BlockSpec((2 * N_PAIR_COLS, tm // CMP_STRIDE, CMP_STRIDE * LANES), lambda i: (0, i, 0)),
                   pl.BlockSpec((2, tm, KV_WIDTH), lambda i: (0, i, 0)),
                   pl.BlockSpec((2, k_per_tile, KV_HEADS * VT_ROWS, ATTN_K), lambda i: (0, i, 0, 0)),
                   col_spec(ATTN_WIDTH), row_spec(SSM_WIDTH), row_spec(CONV_CH), row_spec(SMALL_W),
                   col_spec(SMALL_W)],
        out_shape=[jax.ShapeDtypeStruct((ATTN_WIDTH, rows), BF16),
                   jax.ShapeDtypeStruct((ATTN_WIDTH, rows), BF16),
                   jax.ShapeDtypeStruct((2 * N_PAIR_COLS, rows // CMP_STRIDE, CMP_STRIDE * LANES), BF16),
                   jax.ShapeDtypeStruct((2, rows, KV_WIDTH), BF16),
                   jax.ShapeDtypeStruct((2, rows // ATTN_K, KV_HEADS * VT_ROWS, ATTN_K), BF16),
                   jax.ShapeDtypeStruct((ATTN_WIDTH, rows), BF16),
                   jax.ShapeDtypeStruct((rows, SSM_WIDTH), BF16),
                   jax.ShapeDtypeStruct((rows, CONV_CH), BF16),
                   jax.ShapeDtypeStruct((rows, SMALL_W), F32),
                   jax.ShapeDtypeStruct((SMALL_W, rows), F32)],
        compiler_params=pltpu.CompilerParams(dimension_semantics=("arbitrary",),
                                             vmem_limit_bytes=VMEM_LIMIT),
        scratch_shapes=[pltpu.VMEM((tm, D_MODEL), BF16)],
        name="proj",
    )(x2, x2, pre_w, wt, pick, *tabs)


def _compress_kernel(x_ref, bd1_ref, w1_ref, pos_ref, b1_ref, bd2_ref, b2_ref, bd2t_ref, b2t_ref,
                     out_ref, outt_ref, acc_ref):
    n_rows = x_ref.shape[0]
    hid2 = 2 * CMP_HIDDEN
    acc_ref[n_rows:, :] = jnp.zeros((SUBLANES, 2 * hid2), F32)
    acc_ref[0:n_rows, :] = _dot(x_ref[...], bd1_ref[...])
    first = acc_ref[0:n_rows, 0:hid2]
    second = acc_ref[pl.ds(1, n_rows), hid2:2 * hid2]
    posterm = _dot(pos_ref[...].astype(BF16), w1_ref[...].astype(BF16))[0:1, :] + b1_ref[...]
    hcat = first + second + jnp.concatenate([posterm, posterm], axis=1)
    act = _silu(hcat).astype(BF16)
    out_ref[...] = _dot(act, bd2_ref[...]) + b2_ref[...]
    outt_ref[...] = _dot_nt(bd2t_ref[...], act) + b2t_ref[...]


def _compress_call(cm4, bd1, w1, pos8, b1, bd2, b2, bd2t, b2t):
    n_rows = cm4.shape[1]
    per_s = lambda *tail: pl.BlockSpec((None,) + tail, lambda s, c: (s,) + (0,) * len(tail))
    return pl.pallas_call(
        _compress_kernel,
        grid=(2, N_PAIR_COLS),
        in_specs=[pl.BlockSpec((None, n_rows, CMP_STRIDE * LANES), lambda s, c: (s * N_PAIR_COLS + c, 0, 0)),
                  per_s(CMP_STRIDE * LANES, 4 * CMP_HIDDEN),
                  per_s(CMP_BLOCK * HEAD_DIM, CMP_HIDDEN),
                  per_s(SUBLANES, CMP_BLOCK * HEAD_DIM),
                  per_s(1, CMP_HIDDEN),
                  per_s(2 * CMP_HIDDEN, LANES),
                  per_s(1, LANES),
                  per_s(LANES, 2 * CMP_HIDDEN),
                  per_s(LANES, 1)],
        out_specs=[pl.BlockSpec((None, None, n_rows, LANES), lambda s, c: (s, c, 0, 0)),
                   pl.BlockSpec((None, None, LANES, n_rows), lambda s, c: (s, c, 0, 0))],
        out_shape=[jax.ShapeDtypeStruct((2, N_PAIR_COLS, n_rows, LANES), F32),
                   jax.ShapeDtypeStruct((2, N_PAIR_COLS, LANES, n_rows), F32)],
        scratch_shapes=[pltpu.VMEM((n_rows + SUBLANES, 4 * CMP_HIDDEN), F32)],
        compiler_params=pltpu.CompilerParams(dimension_semantics=("arbitrary", "arbitrary"),
                                             vmem_limit_bytes=VMEM_LIMIT),
        name="compress",
    )(cm4, bd1, w1, pos8, b1, bd2, b2, bd2t, b2t)


def _attn_kernel(qnt_ref, qrt_ref, kc_ref, vct_ref, ksl_ref, kwn_ref, vslt_ref, vwnt_ref, smt_ref, gb_ref,
                 ovt_ref, tri_ref, zat_ref, out_ref,
                 qw_ref, qs_ref, bias_ref, m_ref, acc_ref, osum_ref, s_ref, kaug_ref, gate_ref, cs_ref):
    tq, tk = ATTN_Q, ATTN_K
    n_r = Q_PER_KV
    qt = pl.program_id(2)
    q0 = qt * tq
    row = lax.broadcasted_iota(jnp.int32, (LANES, 1), 0)
    halves = (row < HEAD_DIM, row >= HEAD_DIM)
    qpos = q0 + lax.broadcasted_iota(jnp.int32, (1, tq), 1)

    gate_ref[...] = _sigmoid(smt_ref[...] + gb_ref[...])
    first_group = pl.program_id(1) * 2

    def gate_row(br, idx):
        head = (first_group + idx % 2) * n_r + idx // 2
        return gate_ref[pl.ds(head * N_BRANCH + br, 1), :]

    kc = kc_ref[...].astype(BF16)
    vct = vct_ref[...].astype(BF16)
    cpos = row * CMP_STRIDE + (CMP_BLOCK - 1)
    cbias = jnp.where(cpos <= qpos, 0.0, NEG)
    psum = [jnp.zeros((LANES, tq), F32), jnp.zeros((LANES, tq), F32)]
    osum_ref[...] = jnp.zeros(osum_ref.shape, F32)
    for idx in range(2 * n_r):
        q = qnt_ref[(idx // 2) * LANES:(idx // 2 + 1) * LANES, :]
        qm = jnp.where(halves[idx % 2], q, jnp.zeros_like(q))
        cs_ref[idx] = _dot(kc, qm) + cbias

    def compressed_softmax():
        for idx in range(2 * n_r):
            r, hf = divmod(idx, 2)
            s = cs_ref[idx]
            m = jnp.maximum(jnp.max(s, axis=0, keepdims=True), M_FLOOR)
            p = jnp.exp2(s - m)
            l = jnp.sum(p, axis=0, keepdims=True)
            p = p * jnp.where(l > 0.0, 1.0 / l, 0.0)
            psum[hf] = psum[hf] + p
            hs = slice(hf * HEAD_DIM, (hf + 1) * HEAD_DIM)
            osum_ref[r, hs, :] += gate_row(0, idx) * _dot(vct[hs, :], p.astype(BF16))

    nb = ksl_ref.shape[0] // SLC_BLOCK
    jio = lax.broadcasted_iota(jnp.int32, (nb, tq), 0)
    cur = (q0 + lax.broadcasted_iota(jnp.int32, (nb, tq), 1)) // SLC_BLOCK
    forced = (jio == 0) | (jio == cur) | (jio == cur - 1)

    def select_blocks():
        for hf in range(2):
            imp = _select_dot(ovt_ref[...], psum[hf])[0:nb, :]
            imp = jnp.where(forced, BIG, imp)
            imp = jnp.where(jio > cur, -BIG, imp)
            cnt = jnp.zeros((nb, tq), F32)
            for i in range(nb):
                other = imp[i:i + 1, :]
                beats = (other > imp) | ((other == imp) & (jio > i))
                cnt = cnt + jnp.where(beats, 1.0, 0.0)
            bias_ref[hf] = jnp.where(cnt < float(SLC_TOPN), 0.0, NEG)

    def load_queries(qm_ref, with_selection_bias):
        spare = jnp.zeros((HEAD_DIM - nb, tq), BF16)
        for r in range(n_r):
            q = qrt_ref[r * LANES:(r + 1) * LANES, :]
            for hf in range(2):
                if with_selection_bias:
                    fill = [bias_ref[hf].astype(BF16), spare]
                else:
                    fill = [jnp.zeros((HEAD_DIM, tq), BF16)]
                pieces = [q[0:HEAD_DIM]] + fill if hf == 0 else fill + [q[HEAD_DIM:]]
                qm_ref[r * 2 + hf] = jnp.concatenate(pieces, axis=0)

    lane_io = lax.broadcasted_iota(jnp.int32, (1, LANES), 1)
    key_blk = lax.broadcasted_iota(jnp.int32, (tk, 1), 0) // SLC_BLOCK

    def keys_with_block_indicator(k_tile, kt, hf):
        first = 0 if hf == 1 else HEAD_DIM
        onehot = (lane_io - first) == (kt * (tk // SLC_BLOCK) + key_blk)
        keep = (lane_io >= HEAD_DIM) if hf == 1 else (lane_io < HEAD_DIM)
        return jnp.where(keep, k_tile, jnp.where(onehot, 1.0, 0.0).astype(BF16))

    def reset():
        m_ref[...] = jnp.full(m_ref.shape, M_FLOOR, F32)
        acc_ref[...] = jnp.zeros(acc_ref.shape, F32)

    def flash_group(qm_ref, k_ref, vt_ref, parts, indicator=False):
        if indicator:
            for pi, (kt, _) in enumerate(parts):
                k_tile = k_ref[pl.ds(pl.multiple_of(kt * tk, tk), tk), :]
                for hf in range(2):
                    kaug_ref[pi, hf] = keys_with_block_indicator(k_tile, kt, hf)

        def scores(idx):
            col_max = None
            for pi, (kt, bias_fn) in enumerate(parts):
                if indicator:
                    k_tile = kaug_ref[pi, idx % 2]
                else:
                    k_tile = k_ref[pl.ds(pl.multiple_of(kt * tk, tk), tk), :]
                s = _dot(k_tile, qm_ref[idx])
                if bias_fn is not None:
                    s = bias_fn(s)
                s_ref[idx, pi * tk:(pi + 1) * tk, :] = s
                part_max = jnp.max(s, axis=0, keepdims=True)
                col_max = part_max if col_max is None else jnp.maximum(col_max, part_max)
            return col_max

        n_units = 2 * n_r
        col_maxes = [scores(idx) for idx in range(min(SCORE_LEAD, n_units))]
        for idx in range(n_units):
            if idx + SCORE_LEAD < n_units:
                col_maxes.append(scores(idx + SCORE_LEAD))
            m_old = m_ref[idx:idx + 1, :]
            m_new = jnp.maximum(m_old, col_maxes[idx])
            m_ref[idx:idx + 1, :] = m_new
            rows = slice(idx * VT_ROWS, (idx + 1) * VT_ROWS)
            vrows = slice((idx % 2) * VT_ROWS, (idx % 2 + 1) * VT_ROWS)
            upd = acc_ref[rows, :] * jnp.exp2(m_old - m_new)
            for pi, (kt, _) in enumerate(parts):
                p = jnp.exp2(s_ref[idx, pi * tk:(pi + 1) * tk, :] - m_new).astype(BF16)
                upd = upd + _dot(vt_ref[kt, vrows, :], p)
            acc_ref[rows, :] = upd

    def finish(branch):
        for idx in range(2 * n_r):
            r, hf = divmod(idx, 2)
            base = idx * VT_ROWS
            l = acc_ref[base + HEAD_DIM:base + HEAD_DIM + 1, :]
            scale = gate_row(branch, idx) * jnp.where(l > 0.0, 1.0 / l, 0.0)
            hs = slice(hf * HEAD_DIM, (hf + 1) * HEAD_DIM)
            osum_ref[r, hs, :] += acc_ref[base:base + HEAD_DIM, :] * scale

    tri_causal, tri_tail, tri_open, tri_closed = 0, 1, 2, 3
    causal = lambda s: s + tri_ref[tri_causal]

    load_queries(qw_ref, with_selection_bias=False)
    reset()
    tiles_back = WINDOW // tk
    win_parts = []
    for back in range(tiles_back, -1, -1):
        kt = jnp.maximum(qt - back, 0)
        inside = tri_causal if back == 0 else (tri_tail if back == tiles_back else tri_open)
        table = inside if back == 0 else jnp.where(qt >= back, inside, tri_closed)
        win_parts.append((kt, lambda s, table=table: s + tri_ref[table]))
    flash_group(qw_ref, kwn_ref, vwnt_ref, win_parts)
    finish(2)
    compressed_softmax()

    select_blocks()
    load_queries(qs_ref, with_selection_bias=True)
    reset()
    full_groups = qt // SLC_GROUP

    def slc_full(j, carry):
        flash_group(qs_ref, ksl_ref, vslt_ref, [(SLC_GROUP * j + i, None) for i in range(SLC_GROUP)],
                    indicator=True)
        return carry

    lax.fori_loop(0, full_groups, slc_full, 0)

    for n_last in range(1, SLC_GROUP + 1):
        @pl.when(qt % SLC_GROUP == n_last - 1)
        def _(n_last=n_last):
            first = SLC_GROUP * full_groups
            parts = [(first + i, None) for i in range(n_last - 1)] + [(qt, causal)]
            flash_group(qs_ref, ksl_ref, vslt_ref, parts, indicator=True)

    finish(1)

    for r in range(n_r):
        z = zat_ref[r * LANES:(r + 1) * LANES, :].astype(F32)
        out_ref[:, r * LANES:(r + 1) * LANES] = (osum_ref[r] * _silu(z)).T.astype(BF16)


def _attn_call(qnt, qrt, kc_std, kc_t, kk, vvt, smt, gb, ovt, tri, zat, bsz, seq):
    tq, tk = ATTN_Q, ATTN_K
    n_qt = seq // tq
    n_kt = seq // tk
    qt_spec = pl.BlockSpec((HALF_W, tq), lambda b, c, t: (c, b * n_qt + t))
    out_spec = pl.BlockSpec((tq, HALF_W), lambda b, c, t: (b * n_qt + t, c))
    k_spec = lambda s: pl.BlockSpec((None, seq, LANES), lambda b, c, t: (s, b, c))
    vt_spec = lambda s: pl.BlockSpec((None, n_kt, 2 * VT_ROWS, tk), lambda b, c, t: (s, b, c, 0))
    const = lambda a: pl.BlockSpec(a.shape, lambda b, c, t: (0,) * a.ndim)
    return pl.pallas_call(
        _attn_kernel,
        grid=(bsz, N_PAIR_COLS, n_qt),
        in_specs=[qt_spec, qt_spec,
                  pl.BlockSpec((None, None, LANES, LANES), lambda b, c, t: (0, c, b, 0)),
                  pl.BlockSpec((None, None, LANES, LANES), lambda b, c, t: (1, c, 0, b)),
                  k_spec(0), k_spec(1), vt_spec(0), vt_spec(1),
                  pl.BlockSpec((SMALL_W, tq), lambda b, c, t: (0, b * n_qt + t)),
                  const(gb),
                  const(ovt), const(tri), qt_spec],
        out_specs=out_spec,
        out_shape=jax.ShapeDtypeStruct((bsz * seq, ATTN_WIDTH), BF16),
        scratch_shapes=[pltpu.VMEM((2 * Q_PER_KV, LANES, tq), BF16),
                        pltpu.VMEM((2 * Q_PER_KV, LANES, tq), BF16),
                        pltpu.VMEM((2, seq // SLC_BLOCK, tq), F32),
                        pltpu.VMEM((2 * Q_PER_KV, tq), F32),
                        pltpu.VMEM((2 * Q_PER_KV * VT_ROWS, tq), F32),
                        pltpu.VMEM((Q_PER_KV, LANES, tq), F32),
                        pltpu.VMEM((2 * Q_PER_KV, MAX_GROUP * tk, tq), F32),
                        pltpu.VMEM((SLC_GROUP, 2, tk, LANES), BF16),
                        pltpu.VMEM((SMALL_W, tq), F32),
                        pltpu.VMEM((2 * Q_PER_KV, LANES, tq), F32)],
        compiler_params=pltpu.CompilerParams(dimension_semantics=("arbitrary", "arbitrary", "arbitrary"),
                                             vmem_limit_bytes=VMEM_LIMIT),
        name="attn",
    )(qnt, qrt, kc_std, kc_t, kk, kk, vvt, vvt, smt, gb, ovt, tri, zat)


def _ssd_kernel(xbc_ref, sm_ref, cw_ref, cb_ref, dtb_ref, alog_ref, dskip_ref, zs_ref, nw_ref,
                shift_ref, head_ref, out_ref, carry_ref, state_ref, y_ref):
    L = SSD_CHUNK
    j = pl.program_id(1)
    n_prev = CONV_WIDTH - 1

    @pl.when(j == 0)
    def _():
        carry_ref[...] = jnp.zeros(carry_ref.shape, BF16)
        state_ref[...] = jnp.zeros(state_ref.shape, F32)

    u = xbc_ref[...]
    shifted = _dot(shift_ref[...], u)
    conv = cb_ref[...] + u.astype(F32) * cw_ref[n_prev:n_prev + 1, :]
    head = jnp.zeros((SUBLANES, CONV_CH), F32)
    for w in range(n_prev):
        conv = conv + shifted[w * L:(w + 1) * L] * cw_ref[w:w + 1, :]
        head = head + _dot(head_ref[w], carry_ref[...]) * cw_ref[w:w + 1, :]
    conv = jnp.concatenate([conv[0:SUBLANES] + head, conv[SUBLANES:]], axis=0)
    carry_ref[...] = u[L - CARRY_ROWS:L]
    act = _silu(conv)

    sm = sm_ref[...] + dtb_ref[...]
    dt = jnp.maximum(sm, 0.0) + jnp.log1p(jnp.exp(-jnp.abs(sm)))
    a = dt * (-jnp.exp(alog_ref[...]) * LOG2E)
    tril = lax.broadcasted_iota(jnp.int32, (L, L), 0) >= lax.broadcasted_iota(jnp.int32, (L, L), 1)
    a_cs = _select_dot(jnp.where(tril, 1.0, 0.0).astype(BF16), a)
    a_cs_t = a_cs.T
    lane = lax.broadcasted_iota(jnp.int32, (1, LANES), 1)
    lo = lane < SSM_HEAD_DIM
    halves = (lo, jnp.logical_not(lo))

    def per_head_lanes(v):
        pairs = [jnp.where(lo, v[:, DT_LANE0 + 2 * i:DT_LANE0 + 2 * i + 1],
                           v[:, DT_LANE0 + 2 * i + 1:DT_LANE0 + 2 * i + 2]) for i in range(SSM_HEADS // 2)]
        return jnp.concatenate(pairs, axis=1)

    a_exp = per_head_lanes(a_cs)
    dt_exp = per_head_lanes(dt)
    a_last = a_exp[L - 1:L, :]
    x_c = act[:, 0:SSM_WIDTH]
    xdt = x_c * dt_exp
    xdt_bf = xdt.astype(BF16)
    xw = (xdt * jnp.exp2(a_last - a_exp)).astype(BF16)
    ea = jnp.exp2(a_exp)
    chunk_decay = jnp.exp2(a_last)

    for g in range(SSM_GROUPS):
        bg = act[:, SSM_WIDTH + g * SSM_STATE: SSM_WIDTH + (g + 1) * SSM_STATE]
        cg = act[:, SSM_WIDTH + SSM_GROUPS * SSM_STATE + g * SSM_STATE:
                 SSM_WIDTH + SSM_GROUPS * SSM_STATE + (g + 1) * SSM_STATE].astype(BF16)
        cb = _dot_nt(cg, bg.astype(BF16))
        bg_t = bg.T.astype(BF16)
        for i in range(2 * g, 2 * g + 2):
            sl = slice(i * LANES, (i + 1) * LANES)
            y = jnp.zeros((L, LANES), F32)
            for hh in range(2):
                h = 2 * i + hh
                col = a_cs[:, DT_LANE0 + h:DT_LANE0 + h + 1]
                row = a_cs_t[DT_LANE0 + h:DT_LANE0 + h + 1, :]
                decay = jnp.exp2(jnp.where(tril, col - row, NEG))
                xh = jnp.where(halves[hh], xdt_bf[:, sl], jnp.zeros((), BF16))
                y = y + _dot((cb * decay).astype(BF16), xh)
            st = state_ref[i]
            y = y + _dot(cg, st.astype(BF16)) * ea[:, sl]
            state_ref[i] = st * chunk_decay[:, sl] + _dot(bg_t, xw[:, sl])
            y_ref[:, sl] = y + dskip_ref[:, sl] * x_c[:, sl]

    y = y_ref[...] * _silu(zs_ref[...].astype(F32))
    y = y * lax.rsqrt(jnp.mean(y * y, axis=-1, keepdims=True) + EPS) * nw_ref[...]
    out_ref[...] = y.astype(BF16)


def _ssd_call(xbc, sm, cw, cb, dtb, alog, dskip, zs, nw, shift, head, bsz, seq):
    L = SSD_CHUNK
    n_ch = seq // L
    row = lambda w: pl.BlockSpec((L, w), lambda b, j: (b * n_ch + j, 0))
    const = lambda a: pl.BlockSpec(a.shape, lambda b, j: (0,) * a.ndim)
    return pl.pallas_call(
        _ssd_kernel,
        grid=(bsz, n_ch),
        in_specs=[row(CONV_CH), row(SMALL_W), const(cw), const(cb), const(dtb), const(alog),
                  const(dskip), row(SSM_WIDTH), const(nw), const(shift), const(head)],
        out_specs=row(SSM_WIDTH),
        out_shape=jax.ShapeDtypeStruct((bsz * seq, SSM_WIDTH), BF16),
        scratch_shapes=[pltpu.VMEM((CARRY_ROWS, CONV_CH), BF16),
                        pltpu.VMEM((SSM_HEADS // 2, SSM_STATE, LANES), F32),
                        pltpu.VMEM((L, SSM_WIDTH), F32)],
        compiler_params=pltpu.CompilerParams(dimension_semantics=("arbitrary", "arbitrary"),
                                             vmem_limit_bytes=VMEM_LIMIT),
        name="ssd",
    )(xbc, sm, cw, cb, dtb, alog, dskip, zs, nw, shift, head)


def _out_kernel(att_ref, ssm_ref, x_ref, w_ref, pw_ref, out_ref, wa_scr, ws_scr):
    @pl.when(pl.program_id(0) == 0)
    def _():
        for slot, head in enumerate(HEAD_OF_SLOT):
            wa_scr[slot * HEAD_DIM:(slot + 1) * HEAD_DIM, :] = w_ref[head * HEAD_DIM:(head + 1) * HEAD_DIM,
                                                                     :].astype(BF16)
        ws_scr[...] = w_ref[ATTN_WIDTH:, :].astype(BF16)

    o = _dot(att_ref[...], wa_scr[...]) + _dot(ssm_ref[...], ws_scr[...])
    o = o * lax.rsqrt(jnp.mean(o * o, axis=-1, keepdims=True) + EPS) * pw_ref[...]
    out_ref[...] = x_ref[...] + o


def _out_call(att, ssm, x2, w_out, pw):
    rows = x2.shape[0]
    tm = OUT_ROWS
    row = lambda w: pl.BlockSpec((tm, w), lambda i: (i, 0))
    return pl.pallas_call(
        _out_kernel,
        grid=(rows // tm,),
        in_specs=[row(ATTN_WIDTH), row(SSM_WIDTH), row(D_MODEL), _resident(w_out.shape), _resident(pw.shape)],
        out_specs=row(D_MODEL),
        out_shape=jax.ShapeDtypeStruct((rows, D_MODEL), F32),
        compiler_params=pltpu.CompilerParams(dimension_semantics=("arbitrary",),
                                             vmem_limit_bytes=VMEM_LIMIT),
        scratch_shapes=[pltpu.VMEM((ATTN_WIDTH, D_MODEL), BF16), pltpu.VMEM((SSM_WIDTH, D_MODEL), BF16)],
        name="outproj",
    )(att, ssm, x2, w_out, pw)


def _constants(seq):

    nc = (seq - CMP_BLOCK) // CMP_STRIDE + 1
    nb = seq // SLC_BLOCK
    ci = np.arange(nc)[:, None] * CMP_STRIDE
    bj = np.arange(nb)[None, :] * SLC_BLOCK
    overlap = ((ci <= bj + SLC_BLOCK - 1) & (ci + CMP_BLOCK - 1 >= bj)).astype(np.float32)
    ovt = np.zeros((LANES, LANES), np.float32)
    ovt[:nb, :nc] = overlap.T

    key = np.arange(ATTN_K)[:, None]
    qry = np.arange(ATTN_Q)[None, :]
    tri = np.stack([np.where(key <= qry, 0.0, NEG), np.where(key > qry, 0.0, NEG),
                    np.zeros((ATTN_K, ATTN_Q)), np.full((ATTN_K, ATTN_Q), NEG)]).astype(np.float32)
    n_prev = CONV_WIDTH - 1
    shift = np.zeros((n_prev * SSD_CHUNK, SSD_CHUNK), np.float32)
    head = np.zeros((n_prev, SUBLANES, CARRY_ROWS), np.float32)
    for w in range(n_prev):
        back = n_prev - w
        for t in range(SSD_CHUNK):
            if t - back >= 0:
                shift[w * SSD_CHUNK + t, t - back] = 1.0
            elif t < SUBLANES:
                head[w, t, CARRY_ROWS + t - back] = 1.0
    n_grp = PROJ_ROWS // CMP_STRIDE
    pick = np.zeros((PROJ_ROWS, PROJ_ROWS), np.float32)
    for t in range(CMP_STRIDE):
        pick[t * n_grp + np.arange(n_grp), np.arange(n_grp) * CMP_STRIDE + t] = 1.0
    return ovt, tri, shift, head, pick


def _rope_tables(seq):
    half_freqs = ROPE_THETA ** (-np.arange(ROPE_HALF, dtype=np.float64) * 2.0 / ROPE_DIM)
    ang = np.arange(seq, dtype=np.float64)[:, None] * half_freqs[None, :]
    cos, sin = np.cos(ang), np.sin(ang)
    ones = np.ones((seq, HEAD_DIM - ROPE_DIM))
    zeros_h = np.zeros((seq, ROPE_HALF))
    zeros_r = np.zeros((seq, HEAD_DIM - ROPE_DIM))
    cos_h = np.concatenate([cos, cos, ones], axis=1)
    sina_h = np.concatenate([-sin, zeros_h, zeros_r], axis=1)
    sinb_h = np.concatenate([zeros_h, sin, zeros_r], axis=1)
    tile2 = lambda t: np.concatenate([t, t], axis=1)
    return tuple(jnp.asarray(t, F32) for t in (tile2(cos_h), tile2(sina_h), tile2(sinb_h), cos.T, sin.T))


def _layer(x, w_in, w_out, pre_w, post_w, cmp_pos, cmp_w1, cmp_b1, cmp_w2, cmp_b2,
           gate_b, conv_w, conv_b, dt_bias, a_log, d_skip, ssm_norm_w):
    bsz, seq, _ = x.shape
    ovt, tri, shift, head, pick = _constants(seq)
    tabs = _rope_tables(seq)

    w_t = _cast_call(jnp.swapaxes(w_in, 0, 1))
    pad_small = lambda v, at: jnp.zeros((1, SMALL_W), F32).at[0, at:at + v.shape[0]].set(v)
    gb = pad_small(gate_b, 0).T
    dtb = pad_small(dt_bias, DT_LANE0)
    alog = pad_small(a_log, DT_LANE0)

    x2 = x.reshape(bsz * seq, D_MODEL)
    qnt, qrt, cm, kk, vvt, zat, zs, xbc, sm, smt = _proj_call(x2, pre_w[None, :], w_t, jnp.asarray(pick, BF16), tabs, seq)

    w1r = cmp_w1.reshape(2, 2, CMP_STRIDE, HEAD_DIM, CMP_HIDDEN)
    wa, wb = w1r[:, 0], w1r[:, 1]
    z1 = jnp.zeros_like(wa)
    bd1 = jnp.concatenate([jnp.concatenate([wa, z1, wb, z1], axis=-1),
                           jnp.concatenate([z1, wa, z1, wb], axis=-1)], axis=-2).astype(BF16)
    bd1 = bd1.reshape(2, CMP_STRIDE * LANES, 4 * CMP_HIDDEN)
    z2 = jnp.zeros_like(cmp_w2)
    bd2 = jnp.concatenate([jnp.concatenate([cmp_w2, z2], axis=-1),
                           jnp.concatenate([z2, cmp_w2], axis=-1)], axis=-2).astype(BF16)
    pos8 = jnp.broadcast_to(cmp_pos.reshape(2, 1, CMP_BLOCK * HEAD_DIM), (2, SUBLANES, CMP_BLOCK * HEAD_DIM))
    b2 = jnp.concatenate([cmp_b2, cmp_b2], axis=-1)[:, None, :]
    kc_std, kc_t = _compress_call(cm, bd1, cmp_w1, pos8, cmp_b1[:, None, :], bd2, b2,
                                  jnp.swapaxes(bd2, 1, 2), jnp.swapaxes(b2, 1, 2))

    att = _attn_call(qnt, qrt, kc_std, kc_t, kk, vvt, smt, gb, jnp.asarray(ovt, BF16),
                     jnp.asarray(tri), zat, bsz, seq)

    dskip = jnp.repeat(d_skip, SSM_HEAD_DIM)[None, :]
    ssm = _ssd_call(xbc, sm, conv_w, conv_b[None, :], dtb, alog, dskip, zs,
                    ssm_norm_w[None, :], jnp.asarray(shift, BF16), jnp.asarray(head, BF16), bsz, seq)

    out = _out_call(att, ssm, x2, w_out, post_w[None, :])
    return out.reshape(bsz, seq, D_MODEL)


def kernel(x, w_in, w_out, pre_norm_w, post_norm_w, cmp_pos, cmp_w1, cmp_b1, cmp_w2, cmp_b2, gate_b, conv_w,
           conv_b, dt_bias, a_log, d_skip, ssm_norm_w):
    for l in range(w_in.shape[0]):
        x = _layer(x, w_in[l], w_out[l], pre_norm_w[l], post_norm_w[l], cmp_pos[l], cmp_w1[l], cmp_b1[l],
                   cmp_w2[l], cmp_b2[l], gate_b[l], conv_w[l], conv_b[l], dt_bias[l], a_log[l], d_skip[l],
                   ssm_norm_w[l])
    return x
```

```python
import numpy as np
import jax
import jax.numpy as jnp
from jax import lax
from jax.experimental import pallas as pl
from jax.experimental.pallas import tpu as pltpu

F32 = jnp.float32
BF16 = jnp.bfloat16

D_MODEL = 1024
ATTN_HEADS = 16
HEAD_DIM = 64
ATTN_WIDTH = ATTN_HEADS * HEAD_DIM
KV_HEADS = 4
Q_PER_KV = ATTN_HEADS // KV_HEADS
KV_WIDTH = KV_HEADS * HEAD_DIM
ROPE_DIM = HEAD_DIM // 4
ROPE_HALF = ROPE_DIM // 2
ROPE_THETA = 500000.0
CMP_BLOCK = 32
CMP_STRIDE = 16
CMP_HIDDEN = 256
SLC_BLOCK = 64
SLC_TOPN = 16
WINDOW = 512
N_BRANCH = 3
SSM_HEADS = 16
SSM_HEAD_DIM = 64
SSM_WIDTH = SSM_HEADS * SSM_HEAD_DIM
SSM_GROUPS = 4
SSM_STATE = 128
CONV_WIDTH = 4
CONV_CH = SSM_WIDTH + 2 * SSM_GROUPS * SSM_STATE
MIX_WIDTH = ATTN_WIDTH + SSM_WIDTH
EPS = 1e-6
NEG = -1e30
BIG = 1e30
M_FLOOR = -1e29
LOG2E = 1.4426950408889634
Q_SCALE = HEAD_DIM ** -0.5 * LOG2E

LANES = 128
SUBLANES = 8
N_PAIR_COLS = KV_HEADS // 2
HALF_W = ATTN_WIDTH // N_PAIR_COLS
SMALL_W = LANES
ONES_ROWS = 16
VT_ROWS = HEAD_DIM + ONES_ROWS
DT_LANE0 = ATTN_HEADS * N_BRANCH
VMEM_LIMIT = 56 * 1024 * 1024

PROJ_ROWS = 512
W_CAST_ROWS = 1344

W_ROWS = dict(q=ATTN_WIDTH, cm=2 * KV_WIDTH, k_sl=KV_WIDTH, v_sl=KV_WIDTH, k_wn=KV_WIDTH, v_wn=KV_WIDTH,
              gate=ATTN_HEADS * N_BRANCH, z_att=ATTN_WIDTH, z_ssm=SSM_WIDTH, xbc=CONV_CH, dt=SSM_HEADS)
W_OFF = dict(zip(W_ROWS, np.cumsum([0] + list(W_ROWS.values())[:-1]).tolist()))
HEAD_OF_SLOT = [(2 * c + gp) * Q_PER_KV + r for c in range(N_PAIR_COLS) for r in range(Q_PER_KV) for gp in range(2)]
ATTN_Q = 256
ATTN_K = 256
SLC_GROUP = 4
MAX_GROUP = max(SLC_GROUP, WINDOW // ATTN_K + 1)
SCORE_LEAD = 3
SSD_CHUNK = 256
CARRY_ROWS = 16
OUT_ROWS = 1024


def _dot(a, b):
    return jnp.dot(a, b, preferred_element_type=F32)


def _dot_nt(a, b):
    return lax.dot_general(a, b, (((1,), (1,)), ((), ())), preferred_element_type=F32)


def _split3(x):
    x1 = x.astype(BF16)
    r1 = x - x1.astype(F32)
    x2 = r1.astype(BF16)
    x3 = (r1 - x2.astype(F32)).astype(BF16)
    return x1, x2, x3


def _select_dot(sel, x):
    x1, x2, x3 = _split3(x)
    return _dot(sel, x1) + _dot(sel, x2) + _dot(sel, x3)


def _sigmoid(x):
    return 1.0 / (1.0 + jnp.exp2(x * -LOG2E))


def _silu(x):
    return x * _sigmoid(x)


def _rope_chunk(c, cos_t, sin_a, sin_b):
    return c * cos_t + pltpu.roll(c, LANES - ROPE_HALF, 1) * sin_a + pltpu.roll(c, ROPE_HALF, 1) * sin_b


def _proj_kernel(x_ref, x_next_ref, pre_w_ref, wt_ref, pick_ref, cos_ref, sina_ref, sinb_ref, cos8_ref, sin8_ref,
                 qnt_ref, qrt_ref, cm_ref, kk_ref, vvt_ref, zat_ref, zs_ref, xbc_ref, sm_ref, smt_ref, h_scr):
    def normed(ref):
        x = ref[...]
        return (x * lax.rsqrt(jnp.mean(x * x, axis=-1, keepdims=True) + EPS) * pre_w_ref[...]).astype(BF16)

    @pl.when(pl.program_id(0) == 0)
    def _():
        h_scr[...] = normed(x_ref)

    h = h_scr[...]
    seg = lambda name: wt_ref[W_OFF[name]:W_OFF[name] + W_ROWS[name], :]

    qt = _dot_nt(seg("q"), h)
    zat = _dot_nt(seg("z_att"), h).astype(BF16)
    cos8, sin8 = cos8_ref[...], sin8_ref[...]
    for slot, head in enumerate(HEAD_OF_SLOT):
        src = slice(head * HEAD_DIM, (head + 1) * HEAD_DIM)
        dst = slice(slot * HEAD_DIM, (slot + 1) * HEAD_DIM)
        q_head = qt[src]
        t1, t2 = q_head[0:ROPE_HALF], q_head[ROPE_HALF:ROPE_DIM]
        rot = jnp.concatenate([t1 * cos8 - t2 * sin8, t2 * cos8 + t1 * sin8, q_head[ROPE_DIM:]], axis=0)
        qnt_ref[dst, :] = q_head.astype(BF16)
        qrt_ref[dst, :] = rot.astype(BF16)
        zat_ref[dst, :] = zat[src]

    smt = jnp.concatenate([_dot_nt(seg("gate"), h), _dot_nt(seg("dt"), h),
                           jnp.zeros((SMALL_W - DT_LANE0 - SSM_HEADS, PROJ_ROWS), F32)], axis=0)
    smt_ref[...] = smt
    sm_ref[...] = smt.T

    cm = _dot_nt(h, seg("cm")).astype(BF16)
    by_offset = _dot(pick_ref[...], cm).astype(BF16)
    n_grp = PROJ_ROWS // CMP_STRIDE
    for t in range(CMP_STRIDE):
        for k in range(2 * N_PAIR_COLS):
            cm_ref[k, :, t * LANES:(t + 1) * LANES] = by_offset[t * n_grp:(t + 1) * n_grp, k * LANES:(k + 1) * LANES]

    cos_t, sin_a, sin_b = cos_ref[...], sina_ref[...], sinb_ref[...]
    ones = jnp.ones((ONES_ROWS, ATTN_K), BF16)
    for t, (k_name, v_name) in enumerate((("k_sl", "v_sl"), ("k_wn", "v_wn"))):
        kk = _dot_nt(h, seg(k_name))
        for k in range(N_PAIR_COLS):
            c = kk[:, k * LANES:(k + 1) * LANES]
            kk_ref[t, :, k * LANES:(k + 1) * LANES] = _rope_chunk(c, cos_t, sin_a, sin_b).astype(BF16)
        vt = _dot_nt(seg(v_name), h).astype(BF16)
        for j in range(PROJ_ROWS // ATTN_K):
            for g in range(KV_HEADS):
                vvt_ref[t, j, g * VT_ROWS:g * VT_ROWS + HEAD_DIM, :] = vt[g * HEAD_DIM:(g + 1) * HEAD_DIM,
                                                                          j * ATTN_K:(j + 1) * ATTN_K]
                vvt_ref[t, j, g * VT_ROWS + HEAD_DIM:(g + 1) * VT_ROWS, :] = ones

    zs_ref[...] = _dot_nt(h, seg("z_ssm")).astype(BF16)
    xbc_ref[...] = _dot_nt(h, seg("xbc")).astype(BF16)
    h_scr[...] = normed(x_next_ref)


def _cast_kernel(w_ref, out_ref):
    row = pl.program_id(0) * W_CAST_ROWS + lax.broadcasted_iota(jnp.int32, (W_CAST_ROWS, 1), 0)
    out_ref[...] = (w_ref[...] * jnp.where(row < ATTN_WIDTH, Q_SCALE, 1.0)).astype(BF16)


def _cast_call(w_t):
    rows, width = w_t.shape
    spec = pl.BlockSpec((W_CAST_ROWS, width), lambda i: (i, 0))
    return pl.pallas_call(
        _cast_kernel,
        grid=(rows // W_CAST_ROWS,),
        in_specs=[spec],
        out_specs=spec,
        out_shape=jax.ShapeDtypeStruct((rows, width), BF16),
        compiler_params=pltpu.CompilerParams(dimension_semantics=("arbitrary",)),
        name="wcast",
    )(w_t)


def _resident(shape):
    nd = len(shape)
    return pl.BlockSpec(shape, lambda *_: (0,) * nd, pipeline_mode=pl.Buffered(1))


def _proj_call(x2, pre_w, wt, pick, tabs, seq):
    rows = x2.shape[0]
    tm = PROJ_ROWS
    n_seq_tiles = seq // tm
    k_per_tile = tm // ATTN_K
    row_spec = lambda w: pl.BlockSpec((tm, w), lambda i: (i, 0))
    col_spec = lambda h: pl.BlockSpec((h, tm), lambda i: (0, i))
    tab_spec = pl.BlockSpec((tm, LANES), lambda i: (i % n_seq_tiles, 0))
    tab8_spec = pl.BlockSpec((ROPE_HALF, tm), lambda i: (0, i % n_seq_tiles))
    return pl.pallas_call(
        _proj_kernel,
        grid=(rows // tm,),
        in_specs=[row_spec(D_MODEL), pl.BlockSpec((tm, D_MODEL), lambda i: (jnp.minimum(i + 1, rows // tm - 1), 0)),
                  _resident(pre_w.shape), _resident(wt.shape), _resident(pick.shape),
                  tab_spec, tab_spec, tab_spec, tab8_spec, tab8_spec],
        out_specs=[col_spec(ATTN_WIDTH), col_spec(ATTN_WIDTH),
                   pl.BlockSpec((2 * N_PAIR_COLS, tm // CMP_STRIDE, CMP_STRIDE * LANES), lambda i: (0, i, 0)),
                   pl.BlockSpec((2, tm, KV_WIDTH), lambda i: (0, i, 0)),
                   pl.BlockSpec((2, k_per_tile, KV_HEADS * VT_ROWS, ATTN_K), lambda i: (0, i, 0, 0)),
                   col_spec(ATTN_WIDTH), row_spec(SSM_WIDTH), row_spec(CONV_CH), row_spec(SMALL_W),
                   col_spec(SMALL_W)],
        out_shape=[jax.ShapeDtypeStruct((ATTN_WIDTH, rows), BF16),
                   jax.ShapeDtypeStruct((ATTN_WIDTH, rows), BF16),
                   jax.ShapeDtypeStruct((2 * N_PAIR_COLS, rows // CMP_STRIDE, CMP_STRIDE * LANES), BF16),
                   jax.ShapeDtypeStruct((2, rows, KV_WIDTH), BF16),
                   jax.ShapeDtypeStruct((2, rows // ATTN_K, KV_HEADS * VT_ROWS, ATTN_K), BF16),
                   jax.ShapeDtypeStruct((ATTN_WIDTH, rows), BF16),
                   jax.ShapeDtypeStruct((rows, SSM_WIDTH), BF16),
                   jax.ShapeDtypeStruct((rows, CONV_CH), BF16),
                   jax.ShapeDtypeStruct((rows, SMALL_W), F32),
                   jax.ShapeDtypeStruct((SMALL_W, rows), F32)],
        compiler_params=pltpu.CompilerParams(dimension_semantics=("arbitrary",),
                                             vmem_limit_bytes=VMEM_LIMIT),
        scratch_shapes=[pltpu.VMEM((tm, D_MODEL), BF16)],
        name="proj",
    )(x2, x2, pre_w, wt, pick, *tabs)


def _compress_kernel(x_ref, bd1_ref, w1_ref, pos_ref, b1_ref, bd2_ref, b2_ref, bd2t_ref, b2t_ref,
                     out_ref, outt_ref, acc_ref):
    n_rows = x_ref.shape[0]
    hid2 = 2 * CMP_HIDDEN
    acc_ref[n_rows:, :] = jnp.zeros((SUBLANES, 2 * hid2), F32)
    acc_ref[0:n_rows, :] = _dot(x_ref[...], bd1_ref[...])
    first = acc_ref[0:n_rows, 0:hid2]
    second = acc_ref[pl.ds(1, n_rows), hid2:2 * hid2]
    posterm = _dot(pos_ref[...].astype(BF16), w1_ref[...].astype(BF16))[0:1, :] + b1_ref[...]
    hcat = first + second + jnp.concatenate([posterm, posterm], axis=1)
    act = _silu(hcat).astype(BF16)
    out_ref[...] = _dot(act, bd2_ref[...]) + b2_ref[...]
    outt_ref[...] = _dot_nt(bd2t_ref[...], act) + b2t_ref[...]


def _compress_call(cm4, bd1, w1, pos8, b1, bd2, b2, bd2t, b2t):
    n_rows = cm4.shape[1]
    per_s = lambda *tail: pl.BlockSpec((None,) + tail, lambda s, c: (s,) + (0,) * len(tail))
    return pl.pallas_call(
        _compress_kernel,
        grid=(2, N_PAIR_COLS),
        in_specs=[pl.BlockSpec((None, n_rows, CMP_STRIDE * LANES), lambda s, c: (s * N_PAIR_COLS + c, 0, 0)),
                  per_s(CMP_STRIDE * LANES, 4 * CMP_HIDDEN),
                  per_s(CMP_BLOCK * HEAD_DIM, CMP_HIDDEN),
                  per_s(SUBLANES, CMP_BLOCK * HEAD_DIM),
                  per_s(1, CMP_HIDDEN),
                  per_s(2 * CMP_HIDDEN, LANES),
                  per_s(1, LANES),
                  per_s(LANES, 2 * CMP_HIDDEN),
                  per_s(LANES, 1)],
        out_specs=[pl.BlockSpec((None, None, n_rows, LANES), lambda s, c: (s, c, 0, 0)),
                   pl.BlockSpec((None, None, LANES, n_rows), lambda s, c: (s, c, 0, 0))],
        out_shape=[jax.ShapeDtypeStruct((2, N_PAIR_COLS, n_rows, LANES), F32),
                   jax.ShapeDtypeStruct((2, N_PAIR_COLS, LANES, n_rows), F32)],
        scratch_shapes=[pltpu.VMEM((n_rows + SUBLANES, 4 * CMP_HIDDEN), F32)],
        compiler_params=pltpu.CompilerParams(dimension_semantics=("arbitrary", "arbitrary"),
                                             vmem_limit_bytes=VMEM_LIMIT),
        name="compress",
    )(cm4, bd1, w1, pos8, b1, bd2, b2, bd2t, b2t)


def _attn_kernel(qnt_ref, qrt_ref, kc_ref, vct_ref, ksl_ref, kwn_ref, vslt_ref, vwnt_ref, smt_ref, gb_ref,
                 ovt_ref, tri_ref, zat_ref, out_ref,
                 qw_ref, qs_ref, bias_ref, m_ref, acc_ref, osum_ref, s_ref, kaug_ref, gate_ref, cs_ref):
    tq, tk = ATTN_Q, ATTN_K
    n_r = Q_PER_KV
    qt = pl.program_id(2)
    q0 = qt * tq
    row = lax.broadcasted_iota(jnp.int32, (LANES, 1), 0)
    halves = (row < HEAD_DIM, row >= HEAD_DIM)
    qpos = q0 + lax.broadcasted_iota(jnp.int32, (1, tq), 1)

    gate_ref[...] = _sigmoid(smt_ref[...] + gb_ref[...])
    first_group = pl.program_id(1) * 2

    def gate_row(br, idx):
        head = (first_group + idx % 2) * n_r + idx // 2
        return gate_ref[pl.ds(head * N_BRANCH + br, 1), :]

    kc = kc_ref[...].astype(BF16)
    vct = vct_ref[...].astype(BF16)
    cpos = row * CMP_STRIDE + (CMP_BLOCK - 1)
    cbias = jnp.where(cpos <= qpos, 0.0, NEG)
    psum = [jnp.zeros((LANES, tq), F32), jnp.zeros((LANES, tq), F32)]
    osum_ref[...] = jnp.zeros(osum_ref.shape, F32)
    for idx in range(2 * n_r):
        q = qnt_ref[(idx // 2) * LANES:(idx // 2 + 1) * LANES, :]
        qm = jnp.where(halves[idx % 2], q, jnp.zeros_like(q))
        cs_ref[idx] = _dot(kc, qm) + cbias

    def compressed_softmax():
        for idx in range(2 * n_r):
            r, hf = divmod(idx, 2)
            s = cs_ref[idx]
            m = jnp.maximum(jnp.max(s, axis=0, keepdims=True), M_FLOOR)
            p = jnp.exp2(s - m)
            l = jnp.sum(p, axis=0, keepdims=True)
            p = p * jnp.where(l > 0.0, 1.0 / l, 0.0)
            psum[hf] = psum[hf] + p
            hs = slice(hf * HEAD_DIM, (hf + 1) * HEAD_DIM)
            osum_ref[r, hs, :] += gate_row(0, idx) * _dot(vct[hs, :], p.astype(BF16))

    nb = ksl_ref.shape[0] // SLC_BLOCK
    jio = lax.broadcasted_iota(jnp.int32, (nb, tq), 0)
    cur = (q0 + lax.broadcasted_iota(jnp.int32, (nb, tq), 1)) // SLC_BLOCK
    forced = (jio == 0) | (jio == cur) | (jio == cur - 1)

    def select_blocks():
        for hf in range(2):
            imp = _select_dot(ovt_ref[...], psum[hf])[0:nb, :]
            imp = jnp.where(forced, BIG, imp)
            imp = jnp.where(jio > cur, -BIG, imp)
            cnt = jnp.zeros((nb, tq), F32)
            for i in range(nb):
                other = imp[i:i + 1, :]
                beats = (other > imp) | ((other == imp) & (jio > i))
                cnt = cnt + jnp.where(beats, 1.0, 0.0)
            bias_ref[hf] = jnp.where(cnt < float(SLC_TOPN), 0.0, NEG)

    def load_queries(qm_ref, with_selection_bias):
        spare = jnp.zeros((HEAD_DIM - nb, tq), BF16)
        for r in range(n_r):
            q = qrt_ref[r * LANES:(r + 1) * LANES, :]
            for hf in range(2):
                if with_selection_bias:
                    fill = [bias_ref[hf].astype(BF16), spare]
                else:
                    fill = [jnp.zeros((HEAD_DIM, tq), BF16)]
                pieces = [q[0:HEAD_DIM]] + fill if hf == 0 else fill + [q[HEAD_DIM:]]
                qm_ref[r * 2 + hf] = jnp.concatenate(pieces, axis=0)

    lane_io = lax.broadcasted_iota(jnp.int32, (1, LANES), 1)
    key_blk = lax.broadcasted_iota(jnp.int32, (tk, 1), 0) // SLC_BLOCK

    def keys_with_block_indicator(k_tile, kt, hf):
        first = 0 if hf == 1 else HEAD_DIM
        onehot = (lane_io - first) == (kt * (tk // SLC_BLOCK) + key_blk)
        keep = (lane_io >= HEAD_DIM) if hf == 1 else (lane_io < HEAD_DIM)
        return jnp.where(keep, k_tile, jnp.where(onehot, 1.0, 0.0).astype(BF16))

    def reset():
        m_ref[...] = jnp.full(m_ref.shape, M_FLOOR, F32)
        acc_ref[...] = jnp.zeros(acc_ref.shape, F32)

    def flash_group(qm_ref, k_ref, vt_ref, parts, indicator=False):
        if indicator:
            for pi, (kt, _) in enumerate(parts):
                k_tile = k_ref[pl.ds(pl.multiple_of(kt * tk, tk), tk), :]
                for hf in range(2):
                    kaug_ref[pi, hf] = keys_with_block_indicator(k_tile, kt, hf)

        def scores(idx):
            col_max = None
            for pi, (kt, bias_fn) in enumerate(parts):
                if indicator:
                    k_tile = kaug_ref[pi, idx % 2]
                else:
                    k_tile = k_ref[pl.ds(pl.multiple_of(kt * tk, tk), tk), :]
                s = _dot(k_tile, qm_ref[idx])
                if bias_fn is not None:
                    s = bias_fn(s)
                s_ref[idx, pi * tk:(pi + 1) * tk, :] = s
                part_max = jnp.max(s, axis=0, keepdims=True)
                col_max = part_max if col_max is None else jnp.maximum(col_max, part_max)
            return col_max

        n_units = 2 * n_r
        col_maxes = [scores(idx) for idx in range(min(SCORE_LEAD, n_units))]
        for idx in range(n_units):
            if idx + SCORE_LEAD < n_units:
                col_maxes.append(scores(idx + SCORE_LEAD))
            m_old = m_ref[idx:idx + 1, :]
            m_new = jnp.maximum(m_old, col_maxes[idx])
            m_ref[idx:idx + 1, :] = m_new
            rows = slice(idx * VT_ROWS, (idx + 1) * VT_ROWS)
            vrows = slice((idx % 2) * VT_ROWS, (idx % 2 + 1) * VT_ROWS)
            upd = acc_ref[rows, :] * jnp.exp2(m_old - m_new)
            for pi, (kt, _) in enumerate(parts):
                p = jnp.exp2(s_ref[idx, pi * tk:(pi + 1) * tk, :] - m_new).astype(BF16)
                upd = upd + _dot(vt_ref[kt, vrows, :], p)
            acc_ref[rows, :] = upd

    def finish(branch):
        for idx in range(2 * n_r):
            r, hf = divmod(idx, 2)
            base = idx * VT_ROWS
            l = acc_ref[base + HEAD_DIM:base + HEAD_DIM + 1, :]
            scale = gate_row(branch, idx) * jnp.where(l > 0.0, 1.0 / l, 0.0)
            hs = slice(hf * HEAD_DIM, (hf + 1) * HEAD_DIM)
            osum_ref[r, hs, :] += acc_ref[base:base + HEAD_DIM, :] * scale

    tri_causal, tri_tail, tri_open, tri_closed = 0, 1, 2, 3
    causal = lambda s: s + tri_ref[tri_causal]

    load_queries(qw_ref, with_selection_bias=False)
    reset()
    tiles_back = WINDOW // tk
    win_parts = []
    for back in range(tiles_back, -1, -1):
        kt = jnp.maximum(qt - back, 0)
        inside = tri_causal if back == 0 else (tri_tail if back == tiles_back else tri_open)
        table = inside if back == 0 else jnp.where(qt >= back, inside, tri_closed)
        win_parts.append((kt, lambda s, table=table: s + tri_ref[table]))
    flash_group(qw_ref, kwn_ref, vwnt_ref, win_parts)
    finish(2)
    compressed_softmax()

    select_blocks()
    load_queries(qs_ref, with_selection_bias=True)
    reset()
    full_groups = qt // SLC_GROUP

    def slc_full(j, carry):
        flash_group(qs_ref, ksl_ref, vslt_ref, [(SLC_GROUP * j + i, None) for i in range(SLC_GROUP)],
                    indicator=True)
        return carry

    lax.fori_loop(0, full_groups, slc_full, 0)

    for n_last in range(1, SLC_GROUP + 1):
        @pl.when(qt % SLC_GROUP == n_last - 1)
        def _(n_last=n_last):
            first = SLC_GROUP * full_groups
            parts = [(first + i, None) for i in range(n_last - 1)] + [(qt, causal)]
            flash_group(qs_ref, ksl_ref, vslt_ref, parts, indicator=True)

    finish(1)

    for r in range(n_r):
        z = zat_ref[r * LANES:(r + 1) * LANES, :].astype(F32)
        out_ref[:, r * LANES:(r + 1) * LANES] = (osum_ref[r] * _silu(z)).T.astype(BF16)


def _attn_call(qnt, qrt, kc_std, kc_t, kk, vvt, smt, gb, ovt, tri, zat, bsz, seq):
    tq, tk = ATTN_Q, ATTN_K
    n_qt = seq // tq
    n_kt = seq // tk
    qt_spec = pl.BlockSpec((HALF_W, tq), lambda b, c, t: (c, b * n_qt + t))
    out_spec = pl.BlockSpec((tq, HALF_W), lambda b, c, t: (b * n_qt + t, c))
    k_spec = lambda s: pl.BlockSpec((None, seq, LANES), lambda b, c, t: (s, b, c))
    vt_spec = lambda s: pl.BlockSpec((None, n_kt, 2 * VT_ROWS, tk), lambda b, c, t: (s, b, c, 0))
    const = lambda a: pl.BlockSpec(a.shape, lambda b, c, t: (0,) * a.ndim)
    return pl.pallas_call(
        _attn_kernel,
        grid=(bsz, N_PAIR_COLS, n_qt),
        in_specs=[qt_spec, qt_spec,
                  pl.BlockSpec((None, None, LANES, LANES), lambda b, c, t: (0, c, b, 0)),
                  pl.BlockSpec((None, None, LANES, LANES), lambda b, c, t: (1, c, 0, b)),
                  k_spec(0), k_spec(1), vt_spec(0), vt_spec(1),
                  pl.BlockSpec((SMALL_W, tq), lambda b, c, t: (0, b * n_qt + t)),
                  const(gb),
                  const(ovt), const(tri), qt_spec],
        out_specs=out_spec,
        out_shape=jax.ShapeDtypeStruct((bsz * seq, ATTN_WIDTH), BF16),
        scratch_shapes=[pltpu.VMEM((2 * Q_PER_KV, LANES, tq), BF16),
                        pltpu.VMEM((2 * Q_PER_KV, LANES, tq), BF16),
                        pltpu.VMEM((2, seq // SLC_BLOCK, tq), F32),
                        pltpu.VMEM((2 * Q_PER_KV, tq), F32),
                        pltpu.VMEM((2 * Q_PER_KV * VT_ROWS, tq), F32),
                        pltpu.VMEM((Q_PER_KV, LANES, tq), F32),
                        pltpu.VMEM((2 * Q_PER_KV, MAX_GROUP * tk, tq), F32),
                        pltpu.VMEM((SLC_GROUP, 2, tk, LANES), BF16),
                        pltpu.VMEM((SMALL_W, tq), F32),
                        pltpu.VMEM((2 * Q_PER_KV, LANES, tq), F32)],
        compiler_params=pltpu.CompilerParams(dimension_semantics=("arbitrary", "arbitrary", "arbitrary"),
                                             vmem_limit_bytes=VMEM_LIMIT),
        name="attn",
    )(qnt, qrt, kc_std, kc_t, kk, kk, vvt, vvt, smt, gb, ovt, tri, zat)


def _ssd_kernel(xbc_ref, sm_ref, cw_ref, cb_ref, dtb_ref, alog_ref, dskip_ref, zs_ref, nw_ref,
                shift_ref, head_ref, out_ref, carry_ref, state_ref, y_ref):
    L = SSD_CHUNK
    j = pl.program_id(1)
    n_prev = CONV_WIDTH - 1

    @pl.when(j == 0)
    def _():
        carry_ref[...] = jnp.zeros(carry_ref.shape, BF16)
        state_ref[...] = jnp.zeros(state_ref.shape, F32)

    def conv_act(c0, width):
        cols = slice(c0, c0 + width)
        u = xbc_ref[:, cols]
        shifted = _dot(shift_ref[...], u)
        conv = cb_ref[:, cols] + u.astype(F32) * cw_ref[n_prev:n_prev + 1, cols]
        head = jnp.zeros((SUBLANES, width), F32)
        for w in range(n_prev):
            conv = conv + shifted[w * L:(w + 1) * L] * cw_ref[w:w + 1, cols]
            head = head + _dot(head_ref[w], carry_ref[:, cols]) * cw_ref[w:w + 1, cols]
        return _silu(jnp.concatenate([conv[0:SUBLANES] + head, conv[SUBLANES:]], axis=0))

    sm = sm_ref[...] + dtb_ref[...]
    dt = jnp.maximum(sm, 0.0) + jnp.log1p(jnp.exp(-jnp.abs(sm)))
    a = dt * (-jnp.exp(alog_ref[...]) * LOG2E)
    tril = lax.broadcasted_iota(jnp.int32, (L, L), 0) >= lax.broadcasted_iota(jnp.int32, (L, L), 1)
    a_cs = _select_dot(jnp.where(tril, 1.0, 0.0).astype(BF16), a)
    a_cs_t = a_cs.T
    lane = lax.broadcasted_iota(jnp.int32, (1, LANES), 1)
    lo = lane < SSM_HEAD_DIM
    halves = (lo, jnp.logical_not(lo))

    def per_head_lanes(v, first_pair, n_pairs):
        pairs = [jnp.where(lo, v[:, DT_LANE0 + 2 * i:DT_LANE0 + 2 * i + 1],
                           v[:, DT_LANE0 + 2 * i + 1:DT_LANE0 + 2 * i + 2])
                 for i in range(first_pair, first_pair + n_pairs)]
        return jnp.concatenate(pairs, axis=1)

    groups_per_pass = SSM_GROUPS // 2
    pairs_per_pass = SSM_HEADS // 4
    x_width = SSM_WIDTH // 2
    bc_width = groups_per_pass * SSM_STATE
    for ps in range(2):
        x_c = conv_act(ps * x_width, x_width)
        b_two = conv_act(SSM_WIDTH + ps * bc_width, bc_width)
        c_two = conv_act(SSM_WIDTH + SSM_GROUPS * SSM_STATE + ps * bc_width, bc_width)
        a_exp = per_head_lanes(a_cs, ps * pairs_per_pass, pairs_per_pass)
        a_last = a_exp[L - 1:L, :]
        xdt = x_c * per_head_lanes(dt, ps * pairs_per_pass, pairs_per_pass)
        xdt_bf = xdt.astype(BF16)
        xw = (xdt * jnp.exp2(a_last - a_exp)).astype(BF16)
        ea = jnp.exp2(a_exp)
        chunk_decay = jnp.exp2(a_last)
        for gg in range(groups_per_pass):
            bg = b_two[:, gg * SSM_STATE:(gg + 1) * SSM_STATE]
            cg = c_two[:, gg * SSM_STATE:(gg + 1) * SSM_STATE].astype(BF16)
            cb = _dot_nt(cg, bg.astype(BF16))
            bg_t = bg.T.astype(BF16)
            for ii in range(2):
                loc = slice((2 * gg + ii) * LANES, (2 * gg + ii + 1) * LANES)
                i = ps * pairs_per_pass + 2 * gg + ii
                sl = slice(i * LANES, (i + 1) * LANES)
                y = jnp.zeros((L, LANES), F32)
                for hh in range(2):
                    h = 2 * i + hh
                    col = a_cs[:, DT_LANE0 + h:DT_LANE0 + h + 1]
                    row = a_cs_t[DT_LANE0 + h:DT_LANE0 + h + 1, :]
                    decay = jnp.exp2(jnp.where(tril, col - row, NEG))
                    xh = jnp.where(halves[hh], xdt_bf[:, loc], jnp.zeros((), BF16))
                    y = y + _dot((cb * decay).astype(BF16), xh)
                st = state_ref[i]
                y = y + _dot(cg, st.astype(BF16)) * ea[:, loc]
                state_ref[i] = st * chunk_decay[:, loc] + _dot(bg_t, xw[:, loc])
                y_ref[:, sl] = y + dskip_ref[:, sl] * x_c[:, loc]
    carry_ref[...] = xbc_ref[L - CARRY_ROWS:L, :]

    y = y_ref[...] * _silu(zs_ref[...].astype(F32))
    y = y * lax.rsqrt(jnp.mean(y * y, axis=-1, keepdims=True) + EPS) * nw_ref[...]
    out_ref[...] = y.astype(BF16)


def _ssd_call(xbc, sm, cw, cb, dtb, alog, dskip, zs, nw, shift, head, bsz, seq):
    L = SSD_CHUNK
    n_ch = seq // L
    row = lambda w: pl.BlockSpec((L, w), lambda b, j: (b * n_ch + j, 0))
    const = lambda a: pl.BlockSpec(a.shape, lambda b, j: (0,) * a.ndim)
    return pl.pallas_call(
        _ssd_kernel,
        grid=(bsz, n_ch),
        in_specs=[row(CONV_CH), row(SMALL_W), const(cw), const(cb), const(dtb), const(alog),
                  const(dskip), row(SSM_WIDTH), const(nw), const(shift), const(head)],
        out_specs=row(SSM_WIDTH),
        out_shape=jax.ShapeDtypeStruct((bsz * seq, SSM_WIDTH), BF16),
        scratch_shapes=[pltpu.VMEM((CARRY_ROWS, CONV_CH), BF16),
                        pltpu.VMEM((SSM_HEADS // 2, SSM_STATE, LANES), F32),
                        pltpu.VMEM((L, SSM_WIDTH), F32)],
        compiler_params=pltpu.CompilerParams(dimension_semantics=("arbitrary", "arbitrary"),
                                             vmem_limit_bytes=VMEM_LIMIT),
        name="ssd",
    )(xbc, sm, cw, cb, dtb, alog, dskip, zs, nw, shift, head)


def _out_kernel(att_ref, ssm_ref, x_ref, w_ref, pw_ref, out_ref, wa_scr, ws_scr):
    @pl.when(pl.program_id(0) == 0)
    def _():
        for slot, head in enumerate(HEAD_OF_SLOT):
            wa_scr[slot * HEAD_DIM:(slot + 1) * HEAD_DIM, :] = w_ref[head * HEAD_DIM:(head + 1) * HEAD_DIM,
                                                                     :].astype(BF16)
        ws_scr[...] = w_ref[ATTN_WIDTH:, :].astype(BF16)

    o = _dot(att_ref[...], wa_scr[...]) + _dot(ssm_ref[...], ws_scr[...])
    o = o * lax.rsqrt(jnp.mean(o * o, axis=-1, keepdims=True) + EPS) * pw_ref[...]
    out_ref[...] = x_ref[...] + o


def _out_call(att, ssm, x2, w_out, pw):
    rows = x2.shape[0]
    tm = OUT_ROWS
    row = lambda w: pl.BlockSpec((tm, w), lambda i: (i, 0))
    return pl.pallas_call(
        _out_kernel,
        grid=(rows // tm,),
        in_specs=[row(ATTN_WIDTH), row(SSM_WIDTH), row(D_MODEL), _resident(w_out.shape), _resident(pw.shape)],
        out_specs=row(D_MODEL),
        out_shape=jax.ShapeDtypeStruct((rows, D_MODEL), F32),
        compiler_params=pltpu.CompilerParams(dimension_semantics=("arbitrary",),
                                             vmem_limit_bytes=VMEM_LIMIT),
        scratch_shapes=[pltpu.VMEM((ATTN_WIDTH, D_MODEL), BF16), pltpu.VMEM((SSM_WIDTH, D_MODEL), BF16)],
        name="outproj",
    )(att, ssm, x2, w_out, pw)


def _constants(seq):

    nc = (seq - CMP_BLOCK) // CMP_STRIDE + 1
    nb = seq // SLC_BLOCK
    ci = np.arange(nc)[:, None] * CMP_STRIDE
    bj = np.arange(nb)[None, :] * SLC_BLOCK
    overlap = ((ci <= bj + SLC_BLOCK - 1) & (ci + CMP_BLOCK - 1 >= bj)).astype(np.float32)
    ovt = np.zeros((LANES, LANES), np.float32)
    ovt[:nb, :nc] = overlap.T

    key = np.arange(ATTN_K)[:, None]
    qry = np.arange(ATTN_Q)[None, :]
    tri = np.stack([np.where(key <= qry, 0.0, NEG), np.where(key > qry, 0.0, NEG),
                    np.zeros((ATTN_K, ATTN_Q)), np.full((ATTN_K, ATTN_Q), NEG)]).astype(np.float32)
    n_prev = CONV_WIDTH - 1
    shift = np.zeros((n_prev * SSD_CHUNK, SSD_CHUNK), np.float32)
    head = np.zeros((n_prev, SUBLANES, CARRY_ROWS), np.float32)
    for w in range(n_prev):
        back = n_prev - w
        for t in range(SSD_CHUNK):
            if t - back >= 0:
                shift[w * SSD_CHUNK + t, t - back] = 1.0
            elif t < SUBLANES:
                head[w, t, CARRY_ROWS + t - back] = 1.0
    n_grp = PROJ_ROWS // CMP_STRIDE
    pick = np.zeros((PROJ_ROWS, PROJ_ROWS), np.float32)
    for t in range(CMP_STRIDE):
        pick[t * n_grp + np.arange(n_grp), np.arange(n_grp) * CMP_STRIDE + t] = 1.0
    return ovt, tri, shift, head, pick


def _rope_tables(seq):
    half_freqs = ROPE_THETA ** (-np.arange(ROPE_HALF, dtype=np.float64) * 2.0 / ROPE_DIM)
    ang = np.arange(seq, dtype=np.float64)[:, None] * half_freqs[None, :]
    cos, sin = np.cos(ang), np.sin(ang)
    ones = np.ones((seq, HEAD_DIM - ROPE_DIM))
    zeros_h = np.zeros((seq, ROPE_HALF))
    zeros_r = np.zeros((seq, HEAD_DIM - ROPE_DIM))
    cos_h = np.concatenate([cos, cos, ones], axis=1)
    sina_h = np.concatenate([-sin, zeros_h, zeros_r], axis=1)
    sinb_h = np.concatenate([zeros_h, sin, zeros_r], axis=1)
    tile2 = lambda t: np.concatenate([t, t], axis=1)
    return tuple(jnp.asarray(t, F32) for t in (tile2(cos_h), tile2(sina_h), tile2(sinb_h), cos.T, sin.T))


def _layer(x, w_in, w_out, pre_w, post_w, cmp_pos, cmp_w1, cmp_b1, cmp_w2, cmp_b2,
           gate_b, conv_w, conv_b, dt_bias, a_log, d_skip, ssm_norm_w):
    bsz, seq, _ = x.shape
    ovt, tri, shift, head, pick = _constants(seq)
    tabs = _rope_tables(seq)

    w_t = _cast_call(jnp.swapaxes(w_in, 0, 1))
    pad_small = lambda v, at: jnp.zeros((1, SMALL_W), F32).at[0, at:at + v.shape[0]].set(v)
    gb = pad_small(gate_b, 0).T
    dtb = pad_small(dt_bias, DT_LANE0)
    alog = pad_small(a_log, DT_LANE0)

    x2 = x.reshape(bsz * seq, D_MODEL)
    qnt, qrt, cm, kk, vvt, zat, zs, xbc, sm, smt = _proj_call(x2, pre_w[None, :], w_t, jnp.asarray(pick, BF16), tabs, seq)

    w1r = cmp_w1.reshape(2, 2, CMP_STRIDE, HEAD_DIM, CMP_HIDDEN)
    wa, wb = w1r[:, 0], w1r[:, 1]
    z1 = jnp.zeros_like(wa)
    bd1 = jnp.concatenate([jnp.concatenate([wa, z1, wb, z1], axis=-1),
                           jnp.concatenate([z1, wa, z1, wb], axis=-1)], axis=-2).astype(BF16)
    bd1 = bd1.reshape(2, CMP_STRIDE * LANES, 4 * CMP_HIDDEN)
    z2 = jnp.zeros_like(cmp_w2)
    bd2 = jnp.concatenate([jnp.concatenate([cmp_w2, z2], axis=-1),
                           jnp.concatenate([z2, cmp_w2], axis=-1)], axis=-2).astype(BF16)
    pos8 = jnp.broadcast_to(cmp_pos.reshape(2, 1, CMP_BLOCK * HEAD_DIM), (2, SUBLANES, CMP_BLOCK * HEAD_DIM))
    b2 = jnp.concatenate([cmp_b2, cmp_b2], axis=-1)[:, None, :]
    kc_std, kc_t = _compress_call(cm, bd1, cmp_w1, pos8, cmp_b1[:, None, :], bd2, b2,
                                  jnp.swapaxes(bd2, 1, 2), jnp.swapaxes(b2, 1, 2))

    att = _attn_call(qnt, qrt, kc_std, kc_t, kk, vvt, smt, gb, jnp.asarray(ovt, BF16),
                     jnp.asarray(tri), zat, bsz, seq)

    dskip = jnp.repeat(d_skip, SSM_HEAD_DIM)[None, :]
    ssm = _ssd_call(xbc, sm, conv_w, conv_b[None, :], dtb, alog, dskip, zs,
                    ssm_norm_w[None, :], jnp.asarray(shift, BF16), jnp.asarray(head, BF16), bsz, seq)

    out = _out_call(att, ssm, x2, w_out, post_w[None, :])
    return out.reshape(bsz, seq, D_MODEL)


def kernel(x, w_in, w_out, pre_norm_w, post_norm_w, cmp_pos, cmp_w1, cmp_b1, cmp_w2, cmp_b2, gate_b, conv_w,
           conv_b, dt_bias, a_log, d_skip, ssm_norm_w):
    for l in range(w_in.shape[0]):
        x = _layer(x, w_in[l], w_out[l], pre_norm_w[l], post_norm_w[l], cmp_pos[l], cmp_w1[l], cmp_b1[l],
                   cmp_w2[l], cmp_b2[l], gate_b[l], conv_w[l], conv_b[l], dt_bias[l], a_log[l], d_skip[l],
                   ssm_norm_w[l])
    return x
```

```python
import numpy as np
import jax
import jax.numpy as jnp
from jax import lax
from jax.experimental import pallas as pl
from jax.experimental.pallas import tpu as pltpu

F32 = jnp.float32
BF16 = jnp.bfloat16

D_MODEL = 1024
ATTN_HEADS = 16
HEAD_DIM = 64
ATTN_WIDTH = ATTN_HEADS * HEAD_DIM
KV_HEADS = 4
Q_PER_KV = ATTN_HEADS // KV_HEADS
KV_WIDTH = KV_HEADS * HEAD_DIM
ROPE_DIM = HEAD_DIM // 4
ROPE_HALF = ROPE_DIM // 2
ROPE_THETA = 500000.0
CMP_BLOCK = 32
CMP_STRIDE = 16
CMP_HIDDEN = 256
SLC_BLOCK = 64
SLC_TOPN = 16
WINDOW = 512
N_BRANCH = 3
SSM_HEADS = 16
SSM_HEAD_DIM = 64
SSM_WIDTH = SSM_HEADS * SSM_HEAD_DIM
SSM_GROUPS = 4
SSM_STATE = 128
CONV_WIDTH = 4
CONV_CH = SSM_WIDTH + 2 * SSM_GROUPS * SSM_STATE
MIX_WIDTH = ATTN_WIDTH + SSM_WIDTH
EPS = 1e-6
NEG = -1e30
BIG = 1e30
M_FLOOR = -1e29
LOG2E = 1.4426950408889634
Q_SCALE = HEAD_DIM ** -0.5 * LOG2E

LANES = 128
SUBLANES = 8
N_PAIR_COLS = KV_HEADS // 2
HALF_W = ATTN_WIDTH // N_PAIR_COLS
SMALL_W = LANES
ONES_ROWS = 16
VT_ROWS = HEAD_DIM + ONES_ROWS
DT_LANE0 = ATTN_HEADS * N_BRANCH
VMEM_LIMIT = 56 * 1024 * 1024

PROJ_ROWS = 512
W_CAST_ROWS = 1344

W_ROWS = dict(q=ATTN_WIDTH, cm=2 * KV_WIDTH, k_sl=KV_WIDTH, v_sl=KV_WIDTH, k_wn=KV_WIDTH, v_wn=KV_WIDTH,
              gate=ATTN_HEADS * N_BRANCH, z_att=ATTN_WIDTH, z_ssm=SSM_WIDTH, xbc=CONV_CH, dt=SSM_HEADS)
W_OFF = dict(zip(W_ROWS, np.cumsum([0] + list(W_ROWS.values())[:-1]).tolist()))
HEAD_OF_SLOT = [(2 * c + gp) * Q_PER_KV + r for c in range(N_PAIR_COLS) for r in range(Q_PER_KV) for gp in range(2)]
ATTN_Q = 256
ATTN_K = 256
SLC_GROUP = 4
MAX_GROUP = max(SLC_GROUP, WINDOW // ATTN_K + 1)
SCORE_LEAD = 3
WIDE_GROUP = 3
SCORE_LEAD_WIDE = 2
SSD_CHUNK = 256
CARRY_ROWS = 16
OUT_ROWS = 1024


def _dot(a, b):
    return jnp.dot(a, b, preferred_element_type=F32)


def _dot_nt(a, b):
    return lax.dot_general(a, b, (((1,), (1,)), ((), ())), preferred_element_type=F32)


def _split3(x):
    x1 = x.astype(BF16)
    r1 = x - x1.astype(F32)
    x2 = r1.astype(BF16)
    x3 = (r1 - x2.astype(F32)).astype(BF16)
    return x1, x2, x3


def _select_dot(sel, x):
    x1, x2, x3 = _split3(x)
    return _dot(sel, x1) + _dot(sel, x2) + _dot(sel, x3)


def _sigmoid(x):
    return 1.0 / (1.0 + jnp.exp2(x * -LOG2E))


def _silu(x):
    return x * _sigmoid(x)


def _rope_chunk(c, cos_t, sin_a, sin_b):
    return c * cos_t + pltpu.roll(c, LANES - ROPE_HALF, 1) * sin_a + pltpu.roll(c, ROPE_HALF, 1) * sin_b


def _proj_kernel(x_ref, x_next_ref, pre_w_ref, wt_ref, pick_ref, cos_ref, sina_ref, sinb_ref, cos8_ref, sin8_ref,
                 qnt_ref, qrt_ref, cm_ref, kk_ref, vvt_ref, zat_ref, zs_ref, xbc_ref, sm_ref, smt_ref, h_scr):
    def normed(ref):
        x = ref[...]
        return (x * lax.rsqrt(jnp.mean(x * x, axis=-1, keepdims=True) + EPS) * pre_w_ref[...]).astype(BF16)

    @pl.when(pl.program_id(0) == 0)
    def _():
        h_scr[...] = normed(x_ref)

    h = h_scr[...]
    seg = lambda name: wt_ref[W_OFF[name]:W_OFF[name] + W_ROWS[name], :]

    qt = _dot_nt(seg("q"), h)
    zat = _dot_nt(seg("z_att"), h).astype(BF16)
    cos8, sin8 = cos8_ref[...], sin8_ref[...]
    for slot, head in enumerate(HEAD_OF_SLOT):
        src = slice(head * HEAD_DIM, (head + 1) * HEAD_DIM)
        dst = slice(slot * HEAD_DIM, (slot + 1) * HEAD_DIM)
        q_head = qt[src]
        t1, t2 = q_head[0:ROPE_HALF], q_head[ROPE_HALF:ROPE_DIM]
        rot = jnp.concatenate([t1 * cos8 - t2 * sin8, t2 * cos8 + t1 * sin8, q_head[ROPE_DIM:]], axis=0)
        qnt_ref[dst, :] = q_head.astype(BF16)
        qrt_ref[dst, :] = rot.astype(BF16)
        zat_ref[dst, :] = zat[src]

    smt = jnp.concatenate([_dot_nt(seg("gate"), h), _dot_nt(seg("dt"), h),
                           jnp.zeros((SMALL_W - DT_LANE0 - SSM_HEADS, PROJ_ROWS), F32)], axis=0)
    smt_ref[...] = smt
    sm_ref[...] = smt.T

    cm = _dot_nt(h, seg("cm")).astype(BF16)
    by_offset = _dot(pick_ref[...], cm).astype(BF16)
    n_grp = PROJ_ROWS // CMP_STRIDE
    for t in range(CMP_STRIDE):
        for k in range(2 * N_PAIR_COLS):
            cm_ref[k, :, t * LANES:(t + 1) * LANES] = by_offset[t * n_grp:(t + 1) * n_grp, k * LANES:(k + 1) * LANES]

    cos_t, sin_a, sin_b = cos_ref[...], sina_ref[...], sinb_ref[...]
    ones = jnp.ones((ONES_ROWS, ATTN_K), BF16)
    for t, (k_name, v_name) in enumerate((("k_sl", "v_sl"), ("k_wn", "v_wn"))):
        kk = _dot_nt(h, seg(k_name))
        for k in range(N_PAIR_COLS):
            c = kk[:, k * LANES:(k + 1) * LANES]
            kk_ref[t, :, k * LANES:(k + 1) * LANES] = _rope_chunk(c, cos_t, sin_a, sin_b).astype(BF16)
        vt = _dot_nt(seg(v_name), h).astype(BF16)
        for j in range(PROJ_ROWS // ATTN_K):
            for g in range(KV_HEADS):
                vvt_ref[t, j, g * VT_ROWS:g * VT_ROWS + HEAD_DIM, :] = vt[g * HEAD_DIM:(g + 1) * HEAD_DIM,
                                                                          j * ATTN_K:(j + 1) * ATTN_K]
                vvt_ref[t, j, g * VT_ROWS + HEAD_DIM:(g + 1) * VT_ROWS, :] = ones

    zs_ref[...] = _dot_nt(h, seg("z_ssm")).astype(BF16)
    xbc_ref[...] = _dot_nt(h, seg("xbc")).astype(BF16)
    h_scr[...] = normed(x_next_ref)


def _cast_kernel(w_ref, out_ref):
    row = pl.program_id(0) * W_CAST_ROWS + lax.broadcasted_iota(jnp.int32, (W_CAST_ROWS, 1), 0)
    out_ref[...] = (w_ref[...] * jnp.where(row < ATTN_WIDTH, Q_SCALE, 1.0)).astype(BF16)


def _cast_call(w_t):
    rows, width = w_t.shape
    spec = pl.BlockSpec((W_CAST_ROWS, width), lambda i: (i, 0))
    return pl.pallas_call(
        _cast_kernel,
        grid=(rows // W_CAST_ROWS,),
        in_specs=[spec],
        out_specs=spec,
        out_shape=jax.ShapeDtypeStruct((rows, width), BF16),
        compiler_params=pltpu.CompilerParams(dimension_semantics=("arbitrary",)),
        name="wcast",
    )(w_t)


def _resident(shape):
    nd = len(shape)
    return pl.BlockSpec(shape, lambda *_: (0,) * nd, pipeline_mode=pl.Buffered(1))


def _proj_call(x2, pre_w, wt, pick, tabs, seq):
    rows = x2.shape[0]
    tm = PROJ_ROWS
    n_seq_tiles = seq // tm
    k_per_tile = tm // ATTN_K
    row_spec = lambda w: pl.BlockSpec((tm, w), lambda i: (i, 0))
    col_spec = lambda h: pl.BlockSpec((h, tm), lambda i: (0, i))
    tab_spec = pl.BlockSpec((tm, LANES), lambda i: (i % n_seq_tiles, 0))
    tab8_spec = pl.BlockSpec((ROPE_HALF, tm), lambda i: (0, i % n_seq_tiles))
    return pl.pallas_call(
        _proj_kernel,
        grid=(rows // tm,),
        in_specs=[row_spec(D_MODEL), pl.BlockSpec((tm, D_MODEL), lambda i: (jnp.minimum(i + 1, rows // tm - 1), 0)),
                  _resident(pre_w.shape), _resident(wt.shape), _resident(pick.shape),
                  tab_spec, tab_spec, tab_spec, tab8_spec, tab8_spec],
        out_specs=[col_spec(ATTN_WIDTH), col_spec(ATTN_WIDTH),
                   pl.BlockSpec((2 * N_PAIR_COLS, tm // CMP_STRIDE, CMP_STRIDE * LANES), lambda i: (0, i, 0)),
                   pl.BlockSpec((2, tm, KV_WIDTH), lambda i: (0, i, 0)),
                   pl.BlockSpec((2, k_per_tile, KV_HEADS * VT_ROWS, ATTN_K), lambda i: (0, i, 0, 0)),
                   col_spec(ATTN_WIDTH), row_spec(SSM_WIDTH), row_spec(CONV_CH), row_spec(SMALL_W),
                   col_spec(SMALL_W)],
        out_shape=[jax.ShapeDtypeStruct((ATTN_WIDTH, rows), BF16),
                   jax.ShapeDtypeStruct((ATTN_WIDTH, rows), BF16),
                   jax.ShapeDtypeStruct((2 * N_PAIR_COLS, rows // CMP_STRIDE, CMP_STRIDE * LANES), BF16),
                   jax.ShapeDtypeStruct((2, rows, KV_WIDTH), BF16),
                   jax.ShapeDtypeStruct((2, rows // ATTN_K, KV_HEADS * VT_ROWS, ATTN_K), BF16),
                   jax.ShapeDtypeStruct((ATTN_WIDTH, rows), BF16),
                   jax.ShapeDtypeStruct((rows, SSM_WIDTH), BF16),
                   jax.ShapeDtypeStruct((rows, CONV_CH), BF16),
                   jax.ShapeDtypeStruct((rows, SMALL_W), F32),
                   jax.ShapeDtypeStruct((SMALL_W, rows), F32)],
        compiler_params=pltpu.CompilerParams(dimension_semantics=("arbitrary",),
                                             vmem_limit_bytes=VMEM_LIMIT),
        scratch_shapes=[pltpu.VMEM((tm, D_MODEL), BF16)],
        name="proj",
    )(x2, x2, pre_w, wt, pick, *tabs)


def _compress_kernel(x_ref, bd1_ref, w1_ref, pos_ref, b1_ref, bd2_ref, b2_ref, bd2t_ref, b2t_ref,
                     out_ref, outt_ref, acc_ref):
    n_rows = x_ref.shape[0]
    hid2 = 2 * CMP_HIDDEN
    acc_ref[n_rows:, :] = jnp.zeros((SUBLANES, 2 * hid2), F32)
    acc_ref[0:n_rows, :] = _dot(x_ref[...], bd1_ref[...])
    first = acc_ref[0:n_rows, 0:hid2]
    second = acc_ref[pl.ds(1, n_rows), hid2:2 * hid2]
    posterm = _dot(pos_ref[...].astype(BF16), w1_ref[...].astype(BF16))[0:1, :] + b1_ref[...]
    hcat = first + second + jnp.concatenate([posterm, posterm], axis=1)
    act = _silu(hcat).astype(BF16)
    out_ref[...] = _dot(act, bd2_ref[...]) + b2_ref[...]
    outt_ref[...] = _dot_nt(bd2t_ref[...], act) + b2t_ref[...]


def _compress_call(cm4, bd1, w1, pos8, b1, bd2, b2, bd2t, b2t):
    n_rows = cm4.shape[1]
    per_s = lambda *tail: pl.BlockSpec((None,) + tail, lambda s, c: (s,) + (0,) * len(tail))
    return pl.pallas_call(
        _compress_kernel,
        grid=(2, N_PAIR_COLS),
        in_specs=[pl.BlockSpec((None, n_rows, CMP_STRIDE * LANES), lambda s, c: (s * N_PAIR_COLS + c, 0, 0)),
                  per_s(CMP_STRIDE * LANES, 4 * CMP_HIDDEN),
                  per_s(CMP_BLOCK * HEAD_DIM, CMP_HIDDEN),
                  per_s(SUBLANES, CMP_BLOCK * HEAD_DIM),
                  per_s(1, CMP_HIDDEN),
                  per_s(2 * CMP_HIDDEN, LANES),
                  per_s(1, LANES),
                  per_s(LANES, 2 * CMP_HIDDEN),
                  per_s(LANES, 1)],
        out_specs=[pl.BlockSpec((None, None, n_rows, LANES), lambda s, c: (s, c, 0, 0)),
                   pl.BlockSpec((None, None, LANES, n_rows), lambda s, c: (s, c, 0, 0))],
        out_shape=[jax.ShapeDtypeStruct((2, N_PAIR_COLS, n_rows, LANES), F32),
                   jax.ShapeDtypeStruct((2, N_PAIR_COLS, LANES, n_rows), F32)],
        scratch_shapes=[pltpu.VMEM((n_rows + SUBLANES, 4 * CMP_HIDDEN), F32)],
        compiler_params=pltpu.CompilerParams(dimension_semantics=("arbitrary", "arbitrary"),
                                             vmem_limit_bytes=VMEM_LIMIT),
        name="compress",
    )(cm4, bd1, w1, pos8, b1, bd2, b2, bd2t, b2t)


def _attn_kernel(qnt_ref, qrt_ref, kc_ref, vct_ref, ksl_ref, kwn_ref, vslt_ref, vwnt_ref, smt_ref, gb_ref,
                 ovt_ref, tri_ref, zat_ref, out_ref,
                 qw_ref, qs_ref, bias_ref, m_ref, acc_ref, osum_ref, s_ref, kaug_ref, gate_ref, cs_ref):
    tq, tk = ATTN_Q, ATTN_K
    n_r = Q_PER_KV
    qt = pl.program_id(2)
    q0 = qt * tq
    row = lax.broadcasted_iota(jnp.int32, (LANES, 1), 0)
    halves = (row < HEAD_DIM, row >= HEAD_DIM)
    qpos = q0 + lax.broadcasted_iota(jnp.int32, (1, tq), 1)

    gate_ref[...] = _sigmoid(smt_ref[...] + gb_ref[...])
    first_group = pl.program_id(1) * 2

    def gate_row(br, idx):
        head = (first_group + idx % 2) * n_r + idx // 2
        return gate_ref[pl.ds(head * N_BRANCH + br, 1), :]

    kc = kc_ref[...].astype(BF16)
    vct = vct_ref[...].astype(BF16)
    cpos = row * CMP_STRIDE + (CMP_BLOCK - 1)
    cbias = jnp.where(cpos <= qpos, 0.0, NEG)
    psum = [jnp.zeros((LANES, tq), F32), jnp.zeros((LANES, tq), F32)]
    osum_ref[...] = jnp.zeros(osum_ref.shape, F32)
    for idx in range(2 * n_r):
        q = qnt_ref[(idx // 2) * LANES:(idx // 2 + 1) * LANES, :]
        qm = jnp.where(halves[idx % 2], q, jnp.zeros_like(q))
        cs_ref[idx] = _dot(kc, qm) + cbias

    def compressed_softmax():
        for idx in range(2 * n_r):
            r, hf = divmod(idx, 2)
            s = cs_ref[idx]
            m = jnp.maximum(jnp.max(s, axis=0, keepdims=True), M_FLOOR)
            p = jnp.exp2(s - m)
            l = jnp.sum(p, axis=0, keepdims=True)
            p = p * jnp.where(l > 0.0, 1.0 / l, 0.0)
            psum[hf] = psum[hf] + p
            hs = slice(hf * HEAD_DIM, (hf + 1) * HEAD_DIM)
            osum_ref[r, hs, :] += gate_row(0, idx) * _dot(vct[hs, :], p.astype(BF16))

    nb = ksl_ref.shape[0] // SLC_BLOCK
    jio = lax.broadcasted_iota(jnp.int32, (nb, tq), 0)
    cur = (q0 + lax.broadcasted_iota(jnp.int32, (nb, tq), 1)) // SLC_BLOCK
    forced = (jio == 0) | (jio == cur) | (jio == cur - 1)

    def select_blocks():
        for hf in range(2):
            imp = _select_dot(ovt_ref[...], psum[hf])[0:nb, :]
            imp = jnp.where(forced, BIG, imp)
            imp = jnp.where(jio > cur, -BIG, imp)
            cnt = jnp.zeros((nb, tq), F32)
            for i in range(nb):
                other = imp[i:i + 1, :]
                beats = (other > imp) | ((other == imp) & (jio > i))
                cnt = cnt + jnp.where(beats, 1.0, 0.0)
            bias_ref[hf] = jnp.where(cnt < float(SLC_TOPN), 0.0, NEG)

    def load_queries(qm_ref, with_selection_bias):
        spare = jnp.zeros((HEAD_DIM - nb, tq), BF16)
        for r in range(n_r):
            q = qrt_ref[r * LANES:(r + 1) * LANES, :]
            for hf in range(2):
                if with_selection_bias:
                    fill = [bias_ref[hf].astype(BF16), spare]
                else:
                    fill = [jnp.zeros((HEAD_DIM, tq), BF16)]
                pieces = [q[0:HEAD_DIM]] + fill if hf == 0 else fill + [q[HEAD_DIM:]]
                qm_ref[r * 2 + hf] = jnp.concatenate(pieces, axis=0)

    lane_io = lax.broadcasted_iota(jnp.int32, (1, LANES), 1)
    key_blk = lax.broadcasted_iota(jnp.int32, (tk, 1), 0) // SLC_BLOCK

    def keys_with_block_indicator(k_tile, kt, hf):
        first = 0 if hf == 1 else HEAD_DIM
        onehot = (lane_io - first) == (kt * (tk // SLC_BLOCK) + key_blk)
        keep = (lane_io >= HEAD_DIM) if hf == 1 else (lane_io < HEAD_DIM)
        return jnp.where(keep, k_tile, jnp.where(onehot, 1.0, 0.0).astype(BF16))

    def reset():
        m_ref[...] = jnp.full(m_ref.shape, M_FLOOR, F32)
        acc_ref[...] = jnp.zeros(acc_ref.shape, F32)

    def flash_group(qm_ref, k_ref, vt_ref, parts, indicator=False):
        if indicator:
            for pi, (kt, _) in enumerate(parts):
                k_tile = k_ref[pl.ds(pl.multiple_of(kt * tk, tk), tk), :]
                for hf in range(2):
                    kaug_ref[pi, hf] = keys_with_block_indicator(k_tile, kt, hf)

        def scores(idx):
            col_max = None
            for pi, (kt, bias_fn) in enumerate(parts):
                if indicator:
                    k_tile = kaug_ref[pi, idx % 2]
                else:
                    k_tile = k_ref[pl.ds(pl.multiple_of(kt * tk, tk), tk), :]
                s = _dot(k_tile, qm_ref[idx])
                if bias_fn is not None:
                    s = bias_fn(s)
                s_ref[idx, pi * tk:(pi + 1) * tk, :] = s
                part_max = jnp.max(s, axis=0, keepdims=True)
                col_max = part_max if col_max is None else jnp.maximum(col_max, part_max)
            return col_max

        n_units = 2 * n_r
        lead = SCORE_LEAD_WIDE if len(parts) >= WIDE_GROUP else SCORE_LEAD
        col_maxes = [scores(idx) for idx in range(min(lead, n_units))]
        for idx in range(n_units):
            if idx + lead < n_units:
                col_maxes.append(scores(idx + lead))
            m_old = m_ref[idx:idx + 1, :]
            m_new = jnp.maximum(m_old, col_maxes[idx])
            m_ref[idx:idx + 1, :] = m_new
            rows = slice(idx * VT_ROWS, (idx + 1) * VT_ROWS)
            vrows = slice((idx % 2) * VT_ROWS, (idx % 2 + 1) * VT_ROWS)
            upd = acc_ref[rows, :] * jnp.exp2(m_old - m_new)
            for pi, (kt, _) in enumerate(parts):
                p = jnp.exp2(s_ref[idx, pi * tk:(pi + 1) * tk, :] - m_new).astype(BF16)
                upd = upd + _dot(vt_ref[kt, vrows, :], p)
            acc_ref[rows, :] = upd

    def finish(branch):
        for idx in range(2 * n_r):
            r, hf = divmod(idx, 2)
            base = idx * VT_ROWS
            l = acc_ref[base + HEAD_DIM:base + HEAD_DIM + 1, :]
            scale = gate_row(branch, idx) * jnp.where(l > 0.0, 1.0 / l, 0.0)
            hs = slice(hf * HEAD_DIM, (hf + 1) * HEAD_DIM)
            osum_ref[r, hs, :] += acc_ref[base:base + HEAD_DIM, :] * scale

    tri_causal, tri_tail, tri_open, tri_closed = 0, 1, 2, 3
    causal = lambda s: s + tri_ref[tri_causal]

    load_queries(qw_ref, with_selection_bias=False)
    reset()
    tiles_back = WINDOW // tk
    win_parts = []
    for back in range(tiles_back, -1, -1):
        kt = jnp.maximum(qt - back, 0)
        inside = tri_causal if back == 0 else (tri_tail if back == tiles_back else tri_open)
        table = inside if back == 0 else jnp.where(qt >= back, inside, tri_closed)
        win_parts.append((kt, lambda s, table=table: s + tri_ref[table]))
    flash_group(qw_ref, kwn_ref, vwnt_ref, win_parts)
    finish(2)
    compressed_softmax()

    select_blocks()
    load_queries(qs_ref, with_selection_bias=True)
    reset()
    full_groups = qt // SLC_GROUP

    def slc_full(j, carry):
        flash_group(qs_ref, ksl_ref, vslt_ref, [(SLC_GROUP * j + i, None) for i in range(SLC_GROUP)],
                    indicator=True)
        return carry

    lax.fori_loop(0, full_groups, slc_full, 0)

    for n_last in range(1, SLC_GROUP + 1):
        @pl.when(qt % SLC_GROUP == n_last - 1)
        def _(n_last=n_last):
            first = SLC_GROUP * full_groups
            parts = [(first + i, None) for i in range(n_last - 1)] + [(qt, causal)]
            flash_group(qs_ref, ksl_ref, vslt_ref, parts, indicator=True)

    finish(1)

    for r in range(n_r):
        z = zat_ref[r * LANES:(r + 1) * LANES, :].astype(F32)
        out_ref[:, r * LANES:(r + 1) * LANES] = (osum_ref[r] * _silu(z)).T.astype(BF16)


def _attn_call(qnt, qrt, kc_std, kc_t, kk, vvt, smt, gb, ovt, tri, zat, bsz, seq):
    tq, tk = ATTN_Q, ATTN_K
    n_qt = seq // tq
    n_kt = seq // tk
    qt_spec = pl.BlockSpec((HALF_W, tq), lambda b, c, t: (c, b * n_qt + t))
    out_spec = pl.BlockSpec((tq, HALF_W), lambda b, c, t: (b * n_qt + t, c))
    k_spec = lambda s: pl.BlockSpec((None, seq, LANES), lambda b, c, t: (s, b, c))
    vt_spec = lambda s: pl.BlockSpec((None, n_kt, 2 * VT_ROWS, tk), lambda b, c, t: (s, b, c, 0))
    const = lambda a: pl.BlockSpec(a.shape, lambda b, c, t: (0,) * a.ndim)
    return pl.pallas_call(
        _attn_kernel,
        grid=(bsz, N_PAIR_COLS, n_qt),
        in_specs=[qt_spec, qt_spec,
                  pl.BlockSpec((None, None, LANES, LANES), lambda b, c, t: (0, c, b, 0)),
                  pl.BlockSpec((None, None, LANES, LANES), lambda b, c, t: (1, c, 0, b)),
                  k_spec(0), k_spec(1), vt_spec(0), vt_spec(1),
                  pl.BlockSpec((SMALL_W, tq), lambda b, c, t: (0, b * n_qt + t)),
                  const(gb),
                  const(ovt), const(tri), qt_spec],
        out_specs=out_spec,
        out_shape=jax.ShapeDtypeStruct((bsz * seq, ATTN_WIDTH), BF16),
        scratch_shapes=[pltpu.VMEM((2 * Q_PER_KV, LANES, tq), BF16),
                        pltpu.VMEM((2 * Q_PER_KV, LANES, tq), BF16),
                        pltpu.VMEM((2, seq // SLC_BLOCK, tq), F32),
                        pltpu.VMEM((2 * Q_PER_KV, tq), F32),
                        pltpu.VMEM((2 * Q_PER_KV * VT_ROWS, tq), F32),
                        pltpu.VMEM((Q_PER_KV, LANES, tq), F32),
                        pltpu.VMEM((2 * Q_PER_KV, MAX_GROUP * tk, tq), F32),
                        pltpu.VMEM((SLC_GROUP, 2, tk, LANES), BF16),
                        pltpu.VMEM((SMALL_W, tq), F32),
                        pltpu.VMEM((2 * Q_PER_KV, LANES, tq), F32)],
        compiler_params=pltpu.CompilerParams(dimension_semantics=("arbitrary", "arbitrary", "arbitrary"),
                                             vmem_limit_bytes=VMEM_LIMIT),
        name="attn",
    )(qnt, qrt, kc_std, kc_t, kk, kk, vvt, vvt, smt, gb, ovt, tri, zat)


def _ssd_kernel(xbc_ref, sm_ref, cw_ref, cb_ref, dtb_ref, alog_ref, dskip_ref, zs_ref, nw_ref,
                shift_ref, head_ref, out_ref, carry_ref, state_ref, y_ref):
    L = SSD_CHUNK
    j = pl.program_id(1)
    n_prev = CONV_WIDTH - 1

    @pl.when(j == 0)
    def _():
        carry_ref[...] = jnp.zeros(carry_ref.shape, BF16)
        state_ref[...] = jnp.zeros(state_ref.shape, F32)

    def conv_act(c0, width):
        cols = slice(c0, c0 + width)
        u = xbc_ref[:, cols]
        shifted = _dot(shift_ref[...], u)
        conv = cb_ref[:, cols] + u.astype(F32) * cw_ref[n_prev:n_prev + 1, cols]
        head = jnp.zeros((SUBLANES, width), F32)
        for w in range(n_prev):
            conv = conv + shifted[w * L:(w + 1) * L] * cw_ref[w:w + 1, cols]
            head = head + _dot(head_ref[w], carry_ref[:, cols]) * cw_ref[w:w + 1, cols]
        return _silu(jnp.concatenate([conv[0:SUBLANES] + head, conv[SUBLANES:]], axis=0))

    sm = sm_ref[...] + dtb_ref[...]
    dt = jnp.maximum(sm, 0.0) + jnp.log1p(jnp.exp(-jnp.abs(sm)))
    a = dt * (-jnp.exp(alog_ref[...]) * LOG2E)
    tril = lax.broadcasted_iota(jnp.int32, (L, L), 0) >= lax.broadcasted_iota(jnp.int32, (L, L), 1)
    a_cs = _select_dot(jnp.where(tril, 1.0, 0.0).astype(BF16), a)
    a_cs_t = a_cs.T
    lane = lax.broadcasted_iota(jnp.int32, (1, LANES), 1)
    lo = lane < SSM_HEAD_DIM
    halves = (lo, jnp.logical_not(lo))

    def per_head_lanes(v, first_pair, n_pairs):
        pairs = [jnp.where(lo, v[:, DT_LANE0 + 2 * i:DT_LANE0 + 2 * i + 1],
                           v[:, DT_LANE0 + 2 * i + 1:DT_LANE0 + 2 * i + 2])
                 for i in range(first_pair, first_pair + n_pairs)]
        return jnp.concatenate(pairs, axis=1)

    groups_per_pass = SSM_GROUPS // 2
    pairs_per_pass = SSM_HEADS // 4
    x_width = SSM_WIDTH // 2
    bc_width = groups_per_pass * SSM_STATE
    for ps in range(2):
        x_c = conv_act(ps * x_width, x_width)
        b_two = conv_act(SSM_WIDTH + ps * bc_width, bc_width)
        c_two = conv_act(SSM_WIDTH + SSM_GROUPS * SSM_STATE + ps * bc_width, bc_width)
        a_exp = per_head_lanes(a_cs, ps * pairs_per_pass, pairs_per_pass)
        a_last = a_exp[L - 1:L, :]
        xdt = x_c * per_head_lanes(dt, ps * pairs_per_pass, pairs_per_pass)
        xdt_bf = xdt.astype(BF16)
        xw = (xdt * jnp.exp2(a_last - a_exp)).astype(BF16)
        ea = jnp.exp2(a_exp)
        chunk_decay = jnp.exp2(a_last)
        for gg in range(groups_per_pass):
            bg = b_two[:, gg * SSM_STATE:(gg + 1) * SSM_STATE]
            cg = c_two[:, gg * SSM_STATE:(gg + 1) * SSM_STATE].astype(BF16)
            cb = _dot_nt(cg, bg.astype(BF16))
            bg_t = bg.T.astype(BF16)
            for ii in range(2):
                loc = slice((2 * gg + ii) * LANES, (2 * gg + ii + 1) * LANES)
                i = ps * pairs_per_pass + 2 * gg + ii
                sl = slice(i * LANES, (i + 1) * LANES)
                y = jnp.zeros((L, LANES), F32)
                for hh in range(2):
                    h = 2 * i + hh
                    col = a_cs[:, DT_LANE0 + h:DT_LANE0 + h + 1]
                    row = a_cs_t[DT_LANE0 + h:DT_LANE0 + h + 1, :]
                    decay = jnp.exp2(jnp.where(tril, col - row, NEG))
                    xh = jnp.where(halves[hh], xdt_bf[:, loc], jnp.zeros((), BF16))
                    y = y + _dot((cb * decay).astype(BF16), xh)
                st = state_ref[i]
                y = y + _dot(cg, st.astype(BF16)) * ea[:, loc]
                state_ref[i] = st * chunk_decay[:, loc] + _dot(bg_t, xw[:, loc])
                y_ref[:, sl] = y + dskip_ref[:, sl] * x_c[:, loc]
    carry_ref[...] = xbc_ref[L - CARRY_ROWS:L, :]

    y = y_ref[...] * _silu(zs_ref[...].astype(F32))
    y = y * lax.rsqrt(jnp.mean(y * y, axis=-1, keepdims=True) + EPS) * nw_ref[...]
    out_ref[...] = y.astype(BF16)


def _ssd_call(xbc, sm, cw, cb, dtb, alog, dskip, zs, nw, shift, head, bsz, seq):
    L = SSD_CHUNK
    n_ch = seq // L
    row = lambda w: pl.BlockSpec((L, w), lambda b, j: (b * n_ch + j, 0))
    const = lambda a: pl.BlockSpec(a.shape, lambda b, j: (0,) * a.ndim)
    return pl.pallas_call(
        _ssd_kernel,
        grid=(bsz, n_ch),
        in_specs=[row(CONV_CH), row(SMALL_W), const(cw), const(cb), const(dtb), const(alog),
                  const(dskip), row(SSM_WIDTH), const(nw), const(shift), const(head)],
        out_specs=row(SSM_WIDTH),
        out_shape=jax.ShapeDtypeStruct((bsz * seq, SSM_WIDTH), BF16),
        scratch_shapes=[pltpu.VMEM((CARRY_ROWS, CONV_CH), BF16),
                        pltpu.VMEM((SSM_HEADS // 2, SSM_STATE, LANES), F32),
                        pltpu.VMEM((L, SSM_WIDTH), F32)],
        compiler_params=pltpu.CompilerParams(dimension_semantics=("arbitrary", "arbitrary"),
                                             vmem_limit_bytes=VMEM_LIMIT),
        name="ssd",
    )(xbc, sm, cw, cb, dtb, alog, dskip, zs, nw, shift, head)


def _out_kernel(att_ref, ssm_ref, x_ref, w_ref, pw_ref, out_ref, wa_scr, ws_scr):
    @pl.when(pl.program_id(0) == 0)
    def _():
        for slot, head in enumerate(HEAD_OF_SLOT):
            wa_scr[slot * HEAD_DIM:(slot + 1) * HEAD_DIM, :] = w_ref[head * HEAD_DIM:(head + 1) * HEAD_DIM,
                                                                     :].astype(BF16)
        ws_scr[...] = w_ref[ATTN_WIDTH:, :].astype(BF16)

    o = _dot(att_ref[...], wa_scr[...]) + _dot(ssm_ref[...], ws_scr[...])
    o = o * lax.rsqrt(jnp.mean(o * o, axis=-1, keepdims=True) + EPS) * pw_ref[...]
    out_ref[...] = x_ref[...] + o


def _out_call(att, ssm, x2, w_out, pw):
    rows = x2.shape[0]
    tm = OUT_ROWS
    row = lambda w: pl.BlockSpec((tm, w), lambda i: (i, 0))
    return pl.pallas_call(
        _out_kernel,
        grid=(rows // tm,),
        in_specs=[row(ATTN_WIDTH), row(SSM_WIDTH), row(D_MODEL), _resident(w_out.shape), _resident(pw.shape)],
        out_specs=row(D_MODEL),
        out_shape=jax.ShapeDtypeStruct((rows, D_MODEL), F32),
        compiler_params=pltpu.CompilerParams(dimension_semantics=("arbitrary",),
                                             vmem_limit_bytes=VMEM_LIMIT),
        scratch_shapes=[pltpu.VMEM((ATTN_WIDTH, D_MODEL), BF16), pltpu.VMEM((SSM_WIDTH, D_MODEL), BF16)],
        name="outproj",
    )(att, ssm, x2, w_out, pw)


def _constants(seq):

    nc = (seq - CMP_BLOCK) // CMP_STRIDE + 1
    nb = seq // SLC_BLOCK
    ci = np.arange(nc)[:, None] * CMP_STRIDE
    bj = np.arange(nb)[None, :] * SLC_BLOCK
    overlap = ((ci <= bj + SLC_BLOCK - 1) & (ci + CMP_BLOCK - 1 >= bj)).astype(np.float32)
    ovt = np.zeros((LANES, LANES), np.float32)
    ovt[:nb, :nc] = overlap.T

    key = np.arange(ATTN_K)[:, None]
    qry = np.arange(ATTN_Q)[None, :]
    tri = np.stack([np.where(key <= qry, 0.0, NEG), np.where(key > qry, 0.0, NEG),
                    np.zeros((ATTN_K, ATTN_Q)), np.full((ATTN_K, ATTN_Q), NEG)]).astype(np.float32)
    n_prev = CONV_WIDTH - 1
    shift = np.zeros((n_prev * SSD_CHUNK, SSD_CHUNK), np.float32)
    head = np.zeros((n_prev, SUBLANES, CARRY_ROWS), np.float32)
    for w in range(n_prev):
        back = n_prev - w
        for t in range(SSD_CHUNK):
            if t - back >= 0:
                shift[w * SSD_CHUNK + t, t - back] = 1.0
            elif t < SUBLANES:
                head[w, t, CARRY_ROWS + t - back] = 1.0
    n_grp = PROJ_ROWS // CMP_STRIDE
    pick = np.zeros((PROJ_ROWS, PROJ_ROWS), np.float32)
    for t in range(CMP_STRIDE):
        pick[t * n_grp + np.arange(n_grp), np.arange(n_grp) * CMP_STRIDE + t] = 1.0
    return ovt, tri, shift, head, pick


def _rope_tables(seq):
    half_freqs = ROPE_THETA ** (-np.arange(ROPE_HALF, dtype=np.float64) * 2.0 / ROPE_DIM)
    ang = np.arange(seq, dtype=np.float64)[:, None] * half_freqs[None, :]
    cos, sin = np.cos(ang), np.sin(ang)
    ones = np.ones((seq, HEAD_DIM - ROPE_DIM))
    zeros_h = np.zeros((seq, ROPE_HALF))
    zeros_r = np.zeros((seq, HEAD_DIM - ROPE_DIM))
    cos_h = np.concatenate([cos, cos, ones], axis=1)
    sina_h = np.concatenate([-sin, zeros_h, zeros_r], axis=1)
    sinb_h = np.concatenate([zeros_h, sin, zeros_r], axis=1)
    tile2 = lambda t: np.concatenate([t, t], axis=1)
    return tuple(jnp.asarray(t, F32) for t in (tile2(cos_h), tile2(sina_h), tile2(sinb_h), cos.T, sin.T))


def _layer(x, w_in, w_out, pre_w, post_w, cmp_pos, cmp_w1, cmp_b1, cmp_w2, cmp_b2,
           gate_b, conv_w, conv_b, dt_bias, a_log, d_skip, ssm_norm_w):
    bsz, seq, _ = x.shape
    ovt, tri, shift, head, pick = _constants(seq)
    tabs = _rope_tables(seq)

    w_t = _cast_call(jnp.swapaxes(w_in, 0, 1))
    pad_small = lambda v, at: jnp.zeros((1, SMALL_W), F32).at[0, at:at + v.shape[0]].set(v)
    gb = pad_small(gate_b, 0).T
    dtb = pad_small(dt_bias, DT_LANE0)
    alog = pad_small(a_log, DT_LANE0)

    x2 = x.reshape(bsz * seq, D_MODEL)
    qnt, qrt, cm, kk, vvt, zat, zs, xbc, sm, smt = _proj_call(x2, pre_w[None, :], w_t, jnp.asarray(pick, BF16), tabs, seq)

    w1r = cmp_w1.reshape(2, 2, CMP_STRIDE, HEAD_DIM, CMP_HIDDEN)
    wa, wb = w1r[:, 0], w1r[:, 1]
    z1 = jnp.zeros_like(wa)
    bd1 = jnp.concatenate([jnp.concatenate([wa, z1, wb, z1], axis=-1),
                           jnp.concatenate([z1, wa, z1, wb], axis=-1)], axis=-2).astype(BF16)
    bd1 = bd1.reshape(2, CMP_STRIDE * LANES, 4 * CMP_HIDDEN)
    z2 = jnp.zeros_like(cmp_w2)
    bd2 = jnp.concatenate([jnp.concatenate([cmp_w2, z2], axis=-1),
                           jnp.concatenate([z2, cmp_w2], axis=-1)], axis=-2).astype(BF16)
    pos8 = jnp.broadcast_to(cmp_pos.reshape(2, 1, CMP_BLOCK * HEAD_DIM), (2, SUBLANES, CMP_BLOCK * HEAD_DIM))
    b2 = jnp.concatenate([cmp_b2, cmp_b2], axis=-1)[:, None, :]
    kc_std, kc_t = _compress_call(cm, bd1, cmp_w1, pos8, cmp_b1[:, None, :], bd2, b2,
                                  jnp.swapaxes(bd2, 1, 2), jnp.swapaxes(b2, 1, 2))

    att = _attn_call(qnt, qrt, kc_std, kc_t, kk, vvt, smt, gb, jnp.asarray(ovt, BF16),
                     jnp.asarray(tri), zat, bsz, seq)

    dskip = jnp.repeat(d_skip, SSM_HEAD_DIM)[None, :]
    ssm = _ssd_call(xbc, sm, conv_w, conv_b[None, :], dtb, alog, dskip, zs,
                    ssm_norm_w[None, :], jnp.asarray(shift, BF16), jnp.asarray(head, BF16), bsz, seq)

    out = _out_call(att, ssm, x2, w_out, post_w[None, :])
    return out.reshape(bsz, seq, D_MODEL)


def kernel(x, w_in, w_out, pre_norm_w, post_norm_w, cmp_pos, cmp_w1, cmp_b1, cmp_w2, cmp_b2, gate_b, conv_w,
           conv_b, dt_bias, a_log, d_skip, ssm_norm_w):
    for l in range(w_in.shape[0]):
        x = _layer(x, w_in[l], w_out[l], pre_norm_w[l], post_norm_w[l], cmp_pos[l], cmp_w1[l], cmp_b1[l],
                   cmp_w2[l], cmp_b2[l], gate_b[l], conv_w[l], conv_b[l], dt_bias[l], a_log[l], d_skip[l],
                   ssm_norm_w[l])
    return x
```
